```python
import jax, jax.numpy as jnp
from jax import lax
import numpy as np

D_MODEL = 2048
BATCH = 4
SEQ = 2048
DEPTH = 1
DEC_BATCH = 128
DEC_SEQ = 4
PAST_LEN = 16384
PAGE_SIZE = 128

EXPAND = 2
D_MIX = EXPAND * D_MODEL
D_POOL = D_MIX // 4
POOL_WINDOWS = (2, 4, 8, 16)
POOL_GROUP = D_POOL // len(POOL_WINDOWS)
POOL_HIST = max(POOL_WINDOWS) - 1
D_SSM = D_MIX - D_POOL
SSM_HEAD_DIM = 64
SSM_HEADS = D_SSM // SSM_HEAD_DIM
SSM_GROUPS = 8
D_STATE = 128
CONV_WIDTH = 4
D_CONV = D_SSM + 2 * SSM_GROUPS * D_STATE
SSD_CHUNK = 128
D_IN_PROJ = 2 * D_POOL + D_SSM + D_CONV + SSM_HEADS
EPS = 1e-5

kernel_name = "pool_ssd_hybrid_step"


def rmsnorm(x, g):
    xf = x.astype(jnp.float32)
    y = xf * lax.rsqrt(jnp.mean(xf * xf, axis=-1, keepdims=True) + EPS) * g.astype(jnp.float32)
    return y.astype(x.dtype)


def pool_mix(u, hist, start_pos):
    L = u.shape[1]
    uf = u.astype(jnp.float32)
    ext = jnp.concatenate([hist.astype(jnp.float32), uf], axis=1)
    cs = jnp.cumsum(jnp.pad(ext, ((0, 0), (1, 0), (0, 0))), axis=1)
    pos = start_pos + jnp.arange(L)
    outs = []
    for gi, w in enumerate(POOL_WINDOWS):
        lo_c, hi_c = gi * POOL_GROUP, (gi + 1) * POOL_GROUP
        hi = cs[:, POOL_HIST + 1:POOL_HIST + 1 + L, lo_c:hi_c]
        lo = cs[:, POOL_HIST + 1 - w:POOL_HIST + 1 - w + L, lo_c:hi_c]
        cnt = jnp.minimum(w, pos + 1).astype(jnp.float32)[None, :, None]
        outs.append((hi - lo) / cnt - uf[:, :, lo_c:hi_c])
    pooled = jnp.stack(outs, axis=2)
    new_hist = ext[:, -POOL_HIST:].astype(hist.dtype)
    return pooled, new_hist


def ssd_scan(x, dt, a, b_in, c_in, s0):
    bsz, L, H, P = x.shape
    G, N = b_in.shape[2], b_in.shape[3]
    R = H // G
    chunk = SSD_CHUNK if L % SSD_CHUNK == 0 else L
    nc = L // chunk
    xc = x.astype(jnp.float32).reshape(bsz, nc, chunk, G, R, P)
    dtc = dt.reshape(bsz, nc, chunk, G, R)
    bc = b_in.astype(jnp.float32).reshape(bsz, nc, chunk, G, N)
    cc = c_in.astype(jnp.float32).reshape(bsz, nc, chunk, G, N)
    a_cum = jnp.cumsum(dtc * a.reshape(G, R), axis=2)
    seg = a_cum[:, :, :, None] - a_cum[:, :, None, :]
    causal = jnp.tril(jnp.ones((chunk, chunk), dtype=bool))[None, None, :, :, None, None]
    decay = jnp.exp(jnp.where(causal, seg, -jnp.inf))
    cb = jnp.einsum('bclgn,bcsgn->bclsg', cc, bc)
    scores = cb[..., None] * decay * dtc[:, :, None]
    y_diag = jnp.einsum('bclsgr,bcsgrp->bclgrp', scores, xc)
    decay_to_end = jnp.exp(a_cum[:, :, -1:] - a_cum)
    chunk_states = jnp.einsum('bclgn,bclgr,bclgrp->bcgrpn', bc, decay_to_end * dtc, xc)
    chunk_decay = jnp.exp(a_cum[:, :, -1])

    def step(s, inp):
        dec, st = inp
        return dec[..., None, None] * s + st, s

    s_init = s0.astype(jnp.float32).reshape(bsz, G, R, P, N)
    s_final, s_ins = lax.scan(step, s_init,
                              (jnp.moveaxis(chunk_decay, 1, 0), jnp.moveaxis(chunk_states, 1, 0)))
    s_ins = jnp.moveaxis(s_ins, 0, 1)
    y_off = jnp.einsum('bclgn,bcgrpn,bclgr->bclgrp', cc, s_ins, jnp.exp(a_cum))
    y = (y_diag + y_off).reshape(bsz, L, H, P)
    return y, s_final.reshape(bsz, H, P, N)


def hybrid_layer(x, pool_hist, conv_hist, ssm_state, start_pos, norm_g, w_in, conv_w, conv_b,
                 dt_bias, a_log, d_skip, ssm_norm_g, pool_w, pool_scale, w_out):
    bsz, L, _ = x.shape
    h = rmsnorm(x, norm_g)
    proj = h @ w_in
    o1 = D_POOL
    o2 = 2 * D_POOL
    o3 = o2 + D_SSM
    o4 = o3 + D_CONV
    u, gate, z, xbc, dt_raw = jnp.split(proj, [o1, o2, o3, o4], axis=-1)

    pooled, new_pool = pool_mix(u, pool_hist, start_pos)
    mixed = jnp.einsum('blgc,gcd->blgd', pooled, pool_w.astype(jnp.float32)).reshape(bsz, L, D_POOL)
    out_a = (mixed * pool_scale.astype(jnp.float32) * jax.nn.silu(gate.astype(jnp.float32))).astype(x.dtype)

    ext = jnp.concatenate([conv_hist.astype(xbc.dtype), xbc], axis=1)
    conv = conv_b + sum(ext[:, k:k + L] * conv_w[k] for k in range(CONV_WIDTH))
    new_conv = ext[:, -(CONV_WIDTH - 1):].astype(conv_hist.dtype)
    xbc_c = jax.nn.silu(conv)
    xs, bs, cs = jnp.split(xbc_c, [D_SSM, D_SSM + SSM_GROUPS * D_STATE], axis=-1)
    dt = jax.nn.softplus(dt_raw.astype(jnp.float32) + dt_bias.astype(jnp.float32))
    a = -jnp.exp(a_log.astype(jnp.float32))
    xh = xs.reshape(bsz, L, SSM_HEADS, SSM_HEAD_DIM)
    y, new_ssm = ssd_scan(xh, dt, a,
                          bs.reshape(bsz, L, SSM_GROUPS, D_STATE),
                          cs.reshape(bsz, L, SSM_GROUPS, D_STATE), ssm_state)
    y = y + d_skip.astype(jnp.float32)[:, None] * xh.astype(jnp.float32)
    yz = y.reshape(bsz, L, D_SSM) * jax.nn.silu(z.astype(jnp.float32))
    yz = yz.reshape(bsz, L, SSM_GROUPS, D_SSM // SSM_GROUPS)
    yz = yz * lax.rsqrt(jnp.mean(yz * yz, axis=-1, keepdims=True) + EPS)
    out_b = (yz.reshape(bsz, L, D_SSM) * ssm_norm_g.astype(jnp.float32)).astype(x.dtype)

    out = jnp.concatenate([out_a, out_b], axis=-1) @ w_out
    return x + out, new_pool, new_conv, new_ssm.astype(ssm_state.dtype)


def setup_inputs(seed: int = 0) -> dict:
    key = jax.random.key(seed)
    ks = jax.random.split(key, 20)
    f32 = jnp.float32
    x_prompt = jax.random.normal(ks[0], (BATCH, SEQ, D_MODEL), f32)
    x_sample = jax.random.normal(ks[1], (DEC_BATCH, DEC_SEQ, D_MODEL), f32)
    state_pool = jax.random.normal(ks[2], (DEPTH, DEC_BATCH, POOL_HIST, D_POOL), f32)
    state_conv = jax.random.normal(ks[3], (DEPTH, DEC_BATCH, CONV_WIDTH - 1, D_CONV), f32)
    state_ssm = 0.5 * jax.random.normal(ks[4], (DEPTH, DEC_BATCH, SSM_HEADS, SSM_HEAD_DIM, D_STATE), f32)
    norm_g = 1.0 + 0.02 * jax.random.normal(ks[5], (DEPTH, D_MODEL), f32)
    w_in = jax.random.normal(ks[6], (DEPTH, D_MODEL, D_IN_PROJ), f32) * D_MODEL ** -0.5
    conv_w = jax.random.normal(ks[7], (DEPTH, CONV_WIDTH, D_CONV), f32) * CONV_WIDTH ** -0.5
    conv_b = 0.01 * jax.random.normal(ks[8], (DEPTH, D_CONV), f32)
    dt0 = jnp.exp(jax.random.uniform(ks[9], (DEPTH, SSM_HEADS), f32)
                  * (jnp.log(0.1) - jnp.log(0.001)) + jnp.log(0.001))
    dt_bias = dt0 + jnp.log(-jnp.expm1(-dt0))
    a_log = jnp.log(jax.random.uniform(ks[10], (DEPTH, SSM_HEADS), f32, 1.0, 16.0))
    d_skip = 1.0 + 0.02 * jax.random.normal(ks[11], (DEPTH, SSM_HEADS), f32)
    ssm_norm_g = 1.0 + 0.02 * jax.random.normal(ks[12], (DEPTH, D_SSM), f32)
    pool_w = jax.random.normal(ks[13], (DEPTH, len(POOL_WINDOWS), POOL_GROUP, POOL_GROUP), f32) * POOL_GROUP ** -0.5
    pool_scale = 1.0 + 0.02 * jax.random.normal(ks[14], (DEPTH, D_POOL), f32)
    w_out = jax.random.normal(ks[15], (DEPTH, D_MIX, D_MODEL), f32) * D_MIX ** -0.5
    final_g = 1.0 + 0.02 * jax.random.normal(ks[16], (D_MODEL,), f32)
    return {"x_prompt": x_prompt, "x_sample": x_sample, "state_pool": state_pool,
            "state_conv": state_conv, "state_ssm": state_ssm, "norm_g": norm_g, "w_in": w_in,
            "conv_w": conv_w, "conv_b": conv_b, "dt_bias": dt_bias, "a_log": a_log,
            "d_skip": d_skip, "ssm_norm_g": ssm_norm_g, "pool_w": pool_w,
            "pool_scale": pool_scale, "w_out": w_out, "final_g": final_g}


def reference(x_prompt, x_sample, state_pool, state_conv, state_ssm, norm_g, w_in, conv_w, conv_b,
              dt_bias, a_log, d_skip, ssm_norm_g, pool_w, pool_scale, w_out, final_g):
    bp = x_prompt.shape[0]
    hp = x_prompt
    hs = x_sample
    pool_p, conv_p, ssm_p = [], [], []
    pool_s, conv_s, ssm_s = [], [], []
    for layer in range(DEPTH):
        params = (norm_g[layer], w_in[layer], conv_w[layer], conv_b[layer], dt_bias[layer],
                  a_log[layer], d_skip[layer], ssm_norm_g[layer], pool_w[layer],
                  pool_scale[layer], w_out[layer])
        zp_pool = jnp.zeros((bp, POOL_HIST, D_POOL), state_pool.dtype)
        zp_conv = jnp.zeros((bp, CONV_WIDTH - 1, D_CONV), state_conv.dtype)
        zp_ssm = jnp.zeros((bp, SSM_HEADS, SSM_HEAD_DIM, D_STATE), state_ssm.dtype)
        hp, npool, nconv, nssm = hybrid_layer(hp, zp_pool, zp_conv, zp_ssm, 0, *params)
        pool_p.append(npool)
        conv_p.append(nconv)
        ssm_p.append(nssm)
        hs, npool, nconv, nssm = hybrid_layer(hs, state_pool[layer], state_conv[layer],
                                              state_ssm[layer], PAST_LEN, *params)
        pool_s.append(npool)
        conv_s.append(nconv)
        ssm_s.append(nssm)
    y_prompt = rmsnorm(hp, final_g)
    y_sample = rmsnorm(hs, final_g)
    return (y_prompt, y_sample,
            jnp.stack(pool_p, 0), jnp.stack(conv_p, 0), jnp.stack(ssm_p, 0),
            jnp.stack(pool_s, 0), jnp.stack(conv_s, 0), jnp.stack(ssm_s, 0))
```

```python
import functools

import jax
import jax.numpy as jnp
from jax import lax
from jax.experimental import pallas as pl
from jax.experimental.pallas import tpu as pltpu

F32 = jnp.float32
BF16 = jnp.bfloat16

D_MODEL = 2048
D_POOL = 1024
POOL_WINDOWS = (2, 4, 8, 16)
POOL_GROUP = 256
POOL_HIST = 15
D_SSM = 3072
HEAD_DIM = 64
HEADS = 48
GROUPS = 8
HEADS_PER_GROUP = 6
D_STATE = 128
D_BC = GROUPS * D_STATE
CONV_WIDTH = 4
D_CONV = D_SSM + 2 * D_BC
D_MAIN = 2 * D_POOL + D_SSM + D_CONV
PAST_LEN = 16384
DEC_SEQ = 4
EPS = 1e-5

LANES = 128
SUBLANES = 8
VMEM_LIMIT = 56 * 1024 * 1024

OFF_Z = 0
OFF_XS = D_SSM
OFF_U = 2 * D_SSM
OFF_GATE = OFF_U + D_POOL
OFF_B = OFF_GATE + D_POOL
OFF_C = OFF_B + D_BC

CHUNK = 128
NEG_BIG = -1e30


def _silu(v):
    return v * jax.nn.sigmoid(v)


def _softplus(v):
    return jnp.maximum(v, 0.0) + jnp.log1p(jnp.exp(-jnp.abs(v)))


def _split2(v):
    hi = v.astype(BF16)
    lo = (v - hi.astype(F32)).astype(BF16)
    return hi, lo


def _dot(a, b):
    return jnp.dot(a, b, preferred_element_type=F32)


def _dot_nt(a, b):
    return lax.dot_general(a, b, (((1,), (1,)), ((), ())), preferred_element_type=F32)


def _dot2(v, onehot):
    hi, lo = _split2(v)
    return _dot(hi, onehot) + _dot(lo, onehot)


def _inproj_kernel(x_ref, g_ref, w_ref, wdt_ref, o_ref, dt_ref, h_ref):
    @pl.when(pl.program_id(1) == 0)
    def _():
        x = x_ref[...]
        ms = jnp.mean(x * x, axis=-1, keepdims=True)
        h = (x * lax.rsqrt(ms + EPS) * g_ref[...]).astype(BF16)
        h_ref[...] = h
        dt_ref[...] = _dot(h, wdt_ref[...])

    o_ref[...] = _dot(h_ref[...], w_ref[...]).astype(BF16)


def _inproj(x2d, norm_g, w_main, w_dt):
    m = x2d.shape[0]
    tm = min(1024, m)
    tn = 1024
    return pl.pallas_call(
        _inproj_kernel,
        grid=(m // tm, D_MAIN // tn),
        in_specs=[
            pl.BlockSpec((tm, D_MODEL), lambda i, j: (i, 0)),
            pl.BlockSpec((1, D_MODEL), lambda i, j: (0, 0)),
            pl.BlockSpec((D_MODEL, tn), lambda i, j: (0, j)),
            pl.BlockSpec((D_MODEL, LANES), lambda i, j: (0, 0)),
        ],
        out_specs=[
            pl.BlockSpec((tm, tn), lambda i, j: (i, j)),
            pl.BlockSpec((tm, LANES), lambda i, j: (i, 0)),
        ],
        out_shape=[
            jax.ShapeDtypeStruct((m, D_MAIN), BF16),
            jax.ShapeDtypeStruct((m, LANES), F32),
        ],
        scratch_shapes=[pltpu.VMEM((tm, D_MODEL), BF16)],
        compiler_params=pltpu.CompilerParams(
            dimension_semantics=("arbitrary", "arbitrary"),
            vmem_limit_bytes=VMEM_LIMIT),
        name="inproj",
    )(x2d, norm_g, w_main, w_dt)


def _outproj_kernel(a_ref, b_ref, wa_ref, wb_ref, x_ref, g_ref, y_ref):
    acc = _dot(a_ref[...], wa_ref[...]) + _dot(b_ref[...], wb_ref[...])
    r = x_ref[...] + acc
    ms = jnp.mean(r * r, axis=-1, keepdims=True)
    y_ref[...] = r * lax.rsqrt(ms + EPS) * g_ref[...]


def _outproj(out_a, out_b, w_a, w_b, x2d, final_g):
    m = x2d.shape[0]
    tm = min(256, m)
    return pl.pallas_call(
        _outproj_kernel,
        grid=(m // tm,),
        in_specs=[
            pl.BlockSpec((tm, D_POOL), lambda i: (i, 0)),
            pl.BlockSpec((tm, D_SSM), lambda i: (i, 0)),
            pl.BlockSpec((D_POOL, D_MODEL), lambda i: (0, 0)),
            pl.BlockSpec((D_SSM, D_MODEL), lambda i: (0, 0)),
            pl.BlockSpec((tm, D_MODEL), lambda i: (i, 0)),
            pl.BlockSpec((1, D_MODEL), lambda i: (0, 0)),
        ],
        out_specs=pl.BlockSpec((tm, D_MODEL), lambda i: (i, 0)),
        out_shape=jax.ShapeDtypeStruct((m, D_MODEL), F32),
        compiler_params=pltpu.CompilerParams(
            dimension_semantics=("arbitrary",),
            vmem_limit_bytes=VMEM_LIMIT),
        name="outproj",
    )(out_a, out_b, w_a, w_b, x2d, final_g)


def _pool_kernel(u_ref, gate_ref, pw_ref, ps_ref, oa_ref, np_ref, ubuf, *, tl, nt):
    t = pl.program_id(1)
    hist = POOL_HIST + 1

    @pl.when(t == 0)
    def _():
        ubuf[0:hist, :] = jnp.zeros((hist, D_POOL), F32)

    u = u_ref[...].astype(F32)
    ubuf[hist:hist + tl, :] = u
    pos = t * tl + lax.broadcasted_iota(jnp.int32, (tl, 1), 0)
    for gi, w in enumerate(POOL_WINDOWS):
        cs = slice(gi * POOL_GROUP, (gi + 1) * POOL_GROUP)
        ug = u[:, cs]
        acc = ug
        for k in range(1, w):
            acc = acc + ubuf[hist - k:hist - k + tl, cs]
        cnt = jnp.minimum(w, pos + 1).astype(F32)
        pooled = acc / cnt - ug
        mixed = _dot(pooled.astype(BF16), pw_ref[gi])
        gt = gate_ref[:, cs].astype(F32)
        oa_ref[:, cs] = (mixed * ps_ref[:, cs] * _silu(gt)).astype(BF16)

    ubuf[0:hist, :] = ubuf[tl:tl + hist, :]

    @pl.when(t == nt - 1)
    def _():
        np_ref[...] = ubuf[tl + 1:tl + hist, :]


def _pool_prompt(proj, pool_w, pool_scale, bsz, seq):
    tl = min(512, seq)
    nt = seq // tl
    return pl.pallas_call(
        functools.partial(_pool_kernel, tl=tl, nt=nt),
        grid=(bsz, nt),
        in_specs=[
            pl.BlockSpec((tl, D_POOL), lambda b, t: (b * nt + t, OFF_U // D_POOL)),
            pl.BlockSpec((tl, D_POOL), lambda b, t: (b * nt + t, OFF_GATE // D_POOL)),
            pl.BlockSpec((len(POOL_WINDOWS), POOL_GROUP, POOL_GROUP), lambda b, t: (0, 0, 0)),
            pl.BlockSpec((1, D_POOL), lambda b, t: (0, 0)),
        ],
        out_specs=[
            pl.BlockSpec((tl, D_POOL), lambda b, t: (b * nt + t, 0)),
            pl.BlockSpec((None, POOL_HIST, D_POOL), lambda b, t: (b, 0, 0)),
        ],
        out_shape=[
            jax.ShapeDtypeStruct((bsz * seq, D_POOL), BF16),
            jax.ShapeDtypeStruct((bsz, POOL_HIST, D_POOL), F32),
        ],
        scratch_shapes=[pltpu.VMEM((tl + POOL_HIST + 1, D_POOL), F32)],
        compiler_params=pltpu.CompilerParams(
            dimension_semantics=("arbitrary", "arbitrary"),
            vmem_limit_bytes=VMEM_LIMIT),
        name="pool_prompt",
    )(proj, proj, pool_w, pool_scale)


def _ssd_kernel(z_ref, xs_ref, b_ref, c_ref, dtr_ref, cw_ref, cbias_ref, dtb_ref, alog_ref,
                dskip_ref, ng_ref,
                ob_ref, nconv_ref, nssm_ref,
                cbuf, st_ref, xf_ref, xb_ref, bf_ref, cf_ref, y_ref, *, nc):
    q = CHUNK
    c_idx = pl.program_id(1)
    halo = SUBLANES

    @pl.when(c_idx == 0)
    def _():
        cbuf[0:halo, :] = jnp.zeros((halo, D_CONV), F32)
        st_ref[...] = jnp.zeros(st_ref.shape, F32)

    cbuf[halo:halo + q, 0:D_SSM] = xs_ref[...].astype(F32)
    cbuf[halo:halo + q, D_SSM:D_SSM + D_BC] = b_ref[...].astype(F32)
    cbuf[halo:halo + q, D_SSM + D_BC:D_CONV] = c_ref[...].astype(F32)

    for blk in range(D_CONV // LANES):
        ls = slice(blk * LANES, (blk + 1) * LANES)
        acc = cbias_ref[:, ls] + cw_ref[0:1, ls] * cbuf[halo - 3:halo - 3 + q, ls]
        acc = acc + cw_ref[1:2, ls] * cbuf[halo - 2:halo - 2 + q, ls]
        acc = acc + cw_ref[2:3, ls] * cbuf[halo - 1:halo - 1 + q, ls]
        acc = acc + cw_ref[3:4, ls] * cbuf[halo:halo + q, ls]
        v = _silu(acc)
        if blk < D_SSM // LANES:
            xf_ref[:, ls] = v
            xb_ref[:, ls] = v.astype(BF16)
        elif blk < (D_SSM + D_BC) // LANES:
            off = blk * LANES - D_SSM
            bf_ref[:, off:off + LANES] = v
        else:
            off = blk * LANES - D_SSM - D_BC
            cf_ref[:, off:off + LANES] = v

    dt = _softplus(dtr_ref[...] + dtb_ref[...])
    a_neg = -jnp.exp(alog_ref[...])
    da = dt * a_neg
    row = lax.broadcasted_iota(jnp.int32, (q, LANES), 0)
    a_cum = da
    shift = 1
    while shift < q:
        a_cum = a_cum + jnp.where(row >= shift, pltpu.roll(a_cum, shift, 0), 0.0)
        shift *= 2
    a_cum_t = jnp.transpose(a_cum)
    dt_t = jnp.transpose(dt)
    a_end_t = a_cum_t[:, q - 1:q]
    w_t = jnp.exp(a_end_t - a_cum_t) * dt_t
    cdec_t = jnp.exp(a_end_t)

    li = lax.broadcasted_iota(jnp.int32, (q, q), 0)
    si = lax.broadcasted_iota(jnp.int32, (q, q), 1)
    tri = li >= si
    lane = lax.broadcasted_iota(jnp.int32, (q, LANES), 1)
    lo_half = lane < HEAD_DIM

    for g in range(GROUPS):
        gs = slice(g * D_STATE, (g + 1) * D_STATE)
        c_g = cf_ref[:, gs]
        b_g = bf_ref[:, gs]
        cb = _dot_nt(c_g.astype(BF16), b_g.astype(BF16))
        b_t = jnp.transpose(b_g)
        for j in range(HEADS_PER_GROUP // 2):
            blk = g * (HEADS_PER_GROUP // 2) + j
            ls = slice(blk * LANES, (blk + 1) * LANES)
            sc, ec, bw, dec = [], [], [], []
            for h in (2 * blk, 2 * blk + 1):
                a_col = jnp.broadcast_to(a_cum[:, h:h + 1], (q, q))
                a_row = a_cum_t[h:h + 1, :]
                decay = jnp.exp(jnp.where(tri, a_col - a_row, NEG_BIG))
                sc.append((cb * decay * dt_t[h:h + 1, :]).astype(BF16))
                ec.append((jnp.exp(a_col) * c_g).astype(BF16))
                bw.append((b_t * w_t[h:h + 1, :]).astype(BF16))
                dec.append(jnp.broadcast_to(cdec_t[h:h + 1, :], (q, LANES)))
            x_p = xb_ref[:, ls]
            zero_b = jnp.zeros_like(x_p)
            x_bd = jnp.concatenate([jnp.where(lo_half, x_p, zero_b),
                                    jnp.where(lo_half, zero_b, x_p)], axis=0)
            lhs1 = jnp.concatenate([jnp.concatenate(sc, axis=1),
                                    jnp.concatenate(bw, axis=1)], axis=0)
            r1 = _dot(lhs1, x_bd)
            st_p = st_ref[:, ls]
            st_b = st_p.astype(BF16)
            s_bd = jnp.concatenate([jnp.where(lo_half, st_b, zero_b),
                                    jnp.where(lo_half, zero_b, st_b)], axis=0)
            r2 = _dot(jnp.concatenate(ec, axis=1), s_bd)
            y_ref[:, ls] = r1[0:q] + r2
            st_ref[:, ls] = st_p * jnp.where(lo_half, dec[0], dec[1]) + r1[q:2 * q]

    gw = D_SSM // GROUPS
    for g in range(GROUPS):
        gs = slice(g * gw, (g + 1) * gw)
        y = y_ref[:, gs] + xf_ref[:, gs] * dskip_ref[:, gs]
        yz = y * _silu(z_ref[:, gs].astype(F32))
        ms = jnp.sum(yz * yz, axis=-1, keepdims=True) * (1.0 / gw)
        ob_ref[:, gs] = (yz * lax.rsqrt(ms + EPS) * ng_ref[:, gs]).astype(BF16)

    cbuf[0:halo, :] = cbuf[q:q + halo, :]

    @pl.when(c_idx == nc - 1)
    def _():
        nconv_ref[...] = cbuf[q + halo - 3:q + halo, :]
        for blk in range(D_SSM // LANES):
            t = jnp.transpose(st_ref[:, blk * LANES:(blk + 1) * LANES])
            nssm_ref[2 * blk:2 * blk + 2] = t.reshape(2, HEAD_DIM, D_STATE)


def _ssd_prompt(proj, dt_raw, conv_w, conv_b, dt_bias_p, a_log_p, dskip_row, norm_g, bsz, seq):
    q = CHUNK
    nc = seq // q
    row = lambda b, c: b * nc + c
    const = lambda b, c: (0, 0)
    return pl.pallas_call(
        functools.partial(_ssd_kernel, nc=nc),
        grid=(bsz, nc),
        in_specs=[
            pl.BlockSpec((q, D_SSM), lambda b, c: (row(b, c), OFF_Z // D_SSM)),
            pl.BlockSpec((q, D_SSM), lambda b, c: (row(b, c), OFF_XS // D_SSM)),
            pl.BlockSpec((q, D_BC), lambda b, c: (row(b, c), OFF_B // D_BC)),
            pl.BlockSpec((q, D_BC), lambda b, c: (row(b, c), OFF_C // D_BC)),
            pl.BlockSpec((q, LANES), lambda b, c: (row(b, c), 0)),
            pl.BlockSpec((CONV_WIDTH, D_CONV), const),
            pl.BlockSpec((1, D_CONV), const),
            pl.BlockSpec((1, LANES), const),
            pl.BlockSpec((1, LANES), const),
            pl.BlockSpec((1, D_SSM), const),
            pl.BlockSpec((1, D_SSM), const),
        ],
        out_specs=[
            pl.BlockSpec((q, D_SSM), lambda b, c: (row(b, c), 0)),
            pl.BlockSpec((None, CONV_WIDTH - 1, D_CONV), lambda b, c: (b, 0, 0)),
            pl.BlockSpec((None, HEADS, HEAD_DIM, D_STATE), lambda b, c: (b, 0, 0, 0)),
        ],
        out_shape=[
            jax.ShapeDtypeStruct((bsz * seq, D_SSM), BF16),
            jax.ShapeDtypeStruct((bsz, CONV_WIDTH - 1, D_CONV), F32),
            jax.ShapeDtypeStruct((bsz, HEADS, HEAD_DIM, D_STATE), F32),
        ],
        scratch_shapes=[
            pltpu.VMEM((q + SUBLANES, D_CONV), F32),
            pltpu.VMEM((D_STATE, D_SSM), F32),
            pltpu.VMEM((q, D_SSM), F32),
            pltpu.VMEM((q, D_SSM), BF16),
            pltpu.VMEM((q, D_BC), F32),
            pltpu.VMEM((q, D_BC), F32),
            pltpu.VMEM((q, D_SSM), F32),
        ],
        compiler_params=pltpu.CompilerParams(
            dimension_semantics=("arbitrary", "arbitrary"),
            vmem_limit_bytes=VMEM_LIMIT),
        name="ssd_prompt",
    )(proj, proj, proj, proj, dt_raw, conv_w, conv_b, dt_bias_p, a_log_p, dskip_row, norm_g)


def _sample1_kernel(pj_ref, dtr_ref, sp_ref, scv_ref, pw_ref, ps_ref, cw_ref, cbias_ref, dtb_ref,
                    alog_ref, dskip_ref, ehot_ref, ghot_ref,
                    oa_ref, npool_ref, nconv_ref, cs_ref, bs_ref, xw_ref, ydx_ref, ecum_ref,
                    cdec_ref):
    nl = DEC_SEQ

    def tok(l, off, width):
        return pj_ref[:, l * D_MAIN + off:l * D_MAIN + off + width].astype(F32)

    for gi, w in enumerate(POOL_WINDOWS):
        c0 = gi * POOL_GROUP
        ext = [sp_ref[:, j * D_POOL + c0:j * D_POOL + c0 + POOL_GROUP] for j in range(POOL_HIST)]
        ext += [tok(l, OFF_U + c0, POOL_GROUP) for l in range(nl)]
        for l in range(nl):
            acc = ext[POOL_HIST + l]
            for k in range(1, w):
                acc = acc + ext[POOL_HIST + l - k]
            cnt = float(min(w, PAST_LEN + l + 1))
            pooled = acc / cnt - ext[POOL_HIST + l]
            mixed = _dot(pooled.astype(BF16), pw_ref[gi])
            gt = tok(l, OFF_GATE + c0, POOL_GROUP)
            oa_ref[:, l * D_POOL + c0:l * D_POOL + c0 + POOL_GROUP] = (
                mixed * ps_ref[:, c0:c0 + POOL_GROUP] * _silu(gt)).astype(BF16)
    for j in range(POOL_HIST):
        src = j + nl
        if src < POOL_HIST:
            npool_ref[:, j * D_POOL:(j + 1) * D_POOL] = sp_ref[:, src * D_POOL:(src + 1) * D_POOL]
        else:
            npool_ref[:, j * D_POOL:(j + 1) * D_POOL] = tok(src - POOL_HIST, OFF_U, D_POOL)

    hist = CONV_WIDTH - 1
    conv_out = {}
    for name, poff, coff, width in (("x", OFF_XS, 0, D_SSM), ("b", OFF_B, D_SSM, D_BC),
                                    ("c", OFF_C, D_SSM + D_BC, D_BC)):
        ext = [scv_ref[:, j * D_CONV + coff:j * D_CONV + coff + width] for j in range(hist)]
        ext += [tok(l, poff, width) for l in range(nl)]
        outs = []
        for l in range(nl):
            acc = cbias_ref[:, coff:coff + width]
            for k in range(CONV_WIDTH):
                acc = acc + cw_ref[k:k + 1, coff:coff + width] * ext[l + k]
            outs.append(_silu(acc))
        conv_out[name] = outs
        for j in range(hist):
            nconv_ref[:, j * D_CONV + coff:j * D_CONV + coff + width] = ext[j + nl]
    xs, bs, cs = conv_out["x"], conv_out["b"], conv_out["c"]
    for l in range(nl):
        cs_ref[:, l * D_BC:(l + 1) * D_BC] = cs[l]
        bs_ref[:, l * D_BC:(l + 1) * D_BC] = bs[l].astype(BF16)

    a_neg = -jnp.exp(alog_ref[...])
    dt, a_cum = [], []
    run = None
    for l in range(nl):
        d = _softplus(dtr_ref[:, l * LANES:(l + 1) * LANES] + dtb_ref[...])
        dt.append(d)
        run = d * a_neg if run is None else run + d * a_neg
        a_cum.append(run)
    cdec_ref[...] = jnp.exp(a_cum[nl - 1])

    ehot = ehot_ref[...]
    ghot = ghot_ref[...]
    for l in range(nl):
        ydiag = None
        for s in range(l + 1):
            cbh = _dot2(cs[l] * bs[s], ghot)
            gls = cbh * jnp.exp(a_cum[l] - a_cum[s]) * dt[s]
            term = _dot2(gls, ehot) * xs[s]
            ydiag = term if ydiag is None else ydiag + term
        ydx_ref[:, l * D_SSM:(l + 1) * D_SSM] = ydiag + dskip_ref[...] * xs[l]
        w_end = jnp.exp(a_cum[nl - 1] - a_cum[l]) * dt[l]
        xw_ref[:, l * D_SSM:(l + 1) * D_SSM] = _dot2(w_end, ehot) * xs[l]
        ecum_ref[:, l * D_SSM:(l + 1) * D_SSM] = _dot2(jnp.exp(a_cum[l]), ehot)


def _sample1(pj2, dtr2, sp2, scv2, pool_w, pool_scale, conv_w, conv_b, dt_bias_p, a_log_p,
             dskip_row, ehot, ghot):
    db = pj2.shape[0]
    bb = min(32, db)
    nl = DEC_SEQ
    rowblk = lambda width: pl.BlockSpec((bb, width), lambda i: (i, 0))
    const2 = lambda shape: pl.BlockSpec(shape, lambda i: (0, 0))
    return pl.pallas_call(
        _sample1_kernel,
        grid=(db // bb,),
        in_specs=[
            rowblk(nl * D_MAIN), rowblk(nl * LANES), rowblk(POOL_HIST * D_POOL),
            rowblk((CONV_WIDTH - 1) * D_CONV),
            pl.BlockSpec((len(POOL_WINDOWS), POOL_GROUP, POOL_GROUP), lambda i: (0, 0, 0)),
            const2((1, D_POOL)), const2((CONV_WIDTH, D_CONV)), const2((1, D_CONV)),
            const2((1, LANES)), const2((1, LANES)), const2((1, D_SSM)),
            const2((LANES, D_SSM)), const2((D_BC, LANES)),
        ],
        out_specs=[
            rowblk(nl * D_POOL), rowblk(POOL_HIST * D_POOL), rowblk((CONV_WIDTH - 1) * D_CONV),
            rowblk(nl * D_BC), rowblk(nl * D_BC), rowblk(nl * D_SSM), rowblk(nl * D_SSM),
            rowblk(nl * D_SSM), rowblk(LANES),
        ],
        out_shape=[
            jax.ShapeDtypeStruct((db, nl * D_POOL), BF16),
            jax.ShapeDtypeStruct((db, POOL_HIST * D_POOL), F32),
            jax.ShapeDtypeStruct((db, (CONV_WIDTH - 1) * D_CONV), F32),
            jax.ShapeDtypeStruct((db, nl * D_BC), F32),
            jax.ShapeDtypeStruct((db, nl * D_BC), BF16),
            jax.ShapeDtypeStruct((db, nl * D_SSM), F32),
            jax.ShapeDtypeStruct((db, nl * D_SSM), F32),
            jax.ShapeDtypeStruct((db, nl * D_SSM), F32),
            jax.ShapeDtypeStruct((db, LANES), F32),
        ],
        compiler_params=pltpu.CompilerParams(
            dimension_semantics=("arbitrary",),
            vmem_limit_bytes=VMEM_LIMIT),
        name="sample_elementwise",
    )(pj2, dtr2, sp2, scv2, pool_w, pool_scale, conv_w, conv_b, dt_bias_p, a_log_p, dskip_row,
      ehot, ghot)


def _sample2_kernel(cdec_ref, st_ref, xw_ref, bs_ref, cs_ref, nst_ref, yo_ref, xwt_ref, *, bb):
    i = pl.program_id(0)
    j = pl.program_id(1)
    nl = DEC_SEQ
    rows = bb * nl

    @pl.when(j == 0)
    def _():
        for blk in range(D_SSM // LANES):
            xwt_ref[blk * LANES:(blk + 1) * LANES, :] = jnp.transpose(
                xw_ref[:, blk * LANES:(blk + 1) * LANES]).astype(BF16)

    seq_of_row = lax.shift_right_logical(lax.broadcasted_iota(jnp.int32, (rows, 1), 0), 2)
    r8 = lax.broadcasted_iota(jnp.int32, (2 * nl, 1), 0)
    c8 = cs_ref[pl.ds(pl.multiple_of(j * 2 * nl, 2 * nl), 2 * nl), :].astype(BF16)
    gw = HEADS_PER_GROUP * HEAD_DIM
    for g in range(GROUPS):
        gs = slice(g * D_STATE, (g + 1) * D_STATE)
        b_blk = bs_ref[:, gs]
        zero_b = jnp.zeros_like(b_blk)
        w2 = jnp.concatenate([jnp.where(seq_of_row == 2 * j, b_blk, zero_b),
                              jnp.where(seq_of_row == 2 * j + 1, b_blk, zero_b)], axis=1)
        u2 = _dot(xwt_ref[g * gw:(g + 1) * gw, :], w2)
        yo = []
        for bi in range(2):
            s0 = st_ref[bi, g * HEADS_PER_GROUP:(g + 1) * HEADS_PER_GROUP].reshape(gw, D_STATE)
            yo.append(_dot_nt(c8[:, gs], s0.astype(BF16)))
            base = (i * bb + 2 * j + bi) * HEADS + g * HEADS_PER_GROUP
            for r in range(HEADS_PER_GROUP):
                dec = cdec_ref[base + r]
                rs = slice(r * HEAD_DIM, (r + 1) * HEAD_DIM)
                nst_ref[bi, g * HEADS_PER_GROUP + r] = (
                    s0[rs] * dec + u2[rs, bi * D_STATE:(bi + 1) * D_STATE])
        yo_ref[:, g * gw:(g + 1) * gw] = jnp.where(r8 < nl, yo[0], yo[1])


def _sample2(cdec_flat, state, xw2d, bs2d, cs2d):
    db = state.shape[0]
    bb = min(32, db)
    nl = DEC_SEQ
    nj = bb // 2
    return pl.pallas_call(
        functools.partial(_sample2_kernel, bb=bb),
        grid=(db // bb, nj),
        in_specs=[
            pl.BlockSpec(memory_space=pltpu.SMEM),
            pl.BlockSpec((2, HEADS, HEAD_DIM, D_STATE), lambda i, j: (i * nj + j, 0, 0, 0)),
            pl.BlockSpec((bb * nl, D_SSM), lambda i, j: (i, 0)),
            pl.BlockSpec((bb * nl, D_BC), lambda i, j: (i, 0)),
            pl.BlockSpec((bb * nl, D_BC), lambda i, j: (i, 0)),
        ],
        out_specs=[
            pl.BlockSpec((2, HEADS, HEAD_DIM, D_STATE), lambda i, j: (i * nj + j, 0, 0, 0)),
            pl.BlockSpec((2 * nl, D_SSM), lambda i, j: (i * nj + j, 0)),
        ],
        out_shape=[
            jax.ShapeDtypeStruct(state.shape, F32),
            jax.ShapeDtypeStruct((db * nl, D_SSM), F32),
        ],
        scratch_shapes=[pltpu.VMEM((D_SSM, bb * nl), BF16)],
        compiler_params=pltpu.CompilerParams(
            dimension_semantics=("arbitrary", "arbitrary"),
            vmem_limit_bytes=VMEM_LIMIT),
        name="sample_state",
    )(cdec_flat, state, xw2d, bs2d, cs2d)


def _sample3_kernel(yo_ref, ecum_ref, ydx_ref, pj_ref, ng_ref, ob_ref):
    gw = D_SSM // GROUPS
    for l in range(DEC_SEQ):
        for g in range(GROUPS):
            cs = slice(l * D_SSM + g * gw, l * D_SSM + (g + 1) * gw)
            y = ydx_ref[:, cs] + ecum_ref[:, cs] * yo_ref[:, cs]
            zoff = l * D_MAIN + OFF_Z + g * gw
            yz = y * _silu(pj_ref[:, zoff:zoff + gw].astype(F32))
            ms = jnp.sum(yz * yz, axis=-1, keepdims=True) * (1.0 / gw)
            ob_ref[:, cs] = (yz * lax.rsqrt(ms + EPS) * ng_ref[:, g * gw:(g + 1) * gw]).astype(BF16)


def _sample3(yo2, ecum2, ydx2, pj2, norm_g):
    db = pj2.shape[0]
    bb = min(32, db)
    nl = DEC_SEQ
    rowblk = lambda width: pl.BlockSpec((bb, width), lambda i: (i, 0))
    return pl.pallas_call(
        _sample3_kernel,
        grid=(db // bb,),
        in_specs=[rowblk(nl * D_SSM), rowblk(nl * D_SSM), rowblk(nl * D_SSM), rowblk(nl * D_MAIN),
                  pl.BlockSpec((1, D_SSM), lambda i: (0, 0))],
        out_specs=rowblk(nl * D_SSM),
        out_shape=jax.ShapeDtypeStruct((db, nl * D_SSM), BF16),
        compiler_params=pltpu.CompilerParams(
            dimension_semantics=("arbitrary",),
            vmem_limit_bytes=VMEM_LIMIT),
        name="sample_gate_norm",
    )(yo2, ecum2, ydx2, pj2, norm_g)


def kernel(x_prompt, x_sample, state_pool, state_conv, state_ssm, norm_g, w_in, conv_w, conv_b,
           dt_bias, a_log, d_skip, ssm_norm_g, pool_w, pool_scale, w_out, final_g):
    bsz, seq, _ = x_prompt.shape
    db, nl, _ = x_sample.shape
    assert nl == DEC_SEQ and seq % CHUNK == 0 and w_in.shape[0] == 1

    w = w_in[0]
    o_u, o_gate, o_z, o_x = 0, D_POOL, 2 * D_POOL, 2 * D_POOL + D_SSM
    o_b, o_c, o_dt = o_x + D_SSM, o_x + D_SSM + D_BC, D_MAIN
    w_main = jnp.concatenate(
        [w[:, o_z:o_x], w[:, o_x:o_b], w[:, o_u:o_gate], w[:, o_gate:o_z], w[:, o_b:o_c],
         w[:, o_c:o_dt]], axis=1).astype(BF16)
    w_dt = jnp.pad(w[:, o_dt:], ((0, 0), (0, LANES - HEADS))).astype(BF16)
    w_a = w_out[0, :D_POOL].astype(BF16)
    w_b = w_out[0, D_POOL:].astype(BF16)
    pool_w_b = pool_w[0].astype(BF16)
    g_in = norm_g[0][None, :]
    g_fin = final_g[None, :]
    ps = pool_scale[0][None, :]
    cw = conv_w[0]
    cbias = conv_b[0][None, :]
    pad_h = lambda v: jnp.pad(v, (0, LANES - HEADS))[None, :]
    dtb = pad_h(dt_bias[0])
    alog = pad_h(a_log[0])
    dskip_row = jnp.repeat(d_skip[0], HEAD_DIM)[None, :]
    ng = ssm_norm_g[0][None, :]

    xp2 = x_prompt.reshape(bsz * seq, D_MODEL)
    proj_p, dt_p = _inproj(xp2, g_in, w_main, w_dt)
    oa_p, npool_p = _pool_prompt(proj_p, pool_w_b, ps, bsz, seq)
    ob_p, nconv_p, nssm_p = _ssd_prompt(proj_p, dt_p, cw, cbias, dtb, alog, dskip_row, ng, bsz, seq)
    y_p = _outproj(oa_p, ob_p, w_a, w_b, xp2, g_fin).reshape(bsz, seq, D_MODEL)

    xs2 = x_sample.reshape(db * nl, D_MODEL)
    proj_s, dt_s = _inproj(xs2, g_in, w_main, w_dt)
    pj2 = proj_s.reshape(db, nl * D_MAIN)
    head_of_ch = jnp.arange(D_SSM, dtype=jnp.int32) // HEAD_DIM
    ehot = (jnp.arange(LANES, dtype=jnp.int32)[:, None] == head_of_ch[None, :]).astype(BF16)
    grp_of_row = jnp.arange(D_BC, dtype=jnp.int32) // D_STATE
    head_id = jnp.arange(LANES, dtype=jnp.int32)
    ghot = ((head_id[None, :] // HEADS_PER_GROUP == grp_of_row[:, None])
            & (head_id[None, :] < HEADS)).astype(BF16)
    (oa_s, npool_s, nconv_s, cs_s, bs_s, xw_s, ydx_s, ecum_s, cdec_s) = _sample1(
        pj2, dt_s.reshape(db, nl * LANES), state_pool[0].reshape(db, POOL_HIST * D_POOL),
        state_conv[0].reshape(db, (CONV_WIDTH - 1) * D_CONV), pool_w_b, ps, cw, cbias, dtb, alog,
        dskip_row, ehot, ghot)
    nssm_s, yo_s = _sample2(cdec_s[:, :HEADS].reshape(db * HEADS), state_ssm[0],
                            xw_s.reshape(db * nl, D_SSM), bs_s.reshape(db * nl, D_BC),
                            cs_s.reshape(db * nl, D_BC))
    ob_s = _sample3(yo_s.reshape(db, nl * D_SSM), ecum_s, ydx_s, pj2, ng)
    y_s = _outproj(oa_s.reshape(db * nl, D_POOL), ob_s.reshape(db * nl, D_SSM), w_a, w_b, xs2,
                   g_fin).reshape(db, nl, D_MODEL)

    return (y_p, y_s,
            npool_p[None], nconv_p[None], nssm_p[None],
            npool_s.reshape(1, db, POOL_HIST, D_POOL),
            nconv_s.reshape(1, db, CONV_WIDTH - 1, D_CONV),
            nssm_s[None])
```

```python
import functools

import jax
import jax.numpy as jnp
from jax import lax
from jax.experimental import pallas as pl
from jax.experimental.pallas import tpu as pltpu

F32 = jnp.float32
BF16 = jnp.bfloat16

D_MODEL = 2048
D_POOL = 1024
POOL_WINDOWS = (2, 4, 8, 16)
POOL_GROUP = 256
POOL_HIST = 15
D_SSM = 3072
HEAD_DIM = 64
HEADS = 48
GROUPS = 8
HEADS_PER_GROUP = 6
D_STATE = 128
D_BC = GROUPS * D_STATE
CONV_WIDTH = 4
D_CONV = D_SSM + 2 * D_BC
D_MAIN = 2 * D_POOL + D_SSM + D_CONV
PAST_LEN = 16384
DEC_SEQ = 4
EPS = 1e-5

LANES = 128
SUBLANES = 8
VMEM_LIMIT = 56 * 1024 * 1024

OFF_Z = 0
OFF_XS = D_SSM
OFF_U = 2 * D_SSM
OFF_GATE = OFF_U + D_POOL
OFF_B = OFF_GATE + D_POOL
OFF_C = OFF_B + D_BC

CHUNK = 128
SAMPLE_BLOCK = 32
NEG_BIG = -1e30
LOG2E = 1.4426950408889634


def _silu(v):
    h = 0.5 * v
    return h + h * jnp.tanh(h)


def _softplus(v):
    return jnp.maximum(v, 0.0) + jnp.log1p(jnp.exp(-jnp.abs(v)))


def _split2(v):
    hi = v.astype(BF16)
    lo = (v - hi.astype(F32)).astype(BF16)
    return hi, lo


def _dot(a, b):
    return jnp.dot(a, b, preferred_element_type=F32)


def _dot_nt(a, b):
    return lax.dot_general(a, b, (((1,), (1,)), ((), ())), preferred_element_type=F32)


def _dot2(v, onehot):
    hi, lo = _split2(v)
    return _dot(hi, onehot) + _dot(lo, onehot)


def _inproj_kernel(x_ref, g_ref, w_ref, wdt_ref, o_ref, dt_ref, h_ref):
    @pl.when(pl.program_id(1) == 0)
    def _():
        x = x_ref[...]
        ms = jnp.mean(x * x, axis=-1, keepdims=True)
        h = (x * lax.rsqrt(ms + EPS) * g_ref[...]).astype(BF16)
        h_ref[...] = h
        wrow = lax.broadcasted_iota(jnp.int32, wdt_ref.shape, 0)
        wdt = jnp.where(wrow < HEADS, wdt_ref[...], 0.0).astype(BF16)
        dt_ref[...] = _dot_nt(h, wdt)

    o_ref[...] = _dot_nt(h_ref[...], w_ref[...].astype(BF16)).astype(BF16)


def _src_block(j):
    nz = (D_SSM + D_SSM) // 1024
    npool = 2 * D_POOL // 1024
    return jnp.where(j < nz, j + npool, jnp.where(j < nz + npool, j - nz, j))


def _inproj(x2d, norm_g, w_t):
    m = x2d.shape[0]
    tm = min(1024, m)
    tn = 1024
    return pl.pallas_call(
        _inproj_kernel,
        grid=(m // tm, D_MAIN // tn),
        in_specs=[
            pl.BlockSpec((tm, D_MODEL), lambda i, j: (i, 0)),
            pl.BlockSpec((1, D_MODEL), lambda i, j: (0, 0)),
            pl.BlockSpec((tn, D_MODEL), lambda i, j: (_src_block(j), 0)),
            pl.BlockSpec((LANES, D_MODEL), lambda i, j: (D_MAIN // LANES, 0)),
        ],
        out_specs=[
            pl.BlockSpec((tm, tn), lambda i, j: (i, j)),
            pl.BlockSpec((tm, LANES), lambda i, j: (i, 0)),
        ],
        out_shape=[
            jax.ShapeDtypeStruct((m, D_MAIN), BF16),
            jax.ShapeDtypeStruct((m, LANES), F32),
        ],
        scratch_shapes=[pltpu.VMEM((tm, D_MODEL), BF16)],
        compiler_params=pltpu.CompilerParams(
            dimension_semantics=("arbitrary", "arbitrary"),
            vmem_limit_bytes=VMEM_LIMIT),
        name="inproj",
    )(x2d, norm_g, w_t, w_t)


def _outproj_kernel(a_ref, b_ref, wa_ref, wb_ref, x_ref, g_ref, y_ref):
    acc = _dot(a_ref[...], wa_ref[...]) + _dot(b_ref[...], wb_ref[...])
    r = x_ref[...] + acc
    ms = jnp.mean(r * r, axis=-1, keepdims=True)
    y_ref[...] = r * lax.rsqrt(ms + EPS) * g_ref[...]


def _outproj(out_a, out_b, w_a, w_b, x2d, final_g):
    m = x2d.shape[0]
    tm = min(256, m)
    return pl.pallas_call(
        _outproj_kernel,
        grid=(m // tm,),
        in_specs=[
            pl.BlockSpec((tm, D_POOL), lambda i: (i, 0)),
            pl.BlockSpec((tm, D_SSM), lambda i: (i, 0)),
            pl.BlockSpec((D_POOL, D_MODEL), lambda i: (0, 0)),
            pl.BlockSpec((D_SSM, D_MODEL), lambda i: (0, 0)),
            pl.BlockSpec((tm, D_MODEL), lambda i: (i, 0)),
            pl.BlockSpec((1, D_MODEL), lambda i: (0, 0)),
        ],
        out_specs=pl.BlockSpec((tm, D_MODEL), lambda i: (i, 0)),
        out_shape=jax.ShapeDtypeStruct((m, D_MODEL), F32),
        compiler_params=pltpu.CompilerParams(
            dimension_semantics=("arbitrary",),
            vmem_limit_bytes=VMEM_LIMIT),
        name="outproj",
    )(out_a, out_b, w_a, w_b, x2d, final_g)


def _pool_kernel(u_ref, gate_ref, pw_ref, ps_ref, oa_ref, np_ref, ubuf, *, tl, nt):
    t = pl.program_id(1)
    hist = POOL_HIST + 1

    @pl.when(t == 0)
    def _():
        ubuf[0:hist, :] = jnp.zeros((hist, D_POOL), F32)

    u = u_ref[...].astype(F32)
    ubuf[hist:hist + tl, :] = u
    pos = t * tl + lax.broadcasted_iota(jnp.int32, (tl, 1), 0)
    for gi, w in enumerate(POOL_WINDOWS):
        cs = slice(gi * POOL_GROUP, (gi + 1) * POOL_GROUP)
        ug = u[:, cs]
        acc = ubuf[0:hist + tl, cs]
        span = 1
        while span < w:
            acc = acc + pltpu.roll(acc, span, 0)
            span *= 2
        cnt = jnp.minimum(w, pos + 1).astype(F32)
        pooled = acc[hist:hist + tl] / cnt - ug
        mixed = _dot(pooled.astype(BF16), pw_ref[gi])
        gt = gate_ref[:, cs].astype(F32)
        oa_ref[:, cs] = (mixed * ps_ref[:, cs] * _silu(gt)).astype(BF16)

    ubuf[0:hist, :] = ubuf[tl:tl + hist, :]

    @pl.when(t == nt - 1)
    def _():
        np_ref[...] = ubuf[tl + 1:tl + hist, :]


def _pool_prompt(proj, pool_w, pool_scale, bsz, seq):
    tl = min(512, seq)
    nt = seq // tl
    return pl.pallas_call(
        functools.partial(_pool_kernel, tl=tl, nt=nt),
        grid=(bsz, nt),
        in_specs=[
            pl.BlockSpec((tl, D_POOL), lambda b, t: (b * nt + t, OFF_U // D_POOL)),
            pl.BlockSpec((tl, D_POOL), lambda b, t: (b * nt + t, OFF_GATE // D_POOL)),
            pl.BlockSpec((len(POOL_WINDOWS), POOL_GROUP, POOL_GROUP), lambda b, t: (0, 0, 0)),
            pl.BlockSpec((1, D_POOL), lambda b, t: (0, 0)),
        ],
        out_specs=[
            pl.BlockSpec((tl, D_POOL), lambda b, t: (b * nt + t, 0)),
            pl.BlockSpec((None, POOL_HIST, D_POOL), lambda b, t: (b, 0, 0)),
        ],
        out_shape=[
            jax.ShapeDtypeStruct((bsz * seq, D_POOL), BF16),
            jax.ShapeDtypeStruct((bsz, POOL_HIST, D_POOL), F32),
        ],
        scratch_shapes=[pltpu.VMEM((tl + POOL_HIST + 1, D_POOL), F32)],
        compiler_params=pltpu.CompilerParams(
            dimension_semantics=("arbitrary", "arbitrary"),
            vmem_limit_bytes=VMEM_LIMIT),
        name="pool_prompt",
    )(proj, proj, pool_w, pool_scale)


def _ssd_kernel(z_ref, xs_ref, b_ref, c_ref, dtr_ref, cw_ref, cbias_ref, dtb_ref, alog_ref,
                dskip_ref, ng_ref,
                ob_ref, nconv_ref, nssm_ref,
                cbuf, st_ref, cvx_ref, cvb_ref, cvc_ref, y_ref, *, nc):
    q = CHUNK
    c_idx = pl.program_id(1)
    halo = SUBLANES

    nbx, nbb = D_SSM // LANES, D_BC // LANES

    @pl.when(c_idx == 0)
    def _():
        cbuf[:, 0:halo, :] = jnp.zeros((cbuf.shape[0], halo, LANES), F32)
        st_ref[...] = jnp.zeros(st_ref.shape, F32)

    nv = (halo + q) // SUBLANES
    for blk in range(D_CONV // LANES):
        ls = slice(blk * LANES, (blk + 1) * LANES)
        if blk < nbx:
            src_ref, off, dst = xs_ref, blk * LANES, cvx_ref.at[blk]
        elif blk < nbx + nbb:
            src_ref, off, dst = b_ref, (blk - nbx) * LANES, cvb_ref.at[blk - nbx]
        else:
            src_ref, off, dst = c_ref, (blk - nbx - nbb) * LANES, cvc_ref.at[blk - nbx - nbb]
        cbuf[blk, halo:halo + q, :] = src_ref[:, off:off + LANES].astype(F32)
        xv = [cbuf[blk, pl.ds(a, SUBLANES, stride=nv), :] for a in range(nv)]
        wrap = [pltpu.roll(xv[nv - k], 1, 0) for k in range(1, CONV_WIDTH)]
        taps = [cw_ref[k:k + 1, ls] for k in range(CONV_WIDTH)]
        bias = cbias_ref[:, ls]
        for a in range(nv):
            acc = bias + taps[CONV_WIDTH - 1] * xv[a]
            for k in range(1, CONV_WIDTH):
                src = xv[a - k] if a >= k else wrap[k - a - 1]
                acc = acc + taps[CONV_WIDTH - 1 - k] * src
            dst[pl.ds(a, SUBLANES, stride=nv), :] = _silu(acc)
    rows = slice(halo, halo + q)

    dt = _softplus(dtr_ref[...] + dtb_ref[...])
    a_neg = -jnp.exp(alog_ref[...])
    da = dt * (a_neg * LOG2E)
    row = lax.broadcasted_iota(jnp.int32, (q, LANES), 0)
    a2 = da
    shift = 1
    while shift < q:
        a2 = a2 + jnp.where(row >= shift, pltpu.roll(a2, shift, 0), 0.0)
        shift *= 2
    a2_t = jnp.transpose(a2)
    ldt_t = jnp.log2(jnp.transpose(dt))
    a2_end_t = a2_t[:, q - 1:q]
    w_t = jnp.exp2(a2_end_t - a2_t + ldt_t)
    cdec_t = jnp.exp2(a2_end_t)
    srow_t = a2_t - ldt_t

    li = lax.broadcasted_iota(jnp.int32, (q, q), 0)
    si = lax.broadcasted_iota(jnp.int32, (q, q), 1)
    tri = li >= si
    lane = lax.broadcasted_iota(jnp.int32, (q, LANES), 1)
    lo_half = lane < HEAD_DIM

    for g in range(GROUPS):
        gs = slice(g * D_STATE, (g + 1) * D_STATE)
        c_g = cvc_ref[g, rows, :]
        b_g = cvb_ref[g, rows, :]
        cb = _dot_nt(c_g.astype(BF16), b_g.astype(BF16))
        b_t = jnp.transpose(b_g)
        for j in range(HEADS_PER_GROUP // 2):
            blk = g * (HEADS_PER_GROUP // 2) + j
            ls = slice(blk * LANES, (blk + 1) * LANES)
            sc, ec, bw, dec = [], [], [], []
            for h in (2 * blk, 2 * blk + 1):
                a_col = jnp.broadcast_to(a2[:, h:h + 1], (q, q))
                decay_dt = jnp.exp2(jnp.where(tri, a_col - srow_t[h:h + 1, :], NEG_BIG))
                sc.append((cb * decay_dt).astype(BF16))
                ec.append((jnp.exp2(a_col) * c_g).astype(BF16))
                bw.append((b_t * w_t[h:h + 1, :]).astype(BF16))
                dec.append(jnp.broadcast_to(cdec_t[h:h + 1, :], (q, LANES)))
            x_p = cvx_ref[blk, rows, :].astype(BF16)
            zero_b = jnp.zeros_like(x_p)
            x_bd = jnp.concatenate([jnp.where(lo_half, x_p, zero_b),
                                    jnp.where(lo_half, zero_b, x_p)], axis=0)
            lhs1 = jnp.concatenate([jnp.concatenate(sc, axis=1),
                                    jnp.concatenate(bw, axis=1)], axis=0)
            r1 = _dot(lhs1, x_bd)
            st_p = st_ref[:, ls]
            st_b = st_p.astype(BF16)
            s_bd = jnp.concatenate([jnp.where(lo_half, st_b, zero_b),
                                    jnp.where(lo_half, zero_b, st_b)], axis=0)
            r2 = _dot(jnp.concatenate(ec, axis=1), s_bd)
            y_ref[:, ls] = r1[0:q] + r2
            st_ref[:, ls] = st_p * jnp.where(lo_half, dec[0], dec[1]) + r1[q:2 * q]

    gw = D_SSM // GROUPS
    bpg = gw // LANES
    for g in range(GROUPS):
        yz, ssq = [], None
        for blk in range(g * bpg, (g + 1) * bpg):
            ls = slice(blk * LANES, (blk + 1) * LANES)
            y = y_ref[:, ls] + cvx_ref[blk, rows, :] * dskip_ref[:, ls]
            v = y * _silu(z_ref[:, ls].astype(F32))
            yz.append(v)
            ssq = v * v if ssq is None else ssq + v * v
        scale = lax.rsqrt(jnp.sum(ssq, axis=-1, keepdims=True) * (1.0 / gw) + EPS)
        for i, blk in enumerate(range(g * bpg, (g + 1) * bpg)):
            ls = slice(blk * LANES, (blk + 1) * LANES)
            ob_ref[:, ls] = (yz[i] * scale * ng_ref[:, ls]).astype(BF16)

    cbuf[:, 0:halo, :] = cbuf[:, q:q + halo, :]

    @pl.when(c_idx == nc - 1)
    def _():
        for blk in range(D_CONV // LANES):
            nconv_ref[:, blk * LANES:(blk + 1) * LANES] = cbuf[blk, q + halo - 3:q + halo, :]
        for blk in range(D_SSM // LANES):
            t = jnp.transpose(st_ref[:, blk * LANES:(blk + 1) * LANES])
            nssm_ref[2 * blk:2 * blk + 2] = t.reshape(2, HEAD_DIM, D_STATE)


def _ssd_prompt(proj, dt_raw, conv_w, conv_b, dt_bias_p, a_log_p, dskip_row, norm_g, bsz, seq):
    q = CHUNK
    nc = seq // q
    row = lambda b, c: b * nc + c
    const = lambda b, c: (0, 0)
    return pl.pallas_call(
        functools.partial(_ssd_kernel, nc=nc),
        grid=(bsz, nc),
        in_specs=[
            pl.BlockSpec((q, D_SSM), lambda b, c: (row(b, c), OFF_Z // D_SSM)),
            pl.BlockSpec((q, D_SSM), lambda b, c: (row(b, c), OFF_XS // D_SSM)),
            pl.BlockSpec((q, D_BC), lambda b, c: (row(b, c), OFF_B // D_BC)),
            pl.BlockSpec((q, D_BC), lambda b, c: (row(b, c), OFF_C // D_BC)),
            pl.BlockSpec((q, LANES), lambda b, c: (row(b, c), 0)),
            pl.BlockSpec((CONV_WIDTH, D_CONV), const),
            pl.BlockSpec((1, D_CONV), const),
            pl.BlockSpec((1, LANES), const),
            pl.BlockSpec((1, LANES), const),
            pl.BlockSpec((1, D_SSM), const),
            pl.BlockSpec((1, D_SSM), const),
        ],
        out_specs=[
            pl.BlockSpec((q, D_SSM), lambda b, c: (row(b, c), 0)),
            pl.BlockSpec((None, CONV_WIDTH - 1, D_CONV), lambda b, c: (b, 0, 0)),
            pl.BlockSpec((None, HEADS, HEAD_DIM, D_STATE), lambda b, c: (b, 0, 0, 0)),
        ],
        out_shape=[
            jax.ShapeDtypeStruct((bsz * seq, D_SSM), BF16),
            jax.ShapeDtypeStruct((bsz, CONV_WIDTH - 1, D_CONV), F32),
            jax.ShapeDtypeStruct((bsz, HEADS, HEAD_DIM, D_STATE), F32),
        ],
        scratch_shapes=[
            pltpu.VMEM((D_CONV // LANES, q + SUBLANES, LANES), F32),
            pltpu.VMEM((D_STATE, D_SSM), F32),
            pltpu.VMEM((D_SSM // LANES, q + SUBLANES, LANES), F32),
            pltpu.VMEM((D_BC // LANES, q + SUBLANES, LANES), F32),
            pltpu.VMEM((D_BC // LANES, q + SUBLANES, LANES), F32),
            pltpu.VMEM((q, D_SSM), F32),
        ],
        compiler_params=pltpu.CompilerParams(
            dimension_semantics=("arbitrary", "arbitrary"),
            vmem_limit_bytes=VMEM_LIMIT),
        name="ssd_prompt",
    )(proj, proj, proj, proj, dt_raw, conv_w, conv_b, dt_bias_p, a_log_p, dskip_row, norm_g)


def _sample1_kernel(pj_ref, dtr_ref, sp_ref, scv_ref, pw_ref, ps_ref, cw_ref, cbias_ref, dtb_ref,
                    alog_ref, dskip_ref, ehot_ref, ghot_ref,
                    oa_ref, npool_ref, nconv_ref, cs_ref, bs_ref, xw_ref, ydx_ref, ecum_ref,
                    cdec_ref, *, sb):
    nl = DEC_SEQ

    def rows(l):
        return slice(l * sb, (l + 1) * sb)

    def tok(l, off, width):
        return pj_ref[rows(l), off:off + width].astype(F32)

    for gi, w in enumerate(POOL_WINDOWS):
        c0 = gi * POOL_GROUP
        ext = [sp_ref[j, :, c0:c0 + POOL_GROUP] for j in range(POOL_HIST)]
        ext += [tok(l, OFF_U + c0, POOL_GROUP) for l in range(nl)]
        for l in range(nl):
            acc = ext[POOL_HIST + l]
            for k in range(1, w):
                acc = acc + ext[POOL_HIST + l - k]
            cnt = float(min(w, PAST_LEN + l + 1))
            pooled = acc / cnt - ext[POOL_HIST + l]
            mixed = _dot(pooled.astype(BF16), pw_ref[gi])
            gt = tok(l, OFF_GATE + c0, POOL_GROUP)
            oa_ref[rows(l), c0:c0 + POOL_GROUP] = (
                mixed * ps_ref[:, c0:c0 + POOL_GROUP] * _silu(gt)).astype(BF16)
    for j in range(POOL_HIST):
        src = j + nl
        if src < POOL_HIST:
            npool_ref[j] = sp_ref[src]
        else:
            npool_ref[j] = tok(src - POOL_HIST, OFF_U, D_POOL)

    hist = CONV_WIDTH - 1
    conv_out = {}
    for name, poff, coff, width in (("x", OFF_XS, 0, D_SSM), ("b", OFF_B, D_SSM, D_BC),
                                    ("c", OFF_C, D_SSM + D_BC, D_BC)):
        ext = [scv_ref[j, :, coff:coff + width] for j in range(hist)]
        ext += [tok(l, poff, width) for l in range(nl)]
        outs = []
        for l in range(nl):
            acc = cbias_ref[:, coff:coff + width]
            for k in range(CONV_WIDTH):
                acc = acc + cw_ref[k:k + 1, coff:coff + width] * ext[l + k]
            outs.append(_silu(acc))
        conv_out[name] = outs
        for j in range(hist):
            nconv_ref[j, :, coff:coff + width] = ext[j + nl]
    xs, bs, cs = conv_out["x"], conv_out["b"], conv_out["c"]
    for l in range(nl):
        cs_ref[rows(l), :] = cs[l]
        bs_ref[rows(l), :] = bs[l].astype(BF16)

    a_neg = -jnp.exp(alog_ref[...])
    dt, a_cum = [], []
    run = None
    for l in range(nl):
        d = _softplus(dtr_ref[rows(l), :] + dtb_ref[...])
        dt.append(d)
        run = d * a_neg if run is None else run + d * a_neg
        a_cum.append(run)
    cdec_ref[...] = jnp.exp(a_cum[nl - 1])

    ehot = ehot_ref[...]
    ghot = ghot_ref[...]
    for l in range(nl):
        ydiag = None
        for s in range(l + 1):
            cbh = _dot2(cs[l] * bs[s], ghot)
            gls = cbh * jnp.exp(a_cum[l] - a_cum[s]) * dt[s]
            term = _dot2(gls, ehot) * xs[s]
            ydiag = term if ydiag is None else ydiag + term
        ydx_ref[rows(l), :] = ydiag + dskip_ref[...] * xs[l]
        w_end = jnp.exp(a_cum[nl - 1] - a_cum[l]) * dt[l]
        xw_ref[rows(l), :] = _dot2(w_end, ehot) * xs[l]
        ecum_ref[rows(l), :] = _dot2(jnp.exp(a_cum[l]), ehot)


def _sample1(pj3, dtr3, sp3, scv3, pool_w, pool_scale, conv_w, conv_b, dt_bias_p, a_log_p,
             dskip_row, ehot, ghot, sb):
    nblk = pj3.shape[0]
    db = nblk * sb
    rb = DEC_SEQ * sb
    blk2 = lambda width: pl.BlockSpec((None, rb, width), lambda i: (i, 0, 0))
    hist3 = lambda n, width: pl.BlockSpec((n, sb, width), lambda i: (0, i, 0))
    const2 = lambda shape: pl.BlockSpec(shape, lambda i: (0, 0))
    blk_shape = lambda width, dt: jax.ShapeDtypeStruct((nblk, rb, width), dt)
    return pl.pallas_call(
        functools.partial(_sample1_kernel, sb=sb),
        grid=(nblk,),
        in_specs=[
            blk2(D_MAIN), blk2(LANES), hist3(POOL_HIST, D_POOL), hist3(CONV_WIDTH - 1, D_CONV),
            pl.BlockSpec((len(POOL_WINDOWS), POOL_GROUP, POOL_GROUP), lambda i: (0, 0, 0)),
            const2((1, D_POOL)), const2((CONV_WIDTH, D_CONV)), const2((1, D_CONV)),
            const2((1, LANES)), const2((1, LANES)), const2((1, D_SSM)),
            const2((LANES, D_SSM)), const2((D_BC, LANES)),
        ],
        out_specs=[
            blk2(D_POOL), hist3(POOL_HIST, D_POOL), hist3(CONV_WIDTH - 1, D_CONV),
            blk2(D_BC), blk2(D_BC), blk2(D_SSM), blk2(D_SSM), blk2(D_SSM),
            pl.BlockSpec((sb, LANES), lambda i: (i, 0)),
        ],
        out_shape=[
            blk_shape(D_POOL, BF16),
            jax.ShapeDtypeStruct((POOL_HIST, db, D_POOL), F32),
            jax.ShapeDtypeStruct((CONV_WIDTH - 1, db, D_CONV), F32),
            blk_shape(D_BC, F32), blk_shape(D_BC, BF16),
            blk_shape(D_SSM, F32), blk_shape(D_SSM, F32), blk_shape(D_SSM, F32),
            jax.ShapeDtypeStruct((db, LANES), F32),
        ],
        compiler_params=pltpu.CompilerParams(
            dimension_semantics=("arbitrary",),
            vmem_limit_bytes=VMEM_LIMIT),
        name="sample_elementwise",
    )(pj3, dtr3, sp3, scv3, pool_w, pool_scale, conv_w, conv_b, dt_bias_p, a_log_p, dskip_row,
      ehot, ghot)


def _sample2_kernel(cdec_ref, st_ref, xw_ref, bs_ref, cs_ref, nst_ref, yo_ref, xwt_ref, *, sb):
    i = pl.program_id(0)
    j = pl.program_id(1)
    nl = DEC_SEQ
    rows = sb * nl

    @pl.when(j == 0)
    def _():
        for blk in range(D_SSM // LANES):
            xwt_ref[blk * LANES:(blk + 1) * LANES, :] = jnp.transpose(
                xw_ref[:, blk * LANES:(blk + 1) * LANES]).astype(BF16)

    seq_of_row = lax.broadcasted_iota(jnp.int32, (rows, 1), 0) & (sb - 1)
    c8 = jnp.concatenate([cs_ref[pl.ds(l * sb + 2 * j + bi, 1), :]
                          for bi in range(2) for l in range(nl)], axis=0).astype(BF16)
    gw = HEADS_PER_GROUP * HEAD_DIM
    for g in range(GROUPS):
        gs = slice(g * D_STATE, (g + 1) * D_STATE)
        b_blk = bs_ref[:, gs]
        zero_b = jnp.zeros_like(b_blk)
        w2 = jnp.concatenate([jnp.where(seq_of_row == 2 * j, b_blk, zero_b),
                              jnp.where(seq_of_row == 2 * j + 1, b_blk, zero_b)], axis=1)
        u2 = _dot(xwt_ref[g * gw:(g + 1) * gw, :], w2)
        yo = []
        for bi in range(2):
            s0 = st_ref[bi, g * HEADS_PER_GROUP:(g + 1) * HEADS_PER_GROUP].reshape(gw, D_STATE)
            yo.append(_dot_nt(c8[:, gs], s0.astype(BF16)))
            base = (i * sb + 2 * j + bi) * HEADS + g * HEADS_PER_GROUP
            for r in range(HEADS_PER_GROUP):
                dec = cdec_ref[base + r]
                rs = slice(r * HEAD_DIM, (r + 1) * HEAD_DIM)
                nst_ref[bi, g * HEADS_PER_GROUP + r] = (
                    s0[rs] * dec + u2[rs, bi * D_STATE:(bi + 1) * D_STATE])
            for l in range(nl):
                yo_ref[pl.ds(l * sb + 2 * j + bi, 1), g * gw:(g + 1) * gw] = (
                    yo[bi][bi * nl + l:bi * nl + l + 1])


def _sample2(cdec_flat, state, xw3, bs3, cs3, sb):
    db = state.shape[0]
    nl = DEC_SEQ
    nj = sb // 2
    rb = nl * sb
    assert rb == LANES
    blk2 = lambda width: pl.BlockSpec((None, rb, width), lambda i, j: (i, 0, 0))
    return pl.pallas_call(
        functools.partial(_sample2_kernel, sb=sb),
        grid=(db // sb, nj),
        in_specs=[
            pl.BlockSpec(memory_space=pltpu.SMEM),
            pl.BlockSpec((2, HEADS, HEAD_DIM, D_STATE), lambda i, j: (i * nj + j, 0, 0, 0)),
            blk2(D_SSM), blk2(D_BC), blk2(D_BC),
        ],
        out_specs=[
            pl.BlockSpec((2, HEADS, HEAD_DIM, D_STATE), lambda i, j: (i * nj + j, 0, 0, 0)),
            blk2(D_SSM),
        ],
        out_shape=[
            jax.ShapeDtypeStruct(state.shape, F32),
            jax.ShapeDtypeStruct((db // sb, rb, D_SSM), F32),
        ],
        scratch_shapes=[pltpu.VMEM((D_SSM, rb), BF16)],
        compiler_params=pltpu.CompilerParams(
            dimension_semantics=("arbitrary", "arbitrary"),
            vmem_limit_bytes=VMEM_LIMIT),
        name="sample_state",
    )(cdec_flat, state, xw3, bs3, cs3)


def _sample3_kernel(yo_ref, ecum_ref, ydx_ref, z_ref, ng_ref, ob_ref):
    gw = D_SSM // GROUPS
    for g in range(GROUPS):
        cs = slice(g * gw, (g + 1) * gw)
        y = ydx_ref[:, cs] + ecum_ref[:, cs] * yo_ref[:, cs]
        yz = y * _silu(z_ref[:, cs].astype(F32))
        ms = jnp.sum(yz * yz, axis=-1, keepdims=True) * (1.0 / gw)
        ob_ref[:, cs] = (yz * lax.rsqrt(ms + EPS) * ng_ref[:, cs]).astype(BF16)


def _sample3(yo3, ecum3, ydx3, pj3, norm_g):
    nblk, rb, _ = pj3.shape
    blk2 = lambda width: pl.BlockSpec((None, rb, width), lambda i: (i, 0, 0))
    return pl.pallas_call(
        _sample3_kernel,
        grid=(nblk,),
        in_specs=[blk2(D_SSM), blk2(D_SSM), blk2(D_SSM),
                  pl.BlockSpec((None, rb, D_SSM), lambda i: (i, 0, OFF_Z // D_SSM)),
                  pl.BlockSpec((1, D_SSM), lambda i: (0, 0))],
        out_specs=blk2(D_SSM),
        out_shape=jax.ShapeDtypeStruct((nblk, rb, D_SSM), BF16),
        compiler_params=pltpu.CompilerParams(
            dimension_semantics=("arbitrary",),
            vmem_limit_bytes=VMEM_LIMIT),
        name="sample_gate_norm",
    )(yo3, ecum3, ydx3, pj3, norm_g)


def kernel(x_prompt, x_sample, state_pool, state_conv, state_ssm, norm_g, w_in, conv_w, conv_b,
           dt_bias, a_log, d_skip, ssm_norm_g, pool_w, pool_scale, w_out, final_g):
    bsz, seq, _ = x_prompt.shape
    db, nl, _ = x_sample.shape
    assert nl == DEC_SEQ and seq % CHUNK == 0 and w_in.shape[0] == 1

    w_t = jnp.transpose(w_in[0])
    w_a = w_out[0, :D_POOL].astype(BF16)
    w_b = w_out[0, D_POOL:].astype(BF16)
    pool_w_b = pool_w[0].astype(BF16)
    g_in = norm_g[0][None, :]
    g_fin = final_g[None, :]
    ps = pool_scale[0][None, :]
    cw = conv_w[0]
    cbias = conv_b[0][None, :]
    pad_h = lambda v: jnp.pad(v, (0, LANES - HEADS))[None, :]
    dtb = pad_h(dt_bias[0])
    alog = pad_h(a_log[0])
    dskip_row = jnp.repeat(d_skip[0], HEAD_DIM)[None, :]
    ng = ssm_norm_g[0][None, :]

    xp2 = x_prompt.reshape(bsz * seq, D_MODEL)
    proj_p, dt_p = _inproj(xp2, g_in, w_t)
    oa_p, npool_p = _pool_prompt(proj_p, pool_w_b, ps, bsz, seq)
    ob_p, nconv_p, nssm_p = _ssd_prompt(proj_p, dt_p, cw, cbias, dtb, alog, dskip_row, ng, bsz, seq)
    y_p = _outproj(oa_p, ob_p, w_a, w_b, xp2, g_fin).reshape(bsz, seq, D_MODEL)

    sb = SAMPLE_BLOCK
    nblk = db // sb
    rb = nl * sb
    xs2 = x_sample.reshape(nblk, sb, nl, D_MODEL).transpose(0, 2, 1, 3).reshape(db * nl, D_MODEL)
    proj_s, dt_s = _inproj(xs2, g_in, w_t)
    pj3 = proj_s.reshape(nblk, rb, D_MAIN)
    head_of_ch = jnp.arange(D_SSM, dtype=jnp.int32) // HEAD_DIM
    ehot = (jnp.arange(LANES, dtype=jnp.int32)[:, None] == head_of_ch[None, :]).astype(BF16)
    grp_of_row = jnp.arange(D_BC, dtype=jnp.int32) // D_STATE
    head_id = jnp.arange(LANES, dtype=jnp.int32)
    ghot = ((head_id[None, :] // HEADS_PER_GROUP == grp_of_row[:, None])
            & (head_id[None, :] < HEADS)).astype(BF16)
    (oa_s, npool_s, nconv_s, cs_s, bs_s, xw_s, ydx_s, ecum_s, cdec_s) = _sample1(
        pj3, dt_s.reshape(nblk, rb, LANES), jnp.transpose(state_pool[0], (1, 0, 2)),
        jnp.transpose(state_conv[0], (1, 0, 2)), pool_w_b, ps, cw, cbias, dtb, alog,
        dskip_row, ehot, ghot, sb)
    nssm_s, yo_s = _sample2(cdec_s[:, :HEADS].reshape(db * HEADS), state_ssm[0], xw_s, bs_s, cs_s, sb)
    ob_s = _sample3(yo_s, ecum_s, ydx_s, pj3, ng)
    y_s = _outproj(oa_s.reshape(db * nl, D_POOL), ob_s.reshape(db * nl, D_SSM), w_a, w_b, xs2,
                   g_fin)
    y_s = y_s.reshape(nblk, nl, sb, D_MODEL).transpose(0, 2, 1, 3).reshape(db, nl, D_MODEL)

    return (y_p, y_s,
            npool_p[None], nconv_p[None], nssm_p[None],
            jnp.transpose(npool_s, (1, 0, 2))[None],
            jnp.transpose(nconv_s, (1, 0, 2))[None],
            nssm_s[None])
```

```python
import functools

import jax
import jax.numpy as jnp
from jax import lax
from jax.experimental import pallas as pl
from jax.experimental.pallas import tpu as pltpu

F32 = jnp.float32
BF16 = jnp.bfloat16

D_MODEL = 2048
D_POOL = 1024
POOL_WINDOWS = (2, 4, 8, 16)
POOL_GROUP = 256
POOL_HIST = 15
D_SSM = 3072
HEAD_DIM = 64
HEADS = 48
GROUPS = 8
HEADS_PER_GROUP = 6
D_STATE = 128
D_BC = GROUPS * D_STATE
CONV_WIDTH = 4
D_CONV = D_SSM + 2 * D_BC
D_MAIN = 2 * D_POOL + D_SSM + D_CONV
PAST_LEN = 16384
DEC_SEQ = 4
EPS = 1e-5

LANES = 128
SUBLANES = 8
VMEM_LIMIT = 56 * 1024 * 1024

OFF_Z = 0
OFF_XS = D_SSM
OFF_U = 2 * D_SSM
OFF_GATE = OFF_U + D_POOL
OFF_B = OFF_GATE + D_POOL
OFF_C = OFF_B + D_BC

CHUNK = 128
SAMPLE_BLOCK = 32
SEQ_PER_STEP = 4
NEG_BIG = -1e30
LOG2E = 1.4426950408889634


def _silu(v):
    h = 0.5 * v
    return h + h * jnp.tanh(h)


def _softplus(v):
    y = jnp.exp(-jnp.abs(v))
    u = 1.0 + y
    d = u - 1.0
    l1p = jnp.where(d == 0.0, y, jnp.log(u) * (y / jnp.where(d == 0.0, 1.0, d)))
    return jnp.maximum(v, 0.0) + l1p


def _split2(v):
    hi = v.astype(BF16)
    lo = (v - hi.astype(F32)).astype(BF16)
    return hi, lo


def _dot(a, b):
    return jnp.dot(a, b, preferred_element_type=F32)


def _dot_nt(a, b):
    return lax.dot_general(a, b, (((1,), (1,)), ((), ())), preferred_element_type=F32)


def _dot2(v, onehot):
    hi, lo = _split2(v)
    return _dot(hi, onehot) + _dot(lo, onehot)


def _inproj_kernel(x_ref, g_ref, w_ref, wdt_ref, o_ref, dt_ref, h_ref):
    @pl.when(pl.program_id(1) == 0)
    def _():
        x = x_ref[...]
        ms = jnp.mean(x * x, axis=-1, keepdims=True)
        h = (x * lax.rsqrt(ms + EPS) * g_ref[...]).astype(BF16)
        h_ref[...] = h
        wrow = lax.broadcasted_iota(jnp.int32, wdt_ref.shape, 0)
        wdt = jnp.where(wrow < HEADS, wdt_ref[...], 0.0).astype(BF16)
        dt_ref[...] = _dot_nt(h, wdt)

    o_ref[...] = _dot_nt(h_ref[...], w_ref[...].astype(BF16)).astype(BF16)


def _src_block(j):
    nz = (D_SSM + D_SSM) // 1024
    npool = 2 * D_POOL // 1024
    return jnp.where(j < nz, j + npool, jnp.where(j < nz + npool, j - nz, j))


def _inproj(x2d, norm_g, w_t):
    m = x2d.shape[0]
    tm = min(1024, m)
    tn = 1024
    return pl.pallas_call(
        _inproj_kernel,
        grid=(m // tm, D_MAIN // tn),
        in_specs=[
            pl.BlockSpec((tm, D_MODEL), lambda i, j: (i, 0)),
            pl.BlockSpec((1, D_MODEL), lambda i, j: (0, 0)),
            pl.BlockSpec((tn, D_MODEL), lambda i, j: (_src_block(j), 0)),
            pl.BlockSpec((LANES, D_MODEL), lambda i, j: (D_MAIN // LANES, 0)),
        ],
        out_specs=[
            pl.BlockSpec((tm, tn), lambda i, j: (i, j)),
            pl.BlockSpec((tm, LANES), lambda i, j: (i, 0)),
        ],
        out_shape=[
            jax.ShapeDtypeStruct((m, D_MAIN), BF16),
            jax.ShapeDtypeStruct((m, LANES), F32),
        ],
        scratch_shapes=[pltpu.VMEM((tm, D_MODEL), BF16)],
        compiler_params=pltpu.CompilerParams(
            dimension_semantics=("arbitrary", "arbitrary"),
            vmem_limit_bytes=VMEM_LIMIT),
        name="inproj",
    )(x2d, norm_g, w_t, w_t)


def _outproj_kernel(a_ref, b_ref, wa_ref, wb_ref, x_ref, g_ref, y_ref):
    acc = _dot(a_ref[...], wa_ref[...]) + _dot(b_ref[...], wb_ref[...])
    r = x_ref[...] + acc
    ms = jnp.mean(r * r, axis=-1, keepdims=True)
    y_ref[...] = r * lax.rsqrt(ms + EPS) * g_ref[...]


def _outproj(out_a, out_b, w_a, w_b, x2d, final_g):
    m = x2d.shape[0]
    tm = min(512, m)
    resident = pl.Buffered(1)
    return pl.pallas_call(
        _outproj_kernel,
        grid=(m // tm,),
        in_specs=[
            pl.BlockSpec((tm, D_POOL), lambda i: (i, 0)),
            pl.BlockSpec((tm, D_SSM), lambda i: (i, 0)),
            pl.BlockSpec((D_POOL, D_MODEL), lambda i: (0, 0), pipeline_mode=resident),
            pl.BlockSpec((D_SSM, D_MODEL), lambda i: (0, 0), pipeline_mode=resident),
            pl.BlockSpec((tm, D_MODEL), lambda i: (i, 0)),
            pl.BlockSpec((1, D_MODEL), lambda i: (0, 0)),
        ],
        out_specs=pl.BlockSpec((tm, D_MODEL), lambda i: (i, 0)),
        out_shape=jax.ShapeDtypeStruct((m, D_MODEL), F32),
        compiler_params=pltpu.CompilerParams(
            dimension_semantics=("arbitrary",),
            vmem_limit_bytes=VMEM_LIMIT),
        name="outproj",
    )(out_a, out_b, w_a, w_b, x2d, final_g)


def _pool_kernel(u_ref, gate_ref, pw_ref, ps_ref, oa_ref, np_ref, ubuf, *, tl, nt):
    t = pl.program_id(1)
    hist = POOL_HIST + 1

    @pl.when(t == 0)
    def _():
        ubuf[0:hist, :] = jnp.zeros((hist, D_POOL), F32)

    u = u_ref[...].astype(F32)
    ubuf[hist:hist + tl, :] = u
    pos = t * tl + lax.broadcasted_iota(jnp.int32, (tl, 1), 0)
    for gi, w in enumerate(POOL_WINDOWS):
        cs = slice(gi * POOL_GROUP, (gi + 1) * POOL_GROUP)
        ug = u[:, cs]
        acc = ubuf[0:hist + tl, cs]
        span = 1
        while span < w:
            acc = acc + pltpu.roll(acc, span, 0)
            span *= 2
        cnt = jnp.minimum(w, pos + 1).astype(F32)
        pooled = acc[hist:hist + tl] / cnt - ug
        mixed = _dot(pooled.astype(BF16), pw_ref[gi])
        gt = gate_ref[:, cs].astype(F32)
        oa_ref[:, cs] = (mixed * ps_ref[:, cs] * _silu(gt)).astype(BF16)

    ubuf[0:hist, :] = ubuf[tl:tl + hist, :]

    @pl.when(t == nt - 1)
    def _():
        np_ref[...] = ubuf[tl + 1:tl + hist, :]


def _pool_prompt(proj, pool_w, pool_scale, bsz, seq):
    tl = min(512, seq)
    nt = seq // tl
    return pl.pallas_call(
        functools.partial(_pool_kernel, tl=tl, nt=nt),
        grid=(bsz, nt),
        in_specs=[
            pl.BlockSpec((tl, D_POOL), lambda b, t: (b * nt + t, OFF_U // D_POOL)),
            pl.BlockSpec((tl, D_POOL), lambda b, t: (b * nt + t, OFF_GATE // D_POOL)),
            pl.BlockSpec((len(POOL_WINDOWS), POOL_GROUP, POOL_GROUP), lambda b, t: (0, 0, 0)),
            pl.BlockSpec((1, D_POOL), lambda b, t: (0, 0)),
        ],
        out_specs=[
            pl.BlockSpec((tl, D_POOL), lambda b, t: (b * nt + t, 0)),
            pl.BlockSpec((None, POOL_HIST, D_POOL), lambda b, t: (b, 0, 0)),
        ],
        out_shape=[
            jax.ShapeDtypeStruct((bsz * seq, D_POOL), BF16),
            jax.ShapeDtypeStruct((bsz, POOL_HIST, D_POOL), F32),
        ],
        scratch_shapes=[pltpu.VMEM((tl + POOL_HIST + 1, D_POOL), F32)],
        compiler_params=pltpu.CompilerParams(
            dimension_semantics=("arbitrary", "arbitrary"),
            vmem_limit_bytes=VMEM_LIMIT),
        name="pool_prompt",
    )(proj, proj, pool_w, pool_scale)


def _ssd_kernel(z_ref, xs_ref, b_ref, c_ref, dtr_ref, cw_ref, cbias_ref, dtb_ref, alog_ref,
                dskip_ref, ng_ref,
                ob_ref, nconv_ref, nssm_ref,
                cbuf, st_ref, cvx_ref, cvb_ref, cvc_ref, y_ref, *, nc):
    q = CHUNK
    c_idx = pl.program_id(1)
    halo = SUBLANES

    nbx, nbb = D_SSM // LANES, D_BC // LANES

    @pl.when(c_idx == 0)
    def _():
        cbuf[:, 0:halo, :] = jnp.zeros((cbuf.shape[0], halo, LANES), F32)
        st_ref[...] = jnp.zeros(st_ref.shape, F32)

    nv = (halo + q) // SUBLANES
    for blk in range(D_CONV // LANES):
        ls = slice(blk * LANES, (blk + 1) * LANES)
        if blk < nbx:
            src_ref, off, dst = xs_ref, blk * LANES, cvx_ref.at[blk]
        elif blk < nbx + nbb:
            src_ref, off, dst = b_ref, (blk - nbx) * LANES, cvb_ref.at[blk - nbx]
        else:
            src_ref, off, dst = c_ref, (blk - nbx - nbb) * LANES, cvc_ref.at[blk - nbx - nbb]
        cbuf[blk, halo:halo + q, :] = src_ref[:, off:off + LANES].astype(F32)
        xv = [cbuf[blk, pl.ds(a, SUBLANES, stride=nv), :] for a in range(nv)]
        wrap = [pltpu.roll(xv[nv - k], 1, 0) for k in range(1, CONV_WIDTH)]
        taps = [cw_ref[k:k + 1, ls] for k in range(CONV_WIDTH)]
        bias = cbias_ref[:, ls]
        for a in range(nv):
            acc = bias + taps[CONV_WIDTH - 1] * xv[a]
            for k in range(1, CONV_WIDTH):
                src = xv[a - k] if a >= k else wrap[k - a - 1]
                acc = acc + taps[CONV_WIDTH - 1 - k] * src
            dst[pl.ds(a, SUBLANES, stride=nv), :] = _silu(acc)
    rows = slice(halo, halo + q)

    dt = _softplus(dtr_ref[...] + dtb_ref[...])
    a_neg = -jnp.exp(alog_ref[...])
    da = dt * (a_neg * LOG2E)
    row = lax.broadcasted_iota(jnp.int32, (q, LANES), 0)
    a2 = da
    shift = 1
    while shift < q:
        a2 = a2 + jnp.where(row >= shift, pltpu.roll(a2, shift, 0), 0.0)
        shift *= 2
    a2_t = jnp.transpose(a2)
    ldt_t = jnp.log2(jnp.transpose(dt))
    a2_end_t = a2_t[:, q - 1:q]
    w_t = jnp.exp2(a2_end_t - a2_t + ldt_t)
    cdec_t = jnp.exp2(a2_end_t)
    srow_t = a2_t - ldt_t
    ea = jnp.exp2(a2)

    li = lax.broadcasted_iota(jnp.int32, (q, q), 0)
    si = lax.broadcasted_iota(jnp.int32, (q, q), 1)
    tri = li >= si
    lane = lax.broadcasted_iota(jnp.int32, (q, LANES), 1)
    lo_half = lane < HEAD_DIM

    for g in range(GROUPS):
        gs = slice(g * D_STATE, (g + 1) * D_STATE)
        c_gb = cvc_ref[g, rows, :].astype(BF16)
        b_g = cvb_ref[g, rows, :]
        cb = _dot_nt(c_gb, b_g.astype(BF16))
        b_t = jnp.transpose(b_g)
        ppg = HEADS_PER_GROUP // 2
        gl = slice(g * ppg * LANES, (g + 1) * ppg * LANES)
        y_off = _dot(c_gb, st_ref[:, gl].astype(BF16))
        for j in range(ppg):
            blk = g * ppg + j
            ls = slice(blk * LANES, (blk + 1) * LANES)
            sc, bw, ecol, dec = [], [], [], []
            for h in (2 * blk, 2 * blk + 1):
                a_col = jnp.broadcast_to(a2[:, h:h + 1], (q, q))
                decay_dt = jnp.exp2(jnp.where(tri, a_col - srow_t[h:h + 1, :], NEG_BIG))
                sc.append((cb * decay_dt).astype(BF16))
                bw.append((b_t * w_t[h:h + 1, :]).astype(BF16))
                ecol.append(jnp.broadcast_to(ea[:, h:h + 1], (q, LANES)))
                dec.append(jnp.broadcast_to(cdec_t[h:h + 1, :], (q, LANES)))
            x_p = cvx_ref[blk, rows, :].astype(BF16)
            zero_b = jnp.zeros_like(x_p)
            x_bd = jnp.concatenate([jnp.where(lo_half, x_p, zero_b),
                                    jnp.where(lo_half, zero_b, x_p)], axis=0)
            lhs1 = jnp.concatenate([jnp.concatenate(sc, axis=1),
                                    jnp.concatenate(bw, axis=1)], axis=0)
            r1 = _dot(lhs1, x_bd)
            y_ref[:, ls] = (r1[0:q] + y_off[:, j * LANES:(j + 1) * LANES]
                            * jnp.where(lo_half, ecol[0], ecol[1]))
            st_ref[:, ls] = (st_ref[:, ls] * jnp.where(lo_half, dec[0], dec[1])
                             + r1[q:2 * q])

    gw = D_SSM // GROUPS
    bpg = gw // LANES
    for g in range(GROUPS):
        yz, ssq = [], None
        for blk in range(g * bpg, (g + 1) * bpg):
            ls = slice(blk * LANES, (blk + 1) * LANES)
            y = y_ref[:, ls] + cvx_ref[blk, rows, :] * dskip_ref[:, ls]
            v = y * _silu(z_ref[:, ls].astype(F32))
            yz.append(v)
            ssq = v * v if ssq is None else ssq + v * v
        scale = lax.rsqrt(jnp.sum(ssq, axis=-1, keepdims=True) * (1.0 / gw) + EPS)
        for i, blk in enumerate(range(g * bpg, (g + 1) * bpg)):
            ls = slice(blk * LANES, (blk + 1) * LANES)
            ob_ref[:, ls] = (yz[i] * scale * ng_ref[:, ls]).astype(BF16)

    cbuf[:, 0:halo, :] = cbuf[:, q:q + halo, :]

    @pl.when(c_idx == nc - 1)
    def _():
        for blk in range(D_CONV // LANES):
            nconv_ref[:, blk * LANES:(blk + 1) * LANES] = cbuf[blk, q + halo - 3:q + halo, :]
        for blk in range(D_SSM // LANES):
            t = jnp.transpose(st_ref[:, blk * LANES:(blk + 1) * LANES])
            nssm_ref[2 * blk:2 * blk + 2] = t.reshape(2, HEAD_DIM, D_STATE)


def _ssd_prompt(proj, dt_raw, conv_w, conv_b, dt_bias_p, a_log_p, dskip_row, norm_g, bsz, seq):
    q = CHUNK
    nc = seq // q
    row = lambda b, c: b * nc + c
    const = lambda b, c: (0, 0)
    return pl.pallas_call(
        functools.partial(_ssd_kernel, nc=nc),
        grid=(bsz, nc),
        in_specs=[
            pl.BlockSpec((q, D_SSM), lambda b, c: (row(b, c), OFF_Z // D_SSM)),
            pl.BlockSpec((q, D_SSM), lambda b, c: (row(b, c), OFF_XS // D_SSM)),
            pl.BlockSpec((q, D_BC), lambda b, c: (row(b, c), OFF_B // D_BC)),
            pl.BlockSpec((q, D_BC), lambda b, c: (row(b, c), OFF_C // D_BC)),
            pl.BlockSpec((q, LANES), lambda b, c: (row(b, c), 0)),
            pl.BlockSpec((CONV_WIDTH, D_CONV), const),
            pl.BlockSpec((1, D_CONV), const),
            pl.BlockSpec((1, LANES), const),
            pl.BlockSpec((1, LANES), const),
            pl.BlockSpec((1, D_SSM), const),
            pl.BlockSpec((1, D_SSM), const),
        ],
        out_specs=[
            pl.BlockSpec((q, D_SSM), lambda b, c: (row(b, c), 0)),
            pl.BlockSpec((None, CONV_WIDTH - 1, D_CONV), lambda b, c: (b, 0, 0)),
            pl.BlockSpec((None, HEADS, HEAD_DIM, D_STATE), lambda b, c: (b, 0, 0, 0)),
        ],
        out_shape=[
            jax.ShapeDtypeStruct((bsz * seq, D_SSM), BF16),
            jax.ShapeDtypeStruct((bsz, CONV_WIDTH - 1, D_CONV), F32),
            jax.ShapeDtypeStruct((bsz, HEADS, HEAD_DIM, D_STATE), F32),
        ],
        scratch_shapes=[
            pltpu.VMEM((D_CONV // LANES, q + SUBLANES, LANES), F32),
            pltpu.VMEM((D_STATE, D_SSM), F32),
            pltpu.VMEM((D_SSM // LANES, q + SUBLANES, LANES), F32),
            pltpu.VMEM((D_BC // LANES, q + SUBLANES, LANES), F32),
            pltpu.VMEM((D_BC // LANES, q + SUBLANES, LANES), F32),
            pltpu.VMEM((q, D_SSM), F32),
        ],
        compiler_params=pltpu.CompilerParams(
            dimension_semantics=("arbitrary", "arbitrary"),
            vmem_limit_bytes=VMEM_LIMIT),
        name="ssd_prompt",
    )(proj, proj, proj, proj, dt_raw, conv_w, conv_b, dt_bias_p, a_log_p, dskip_row, norm_g)


def _sample1_kernel(pj_ref, dtr_ref, sp_ref, scv_ref, pw_ref, ps_ref, cw_ref, cbias_ref, dtb_ref,
                    alog_ref, dskip_ref, ehot_ref, ghot_ref,
                    oa_ref, npool_ref, nconv_ref, cs_ref, bs_ref, xw_ref, ydx_ref, ecum_ref,
                    cdec_ref, *, sb):
    nl = DEC_SEQ

    def rows(l):
        return slice(l * sb, (l + 1) * sb)

    def tok(l, off, width):
        return pj_ref[rows(l), off:off + width].astype(F32)

    for gi, w in enumerate(POOL_WINDOWS):
        c0 = gi * POOL_GROUP
        ext = [sp_ref[j, :, c0:c0 + POOL_GROUP] for j in range(POOL_HIST)]
        ext += [tok(l, OFF_U + c0, POOL_GROUP) for l in range(nl)]
        for l in range(nl):
            acc = ext[POOL_HIST + l]
            for k in range(1, w):
                acc = acc + ext[POOL_HIST + l - k]
            cnt = float(min(w, PAST_LEN + l + 1))
            pooled = acc / cnt - ext[POOL_HIST + l]
            mixed = _dot(pooled.astype(BF16), pw_ref[gi])
            gt = tok(l, OFF_GATE + c0, POOL_GROUP)
            oa_ref[rows(l), c0:c0 + POOL_GROUP] = (
                mixed * ps_ref[:, c0:c0 + POOL_GROUP] * _silu(gt)).astype(BF16)
    for j in range(POOL_HIST):
        src = j + nl
        if src < POOL_HIST:
            npool_ref[j] = sp_ref[src]
        else:
            npool_ref[j] = tok(src - POOL_HIST, OFF_U, D_POOL)

    hist = CONV_WIDTH - 1
    conv_out = {}
    for name, poff, coff, width in (("x", OFF_XS, 0, D_SSM), ("b", OFF_B, D_SSM, D_BC),
                                    ("c", OFF_C, D_SSM + D_BC, D_BC)):
        ext = [scv_ref[j, :, coff:coff + width] for j in range(hist)]
        ext += [tok(l, poff, width) for l in range(nl)]
        outs = []
        for l in range(nl):
            acc = cbias_ref[:, coff:coff + width]
            for k in range(CONV_WIDTH):
                acc = acc + cw_ref[k:k + 1, coff:coff + width] * ext[l + k]
            outs.append(_silu(acc))
        conv_out[name] = outs
        for j in range(hist):
            nconv_ref[j, :, coff:coff + width] = ext[j + nl]
    xs, bs, cs = conv_out["x"], conv_out["b"], conv_out["c"]
    for l in range(nl):
        cs_ref[rows(l), :] = cs[l]
        bs_ref[rows(l), :] = bs[l].astype(BF16)

    a_neg = -jnp.exp(alog_ref[...])
    dt, a_cum = [], []
    run = None
    for l in range(nl):
        d = _softplus(dtr_ref[rows(l), :] + dtb_ref[...])
        dt.append(d)
        run = d * a_neg if run is None else run + d * a_neg
        a_cum.append(run)
    cdec_ref[...] = jnp.exp(a_cum[nl - 1])

    ehot = ehot_ref[...]
    ghot = ghot_ref[...]
    for l in range(nl):
        ydiag = None
        for s in range(l + 1):
            cbh = _dot2(cs[l] * bs[s], ghot)
            gls = cbh * jnp.exp(a_cum[l] - a_cum[s]) * dt[s]
            term = _dot2(gls, ehot) * xs[s]
            ydiag = term if ydiag is None else ydiag + term
        ydx_ref[rows(l), :] = ydiag + dskip_ref[...] * xs[l]
        w_end = jnp.exp(a_cum[nl - 1] - a_cum[l]) * dt[l]
        xw_ref[rows(l), :] = _dot2(w_end, ehot) * xs[l]
        ecum_ref[rows(l), :] = _dot2(jnp.exp(a_cum[l]), ehot)


def _sample1(pj3, dtr3, sp3, scv3, pool_w, pool_scale, conv_w, conv_b, dt_bias_p, a_log_p,
             dskip_row, ehot, ghot, sb):
    nblk = pj3.shape[0]
    db = nblk * sb
    rb = DEC_SEQ * sb
    blk2 = lambda width: pl.BlockSpec((None, rb, width), lambda i: (i, 0, 0))
    hist3 = lambda n, width: pl.BlockSpec((n, sb, width), lambda i: (0, i, 0))
    const2 = lambda shape: pl.BlockSpec(shape, lambda i: (0, 0))
    blk_shape = lambda width, dt: jax.ShapeDtypeStruct((nblk, rb, width), dt)
    return pl.pallas_call(
        functools.partial(_sample1_kernel, sb=sb),
        grid=(nblk,),
        in_specs=[
            blk2(D_MAIN), blk2(LANES), hist3(POOL_HIST, D_POOL), hist3(CONV_WIDTH - 1, D_CONV),
            pl.BlockSpec((len(POOL_WINDOWS), POOL_GROUP, POOL_GROUP), lambda i: (0, 0, 0)),
            const2((1, D_POOL)), const2((CONV_WIDTH, D_CONV)), const2((1, D_CONV)),
            const2((1, LANES)), const2((1, LANES)), const2((1, D_SSM)),
            const2((LANES, D_SSM)), const2((D_BC, LANES)),
        ],
        out_specs=[
            blk2(D_POOL), hist3(POOL_HIST, D_POOL), hist3(CONV_WIDTH - 1, D_CONV),
            blk2(D_BC), blk2(D_BC), blk2(D_SSM), blk2(D_SSM), blk2(D_SSM),
            pl.BlockSpec((sb, LANES), lambda i: (i, 0)),
        ],
        out_shape=[
            blk_shape(D_POOL, BF16),
            jax.ShapeDtypeStruct((POOL_HIST, db, D_POOL), F32),
            jax.ShapeDtypeStruct((CONV_WIDTH - 1, db, D_CONV), F32),
            blk_shape(D_BC, F32), blk_shape(D_BC, BF16),
            blk_shape(D_SSM, F32), blk_shape(D_SSM, F32), blk_shape(D_SSM, F32),
            jax.ShapeDtypeStruct((db, LANES), F32),
        ],
        compiler_params=pltpu.CompilerParams(
            dimension_semantics=("arbitrary",),
            vmem_limit_bytes=VMEM_LIMIT),
        name="sample_elementwise",
    )(pj3, dtr3, sp3, scv3, pool_w, pool_scale, conv_w, conv_b, dt_bias_p, a_log_p, dskip_row,
      ehot, ghot)


def _sample2_kernel(cdec_ref, st_ref, xw_ref, bs_ref, cs_ref, nst_ref, yo_ref, xwt_ref, *, sb):
    i = pl.program_id(0)
    j = pl.program_id(1)
    nl = DEC_SEQ
    rows = sb * nl

    @pl.when(j == 0)
    def _():
        for blk in range(D_SSM // LANES):
            xwt_ref[blk * LANES:(blk + 1) * LANES, :] = jnp.transpose(
                xw_ref[:, blk * LANES:(blk + 1) * LANES]).astype(BF16)

    seq_of_row = lax.broadcasted_iota(jnp.int32, (rows, 1), 0) & (sb - 1)
    gw = HEADS_PER_GROUP * HEAD_DIM
    for pr in range(SEQ_PER_STEP // 2):
        q0 = SEQ_PER_STEP * j + 2 * pr
        c8 = jnp.concatenate([cs_ref[pl.ds(l * sb + q0 + bi, 1), :]
                              for bi in range(2) for l in range(nl)], axis=0).astype(BF16)
        for g in range(GROUPS):
            gs = slice(g * D_STATE, (g + 1) * D_STATE)
            b_blk = bs_ref[:, gs]
            zero_b = jnp.zeros_like(b_blk)
            w2 = jnp.concatenate([jnp.where(seq_of_row == q0, b_blk, zero_b),
                                  jnp.where(seq_of_row == q0 + 1, b_blk, zero_b)], axis=1)
            u2 = _dot(xwt_ref[g * gw:(g + 1) * gw, :], w2)
            for bi in range(2):
                sq = 2 * pr + bi
                s0 = st_ref[sq, g * HEADS_PER_GROUP:(g + 1) * HEADS_PER_GROUP].reshape(gw, D_STATE)
                yo = _dot_nt(c8[:, gs], s0.astype(BF16))
                base = (i * sb + q0 + bi) * HEADS + g * HEADS_PER_GROUP
                for r in range(HEADS_PER_GROUP):
                    dec = cdec_ref[base + r]
                    rs = slice(r * HEAD_DIM, (r + 1) * HEAD_DIM)
                    nst_ref[sq, g * HEADS_PER_GROUP + r] = (
                        s0[rs] * dec + u2[rs, bi * D_STATE:(bi + 1) * D_STATE])
                for l in range(nl):
                    yo_ref[pl.ds(l * sb + q0 + bi, 1), g * gw:(g + 1) * gw] = (
                        yo[bi * nl + l:bi * nl + l + 1])


def _sample2(cdec_flat, state, xw3, bs3, cs3, sb):
    db = state.shape[0]
    nl = DEC_SEQ
    per = SEQ_PER_STEP
    nj = sb // per
    rb = nl * sb
    assert rb == LANES
    blk2 = lambda width: pl.BlockSpec((None, rb, width), lambda i, j: (i, 0, 0))
    return pl.pallas_call(
        functools.partial(_sample2_kernel, sb=sb),
        grid=(db // sb, nj),
        in_specs=[
            pl.BlockSpec(memory_space=pltpu.SMEM),
            pl.BlockSpec((per, HEADS, HEAD_DIM, D_STATE), lambda i, j: (i * nj + j, 0, 0, 0)),
            blk2(D_SSM), blk2(D_BC), blk2(D_BC),
        ],
        out_specs=[
            pl.BlockSpec((per, HEADS, HEAD_DIM, D_STATE), lambda i, j: (i * nj + j, 0, 0, 0)),
            blk2(D_SSM),
        ],
        out_shape=[
            jax.ShapeDtypeStruct(state.shape, F32),
            jax.ShapeDtypeStruct((db // sb, rb, D_SSM), F32),
        ],
        scratch_shapes=[pltpu.VMEM((D_SSM, rb), BF16)],
        compiler_params=pltpu.CompilerParams(
            dimension_semantics=("arbitrary", "arbitrary"),
            vmem_limit_bytes=VMEM_LIMIT),
        name="sample_state",
    )(cdec_flat, state, xw3, bs3, cs3)


def _sample3_kernel(yo_ref, ecum_ref, ydx_ref, z_ref, ng_ref, ob_ref):
    gw = D_SSM // GROUPS
    for g in range(GROUPS):
        cs = slice(g * gw, (g + 1) * gw)
        y = ydx_ref[:, cs] + ecum_ref[:, cs] * yo_ref[:, cs]
        yz = y * _silu(z_ref[:, cs].astype(F32))
        ms = jnp.sum(yz * yz, axis=-1, keepdims=True) * (1.0 / gw)
        ob_ref[:, cs] = (yz * lax.rsqrt(ms + EPS) * ng_ref[:, cs]).astype(BF16)


def _sample3(yo3, ecum3, ydx3, pj3, norm_g):
    nblk, rb, _ = pj3.shape
    blk2 = lambda width: pl.BlockSpec((None, rb, width), lambda i: (i, 0, 0))
    return pl.pallas_call(
        _sample3_kernel,
        grid=(nblk,),
        in_specs=[blk2(D_SSM), blk2(D_SSM), blk2(D_SSM),
                  pl.BlockSpec((None, rb, D_SSM), lambda i: (i, 0, OFF_Z // D_SSM)),
                  pl.BlockSpec((1, D_SSM), lambda i: (0, 0))],
        out_specs=blk2(D_SSM),
        out_shape=jax.ShapeDtypeStruct((nblk, rb, D_SSM), BF16),
        compiler_params=pltpu.CompilerParams(
            dimension_semantics=("arbitrary",),
            vmem_limit_bytes=VMEM_LIMIT),
        name="sample_gate_norm",
    )(yo3, ecum3, ydx3, pj3, norm_g)


def kernel(x_prompt, x_sample, state_pool, state_conv, state_ssm, norm_g, w_in, conv_w, conv_b,
           dt_bias, a_log, d_skip, ssm_norm_g, pool_w, pool_scale, w_out, final_g):
    bsz, seq, _ = x_prompt.shape
    db, nl, _ = x_sample.shape
    assert nl == DEC_SEQ and seq % CHUNK == 0 and w_in.shape[0] == 1

    w_t = jnp.transpose(w_in[0])
    w_a = w_out[0, :D_POOL].astype(BF16)
    w_b = w_out[0, D_POOL:].astype(BF16)
    pool_w_b = pool_w[0].astype(BF16)
    g_in = norm_g[0][None, :]
    g_fin = final_g[None, :]
    ps = pool_scale[0][None, :]
    cw = conv_w[0]
    cbias = conv_b[0][None, :]
    pad_h = lambda v: jnp.pad(v, (0, LANES - HEADS))[None, :]
    dtb = pad_h(dt_bias[0])
    alog = pad_h(a_log[0])
    dskip_row = jnp.repeat(d_skip[0], HEAD_DIM)[None, :]
    ng = ssm_norm_g[0][None, :]

    xp2 = x_prompt.reshape(bsz * seq, D_MODEL)
    proj_p, dt_p = _inproj(xp2, g_in, w_t)
    oa_p, npool_p = _pool_prompt(proj_p, pool_w_b, ps, bsz, seq)
    ob_p, nconv_p, nssm_p = _ssd_prompt(proj_p, dt_p, cw, cbias, dtb, alog, dskip_row, ng, bsz, seq)
    y_p = _outproj(oa_p, ob_p, w_a, w_b, xp2, g_fin).reshape(bsz, seq, D_MODEL)

    sb = SAMPLE_BLOCK
    nblk = db // sb
    rb = nl * sb
    xs2 = x_sample.reshape(nblk, sb, nl, D_MODEL).transpose(0, 2, 1, 3).reshape(db * nl, D_MODEL)
    proj_s, dt_s = _inproj(xs2, g_in, w_t)
    pj3 = proj_s.reshape(nblk, rb, D_MAIN)
    head_of_ch = jnp.arange(D_SSM, dtype=jnp.int32) // HEAD_DIM
    ehot = (jnp.arange(LANES, dtype=jnp.int32)[:, None] == head_of_ch[None, :]).astype(BF16)
    grp_of_row = jnp.arange(D_BC, dtype=jnp.int32) // D_STATE
    head_id = jnp.arange(LANES, dtype=jnp.int32)
    ghot = ((head_id[None, :] // HEADS_PER_GROUP == grp_of_row[:, None])
            & (head_id[None, :] < HEADS)).astype(BF16)
    (oa_s, npool_s, nconv_s, cs_s, bs_s, xw_s, ydx_s, ecum_s, cdec_s) = _sample1(
        pj3, dt_s.reshape(nblk, rb, LANES), jnp.transpose(state_pool[0], (1, 0, 2)),
        jnp.transpose(state_conv[0], (1, 0, 2)), pool_w_b, ps, cw, cbias, dtb, alog,
        dskip_row, ehot, ghot, sb)
    nssm_s, yo_s = _sample2(cdec_s[:, :HEADS].reshape(db * HEADS), state_ssm[0], xw_s, bs_s, cs_s, sb)
    ob_s = _sample3(yo_s, ecum_s, ydx_s, pj3, ng)
    y_s = _outproj(oa_s.reshape(db * nl, D_POOL), ob_s.reshape(db * nl, D_SSM), w_a, w_b, xs2,
                   g_fin)
    y_s = y_s.reshape(nblk, nl, sb, D_MODEL).transpose(0, 2, 1, 3).reshape(db, nl, D_MODEL)

    return (y_p, y_s,
            npool_p[None], nconv_p[None], nssm_p[None],
            jnp.transpose(npool_s, (1, 0, 2))[None],
            jnp.transpose(nconv_s, (1, 0, 2))[None],
            nssm_s[None])
```

```python
import functools

import jax
import jax.numpy as jnp
from jax import lax
from jax.experimental import pallas as pl
from jax.experimental.pallas import tpu as pltpu

F32 = jnp.float32
BF16 = jnp.bfloat16

D_MODEL = 2048
D_POOL = 1024
POOL_WINDOWS = (2, 4, 8, 16)
POOL_GROUP = 256
POOL_HIST = 15
D_SSM = 3072
HEAD_DIM = 64
HEADS = 48
GROUPS = 8
HEADS_PER_GROUP = 6
D_STATE = 128
D_BC = GROUPS * D_STATE
CONV_WIDTH = 4
D_CONV = D_SSM + 2 * D_BC
D_MAIN = 2 * D_POOL + D_SSM + D_CONV
PAST_LEN = 16384
DEC_SEQ = 4
EPS = 1e-5

LANES = 128
SUBLANES = 8
VMEM_LIMIT = 56 * 1024 * 1024

OFF_Z = 0
OFF_XS = D_SSM
OFF_U = 2 * D_SSM
OFF_GATE = OFF_U + D_POOL
OFF_B = OFF_GATE + D_POOL
OFF_C = OFF_B + D_BC

CHUNK = 128
SAMPLE_BLOCK = 32
NEG_BIG = -1e30
LOG2E = 1.4426950408889634


def _silu(v):
    h = 0.5 * v
    return h + h * jnp.tanh(h)


def _softplus(v):
    y = jnp.exp(-jnp.abs(v))
    u = 1.0 + y
    d = u - 1.0
    l1p = jnp.where(d == 0.0, y, jnp.log(u) * (y / jnp.where(d == 0.0, 1.0, d)))
    return jnp.maximum(v, 0.0) + l1p


def _split2(v):
    hi = v.astype(BF16)
    lo = (v - hi.astype(F32)).astype(BF16)
    return hi, lo


def _dot(a, b):
    return jnp.dot(a, b, preferred_element_type=F32)


def _dot_nt(a, b):
    return lax.dot_general(a, b, (((1,), (1,)), ((), ())), preferred_element_type=F32)


def _dot2(v, onehot):
    hi, lo = _split2(v)
    return _dot(hi, onehot) + _dot(lo, onehot)


def _inproj_kernel(x_ref, g_ref, w_ref, wdt_ref, o_ref, dt_ref, h_ref):
    @pl.when(pl.program_id(1) == 0)
    def _():
        x = x_ref[...]
        ms = jnp.mean(x * x, axis=-1, keepdims=True)
        h = (x * lax.rsqrt(ms + EPS) * g_ref[...]).astype(BF16)
        h_ref[...] = h
        wrow = lax.broadcasted_iota(jnp.int32, wdt_ref.shape, 0)
        wdt = jnp.where(wrow < HEADS, wdt_ref[...], 0.0).astype(BF16)
        dt_ref[...] = _dot_nt(h, wdt)

    o_ref[...] = _dot_nt(h_ref[...], w_ref[...].astype(BF16)).astype(BF16)


def _src_block(j):
    nz = (D_SSM + D_SSM) // 1024
    npool = 2 * D_POOL // 1024
    return jnp.where(j < nz, j + npool, jnp.where(j < nz + npool, j - nz, j))


def _inproj(x2d, norm_g, w_t):
    m = x2d.shape[0]
    tm = min(1024, m)
    tn = 1024
    return pl.pallas_call(
        _inproj_kernel,
        grid=(m // tm, D_MAIN // tn),
        in_specs=[
            pl.BlockSpec((tm, D_MODEL), lambda i, j: (i, 0)),
            pl.BlockSpec((1, D_MODEL), lambda i, j: (0, 0)),
            pl.BlockSpec((tn, D_MODEL), lambda i, j: (_src_block(j), 0)),
            pl.BlockSpec((LANES, D_MODEL), lambda i, j: (D_MAIN // LANES, 0)),
        ],
        out_specs=[
            pl.BlockSpec((tm, tn), lambda i, j: (i, j)),
            pl.BlockSpec((tm, LANES), lambda i, j: (i, 0)),
        ],
        out_shape=[
            jax.ShapeDtypeStruct((m, D_MAIN), BF16),
            jax.ShapeDtypeStruct((m, LANES), F32),
        ],
        scratch_shapes=[pltpu.VMEM((tm, D_MODEL), BF16)],
        compiler_params=pltpu.CompilerParams(
            dimension_semantics=("arbitrary", "arbitrary"),
            vmem_limit_bytes=VMEM_LIMIT),
        name="inproj",
    )(x2d, norm_g, w_t, w_t)


def _outproj_kernel(a_ref, b_ref, wa_ref, wb_ref, x_ref, g_ref, y_ref):
    acc = _dot(a_ref[...], wa_ref[...]) + _dot(b_ref[...], wb_ref[...])
    r = x_ref[...] + acc
    ms = jnp.mean(r * r, axis=-1, keepdims=True)
    y_ref[...] = r * lax.rsqrt(ms + EPS) * g_ref[...]


def _outproj(out_a, out_b, w_a, w_b, x2d, final_g):
    m = x2d.shape[0]
    tm = min(512, m)
    resident = pl.Buffered(1)
    return pl.pallas_call(
        _outproj_kernel,
        grid=(m // tm,),
        in_specs=[
            pl.BlockSpec((tm, D_POOL), lambda i: (i, 0)),
            pl.BlockSpec((tm, D_SSM), lambda i: (i, 0)),
            pl.BlockSpec((D_POOL, D_MODEL), lambda i: (0, 0), pipeline_mode=resident),
            pl.BlockSpec((D_SSM, D_MODEL), lambda i: (0, 0), pipeline_mode=resident),
            pl.BlockSpec((tm, D_MODEL), lambda i: (i, 0)),
            pl.BlockSpec((1, D_MODEL), lambda i: (0, 0)),
        ],
        out_specs=pl.BlockSpec((tm, D_MODEL), lambda i: (i, 0)),
        out_shape=jax.ShapeDtypeStruct((m, D_MODEL), F32),
        compiler_params=pltpu.CompilerParams(
            dimension_semantics=("arbitrary",),
            vmem_limit_bytes=VMEM_LIMIT),
        name="outproj",
    )(out_a, out_b, w_a, w_b, x2d, final_g)


def _pool_kernel(u_ref, gate_ref, pw_ref, ps_ref, oa_ref, np_ref, ubuf, *, tl, nt):
    t = pl.program_id(1)
    hist = POOL_HIST + 1

    @pl.when(t == 0)
    def _():
        ubuf[0:hist, :] = jnp.zeros((hist, D_POOL), F32)

    u = u_ref[...].astype(F32)
    ubuf[hist:hist + tl, :] = u
    pos = t * tl + lax.broadcasted_iota(jnp.int32, (tl, 1), 0)
    for gi, w in enumerate(POOL_WINDOWS):
        cs = slice(gi * POOL_GROUP, (gi + 1) * POOL_GROUP)
        ug = u[:, cs]
        acc = ubuf[0:hist + tl, cs]
        span = 1
        while span < w:
            acc = acc + pltpu.roll(acc, span, 0)
            span *= 2
        cnt = jnp.minimum(w, pos + 1).astype(F32)
        pooled = acc[hist:hist + tl] / cnt - ug
        mixed = _dot(pooled.astype(BF16), pw_ref[gi])
        gt = gate_ref[:, cs].astype(F32)
        oa_ref[:, cs] = (mixed * ps_ref[:, cs] * _silu(gt)).astype(BF16)

    ubuf[0:hist, :] = ubuf[tl:tl + hist, :]

    @pl.when(t == nt - 1)
    def _():
        np_ref[...] = ubuf[tl + 1:tl + hist, :]


def _pool_prompt(proj, pool_w, pool_scale, bsz, seq):
    tl = min(512, seq)
    nt = seq // tl
    return pl.pallas_call(
        functools.partial(_pool_kernel, tl=tl, nt=nt),
        grid=(bsz, nt),
        in_specs=[
            pl.BlockSpec((tl, D_POOL), lambda b, t: (b * nt + t, OFF_U // D_POOL)),
            pl.BlockSpec((tl, D_POOL), lambda b, t: (b * nt + t, OFF_GATE // D_POOL)),
            pl.BlockSpec((len(POOL_WINDOWS), POOL_GROUP, POOL_GROUP), lambda b, t: (0, 0, 0)),
            pl.BlockSpec((1, D_POOL), lambda b, t: (0, 0)),
        ],
        out_specs=[
            pl.BlockSpec((tl, D_POOL), lambda b, t: (b * nt + t, 0)),
            pl.BlockSpec((None, POOL_HIST, D_POOL), lambda b, t: (b, 0, 0)),
        ],
        out_shape=[
            jax.ShapeDtypeStruct((bsz * seq, D_POOL), BF16),
            jax.ShapeDtypeStruct((bsz, POOL_HIST, D_POOL), F32),
        ],
        scratch_shapes=[pltpu.VMEM((tl + POOL_HIST + 1, D_POOL), F32)],
        compiler_params=pltpu.CompilerParams(
            dimension_semantics=("arbitrary", "arbitrary"),
            vmem_limit_bytes=VMEM_LIMIT),
        name="pool_prompt",
    )(proj, proj, pool_w, pool_scale)


def _ssd_body(z_ref, xs_ref, b_ref, c_ref, dtr_ref, cw_ref, cbias_ref, dtb_ref, alog_ref,
              dskip_ref, ng_ref,
              ob_ref, nconv_ref, nssm_ref,
              cbuf, st_ref, cvx_ref, cvb_ref, cvc_ref, y_ref, *, nc):
    q = CHUNK
    c_idx = pl.program_id(1)
    halo = SUBLANES

    nbx, nbb = D_SSM // LANES, D_BC // LANES

    @pl.when(c_idx == 0)
    def _():
        cbuf[:, 0:halo, :] = jnp.zeros((cbuf.shape[0], halo, LANES), F32)
        st_ref[...] = jnp.zeros(st_ref.shape, F32)

    nv = (halo + q) // SUBLANES
    for blk in range(D_CONV // LANES):
        ls = slice(blk * LANES, (blk + 1) * LANES)
        if blk < nbx:
            src_ref, off, dst = xs_ref, blk * LANES, cvx_ref.at[blk]
        elif blk < nbx + nbb:
            src_ref, off, dst = b_ref, (blk - nbx) * LANES, cvb_ref.at[blk - nbx]
        else:
            src_ref, off, dst = c_ref, (blk - nbx - nbb) * LANES, cvc_ref.at[blk - nbx - nbb]
        cbuf[blk, halo:halo + q, :] = src_ref[:, off:off + LANES].astype(F32)
        xv = [cbuf[blk, pl.ds(a, SUBLANES, stride=nv), :] for a in range(nv)]
        wrap = [pltpu.roll(xv[nv - k], 1, 0) for k in range(1, CONV_WIDTH)]
        taps = [cw_ref[k:k + 1, ls] for k in range(CONV_WIDTH)]
        bias = cbias_ref[:, ls]
        for a in range(nv):
            acc = bias + taps[CONV_WIDTH - 1] * xv[a]
            for k in range(1, CONV_WIDTH):
                src = xv[a - k] if a >= k else wrap[k - a - 1]
                acc = acc + taps[CONV_WIDTH - 1 - k] * src
            dst[pl.ds(a, SUBLANES, stride=nv), :] = _silu(acc)
    rows = slice(halo, halo + q)

    dt = _softplus(dtr_ref[...] + dtb_ref[...])
    a_neg = -jnp.exp(alog_ref[...])
    da = dt * (a_neg * LOG2E)
    row = lax.broadcasted_iota(jnp.int32, (q, LANES), 0)
    a2 = da
    shift = 1
    while shift < q:
        a2 = a2 + jnp.where(row >= shift, pltpu.roll(a2, shift, 0), 0.0)
        shift *= 2
    a2_t = jnp.transpose(a2)
    ldt_t = jnp.log2(jnp.transpose(dt))
    a2_end_t = a2_t[:, q - 1:q]
    w_t = jnp.exp2(a2_end_t - a2_t + ldt_t)
    cdec_t = jnp.exp2(a2_end_t)
    srow_t = a2_t - ldt_t
    ea = jnp.exp2(a2)

    li = lax.broadcasted_iota(jnp.int32, (q, q), 0)
    si = lax.broadcasted_iota(jnp.int32, (q, q), 1)
    tri = li >= si
    lane = lax.broadcasted_iota(jnp.int32, (q, LANES), 1)
    lo_half = lane < HEAD_DIM

    for g in range(GROUPS):
        gs = slice(g * D_STATE, (g + 1) * D_STATE)
        c_gb = cvc_ref[g, rows, :].astype(BF16)
        b_g = cvb_ref[g, rows, :]
        cb = _dot_nt(c_gb, b_g.astype(BF16))
        b_t = jnp.transpose(b_g)
        ppg = HEADS_PER_GROUP // 2
        gl = slice(g * ppg * LANES, (g + 1) * ppg * LANES)
        y_off = _dot(c_gb, st_ref[:, gl].astype(BF16))
        for j in range(ppg):
            blk = g * ppg + j
            ls = slice(blk * LANES, (blk + 1) * LANES)
            sc, bw, ecol, dec = [], [], [], []
            for h in (2 * blk, 2 * blk + 1):
                a_col = jnp.broadcast_to(a2[:, h:h + 1], (q, q))
                decay_dt = jnp.exp2(jnp.where(tri, a_col - srow_t[h:h + 1, :], NEG_BIG))
                sc.append((cb * decay_dt).astype(BF16))
                bw.append((b_t * w_t[h:h + 1, :]).astype(BF16))
                ecol.append(jnp.broadcast_to(ea[:, h:h + 1], (q, LANES)))
                dec.append(jnp.broadcast_to(cdec_t[h:h + 1, :], (q, LANES)))
            x_p = cvx_ref[blk, rows, :].astype(BF16)
            zero_b = jnp.zeros_like(x_p)
            x_bd = jnp.concatenate([jnp.where(lo_half, x_p, zero_b),
                                    jnp.where(lo_half, zero_b, x_p)], axis=0)
            lhs1 = jnp.concatenate([jnp.concatenate(sc, axis=1),
                                    jnp.concatenate(bw, axis=1)], axis=0)
            r1 = _dot(lhs1, x_bd)
            y_ref[:, ls] = (r1[0:q] + y_off[:, j * LANES:(j + 1) * LANES]
                            * jnp.where(lo_half, ecol[0], ecol[1]))
            st_ref[:, ls] = (st_ref[:, ls] * jnp.where(lo_half, dec[0], dec[1])
                             + r1[q:2 * q])

    gw = D_SSM // GROUPS
    bpg = gw // LANES
    for g in range(GROUPS):
        yz, ssq = [], None
        for blk in range(g * bpg, (g + 1) * bpg):
            ls = slice(blk * LANES, (blk + 1) * LANES)
            y = y_ref[:, ls] + cvx_ref[blk, rows, :] * dskip_ref[:, ls]
            v = y * _silu(z_ref[:, ls].astype(F32))
            yz.append(v)
            ssq = v * v if ssq is None else ssq + v * v
        scale = lax.rsqrt(jnp.sum(ssq, axis=-1, keepdims=True) * (1.0 / gw) + EPS)
        for i, blk in enumerate(range(g * bpg, (g + 1) * bpg)):
            ls = slice(blk * LANES, (blk + 1) * LANES)
            ob_ref[:, ls] = (yz[i] * scale * ng_ref[:, ls]).astype(BF16)

    cbuf[:, 0:halo, :] = cbuf[:, q:q + halo, :]

    @pl.when(c_idx == nc - 1)
    def _():
        for blk in range(D_CONV // LANES):
            nconv_ref[:, blk * LANES:(blk + 1) * LANES] = cbuf[blk, q + halo - 3:q + halo, :]
        for blk in range(D_SSM // LANES):
            t = jnp.transpose(st_ref[:, blk * LANES:(blk + 1) * LANES])
            nssm_ref[2 * blk:2 * blk + 2] = t.reshape(2, HEAD_DIM, D_STATE)


def _sample1_kernel(pj_ref, dtr_ref, sp_ref, scv_ref, pw_ref, ps_ref, cw_ref, cbias_ref, dtb_ref,
                    alog_ref, dskip_ref, ehot_ref, ghot_ref,
                    oa_ref, npool_ref, nconv_ref, cs_ref, bs_ref, xw_ref, ydx_ref, ecum_ref,
                    cdec_ref, *, sb):
    nl = DEC_SEQ

    def rows(l):
        return slice(l * sb, (l + 1) * sb)

    def tok(l, off, width):
        return pj_ref[rows(l), off:off + width].astype(F32)

    for gi, w in enumerate(POOL_WINDOWS):
        c0 = gi * POOL_GROUP
        ext = [sp_ref[j, :, c0:c0 + POOL_GROUP] for j in range(POOL_HIST)]
        ext += [tok(l, OFF_U + c0, POOL_GROUP) for l in range(nl)]
        for l in range(nl):
            acc = ext[POOL_HIST + l]
            for k in range(1, w):
                acc = acc + ext[POOL_HIST + l - k]
            cnt = float(min(w, PAST_LEN + l + 1))
            pooled = acc / cnt - ext[POOL_HIST + l]
            mixed = _dot(pooled.astype(BF16), pw_ref[gi])
            gt = tok(l, OFF_GATE + c0, POOL_GROUP)
            oa_ref[rows(l), c0:c0 + POOL_GROUP] = (
                mixed * ps_ref[:, c0:c0 + POOL_GROUP] * _silu(gt)).astype(BF16)
    for j in range(POOL_HIST):
        src = j + nl
        if src < POOL_HIST:
            npool_ref[j] = sp_ref[src]
        else:
            npool_ref[j] = tok(src - POOL_HIST, OFF_U, D_POOL)

    hist = CONV_WIDTH - 1
    conv_out = {}
    for name, poff, coff, width in (("x", OFF_XS, 0, D_SSM), ("b", OFF_B, D_SSM, D_BC),
                                    ("c", OFF_C, D_SSM + D_BC, D_BC)):
        ext = [scv_ref[j, :, coff:coff + width] for j in range(hist)]
        ext += [tok(l, poff, width) for l in range(nl)]
        outs = []
        for l in range(nl):
            acc = cbias_ref[:, coff:coff + width]
            for k in range(CONV_WIDTH):
                acc = acc + cw_ref[k:k + 1, coff:coff + width] * ext[l + k]
            outs.append(_silu(acc))
        conv_out[name] = outs
        for j in range(hist):
            nconv_ref[j, :, coff:coff + width] = ext[j + nl]
    xs, bs, cs = conv_out["x"], conv_out["b"], conv_out["c"]
    for l in range(nl):
        cs_ref[rows(l), :] = cs[l]
        bs_ref[rows(l), :] = bs[l].astype(BF16)

    a_neg = -jnp.exp(alog_ref[...])
    dt, a_cum = [], []
    run = None
    for l in range(nl):
        d = _softplus(dtr_ref[rows(l), :] + dtb_ref[...])
        dt.append(d)
        run = d * a_neg if run is None else run + d * a_neg
        a_cum.append(run)
    cdec_ref[...] = jnp.exp(a_cum[nl - 1])

    ehot = ehot_ref[...]
    ghot = ghot_ref[...]
    for l in range(nl):
        ydiag = None
        for s in range(l + 1):
            cbh = _dot2(cs[l] * bs[s], ghot)
            gls = cbh * jnp.exp(a_cum[l] - a_cum[s]) * dt[s]
            term = _dot2(gls, ehot) * xs[s]
            ydiag = term if ydiag is None else ydiag + term
        ydx_ref[rows(l), :] = ydiag + dskip_ref[...] * xs[l]
        w_end = jnp.exp(a_cum[nl - 1] - a_cum[l]) * dt[l]
        xw_ref[rows(l), :] = _dot2(w_end, ehot) * xs[l]
        ecum_ref[rows(l), :] = _dot2(jnp.exp(a_cum[l]), ehot)


def _sample1(pj3, dtr3, sp3, scv3, pool_w, pool_scale, conv_w, conv_b, dt_bias_p, a_log_p,
             dskip_row, ehot, ghot, sb):
    nblk = pj3.shape[0]
    db = nblk * sb
    rb = DEC_SEQ * sb
    blk2 = lambda width: pl.BlockSpec((None, rb, width), lambda i: (i, 0, 0))
    hist3 = lambda n, width: pl.BlockSpec((n, sb, width), lambda i: (0, i, 0))
    const2 = lambda shape: pl.BlockSpec(shape, lambda i: (0, 0))
    blk_shape = lambda width, dt: jax.ShapeDtypeStruct((nblk, rb, width), dt)
    return pl.pallas_call(
        functools.partial(_sample1_kernel, sb=sb),
        grid=(nblk,),
        in_specs=[
            blk2(D_MAIN), blk2(LANES), hist3(POOL_HIST, D_POOL), hist3(CONV_WIDTH - 1, D_CONV),
            pl.BlockSpec((len(POOL_WINDOWS), POOL_GROUP, POOL_GROUP), lambda i: (0, 0, 0)),
            const2((1, D_POOL)), const2((CONV_WIDTH, D_CONV)), const2((1, D_CONV)),
            const2((1, LANES)), const2((1, LANES)), const2((1, D_SSM)),
            const2((LANES, D_SSM)), const2((D_BC, LANES)),
        ],
        out_specs=[
            blk2(D_POOL), hist3(POOL_HIST, D_POOL), hist3(CONV_WIDTH - 1, D_CONV),
            blk2(D_BC), blk2(D_BC), blk2(D_SSM), blk2(D_SSM), blk2(D_SSM),
            pl.BlockSpec((sb, LANES), lambda i: (i, 0)),
        ],
        out_shape=[
            blk_shape(D_POOL, BF16),
            jax.ShapeDtypeStruct((POOL_HIST, db, D_POOL), F32),
            jax.ShapeDtypeStruct((CONV_WIDTH - 1, db, D_CONV), F32),
            blk_shape(D_BC, F32), blk_shape(D_BC, BF16),
            blk_shape(D_SSM, F32), blk_shape(D_SSM, F32), blk_shape(D_SSM, F32),
            jax.ShapeDtypeStruct((db, LANES), F32),
        ],
        compiler_params=pltpu.CompilerParams(
            dimension_semantics=("arbitrary",),
            vmem_limit_bytes=VMEM_LIMIT),
        name="sample_elementwise",
    )(pj3, dtr3, sp3, scv3, pool_w, pool_scale, conv_w, conv_b, dt_bias_p, a_log_p, dskip_row,
      ehot, ghot)


def _sample2_body(cdec_ref, st_ref, xw_ref, bs_ref, cs_ref, nst_ref, yo_ref, xwt_ref, *, sb, per):
    i = pl.program_id(0)
    j = pl.program_id(1)
    nl = DEC_SEQ
    rows = sb * nl

    @pl.when(j == 0)
    def _():
        for blk in range(D_SSM // LANES):
            xwt_ref[blk * LANES:(blk + 1) * LANES, :] = jnp.transpose(
                xw_ref[:, blk * LANES:(blk + 1) * LANES]).astype(BF16)

    seq_of_row = lax.broadcasted_iota(jnp.int32, (rows, 1), 0) & (sb - 1)
    gw = HEADS_PER_GROUP * HEAD_DIM
    for pr in range(per // 2):
        q0 = per * j + 2 * pr
        c8 = jnp.concatenate([cs_ref[pl.ds(l * sb + q0 + bi, 1), :]
                              for bi in range(2) for l in range(nl)], axis=0).astype(BF16)
        for g in range(GROUPS):
            gs = slice(g * D_STATE, (g + 1) * D_STATE)
            b_blk = bs_ref[:, gs]
            zero_b = jnp.zeros_like(b_blk)
            w2 = jnp.concatenate([jnp.where(seq_of_row == q0, b_blk, zero_b),
                                  jnp.where(seq_of_row == q0 + 1, b_blk, zero_b)], axis=1)
            u2 = _dot(xwt_ref[g * gw:(g + 1) * gw, :], w2)
            for bi in range(2):
                sq = 2 * pr + bi
                s0 = st_ref[sq, g * HEADS_PER_GROUP:(g + 1) * HEADS_PER_GROUP].reshape(gw, D_STATE)
                yo = _dot_nt(c8[:, gs], s0.astype(BF16))
                base = (i * sb + q0 + bi) * HEADS + g * HEADS_PER_GROUP
                for r in range(HEADS_PER_GROUP):
                    dec = cdec_ref[base + r]
                    rs = slice(r * HEAD_DIM, (r + 1) * HEAD_DIM)
                    nst_ref[sq, g * HEADS_PER_GROUP + r] = (
                        s0[rs] * dec + u2[rs, bi * D_STATE:(bi + 1) * D_STATE])
                for l in range(nl):
                    yo_ref[pl.ds(l * sb + q0 + bi, 1), g * gw:(g + 1) * gw] = (
                        yo[bi * nl + l:bi * nl + l + 1])


N_SSD_IN, N_SSD_OUT, N_SSD_SCRATCH = 11, 3, 6
N_ST_IN, N_ST_OUT = 5, 2


def _ssd_state_kernel(*refs, nc, sb, per):
    ins, rest = refs[:N_SSD_IN + N_ST_IN], refs[N_SSD_IN + N_ST_IN:]
    outs, scratch = rest[:N_SSD_OUT + N_ST_OUT], rest[N_SSD_OUT + N_ST_OUT:]
    _ssd_body(*ins[:N_SSD_IN], *outs[:N_SSD_OUT], *scratch[:N_SSD_SCRATCH], nc=nc)
    _sample2_body(*ins[N_SSD_IN:], *outs[N_SSD_OUT:], *scratch[N_SSD_SCRATCH:], sb=sb, per=per)


def _ssd_prompt_sample_state(proj, dt_raw, conv_w, conv_b, dt_bias_p, a_log_p, dskip_row, norm_g,
                             bsz, seq, cdec_flat, state, xw3, bs3, cs3, sb):
    q = CHUNK
    nc = seq // q
    db = state.shape[0]
    rb = DEC_SEQ * sb
    per = sb // nc
    assert rb == LANES and db // sb == bsz and per * nc == sb and per % 2 == 0
    row = lambda b, c: b * nc + c
    const = lambda b, c: (0, 0)
    blk2 = lambda width: pl.BlockSpec((None, rb, width), lambda i, j: (i, 0, 0))
    st_spec = pl.BlockSpec((per, HEADS, HEAD_DIM, D_STATE), lambda i, j: (i * nc + j, 0, 0, 0))
    return pl.pallas_call(
        functools.partial(_ssd_state_kernel, nc=nc, sb=sb, per=per),
        grid=(bsz, nc),
        in_specs=[
            pl.BlockSpec((q, D_SSM), lambda b, c: (row(b, c), OFF_Z // D_SSM)),
            pl.BlockSpec((q, D_SSM), lambda b, c: (row(b, c), OFF_XS // D_SSM)),
            pl.BlockSpec((q, D_BC), lambda b, c: (row(b, c), OFF_B // D_BC)),
            pl.BlockSpec((q, D_BC), lambda b, c: (row(b, c), OFF_C // D_BC)),
            pl.BlockSpec((q, LANES), lambda b, c: (row(b, c), 0)),
            pl.BlockSpec((CONV_WIDTH, D_CONV), const),
            pl.BlockSpec((1, D_CONV), const),
            pl.BlockSpec((1, LANES), const),
            pl.BlockSpec((1, LANES), const),
            pl.BlockSpec((1, D_SSM), const),
            pl.BlockSpec((1, D_SSM), const),
            pl.BlockSpec(memory_space=pltpu.SMEM),
            st_spec, blk2(D_SSM), blk2(D_BC), blk2(D_BC),
        ],
        out_specs=[
            pl.BlockSpec((q, D_SSM), lambda b, c: (row(b, c), 0)),
            pl.BlockSpec((None, CONV_WIDTH - 1, D_CONV), lambda b, c: (b, 0, 0)),
            pl.BlockSpec((None, HEADS, HEAD_DIM, D_STATE), lambda b, c: (b, 0, 0, 0)),
            st_spec, blk2(D_SSM),
        ],
        out_shape=[
            jax.ShapeDtypeStruct((bsz * seq, D_SSM), BF16),
            jax.ShapeDtypeStruct((bsz, CONV_WIDTH - 1, D_CONV), F32),
            jax.ShapeDtypeStruct((bsz, HEADS, HEAD_DIM, D_STATE), F32),
            jax.ShapeDtypeStruct(state.shape, F32),
            jax.ShapeDtypeStruct((db // sb, rb, D_SSM), F32),
        ],
        scratch_shapes=[
            pltpu.VMEM((D_CONV // LANES, q + SUBLANES, LANES), F32),
            pltpu.VMEM((D_STATE, D_SSM), F32),
            pltpu.VMEM((D_SSM // LANES, q + SUBLANES, LANES), F32),
            pltpu.VMEM((D_BC // LANES, q + SUBLANES, LANES), F32),
            pltpu.VMEM((D_BC // LANES, q + SUBLANES, LANES), F32),
            pltpu.VMEM((q, D_SSM), F32),
            pltpu.VMEM((D_SSM, rb), BF16),
        ],
        compiler_params=pltpu.CompilerParams(
            dimension_semantics=("arbitrary", "arbitrary"),
            vmem_limit_bytes=VMEM_LIMIT),
        name="ssd_prompt_sample_state",
    )(proj, proj, proj, proj, dt_raw, conv_w, conv_b, dt_bias_p, a_log_p, dskip_row, norm_g,
      cdec_flat, state, xw3, bs3, cs3)


def _sample3_kernel(yo_ref, ecum_ref, ydx_ref, z_ref, ng_ref, ob_ref):
    gw = D_SSM // GROUPS
    for g in range(GROUPS):
        cs = slice(g * gw, (g + 1) * gw)
        y = ydx_ref[:, cs] + ecum_ref[:, cs] * yo_ref[:, cs]
        yz = y * _silu(z_ref[:, cs].astype(F32))
        ms = jnp.sum(yz * yz, axis=-1, keepdims=True) * (1.0 / gw)
        ob_ref[:, cs] = (yz * lax.rsqrt(ms + EPS) * ng_ref[:, cs]).astype(BF16)


def _sample3(yo3, ecum3, ydx3, pj3, norm_g):
    nblk, rb, _ = pj3.shape
    blk2 = lambda width: pl.BlockSpec((None, rb, width), lambda i: (i, 0, 0))
    return pl.pallas_call(
        _sample3_kernel,
        grid=(nblk,),
        in_specs=[blk2(D_SSM), blk2(D_SSM), blk2(D_SSM),
                  pl.BlockSpec((None, rb, D_SSM), lambda i: (i, 0, OFF_Z // D_SSM)),
                  pl.BlockSpec((1, D_SSM), lambda i: (0, 0))],
        out_specs=blk2(D_SSM),
        out_shape=jax.ShapeDtypeStruct((nblk, rb, D_SSM), BF16),
        compiler_params=pltpu.CompilerParams(
            dimension_semantics=("arbitrary",),
            vmem_limit_bytes=VMEM_LIMIT),
        name="sample_gate_norm",
    )(yo3, ecum3, ydx3, pj3, norm_g)


def kernel(x_prompt, x_sample, state_pool, state_conv, state_ssm, norm_g, w_in, conv_w, conv_b,
           dt_bias, a_log, d_skip, ssm_norm_g, pool_w, pool_scale, w_out, final_g):
    bsz, seq, _ = x_prompt.shape
    db, nl, _ = x_sample.shape
    assert nl == DEC_SEQ and seq % CHUNK == 0 and w_in.shape[0] == 1

    w_t = jnp.transpose(w_in[0])
    w_a = w_out[0, :D_POOL].astype(BF16)
    w_b = w_out[0, D_POOL:].astype(BF16)
    pool_w_b = pool_w[0].astype(BF16)
    g_in = norm_g[0][None, :]
    g_fin = final_g[None, :]
    ps = pool_scale[0][None, :]
    cw = conv_w[0]
    cbias = conv_b[0][None, :]
    pad_h = lambda v: jnp.pad(v, (0, LANES - HEADS))[None, :]
    dtb = pad_h(dt_bias[0])
    alog = pad_h(a_log[0])
    dskip_row = jnp.repeat(d_skip[0], HEAD_DIM)[None, :]
    ng = ssm_norm_g[0][None, :]

    sb = SAMPLE_BLOCK
    nblk = db // sb
    rb = nl * sb
    xs2 = x_sample.reshape(nblk, sb, nl, D_MODEL).transpose(0, 2, 1, 3).reshape(db * nl, D_MODEL)
    proj_s, dt_s = _inproj(xs2, g_in, w_t)
    pj3 = proj_s.reshape(nblk, rb, D_MAIN)
    head_of_ch = jnp.arange(D_SSM, dtype=jnp.int32) // HEAD_DIM
    ehot = (jnp.arange(LANES, dtype=jnp.int32)[:, None] == head_of_ch[None, :]).astype(BF16)
    grp_of_row = jnp.arange(D_BC, dtype=jnp.int32) // D_STATE
    head_id = jnp.arange(LANES, dtype=jnp.int32)
    ghot = ((head_id[None, :] // HEADS_PER_GROUP == grp_of_row[:, None])
            & (head_id[None, :] < HEADS)).astype(BF16)
    (oa_s, npool_s, nconv_s, cs_s, bs_s, xw_s, ydx_s, ecum_s, cdec_s) = _sample1(
        pj3, dt_s.reshape(nblk, rb, LANES), jnp.transpose(state_pool[0], (1, 0, 2)),
        jnp.transpose(state_conv[0], (1, 0, 2)), pool_w_b, ps, cw, cbias, dtb, alog,
        dskip_row, ehot, ghot, sb)

    xp2 = x_prompt.reshape(bsz * seq, D_MODEL)
    proj_p, dt_p = _inproj(xp2, g_in, w_t)
    oa_p, npool_p = _pool_prompt(proj_p, pool_w_b, ps, bsz, seq)
    ob_p, nconv_p, nssm_p, nssm_s, yo_s = _ssd_prompt_sample_state(
        proj_p, dt_p, cw, cbias, dtb, alog, dskip_row, ng, bsz, seq,
        cdec_s[:, :HEADS].reshape(db * HEADS), state_ssm[0], xw_s, bs_s, cs_s, sb)
    y_p = _outproj(oa_p, ob_p, w_a, w_b, xp2, g_fin).reshape(bsz, seq, D_MODEL)

    ob_s = _sample3(yo_s, ecum_s, ydx_s, pj3, ng)
    y_s = _outproj(oa_s.reshape(db * nl, D_POOL), ob_s.reshape(db * nl, D_SSM), w_a, w_b, xs2,
                   g_fin)
    y_s = y_s.reshape(nblk, nl, sb, D_MODEL).transpose(0, 2, 1, 3).reshape(db, nl, D_MODEL)

    return (y_p, y_s,
            npool_p[None], nconv_p[None], nssm_p[None],
            jnp.transpose(npool_s, (1, 0, 2))[None],
            jnp.transpose(nconv_s, (1, 0, 2))[None],
            nssm_s[None])
```

```python
import functools

import jax
import jax.numpy as jnp
from jax import lax
from jax.experimental import pallas as pl
from jax.experimental.pallas import tpu as pltpu

F32 = jnp.float32
BF16 = jnp.bfloat16

D_MODEL = 2048
D_POOL = 1024
POOL_WINDOWS = (2, 4, 8, 16)
POOL_GROUP = 256
POOL_HIST = 15
D_SSM = 3072
HEAD_DIM = 64
HEADS = 48
GROUPS = 8
HEADS_PER_GROUP = 6
D_STATE = 128
D_BC = GROUPS * D_STATE
CONV_WIDTH = 4
D_CONV = D_SSM + 2 * D_BC
D_MAIN = 2 * D_POOL + D_SSM + D_CONV
PAST_LEN = 16384
DEC_SEQ = 4
EPS = 1e-5

LANES = 128
SUBLANES = 8
VMEM_LIMIT = 56 * 1024 * 1024

OFF_Z = 0
OFF_XS = D_SSM
OFF_U = 2 * D_SSM
OFF_GATE = OFF_U + D_POOL
OFF_B = OFF_GATE + D_POOL
OFF_C = OFF_B + D_BC

CHUNK = 128
SAMPLE_BLOCK = 32
NEG_BIG = -1e30
LOG2E = 1.4426950408889634


def _silu(v):
    h = 0.5 * v
    return h + h * jnp.tanh(h)


def _softplus(v):
    y = jnp.exp(-jnp.abs(v))
    u = 1.0 + y
    d = u - 1.0
    l1p = jnp.where(d == 0.0, y, jnp.log(u) * (y / jnp.where(d == 0.0, 1.0, d)))
    return jnp.maximum(v, 0.0) + l1p


def _split2(v):
    hi = v.astype(BF16)
    lo = (v - hi.astype(F32)).astype(BF16)
    return hi, lo


def _dot(a, b):
    return jnp.dot(a, b, preferred_element_type=F32)


def _dot_nt(a, b):
    return lax.dot_general(a, b, (((1,), (1,)), ((), ())), preferred_element_type=F32)


def _dot2(v, onehot):
    hi, lo = _split2(v)
    return _dot(hi, onehot) + _dot(lo, onehot)


def _inproj_kernel(x_ref, g_ref, w_ref, wdt_ref, o_ref, dt_ref, h_ref):
    @pl.when(pl.program_id(1) == 0)
    def _():
        x = x_ref[...]
        ms = jnp.mean(x * x, axis=-1, keepdims=True)
        h = (x * lax.rsqrt(ms + EPS) * g_ref[...]).astype(BF16)
        h_ref[...] = h
        wrow = lax.broadcasted_iota(jnp.int32, wdt_ref.shape, 0)
        wdt = jnp.where(wrow < HEADS, wdt_ref[...], 0.0).astype(BF16)
        dt_ref[...] = _dot_nt(h, wdt)

    o_ref[...] = _dot_nt(h_ref[...], w_ref[...].astype(BF16)).astype(BF16)


def _src_block(j):
    nz = (D_SSM + D_SSM) // 1024
    npool = 2 * D_POOL // 1024
    return jnp.where(j < nz, j + npool, jnp.where(j < nz + npool, j - nz, j))


def _inproj(x2d, norm_g, w_t):
    m = x2d.shape[0]
    tm = min(1024, m)
    tn = 1024
    return pl.pallas_call(
        _inproj_kernel,
        grid=(m // tm, D_MAIN // tn),
        in_specs=[
            pl.BlockSpec((tm, D_MODEL), lambda i, j: (i, 0)),
            pl.BlockSpec((1, D_MODEL), lambda i, j: (0, 0)),
            pl.BlockSpec((tn, D_MODEL), lambda i, j: (_src_block(j), 0)),
            pl.BlockSpec((LANES, D_MODEL), lambda i, j: (D_MAIN // LANES, 0)),
        ],
        out_specs=[
            pl.BlockSpec((tm, tn), lambda i, j: (i, j)),
            pl.BlockSpec((tm, LANES), lambda i, j: (i, 0)),
        ],
        out_shape=[
            jax.ShapeDtypeStruct((m, D_MAIN), BF16),
            jax.ShapeDtypeStruct((m, LANES), F32),
        ],
        scratch_shapes=[pltpu.VMEM((tm, D_MODEL), BF16)],
        compiler_params=pltpu.CompilerParams(
            dimension_semantics=("arbitrary", "arbitrary"),
            vmem_limit_bytes=VMEM_LIMIT),
        name="inproj",
    )(x2d, norm_g, w_t, w_t)


def _outproj_kernel(a_ref, b_ref, wa_ref, wb_ref, x_ref, g_ref, y_ref):
    acc = _dot(a_ref[...], wa_ref[...]) + _dot(b_ref[...], wb_ref[...])
    r = x_ref[...] + acc
    ms = jnp.mean(r * r, axis=-1, keepdims=True)
    y_ref[...] = r * lax.rsqrt(ms + EPS) * g_ref[...]


def _outproj(out_a, out_b, w_a, w_b, x2d, final_g):
    m = x2d.shape[0]
    tm = min(512, m)
    resident = pl.Buffered(1)
    return pl.pallas_call(
        _outproj_kernel,
        grid=(m // tm,),
        in_specs=[
            pl.BlockSpec((tm, D_POOL), lambda i: (i, 0)),
            pl.BlockSpec((tm, D_SSM), lambda i: (i, 0)),
            pl.BlockSpec((D_POOL, D_MODEL), lambda i: (0, 0), pipeline_mode=resident),
            pl.BlockSpec((D_SSM, D_MODEL), lambda i: (0, 0), pipeline_mode=resident),
            pl.BlockSpec((tm, D_MODEL), lambda i: (i, 0)),
            pl.BlockSpec((1, D_MODEL), lambda i: (0, 0)),
        ],
        out_specs=pl.BlockSpec((tm, D_MODEL), lambda i: (i, 0)),
        out_shape=jax.ShapeDtypeStruct((m, D_MODEL), F32),
        compiler_params=pltpu.CompilerParams(
            dimension_semantics=("arbitrary",),
            vmem_limit_bytes=VMEM_LIMIT),
        name="outproj",
    )(out_a, out_b, w_a, w_b, x2d, final_g)


def _pool_kernel(u_ref, gate_ref, pw_ref, ps_ref, oa_ref, np_ref, ubuf, *, tl, nt):
    t = pl.program_id(1)
    hist = POOL_HIST + 1

    @pl.when(t == 0)
    def _():
        ubuf[0:hist, :] = jnp.zeros((hist, D_POOL), F32)

    u = u_ref[...].astype(F32)
    ubuf[hist:hist + tl, :] = u
    pos = t * tl + lax.broadcasted_iota(jnp.int32, (tl, 1), 0)
    for gi, w in enumerate(POOL_WINDOWS):
        cs = slice(gi * POOL_GROUP, (gi + 1) * POOL_GROUP)
        ug = u[:, cs]
        acc = ubuf[0:hist + tl, cs]
        span = 1
        while span < w:
            acc = acc + pltpu.roll(acc, span, 0)
            span *= 2
        cnt = jnp.minimum(w, pos + 1).astype(F32)
        pooled = acc[hist:hist + tl] / cnt - ug
        mixed = _dot(pooled.astype(BF16), pw_ref[gi])
        gt = gate_ref[:, cs].astype(F32)
        oa_ref[:, cs] = (mixed * ps_ref[:, cs] * _silu(gt)).astype(BF16)

    ubuf[0:hist, :] = ubuf[tl:tl + hist, :]

    @pl.when(t == nt - 1)
    def _():
        np_ref[...] = ubuf[tl + 1:tl + hist, :]


def _pool_prompt(proj, pool_w, pool_scale, bsz, seq):
    tl = min(512, seq)
    nt = seq // tl
    return pl.pallas_call(
        functools.partial(_pool_kernel, tl=tl, nt=nt),
        grid=(bsz, nt),
        in_specs=[
            pl.BlockSpec((tl, D_POOL), lambda b, t: (b * nt + t, OFF_U // D_POOL)),
            pl.BlockSpec((tl, D_POOL), lambda b, t: (b * nt + t, OFF_GATE // D_POOL)),
            pl.BlockSpec((len(POOL_WINDOWS), POOL_GROUP, POOL_GROUP), lambda b, t: (0, 0, 0)),
            pl.BlockSpec((1, D_POOL), lambda b, t: (0, 0)),
        ],
        out_specs=[
            pl.BlockSpec((tl, D_POOL), lambda b, t: (b * nt + t, 0)),
            pl.BlockSpec((None, POOL_HIST, D_POOL), lambda b, t: (b, 0, 0)),
        ],
        out_shape=[
            jax.ShapeDtypeStruct((bsz * seq, D_POOL), BF16),
            jax.ShapeDtypeStruct((bsz, POOL_HIST, D_POOL), F32),
        ],
        scratch_shapes=[pltpu.VMEM((tl + POOL_HIST + 1, D_POOL), F32)],
        compiler_params=pltpu.CompilerParams(
            dimension_semantics=("arbitrary", "arbitrary"),
            vmem_limit_bytes=VMEM_LIMIT),
        name="pool_prompt",
    )(proj, proj, pool_w, pool_scale)


def _ssd_body(z_ref, xs_ref, b_ref, c_ref, dtr_ref, cw_ref, cbias_ref, dtb_ref, alog_ref,
              dskip_ref, ng_ref,
              ob_ref, nconv_ref, nssm_ref,
              cbuf, st_ref, cvx_ref, cvb_ref, cvc_ref, y_ref, *, nc, after_group=None):
    q = CHUNK
    c_idx = pl.program_id(1)
    halo = SUBLANES

    nbx, nbb = D_SSM // LANES, D_BC // LANES

    @pl.when(c_idx == 0)
    def _():
        cbuf[:, 0:halo, :] = jnp.zeros((cbuf.shape[0], halo, LANES), F32)
        st_ref[...] = jnp.zeros(st_ref.shape, F32)

    nv = (halo + q) // SUBLANES
    for blk in range(D_CONV // LANES):
        ls = slice(blk * LANES, (blk + 1) * LANES)
        if blk < nbx:
            src_ref, off, dst = xs_ref, blk * LANES, cvx_ref.at[blk]
        elif blk < nbx + nbb:
            src_ref, off, dst = b_ref, (blk - nbx) * LANES, cvb_ref.at[blk - nbx]
        else:
            src_ref, off, dst = c_ref, (blk - nbx - nbb) * LANES, cvc_ref.at[blk - nbx - nbb]
        cbuf[blk, halo:halo + q, :] = src_ref[:, off:off + LANES].astype(F32)
        xv = [cbuf[blk, pl.ds(a, SUBLANES, stride=nv), :] for a in range(nv)]
        wrap = [pltpu.roll(xv[nv - k], 1, 0) for k in range(1, CONV_WIDTH)]
        taps = [cw_ref[k:k + 1, ls] for k in range(CONV_WIDTH)]
        bias = cbias_ref[:, ls]
        for a in range(nv):
            acc = bias + taps[CONV_WIDTH - 1] * xv[a]
            for k in range(1, CONV_WIDTH):
                src = xv[a - k] if a >= k else wrap[k - a - 1]
                acc = acc + taps[CONV_WIDTH - 1 - k] * src
            dst[pl.ds(a, SUBLANES, stride=nv), :] = _silu(acc)
    rows = slice(halo, halo + q)

    dt = _softplus(dtr_ref[...] + dtb_ref[...])
    a_neg = -jnp.exp(alog_ref[...])
    da = dt * (a_neg * LOG2E)
    row = lax.broadcasted_iota(jnp.int32, (q, LANES), 0)
    a2 = da
    shift = 1
    while shift < q:
        a2 = a2 + jnp.where(row >= shift, pltpu.roll(a2, shift, 0), 0.0)
        shift *= 2
    a2_t = jnp.transpose(a2)
    ldt_t = jnp.log2(jnp.transpose(dt))
    a2_end_t = a2_t[:, q - 1:q]
    w_t = jnp.exp2(a2_end_t - a2_t + ldt_t)
    cdec_t = jnp.exp2(a2_end_t)
    srow_t = a2_t - ldt_t
    ea = jnp.exp2(a2)

    li = lax.broadcasted_iota(jnp.int32, (q, q), 0)
    si = lax.broadcasted_iota(jnp.int32, (q, q), 1)
    tri = li >= si
    lane = lax.broadcasted_iota(jnp.int32, (q, LANES), 1)
    lo_half = lane < HEAD_DIM

    for g in range(GROUPS):
        gs = slice(g * D_STATE, (g + 1) * D_STATE)
        c_gb = cvc_ref[g, rows, :].astype(BF16)
        b_g = cvb_ref[g, rows, :]
        cb = _dot_nt(c_gb, b_g.astype(BF16))
        b_t = jnp.transpose(b_g)
        ppg = HEADS_PER_GROUP // 2
        gl = slice(g * ppg * LANES, (g + 1) * ppg * LANES)
        y_off = _dot(c_gb, st_ref[:, gl].astype(BF16))
        for j in range(ppg):
            blk = g * ppg + j
            ls = slice(blk * LANES, (blk + 1) * LANES)
            sc, bw, ecol, dec = [], [], [], []
            for h in (2 * blk, 2 * blk + 1):
                a_col = jnp.broadcast_to(a2[:, h:h + 1], (q, q))
                decay_dt = jnp.exp2(jnp.where(tri, a_col - srow_t[h:h + 1, :], NEG_BIG))
                sc.append((cb * decay_dt).astype(BF16))
                bw.append((b_t * w_t[h:h + 1, :]).astype(BF16))
                ecol.append(jnp.broadcast_to(ea[:, h:h + 1], (q, LANES)))
                dec.append(jnp.broadcast_to(cdec_t[h:h + 1, :], (q, LANES)))
            x_p = cvx_ref[blk, rows, :].astype(BF16)
            zero_b = jnp.zeros_like(x_p)
            x_bd = jnp.concatenate([jnp.where(lo_half, x_p, zero_b),
                                    jnp.where(lo_half, zero_b, x_p)], axis=0)
            lhs1 = jnp.concatenate([jnp.concatenate(sc, axis=1),
                                    jnp.concatenate(bw, axis=1)], axis=0)
            r1 = _dot(lhs1, x_bd)
            y_ref[:, ls] = (r1[0:q] + y_off[:, j * LANES:(j + 1) * LANES]
                            * jnp.where(lo_half, ecol[0], ecol[1]))
            st_ref[:, ls] = (st_ref[:, ls] * jnp.where(lo_half, dec[0], dec[1])
                             + r1[q:2 * q])
        if after_group is not None:
            after_group(g)

    gw = D_SSM // GROUPS
    bpg = gw // LANES
    for g in range(GROUPS):
        yz, ssq = [], None
        for blk in range(g * bpg, (g + 1) * bpg):
            ls = slice(blk * LANES, (blk + 1) * LANES)
            y = y_ref[:, ls] + cvx_ref[blk, rows, :] * dskip_ref[:, ls]
            v = y * _silu(z_ref[:, ls].astype(F32))
            yz.append(v)
            ssq = v * v if ssq is None else ssq + v * v
        scale = lax.rsqrt(jnp.sum(ssq, axis=-1, keepdims=True) * (1.0 / gw) + EPS)
        for i, blk in enumerate(range(g * bpg, (g + 1) * bpg)):
            ls = slice(blk * LANES, (blk + 1) * LANES)
            ob_ref[:, ls] = (yz[i] * scale * ng_ref[:, ls]).astype(BF16)

    cbuf[:, 0:halo, :] = cbuf[:, q:q + halo, :]

    @pl.when(c_idx == nc - 1)
    def _():
        for blk in range(D_CONV // LANES):
            nconv_ref[:, blk * LANES:(blk + 1) * LANES] = cbuf[blk, q + halo - 3:q + halo, :]
        for blk in range(D_SSM // LANES):
            t = jnp.transpose(st_ref[:, blk * LANES:(blk + 1) * LANES])
            nssm_ref[2 * blk:2 * blk + 2] = t.reshape(2, HEAD_DIM, D_STATE)


def _sample1_kernel(pj_ref, dtr_ref, sp_ref, scv_ref, pw_ref, ps_ref, cw_ref, cbias_ref, dtb_ref,
                    alog_ref, dskip_ref, ehot_ref, ghot_ref,
                    oa_ref, npool_ref, nconv_ref, cs_ref, bs_ref, xw_ref, ydx_ref, ecum_ref,
                    cdec_ref, *, sb):
    nl = DEC_SEQ

    def rows(l):
        return slice(l * sb, (l + 1) * sb)

    def tok(l, off, width):
        return pj_ref[rows(l), off:off + width].astype(F32)

    for gi, w in enumerate(POOL_WINDOWS):
        c0 = gi * POOL_GROUP
        ext = [sp_ref[j, :, c0:c0 + POOL_GROUP] for j in range(POOL_HIST)]
        ext += [tok(l, OFF_U + c0, POOL_GROUP) for l in range(nl)]
        for l in range(nl):
            acc = ext[POOL_HIST + l]
            for k in range(1, w):
                acc = acc + ext[POOL_HIST + l - k]
            cnt = float(min(w, PAST_LEN + l + 1))
            pooled = acc / cnt - ext[POOL_HIST + l]
            mixed = _dot(pooled.astype(BF16), pw_ref[gi])
            gt = tok(l, OFF_GATE + c0, POOL_GROUP)
            oa_ref[rows(l), c0:c0 + POOL_GROUP] = (
                mixed * ps_ref[:, c0:c0 + POOL_GROUP] * _silu(gt)).astype(BF16)
    for j in range(POOL_HIST):
        src = j + nl
        if src < POOL_HIST:
            npool_ref[j] = sp_ref[src]
        else:
            npool_ref[j] = tok(src - POOL_HIST, OFF_U, D_POOL)

    hist = CONV_WIDTH - 1
    conv_out = {}
    for name, poff, coff, width in (("x", OFF_XS, 0, D_SSM), ("b", OFF_B, D_SSM, D_BC),
                                    ("c", OFF_C, D_SSM + D_BC, D_BC)):
        ext = [scv_ref[j, :, coff:coff + width] for j in range(hist)]
        ext += [tok(l, poff, width) for l in range(nl)]
        outs = []
        for l in range(nl):
            acc = cbias_ref[:, coff:coff + width]
            for k in range(CONV_WIDTH):
                acc = acc + cw_ref[k:k + 1, coff:coff + width] * ext[l + k]
            outs.append(_silu(acc))
        conv_out[name] = outs
        for j in range(hist):
            nconv_ref[j, :, coff:coff + width] = ext[j + nl]
    xs, bs, cs = conv_out["x"], conv_out["b"], conv_out["c"]
    for l in range(nl):
        cs_ref[rows(l), :] = cs[l]
        bs_ref[rows(l), :] = bs[l].astype(BF16)

    a_neg = -jnp.exp(alog_ref[...])
    dt, a_cum = [], []
    run = None
    for l in range(nl):
        d = _softplus(dtr_ref[rows(l), :] + dtb_ref[...])
        dt.append(d)
        run = d * a_neg if run is None else run + d * a_neg
        a_cum.append(run)
    cdec_ref[...] = jnp.exp(a_cum[nl - 1])

    ehot = ehot_ref[...]
    ghot = ghot_ref[...]
    for l in range(nl):
        ydiag = None
        for s in range(l + 1):
            cbh = _dot2(cs[l] * bs[s], ghot)
            gls = cbh * jnp.exp(a_cum[l] - a_cum[s]) * dt[s]
            term = _dot2(gls, ehot) * xs[s]
            ydiag = term if ydiag is None else ydiag + term
        ydx_ref[rows(l), :] = ydiag + dskip_ref[...] * xs[l]
        w_end = jnp.exp(a_cum[nl - 1] - a_cum[l]) * dt[l]
        xw_ref[rows(l), :] = _dot2(w_end, ehot) * xs[l]
        ecum_ref[rows(l), :] = _dot2(jnp.exp(a_cum[l]), ehot)


def _sample1(pj3, dtr3, sp3, scv3, pool_w, pool_scale, conv_w, conv_b, dt_bias_p, a_log_p,
             dskip_row, ehot, ghot, sb):
    nblk = pj3.shape[0]
    db = nblk * sb
    rb = DEC_SEQ * sb
    blk2 = lambda width: pl.BlockSpec((None, rb, width), lambda i: (i, 0, 0))
    hist3 = lambda n, width: pl.BlockSpec((n, sb, width), lambda i: (0, i, 0))
    const2 = lambda shape: pl.BlockSpec(shape, lambda i: (0, 0))
    blk_shape = lambda width, dt: jax.ShapeDtypeStruct((nblk, rb, width), dt)
    return pl.pallas_call(
        functools.partial(_sample1_kernel, sb=sb),
        grid=(nblk,),
        in_specs=[
            blk2(D_MAIN), blk2(LANES), hist3(POOL_HIST, D_POOL), hist3(CONV_WIDTH - 1, D_CONV),
            pl.BlockSpec((len(POOL_WINDOWS), POOL_GROUP, POOL_GROUP), lambda i: (0, 0, 0)),
            const2((1, D_POOL)), const2((CONV_WIDTH, D_CONV)), const2((1, D_CONV)),
            const2((1, LANES)), const2((1, LANES)), const2((1, D_SSM)),
            const2((LANES, D_SSM)), const2((D_BC, LANES)),
        ],
        out_specs=[
            blk2(D_POOL), hist3(POOL_HIST, D_POOL), hist3(CONV_WIDTH - 1, D_CONV),
            blk2(D_BC), blk2(D_BC), blk2(D_SSM), blk2(D_SSM), blk2(D_SSM),
            pl.BlockSpec((sb, LANES), lambda i: (i, 0)),
        ],
        out_shape=[
            blk_shape(D_POOL, BF16),
            jax.ShapeDtypeStruct((POOL_HIST, db, D_POOL), F32),
            jax.ShapeDtypeStruct((CONV_WIDTH - 1, db, D_CONV), F32),
            blk_shape(D_BC, F32), blk_shape(D_BC, BF16),
            blk_shape(D_SSM, F32), blk_shape(D_SSM, F32), blk_shape(D_SSM, F32),
            jax.ShapeDtypeStruct((db, LANES), F32),
        ],
        compiler_params=pltpu.CompilerParams(
            dimension_semantics=("arbitrary",),
            vmem_limit_bytes=VMEM_LIMIT),
        name="sample_elementwise",
    )(pj3, dtr3, sp3, scv3, pool_w, pool_scale, conv_w, conv_b, dt_bias_p, a_log_p, dskip_row,
      ehot, ghot)


def _sample2_setup(xw_ref, cs_ref, xwt_ref, *, sb, per):
    j = pl.program_id(1)

    @pl.when(j == 0)
    def _():
        for blk in range(D_SSM // LANES):
            xwt_ref[blk * LANES:(blk + 1) * LANES, :] = jnp.transpose(
                xw_ref[:, blk * LANES:(blk + 1) * LANES]).astype(BF16)

    return [jnp.concatenate([cs_ref[pl.ds(l * sb + per * j + 2 * pr + bi, 1), :]
                             for bi in range(2) for l in range(DEC_SEQ)], axis=0).astype(BF16)
            for pr in range(per // 2)]


def _sample2_group(g, refs, c_rows, *, sb, per):
    cdec_ref, st_ref, _, bs_ref, _, nst_ref, yo_ref, xwt_ref = refs
    i = pl.program_id(0)
    j = pl.program_id(1)
    nl = DEC_SEQ
    rows = sb * nl
    seq_of_row = lax.broadcasted_iota(jnp.int32, (rows, 1), 0) & (sb - 1)
    gw = HEADS_PER_GROUP * HEAD_DIM
    gs = slice(g * D_STATE, (g + 1) * D_STATE)
    for pr in range(per // 2):
        q0 = per * j + 2 * pr
        c8 = c_rows[pr][:, gs]
        b_blk = bs_ref[:, gs]
        zero_b = jnp.zeros_like(b_blk)
        w2 = jnp.concatenate([jnp.where(seq_of_row == q0, b_blk, zero_b),
                              jnp.where(seq_of_row == q0 + 1, b_blk, zero_b)], axis=1)
        u2 = _dot(xwt_ref[g * gw:(g + 1) * gw, :], w2)
        for bi in range(2):
            sq = 2 * pr + bi
            s0 = st_ref[sq, g * HEADS_PER_GROUP:(g + 1) * HEADS_PER_GROUP].reshape(gw, D_STATE)
            yo = _dot_nt(c8, s0.astype(BF16))
            base = (i * sb + q0 + bi) * HEADS + g * HEADS_PER_GROUP
            for r in range(HEADS_PER_GROUP):
                dec = cdec_ref[base + r]
                rs = slice(r * HEAD_DIM, (r + 1) * HEAD_DIM)
                nst_ref[sq, g * HEADS_PER_GROUP + r] = (
                    s0[rs] * dec + u2[rs, bi * D_STATE:(bi + 1) * D_STATE])
            for l in range(nl):
                yo_ref[pl.ds(l * sb + q0 + bi, 1), g * gw:(g + 1) * gw] = (
                    yo[bi * nl + l:bi * nl + l + 1])


N_SSD_IN, N_SSD_OUT, N_SSD_SCRATCH = 11, 3, 6
N_ST_IN, N_ST_OUT = 5, 2


def _ssd_state_kernel(*refs, nc, sb, per):
    ins, rest = refs[:N_SSD_IN + N_ST_IN], refs[N_SSD_IN + N_ST_IN:]
    outs, scratch = rest[:N_SSD_OUT + N_ST_OUT], rest[N_SSD_OUT + N_ST_OUT:]
    st_refs = (*ins[N_SSD_IN:], *outs[N_SSD_OUT:], *scratch[N_SSD_SCRATCH:])
    c_rows = _sample2_setup(st_refs[2], st_refs[4], st_refs[7], sb=sb, per=per)
    _ssd_body(*ins[:N_SSD_IN], *outs[:N_SSD_OUT], *scratch[:N_SSD_SCRATCH], nc=nc,
              after_group=functools.partial(_sample2_group, refs=st_refs, c_rows=c_rows,
                                            sb=sb, per=per))


def _ssd_prompt_sample_state(proj, dt_raw, conv_w, conv_b, dt_bias_p, a_log_p, dskip_row, norm_g,
                             bsz, seq, cdec_flat, state, xw3, bs3, cs3, sb):
    q = CHUNK
    nc = seq // q
    db = state.shape[0]
    rb = DEC_SEQ * sb
    per = sb // nc
    assert rb == LANES and db // sb == bsz and per * nc == sb and per % 2 == 0
    row = lambda b, c: b * nc + c
    const = lambda b, c: (0, 0)
    blk2 = lambda width: pl.BlockSpec((None, rb, width), lambda i, j: (i, 0, 0))
    st_spec = pl.BlockSpec((per, HEADS, HEAD_DIM, D_STATE), lambda i, j: (i * nc + j, 0, 0, 0))
    return pl.pallas_call(
        functools.partial(_ssd_state_kernel, nc=nc, sb=sb, per=per),
        grid=(bsz, nc),
        in_specs=[
            pl.BlockSpec((q, D_SSM), lambda b, c: (row(b, c), OFF_Z // D_SSM)),
            pl.BlockSpec((q, D_SSM), lambda b, c: (row(b, c), OFF_XS // D_SSM)),
            pl.BlockSpec((q, D_BC), lambda b, c: (row(b, c), OFF_B // D_BC)),
            pl.BlockSpec((q, D_BC), lambda b, c: (row(b, c), OFF_C // D_BC)),
            pl.BlockSpec((q, LANES), lambda b, c: (row(b, c), 0)),
            pl.BlockSpec((CONV_WIDTH, D_CONV), const),
            pl.BlockSpec((1, D_CONV), const),
            pl.BlockSpec((1, LANES), const),
            pl.BlockSpec((1, LANES), const),
            pl.BlockSpec((1, D_SSM), const),
            pl.BlockSpec((1, D_SSM), const),
            pl.BlockSpec(memory_space=pltpu.SMEM),
            st_spec, blk2(D_SSM), blk2(D_BC), blk2(D_BC),
        ],
        out_specs=[
            pl.BlockSpec((q, D_SSM), lambda b, c: (row(b, c), 0)),
            pl.BlockSpec((None, CONV_WIDTH - 1, D_CONV), lambda b, c: (b, 0, 0)),
            pl.BlockSpec((None, HEADS, HEAD_DIM, D_STATE), lambda b, c: (b, 0, 0, 0)),
            st_spec, blk2(D_SSM),
        ],
        out_shape=[
            jax.ShapeDtypeStruct((bsz * seq, D_SSM), BF16),
            jax.ShapeDtypeStruct((bsz, CONV_WIDTH - 1, D_CONV), F32),
            jax.ShapeDtypeStruct((bsz, HEADS, HEAD_DIM, D_STATE), F32),
            jax.ShapeDtypeStruct(state.shape, F32),
            jax.ShapeDtypeStruct((db // sb, rb, D_SSM), F32),
        ],
        scratch_shapes=[
            pltpu.VMEM((D_CONV // LANES, q + SUBLANES, LANES), F32),
            pltpu.VMEM((D_STATE, D_SSM), F32),
            pltpu.VMEM((D_SSM // LANES, q + SUBLANES, LANES), F32),
            pltpu.VMEM((D_BC // LANES, q + SUBLANES, LANES), F32),
            pltpu.VMEM((D_BC // LANES, q + SUBLANES, LANES), F32),
            pltpu.VMEM((q, D_SSM), F32),
            pltpu.VMEM((D_SSM, rb), BF16),
        ],
        compiler_params=pltpu.CompilerParams(
            dimension_semantics=("arbitrary", "arbitrary"),
            vmem_limit_bytes=VMEM_LIMIT),
        name="ssd_prompt_sample_state",
    )(proj, proj, proj, proj, dt_raw, conv_w, conv_b, dt_bias_p, a_log_p, dskip_row, norm_g,
      cdec_flat, state, xw3, bs3, cs3)


def _sample3_kernel(yo_ref, ecum_ref, ydx_ref, z_ref, ng_ref, ob_ref):
    gw = D_SSM // GROUPS
    for g in range(GROUPS):
        cs = slice(g * gw, (g + 1) * gw)
        y = ydx_ref[:, cs] + ecum_ref[:, cs] * yo_ref[:, cs]
        yz = y * _silu(z_ref[:, cs].astype(F32))
        ms = jnp.sum(yz * yz, axis=-1, keepdims=True) * (1.0 / gw)
        ob_ref[:, cs] = (yz * lax.rsqrt(ms + EPS) * ng_ref[:, cs]).astype(BF16)


def _sample3(yo3, ecum3, ydx3, pj3, norm_g):
    nblk, rb, _ = pj3.shape
    blk2 = lambda width: pl.BlockSpec((None, rb, width), lambda i: (i, 0, 0))
    return pl.pallas_call(
        _sample3_kernel,
        grid=(nblk,),
        in_specs=[blk2(D_SSM), blk2(D_SSM), blk2(D_SSM),
                  pl.BlockSpec((None, rb, D_SSM), lambda i: (i, 0, OFF_Z // D_SSM)),
                  pl.BlockSpec((1, D_SSM), lambda i: (0, 0))],
        out_specs=blk2(D_SSM),
        out_shape=jax.ShapeDtypeStruct((nblk, rb, D_SSM), BF16),
        compiler_params=pltpu.CompilerParams(
            dimension_semantics=("arbitrary",),
            vmem_limit_bytes=VMEM_LIMIT),
        name="sample_gate_norm",
    )(yo3, ecum3, ydx3, pj3, norm_g)


def kernel(x_prompt, x_sample, state_pool, state_conv, state_ssm, norm_g, w_in, conv_w, conv_b,
           dt_bias, a_log, d_skip, ssm_norm_g, pool_w, pool_scale, w_out, final_g):
    bsz, seq, _ = x_prompt.shape
    db, nl, _ = x_sample.shape
    assert nl == DEC_SEQ and seq % CHUNK == 0 and w_in.shape[0] == 1

    w_t = jnp.transpose(w_in[0])
    w_a = w_out[0, :D_POOL].astype(BF16)
    w_b = w_out[0, D_POOL:].astype(BF16)
    pool_w_b = pool_w[0].astype(BF16)
    g_in = norm_g[0][None, :]
    g_fin = final_g[None, :]
    ps = pool_scale[0][None, :]
    cw = conv_w[0]
    cbias = conv_b[0][None, :]
    pad_h = lambda v: jnp.pad(v, (0, LANES - HEADS))[None, :]
    dtb = pad_h(dt_bias[0])
    alog = pad_h(a_log[0])
    dskip_row = jnp.repeat(d_skip[0], HEAD_DIM)[None, :]
    ng = ssm_norm_g[0][None, :]

    sb = SAMPLE_BLOCK
    nblk = db // sb
    rb = nl * sb
    xs2 = x_sample.reshape(nblk, sb, nl, D_MODEL).transpose(0, 2, 1, 3).reshape(db * nl, D_MODEL)
    proj_s, dt_s = _inproj(xs2, g_in, w_t)
    pj3 = proj_s.reshape(nblk, rb, D_MAIN)
    head_of_ch = jnp.arange(D_SSM, dtype=jnp.int32) // HEAD_DIM
    ehot = (jnp.arange(LANES, dtype=jnp.int32)[:, None] == head_of_ch[None, :]).astype(BF16)
    grp_of_row = jnp.arange(D_BC, dtype=jnp.int32) // D_STATE
    head_id = jnp.arange(LANES, dtype=jnp.int32)
    ghot = ((head_id[None, :] // HEADS_PER_GROUP == grp_of_row[:, None])
            & (head_id[None, :] < HEADS)).astype(BF16)
    (oa_s, npool_s, nconv_s, cs_s, bs_s, xw_s, ydx_s, ecum_s, cdec_s) = _sample1(
        pj3, dt_s.reshape(nblk, rb, LANES), jnp.transpose(state_pool[0], (1, 0, 2)),
        jnp.transpose(state_conv[0], (1, 0, 2)), pool_w_b, ps, cw, cbias, dtb, alog,
        dskip_row, ehot, ghot, sb)

    xp2 = x_prompt.reshape(bsz * seq, D_MODEL)
    proj_p, dt_p = _inproj(xp2, g_in, w_t)
    oa_p, npool_p = _pool_prompt(proj_p, pool_w_b, ps, bsz, seq)
    ob_p, nconv_p, nssm_p, nssm_s, yo_s = _ssd_prompt_sample_state(
        proj_p, dt_p, cw, cbias, dtb, alog, dskip_row, ng, bsz, seq,
        cdec_s[:, :HEADS].reshape(db * HEADS), state_ssm[0], xw_s, bs_s, cs_s, sb)
    y_p = _outproj(oa_p, ob_p, w_a, w_b, xp2, g_fin).reshape(bsz, seq, D_MODEL)

    ob_s = _sample3(yo_s, ecum_s, ydx_s, pj3, ng)
    y_s = _outproj(oa_s.reshape(db * nl, D_POOL), ob_s.reshape(db * nl, D_SSM), w_a, w_b, xs2,
                   g_fin)
    y_s = y_s.reshape(nblk, nl, sb, D_MODEL).transpose(0, 2, 1, 3).reshape(db, nl, D_MODEL)

    return (y_p, y_s,
            npool_p[None], nconv_p[None], nssm_p[None],
            jnp.transpose(npool_s, (1, 0, 2))[None],
            jnp.transpose(nconv_s, (1, 0, 2))[None],
            nssm_s[None])
```

```python
import functools

import jax
import jax.numpy as jnp
from jax import lax
from jax.experimental import pallas as pl
from jax.experimental.pallas import tpu as pltpu

F32 = jnp.float32
BF16 = jnp.bfloat16

D_MODEL = 2048
D_POOL = 1024
POOL_WINDOWS = (2, 4, 8, 16)
POOL_GROUP = 256
POOL_HIST = 15
D_SSM = 3072
HEAD_DIM = 64
HEADS = 48
GROUPS = 8
HEADS_PER_GROUP = 6
D_STATE = 128
D_BC = GROUPS * D_STATE
CONV_WIDTH = 4
D_CONV = D_SSM + 2 * D_BC
D_MAIN = 2 * D_POOL + D_SSM + D_CONV
PAST_LEN = 16384
DEC_SEQ = 4
EPS = 1e-5

LANES = 128
SUBLANES = 8
VMEM_LIMIT = 56 * 1024 * 1024

OFF_Z = 0
OFF_XS = D_SSM
OFF_U = 2 * D_SSM
OFF_GATE = OFF_U + D_POOL
OFF_B = OFF_GATE + D_POOL
OFF_C = OFF_B + D_BC

CHUNK = 128
SAMPLE_BLOCK = 32
NEG_BIG = -1e30
LOG2E = 1.4426950408889634


def _silu(v):
    h = 0.5 * v
    return h + h * jnp.tanh(h)


def _softplus(v):
    y = jnp.exp(-jnp.abs(v))
    u = 1.0 + y
    d = u - 1.0
    l1p = jnp.where(d == 0.0, y, jnp.log(u) * (y / jnp.where(d == 0.0, 1.0, d)))
    return jnp.maximum(v, 0.0) + l1p


def _split2(v):
    hi = v.astype(BF16)
    lo = (v - hi.astype(F32)).astype(BF16)
    return hi, lo


def _dot(a, b):
    return jnp.dot(a, b, preferred_element_type=F32)


def _dot_nt(a, b):
    return lax.dot_general(a, b, (((1,), (1,)), ((), ())), preferred_element_type=F32)


def _dot2(v, onehot):
    hi, lo = _split2(v)
    return _dot(hi, onehot) + _dot(lo, onehot)


def _inproj_kernel(x_ref, g_ref, w_ref, wdt_ref, o_ref, dt_ref, h_ref):
    @pl.when(pl.program_id(1) == 0)
    def _():
        x = x_ref[...]
        ms = jnp.mean(x * x, axis=-1, keepdims=True)
        h = (x * lax.rsqrt(ms + EPS) * g_ref[...]).astype(BF16)
        h_ref[...] = h
        wrow = lax.broadcasted_iota(jnp.int32, wdt_ref.shape, 0)
        wdt = jnp.where(wrow < HEADS, wdt_ref[...], 0.0).astype(BF16)
        dt_ref[...] = _dot_nt(h, wdt)

    o_ref[...] = _dot_nt(h_ref[...], w_ref[...].astype(BF16)).astype(BF16)


def _src_block(j):
    nz = (D_SSM + D_SSM) // 1024
    npool = 2 * D_POOL // 1024
    return jnp.where(j < nz, j + npool, jnp.where(j < nz + npool, j - nz, j))


def _inproj(x2d, norm_g, w_t):
    m = x2d.shape[0]
    tm = min(1024, m)
    tn = 1024
    return pl.pallas_call(
        _inproj_kernel,
        grid=(m // tm, D_MAIN // tn),
        in_specs=[
            pl.BlockSpec((tm, D_MODEL), lambda i, j: (i, 0)),
            pl.BlockSpec((1, D_MODEL), lambda i, j: (0, 0)),
            pl.BlockSpec((tn, D_MODEL), lambda i, j: (_src_block(j), 0)),
            pl.BlockSpec((LANES, D_MODEL), lambda i, j: (D_MAIN // LANES, 0)),
        ],
        out_specs=[
            pl.BlockSpec((tm, tn), lambda i, j: (i, j)),
            pl.BlockSpec((tm, LANES), lambda i, j: (i, 0)),
        ],
        out_shape=[
            jax.ShapeDtypeStruct((m, D_MAIN), BF16),
            jax.ShapeDtypeStruct((m, LANES), F32),
        ],
        scratch_shapes=[pltpu.VMEM((tm, D_MODEL), BF16)],
        compiler_params=pltpu.CompilerParams(
            dimension_semantics=("arbitrary", "arbitrary"),
            vmem_limit_bytes=VMEM_LIMIT),
        name="inproj",
    )(x2d, norm_g, w_t, w_t)


def _outproj_kernel(a_ref, b_ref, wa_ref, wb_ref, x_ref, g_ref, y_ref):
    acc = _dot(a_ref[...], wa_ref[...]) + _dot(b_ref[...], wb_ref[...])
    r = x_ref[...] + acc
    ms = jnp.mean(r * r, axis=-1, keepdims=True)
    y_ref[...] = r * lax.rsqrt(ms + EPS) * g_ref[...]


def _outproj(out_a, out_b, w_a, w_b, x2d, final_g):
    m = x2d.shape[0]
    tm = min(512, m)
    resident = pl.Buffered(1)
    return pl.pallas_call(
        _outproj_kernel,
        grid=(m // tm,),
        in_specs=[
            pl.BlockSpec((tm, D_POOL), lambda i: (i, 0)),
            pl.BlockSpec((tm, D_SSM), lambda i: (i, 0)),
            pl.BlockSpec((D_POOL, D_MODEL), lambda i: (0, 0), pipeline_mode=resident),
            pl.BlockSpec((D_SSM, D_MODEL), lambda i: (0, 0), pipeline_mode=resident),
            pl.BlockSpec((tm, D_MODEL), lambda i: (i, 0)),
            pl.BlockSpec((1, D_MODEL), lambda i: (0, 0)),
        ],
        out_specs=pl.BlockSpec((tm, D_MODEL), lambda i: (i, 0)),
        out_shape=jax.ShapeDtypeStruct((m, D_MODEL), F32),
        compiler_params=pltpu.CompilerParams(
            dimension_semantics=("arbitrary",),
            vmem_limit_bytes=VMEM_LIMIT),
        name="outproj",
    )(out_a, out_b, w_a, w_b, x2d, final_g)


def _outproj_tail_kernel(a_ref, part_ref, wa_ref, x_ref, g_ref, y_ref):
    r = x_ref[...] + part_ref[...] + _dot(a_ref[...], wa_ref[...])
    ms = jnp.mean(r * r, axis=-1, keepdims=True)
    y_ref[...] = r * lax.rsqrt(ms + EPS) * g_ref[...]


def _outproj_tail(out_a, part_b, w_a, x2d, final_g):
    m = x2d.shape[0]
    tm = min(512, m)
    rowblk = lambda width: pl.BlockSpec((tm, width), lambda i: (i, 0))
    return pl.pallas_call(
        _outproj_tail_kernel,
        grid=(m // tm,),
        in_specs=[
            rowblk(D_POOL), rowblk(D_MODEL),
            pl.BlockSpec((D_POOL, D_MODEL), lambda i: (0, 0), pipeline_mode=pl.Buffered(1)),
            rowblk(D_MODEL),
            pl.BlockSpec((1, D_MODEL), lambda i: (0, 0)),
        ],
        out_specs=rowblk(D_MODEL),
        out_shape=jax.ShapeDtypeStruct((m, D_MODEL), F32),
        compiler_params=pltpu.CompilerParams(
            dimension_semantics=("arbitrary",),
            vmem_limit_bytes=VMEM_LIMIT),
        name="outproj_tail",
    )(out_a, part_b, w_a, x2d, final_g)


def _pool_kernel(u_ref, gate_ref, pw_ref, ps_ref, oa_ref, np_ref, ubuf, *, tl, nt):
    t = pl.program_id(1)
    hist = POOL_HIST + 1

    @pl.when(t == 0)
    def _():
        ubuf[0:hist, :] = jnp.zeros((hist, D_POOL), F32)

    u = u_ref[...].astype(F32)
    ubuf[hist:hist + tl, :] = u
    pos = t * tl + lax.broadcasted_iota(jnp.int32, (tl, 1), 0)
    for gi, w in enumerate(POOL_WINDOWS):
        cs = slice(gi * POOL_GROUP, (gi + 1) * POOL_GROUP)
        ug = u[:, cs]
        acc = ubuf[0:hist + tl, cs]
        span = 1
        while span < w:
            acc = acc + pltpu.roll(acc, span, 0)
            span *= 2
        cnt = jnp.minimum(w, pos + 1).astype(F32)
        pooled = acc[hist:hist + tl] / cnt - ug
        mixed = _dot(pooled.astype(BF16), pw_ref[gi])
        gt = gate_ref[:, cs].astype(F32)
        oa_ref[:, cs] = (mixed * ps_ref[:, cs] * _silu(gt)).astype(BF16)

    ubuf[0:hist, :] = ubuf[tl:tl + hist, :]

    @pl.when(t == nt - 1)
    def _():
        np_ref[...] = ubuf[tl + 1:tl + hist, :]


def _pool_prompt(proj, pool_w, pool_scale, bsz, seq):
    tl = min(512, seq)
    nt = seq // tl
    return pl.pallas_call(
        functools.partial(_pool_kernel, tl=tl, nt=nt),
        grid=(bsz, nt),
        in_specs=[
            pl.BlockSpec((tl, D_POOL), lambda b, t: (b * nt + t, OFF_U // D_POOL)),
            pl.BlockSpec((tl, D_POOL), lambda b, t: (b * nt + t, OFF_GATE // D_POOL)),
            pl.BlockSpec((len(POOL_WINDOWS), POOL_GROUP, POOL_GROUP), lambda b, t: (0, 0, 0)),
            pl.BlockSpec((1, D_POOL), lambda b, t: (0, 0)),
        ],
        out_specs=[
            pl.BlockSpec((tl, D_POOL), lambda b, t: (b * nt + t, 0)),
            pl.BlockSpec((None, POOL_HIST, D_POOL), lambda b, t: (b, 0, 0)),
        ],
        out_shape=[
            jax.ShapeDtypeStruct((bsz * seq, D_POOL), BF16),
            jax.ShapeDtypeStruct((bsz, POOL_HIST, D_POOL), F32),
        ],
        scratch_shapes=[pltpu.VMEM((tl + POOL_HIST + 1, D_POOL), F32)],
        compiler_params=pltpu.CompilerParams(
            dimension_semantics=("arbitrary", "arbitrary"),
            vmem_limit_bytes=VMEM_LIMIT),
        name="pool_prompt",
    )(proj, proj, pool_w, pool_scale)


def _ssd_body(z_ref, xs_ref, b_ref, c_ref, dtr_ref, cw_ref, cbias_ref, dtb_ref, alog_ref,
              dskip_ref, ng_ref,
              ob_ref, nconv_ref, nssm_ref,
              cbuf, st_ref, cvx_ref, cvb_ref, cvc_ref, y_ref, *, c_idx, is_last,
              after_conv_block=None, after_group=None):
    q = CHUNK
    halo = SUBLANES

    nbx, nbb = D_SSM // LANES, D_BC // LANES

    @pl.when(c_idx == 0)
    def _():
        cbuf[:, 0:halo, :] = jnp.zeros((cbuf.shape[0], halo, LANES), F32)
        st_ref[...] = jnp.zeros(st_ref.shape, F32)

    nv = (halo + q) // SUBLANES
    for blk in range(D_CONV // LANES):
        ls = slice(blk * LANES, (blk + 1) * LANES)
        if blk < nbx:
            src_ref, off, dst = xs_ref, blk * LANES, cvx_ref.at[blk]
        elif blk < nbx + nbb:
            src_ref, off, dst = b_ref, (blk - nbx) * LANES, cvb_ref.at[blk - nbx]
        else:
            src_ref, off, dst = c_ref, (blk - nbx - nbb) * LANES, cvc_ref.at[blk - nbx - nbb]
        cbuf[blk, halo:halo + q, :] = src_ref[:, off:off + LANES].astype(F32)
        xv = [cbuf[blk, pl.ds(a, SUBLANES, stride=nv), :] for a in range(nv)]
        wrap = [pltpu.roll(xv[nv - k], 1, 0) for k in range(1, CONV_WIDTH)]
        taps = [cw_ref[k:k + 1, ls] for k in range(CONV_WIDTH)]
        bias = cbias_ref[:, ls]
        for a in range(nv):
            acc = bias + taps[CONV_WIDTH - 1] * xv[a]
            for k in range(1, CONV_WIDTH):
                src = xv[a - k] if a >= k else wrap[k - a - 1]
                acc = acc + taps[CONV_WIDTH - 1 - k] * src
            dst[pl.ds(a, SUBLANES, stride=nv), :] = _silu(acc)
        if after_conv_block is not None:
            after_conv_block(blk)
    rows = slice(halo, halo + q)

    dt = _softplus(dtr_ref[...] + dtb_ref[...])
    a_neg = -jnp.exp(alog_ref[...])
    da = dt * (a_neg * LOG2E)
    row = lax.broadcasted_iota(jnp.int32, (q, LANES), 0)
    a2 = da
    shift = 1
    while shift < q:
        a2 = a2 + jnp.where(row >= shift, pltpu.roll(a2, shift, 0), 0.0)
        shift *= 2
    a2_t = jnp.transpose(a2)
    ldt_t = jnp.log2(jnp.transpose(dt))
    a2_end_t = a2_t[:, q - 1:q]
    w_t = jnp.exp2(a2_end_t - a2_t + ldt_t)
    cdec_t = jnp.exp2(a2_end_t)
    srow_t = a2_t - ldt_t
    ea = jnp.exp2(a2)

    li = lax.broadcasted_iota(jnp.int32, (q, q), 0)
    si = lax.broadcasted_iota(jnp.int32, (q, q), 1)
    tri = li >= si
    lane = lax.broadcasted_iota(jnp.int32, (q, LANES), 1)
    lo_half = lane < HEAD_DIM

    for g in range(GROUPS):
        gs = slice(g * D_STATE, (g + 1) * D_STATE)
        c_gb = cvc_ref[g, rows, :].astype(BF16)
        b_g = cvb_ref[g, rows, :]
        cb = _dot_nt(c_gb, b_g.astype(BF16))
        b_t = jnp.transpose(b_g)
        ppg = HEADS_PER_GROUP // 2
        gl = slice(g * ppg * LANES, (g + 1) * ppg * LANES)
        y_off = _dot(c_gb, st_ref[:, gl].astype(BF16))
        for j in range(ppg):
            blk = g * ppg + j
            ls = slice(blk * LANES, (blk + 1) * LANES)
            sc, bw, ecol, dec = [], [], [], []
            for h in (2 * blk, 2 * blk + 1):
                a_col = jnp.broadcast_to(a2[:, h:h + 1], (q, q))
                decay_dt = jnp.exp2(jnp.where(tri, a_col - srow_t[h:h + 1, :], NEG_BIG))
                sc.append((cb * decay_dt).astype(BF16))
                bw.append((b_t * w_t[h:h + 1, :]).astype(BF16))
                ecol.append(jnp.broadcast_to(ea[:, h:h + 1], (q, LANES)))
                dec.append(jnp.broadcast_to(cdec_t[h:h + 1, :], (q, LANES)))
            x_p = cvx_ref[blk, rows, :].astype(BF16)
            zero_b = jnp.zeros_like(x_p)
            x_bd = jnp.concatenate([jnp.where(lo_half, x_p, zero_b),
                                    jnp.where(lo_half, zero_b, x_p)], axis=0)
            lhs1 = jnp.concatenate([jnp.concatenate(sc, axis=1),
                                    jnp.concatenate(bw, axis=1)], axis=0)
            r1 = _dot(lhs1, x_bd)
            y_ref[:, ls] = (r1[0:q] + y_off[:, j * LANES:(j + 1) * LANES]
                            * jnp.where(lo_half, ecol[0], ecol[1]))
            st_ref[:, ls] = (st_ref[:, ls] * jnp.where(lo_half, dec[0], dec[1])
                             + r1[q:2 * q])
        if after_group is not None:
            after_group(g)

    gw = D_SSM // GROUPS
    bpg = gw // LANES
    for g in range(GROUPS):
        yz, ssq = [], None
        for blk in range(g * bpg, (g + 1) * bpg):
            ls = slice(blk * LANES, (blk + 1) * LANES)
            y = y_ref[:, ls] + cvx_ref[blk, rows, :] * dskip_ref[:, ls]
            v = y * _silu(z_ref[:, ls].astype(F32))
            yz.append(v)
            ssq = v * v if ssq is None else ssq + v * v
        scale = lax.rsqrt(jnp.sum(ssq, axis=-1, keepdims=True) * (1.0 / gw) + EPS)
        for i, blk in enumerate(range(g * bpg, (g + 1) * bpg)):
            ls = slice(blk * LANES, (blk + 1) * LANES)
            ob_ref[:, ls] = (yz[i] * scale * ng_ref[:, ls]).astype(BF16)

    cbuf[:, 0:halo, :] = cbuf[:, q:q + halo, :]

    @pl.when(is_last)
    def _():
        for blk in range(D_CONV // LANES):
            nconv_ref[:, blk * LANES:(blk + 1) * LANES] = cbuf[blk, q + halo - 3:q + halo, :]
        for blk in range(D_SSM // LANES):
            t = jnp.transpose(st_ref[:, blk * LANES:(blk + 1) * LANES])
            nssm_ref[2 * blk:2 * blk + 2] = t.reshape(2, HEAD_DIM, D_STATE)


def _sample1_kernel(pj_ref, dtr_ref, sp_ref, scv_ref, pw_ref, ps_ref, cw_ref, cbias_ref, dtb_ref,
                    alog_ref, dskip_ref, ehot_ref, ghot_ref,
                    oa_ref, npool_ref, nconv_ref, cs_ref, bs_ref, xw_ref, ydx_ref, ecum_ref,
                    cdec_ref, *, sb):
    nl = DEC_SEQ

    def rows(l):
        return slice(l * sb, (l + 1) * sb)

    def tok(l, off, width):
        return pj_ref[rows(l), off:off + width].astype(F32)

    for gi, w in enumerate(POOL_WINDOWS):
        c0 = gi * POOL_GROUP
        ext = [sp_ref[j, :, c0:c0 + POOL_GROUP] for j in range(POOL_HIST)]
        ext += [tok(l, OFF_U + c0, POOL_GROUP) for l in range(nl)]
        for l in range(nl):
            acc = ext[POOL_HIST + l]
            for k in range(1, w):
                acc = acc + ext[POOL_HIST + l - k]
            cnt = float(min(w, PAST_LEN + l + 1))
            pooled = acc / cnt - ext[POOL_HIST + l]
            mixed = _dot(pooled.astype(BF16), pw_ref[gi])
            gt = tok(l, OFF_GATE + c0, POOL_GROUP)
            oa_ref[rows(l), c0:c0 + POOL_GROUP] = (
                mixed * ps_ref[:, c0:c0 + POOL_GROUP] * _silu(gt)).astype(BF16)
    for j in range(POOL_HIST):
        src = j + nl
        if src < POOL_HIST:
            npool_ref[j] = sp_ref[src]
        else:
            npool_ref[j] = tok(src - POOL_HIST, OFF_U, D_POOL)

    hist = CONV_WIDTH - 1
    conv_out = {}
    for name, poff, coff, width in (("x", OFF_XS, 0, D_SSM), ("b", OFF_B, D_SSM, D_BC),
                                    ("c", OFF_C, D_SSM + D_BC, D_BC)):
        ext = [scv_ref[j, :, coff:coff + width] for j in range(hist)]
        ext += [tok(l, poff, width) for l in range(nl)]
        outs = []
        for l in range(nl):
            acc = cbias_ref[:, coff:coff + width]
            for k in range(CONV_WIDTH):
                acc = acc + cw_ref[k:k + 1, coff:coff + width] * ext[l + k]
            outs.append(_silu(acc))
        conv_out[name] = outs
        for j in range(hist):
            nconv_ref[j, :, coff:coff + width] = ext[j + nl]
    xs, bs, cs = conv_out["x"], conv_out["b"], conv_out["c"]
    for l in range(nl):
        cs_ref[rows(l), :] = cs[l]
        bs_ref[rows(l), :] = bs[l].astype(BF16)

    a_neg = -jnp.exp(alog_ref[...])
    dt, a_cum = [], []
    run = None
    for l in range(nl):
        d = _softplus(dtr_ref[rows(l), :] + dtb_ref[...])
        dt.append(d)
        run = d * a_neg if run is None else run + d * a_neg
        a_cum.append(run)
    cdec_ref[...] = jnp.exp(a_cum[nl - 1])

    ehot = ehot_ref[...]
    ghot = ghot_ref[...]
    for l in range(nl):
        ydiag = None
        for s in range(l + 1):
            cbh = _dot2(cs[l] * bs[s], ghot)
            gls = cbh * jnp.exp(a_cum[l] - a_cum[s]) * dt[s]
            term = _dot2(gls, ehot) * xs[s]
            ydiag = term if ydiag is None else ydiag + term
        ydx_ref[rows(l), :] = ydiag + dskip_ref[...] * xs[l]
        w_end = jnp.exp(a_cum[nl - 1] - a_cum[l]) * dt[l]
        xw_ref[rows(l), :] = _dot2(w_end, ehot) * xs[l]
        ecum_ref[rows(l), :] = _dot2(jnp.exp(a_cum[l]), ehot)


def _sample1(pj3, dtr3, sp3, scv3, pool_w, pool_scale, conv_w, conv_b, dt_bias_p, a_log_p,
             dskip_row, ehot, ghot, sb):
    nblk = pj3.shape[0]
    db = nblk * sb
    rb = DEC_SEQ * sb
    blk2 = lambda width: pl.BlockSpec((None, rb, width), lambda i: (i, 0, 0))
    hist3 = lambda n, width: pl.BlockSpec((n, sb, width), lambda i: (0, i, 0))
    const2 = lambda shape: pl.BlockSpec(shape, lambda i: (0, 0))
    blk_shape = lambda width, dt: jax.ShapeDtypeStruct((nblk, rb, width), dt)
    return pl.pallas_call(
        functools.partial(_sample1_kernel, sb=sb),
        grid=(nblk,),
        in_specs=[
            blk2(D_MAIN), blk2(LANES), hist3(POOL_HIST, D_POOL), hist3(CONV_WIDTH - 1, D_CONV),
            pl.BlockSpec((len(POOL_WINDOWS), POOL_GROUP, POOL_GROUP), lambda i: (0, 0, 0)),
            const2((1, D_POOL)), const2((CONV_WIDTH, D_CONV)), const2((1, D_CONV)),
            const2((1, LANES)), const2((1, LANES)), const2((1, D_SSM)),
            const2((LANES, D_SSM)), const2((D_BC, LANES)),
        ],
        out_specs=[
            blk2(D_POOL), hist3(POOL_HIST, D_POOL), hist3(CONV_WIDTH - 1, D_CONV),
            blk2(D_BC), blk2(D_BC), blk2(D_SSM), blk2(D_SSM), blk2(D_SSM),
            pl.BlockSpec((sb, LANES), lambda i: (i, 0)),
        ],
        out_shape=[
            blk_shape(D_POOL, BF16),
            jax.ShapeDtypeStruct((POOL_HIST, db, D_POOL), F32),
            jax.ShapeDtypeStruct((CONV_WIDTH - 1, db, D_CONV), F32),
            blk_shape(D_BC, F32), blk_shape(D_BC, BF16),
            blk_shape(D_SSM, F32), blk_shape(D_SSM, F32), blk_shape(D_SSM, F32),
            jax.ShapeDtypeStruct((db, LANES), F32),
        ],
        compiler_params=pltpu.CompilerParams(
            dimension_semantics=("arbitrary",),
            vmem_limit_bytes=VMEM_LIMIT),
        name="sample_elementwise",
    )(pj3, dtr3, sp3, scv3, pool_w, pool_scale, conv_w, conv_b, dt_bias_p, a_log_p, dskip_row,
      ehot, ghot)


def _sample2_setup(xw_ref, cs_ref, xwt_ref, *, j, sb, per):
    @pl.when(j == 0)
    def _():
        for blk in range(D_SSM // LANES):
            xwt_ref[blk * LANES:(blk + 1) * LANES, :] = jnp.transpose(
                xw_ref[:, blk * LANES:(blk + 1) * LANES]).astype(BF16)

    return [jnp.concatenate([cs_ref[pl.ds(l * sb + per * j + 2 * pr + bi, 1), :]
                             for bi in range(2) for l in range(DEC_SEQ)], axis=0).astype(BF16)
            for pr in range(per // 2)]


def _sample2_group(g, refs, c_rows, *, i, j, sb, per):
    cdec_ref, st_ref, _, bs_ref, _, nst_ref, yo_ref, xwt_ref = refs
    nl = DEC_SEQ
    rows = sb * nl
    seq_of_row = lax.broadcasted_iota(jnp.int32, (rows, 1), 0) & (sb - 1)
    gw = HEADS_PER_GROUP * HEAD_DIM
    gs = slice(g * D_STATE, (g + 1) * D_STATE)
    for pr in range(per // 2):
        q0 = per * j + 2 * pr
        c8 = c_rows[pr][:, gs]
        b_blk = bs_ref[:, gs]
        zero_b = jnp.zeros_like(b_blk)
        w2 = jnp.concatenate([jnp.where(seq_of_row == q0, b_blk, zero_b),
                              jnp.where(seq_of_row == q0 + 1, b_blk, zero_b)], axis=1)
        u2 = _dot(xwt_ref[g * gw:(g + 1) * gw, :], w2)
        for bi in range(2):
            sq = 2 * pr + bi
            s0 = st_ref[sq, g * HEADS_PER_GROUP:(g + 1) * HEADS_PER_GROUP].reshape(gw, D_STATE)
            yo = _dot_nt(c8, s0.astype(BF16))
            base = (i * sb + q0 + bi) * HEADS + g * HEADS_PER_GROUP
            for r in range(HEADS_PER_GROUP):
                dec = cdec_ref[base + r]
                rs = slice(r * HEAD_DIM, (r + 1) * HEAD_DIM)
                nst_ref[sq, g * HEADS_PER_GROUP + r] = (
                    s0[rs] * dec + u2[rs, bi * D_STATE:(bi + 1) * D_STATE])
            for l in range(nl):
                yo_ref[pl.ds(l * sb + q0 + bi, 1), g * gw:(g + 1) * gw] = (
                    yo[bi * nl + l:bi * nl + l + 1])


N_SSD_IN, N_ST_IN = 11, 5
OUTPROJ_SLICES = 8
OUTPROJ_SLICE = D_MODEL // OUTPROJ_SLICES


def _ssd_state_kernel(*refs, nc, nsteps, sb, per):
    ssd_in, st_in, wb_ref = refs[:N_SSD_IN], refs[N_SSD_IN:N_SSD_IN + N_ST_IN], refs[N_SSD_IN + N_ST_IN]
    (nconv_ref, nssm_ref, nst_ref, yo_ref, part_ref,
     cbuf, st_ref, cvx_ref, cvb_ref, cvc_ref, y_ref, xwt_ref, ob_ref) = refs[N_SSD_IN + N_ST_IN + 1:]
    s = pl.program_id(0)
    sc = jnp.minimum(s, nsteps - 1)
    blk_i, c_idx = sc // nc, sc % nc

    @pl.when(s == 0)
    def _():
        ob_ref[...] = jnp.zeros(ob_ref.shape, BF16)

    st_refs = (*st_in, nst_ref, yo_ref, xwt_ref)
    c_rows = _sample2_setup(st_in[2], st_in[4], xwt_ref, j=c_idx, sb=sb, per=per)

    conv_every = (D_CONV // LANES) // OUTPROJ_SLICES

    def after_conv_block(blk):
        if blk % conv_every == conv_every - 1:
            k = blk // conv_every
            cols = slice(k * OUTPROJ_SLICE, (k + 1) * OUTPROJ_SLICE)
            part_ref[:, cols] = _dot(ob_ref[...], wb_ref[:, cols])

    _ssd_body(*ssd_in, ob_ref, nconv_ref, nssm_ref, cbuf, st_ref, cvx_ref, cvb_ref, cvc_ref, y_ref,
              c_idx=c_idx, is_last=jnp.logical_and(c_idx == nc - 1, s < nsteps),
              after_conv_block=after_conv_block,
              after_group=functools.partial(_sample2_group, refs=st_refs, c_rows=c_rows,
                                            i=blk_i, j=c_idx, sb=sb, per=per))


def _ssd_prompt_sample_state(proj, dt_raw, conv_w, conv_b, dt_bias_p, a_log_p, dskip_row, norm_g,
                             bsz, seq, cdec_flat, state, xw3, bs3, cs3, sb, w_b):
    q = CHUNK
    nc = seq // q
    nsteps = bsz * nc
    db = state.shape[0]
    rb = DEC_SEQ * sb
    per = sb // nc
    assert rb == LANES and db // sb == bsz and per * nc == sb and per % 2 == 0
    cur = lambda s: jnp.minimum(s, nsteps - 1)
    prv = lambda s: jnp.maximum(s - 1, 0)
    const = lambda s: (0, 0)
    blk2 = lambda width: pl.BlockSpec((None, rb, width), lambda s: (cur(s) // nc, 0, 0))
    st_spec = pl.BlockSpec((per, HEADS, HEAD_DIM, D_STATE), lambda s: (cur(s), 0, 0, 0))
    return pl.pallas_call(
        functools.partial(_ssd_state_kernel, nc=nc, nsteps=nsteps, sb=sb, per=per),
        grid=(nsteps + 1,),
        in_specs=[
            pl.BlockSpec((q, D_SSM), lambda s: (cur(s), OFF_Z // D_SSM)),
            pl.BlockSpec((q, D_SSM), lambda s: (cur(s), OFF_XS // D_SSM)),
            pl.BlockSpec((q, D_BC), lambda s: (cur(s), OFF_B // D_BC)),
            pl.BlockSpec((q, D_BC), lambda s: (cur(s), OFF_C // D_BC)),
            pl.BlockSpec((q, LANES), lambda s: (cur(s), 0)),
            pl.BlockSpec((CONV_WIDTH, D_CONV), const),
            pl.BlockSpec((1, D_CONV), const),
            pl.BlockSpec((1, LANES), const),
            pl.BlockSpec((1, LANES), const),
            pl.BlockSpec((1, D_SSM), const),
            pl.BlockSpec((1, D_SSM), const),
            pl.BlockSpec(memory_space=pltpu.SMEM),
            st_spec, blk2(D_SSM), blk2(D_BC), blk2(D_BC),
            pl.BlockSpec((D_SSM, D_MODEL), const, pipeline_mode=pl.Buffered(1)),
        ],
        out_specs=[
            pl.BlockSpec((None, CONV_WIDTH - 1, D_CONV), lambda s: (cur(s) // nc, 0, 0)),
            pl.BlockSpec((None, HEADS, HEAD_DIM, D_STATE), lambda s: (cur(s) // nc, 0, 0, 0)),
            st_spec, blk2(D_SSM),
            pl.BlockSpec((q, D_MODEL), lambda s: (prv(s), 0)),
        ],
        out_shape=[
            jax.ShapeDtypeStruct((bsz, CONV_WIDTH - 1, D_CONV), F32),
            jax.ShapeDtypeStruct((bsz, HEADS, HEAD_DIM, D_STATE), F32),
            jax.ShapeDtypeStruct(state.shape, F32),
            jax.ShapeDtypeStruct((db // sb, rb, D_SSM), F32),
            jax.ShapeDtypeStruct((bsz * seq, D_MODEL), F32),
        ],
        scratch_shapes=[
            pltpu.VMEM((D_CONV // LANES, q + SUBLANES, LANES), F32),
            pltpu.VMEM((D_STATE, D_SSM), F32),
            pltpu.VMEM((D_SSM // LANES, q + SUBLANES, LANES), F32),
            pltpu.VMEM((D_BC // LANES, q + SUBLANES, LANES), F32),
            pltpu.VMEM((D_BC // LANES, q + SUBLANES, LANES), F32),
            pltpu.VMEM((q, D_SSM), F32),
            pltpu.VMEM((D_SSM, rb), BF16),
            pltpu.VMEM((q, D_SSM), BF16),
        ],
        compiler_params=pltpu.CompilerParams(
            dimension_semantics=("arbitrary",),
            vmem_limit_bytes=VMEM_LIMIT),
        name="ssd_prompt_sample_state",
    )(proj, proj, proj, proj, dt_raw, conv_w, conv_b, dt_bias_p, a_log_p, dskip_row, norm_g,
      cdec_flat, state, xw3, bs3, cs3, w_b)


def _sample3_kernel(yo_ref, ecum_ref, ydx_ref, z_ref, ng_ref, ob_ref):
    gw = D_SSM // GROUPS
    for g in range(GROUPS):
        cs = slice(g * gw, (g + 1) * gw)
        y = ydx_ref[:, cs] + ecum_ref[:, cs] * yo_ref[:, cs]
        yz = y * _silu(z_ref[:, cs].astype(F32))
        ms = jnp.sum(yz * yz, axis=-1, keepdims=True) * (1.0 / gw)
        ob_ref[:, cs] = (yz * lax.rsqrt(ms + EPS) * ng_ref[:, cs]).astype(BF16)


def _sample3(yo3, ecum3, ydx3, pj3, norm_g):
    nblk, rb, _ = pj3.shape
    blk2 = lambda width: pl.BlockSpec((None, rb, width), lambda i: (i, 0, 0))
    return pl.pallas_call(
        _sample3_kernel,
        grid=(nblk,),
        in_specs=[blk2(D_SSM), blk2(D_SSM), blk2(D_SSM),
                  pl.BlockSpec((None, rb, D_SSM), lambda i: (i, 0, OFF_Z // D_SSM)),
                  pl.BlockSpec((1, D_SSM), lambda i: (0, 0))],
        out_specs=blk2(D_SSM),
        out_shape=jax.ShapeDtypeStruct((nblk, rb, D_SSM), BF16),
        compiler_params=pltpu.CompilerParams(
            dimension_semantics=("arbitrary",),
            vmem_limit_bytes=VMEM_LIMIT),
        name="sample_gate_norm",
    )(yo3, ecum3, ydx3, pj3, norm_g)


def kernel(x_prompt, x_sample, state_pool, state_conv, state_ssm, norm_g, w_in, conv_w, conv_b,
           dt_bias, a_log, d_skip, ssm_norm_g, pool_w, pool_scale, w_out, final_g):
    bsz, seq, _ = x_prompt.shape
    db, nl, _ = x_sample.shape
    assert nl == DEC_SEQ and seq % CHUNK == 0 and w_in.shape[0] == 1

    w_t = jnp.transpose(w_in[0])
    w_a = w_out[0, :D_POOL].astype(BF16)
    w_b = w_out[0, D_POOL:].astype(BF16)
    pool_w_b = pool_w[0].astype(BF16)
    g_in = norm_g[0][None, :]
    g_fin = final_g[None, :]
    ps = pool_scale[0][None, :]
    cw = conv_w[0]
    cbias = conv_b[0][None, :]
    pad_h = lambda v: jnp.pad(v, (0, LANES - HEADS))[None, :]
    dtb = pad_h(dt_bias[0])
    alog = pad_h(a_log[0])
    dskip_row = jnp.repeat(d_skip[0], HEAD_DIM)[None, :]
    ng = ssm_norm_g[0][None, :]

    sb = SAMPLE_BLOCK
    nblk = db // sb
    rb = nl * sb
    xs2 = x_sample.reshape(nblk, sb, nl, D_MODEL).transpose(0, 2, 1, 3).reshape(db * nl, D_MODEL)
    proj_s, dt_s = _inproj(xs2, g_in, w_t)
    pj3 = proj_s.reshape(nblk, rb, D_MAIN)
    head_of_ch = jnp.arange(D_SSM, dtype=jnp.int32) // HEAD_DIM
    ehot = (jnp.arange(LANES, dtype=jnp.int32)[:, None] == head_of_ch[None, :]).astype(BF16)
    grp_of_row = jnp.arange(D_BC, dtype=jnp.int32) // D_STATE
    head_id = jnp.arange(LANES, dtype=jnp.int32)
    ghot = ((head_id[None, :] // HEADS_PER_GROUP == grp_of_row[:, None])
            & (head_id[None, :] < HEADS)).astype(BF16)
    (oa_s, npool_s, nconv_s, cs_s, bs_s, xw_s, ydx_s, ecum_s, cdec_s) = _sample1(
        pj3, dt_s.reshape(nblk, rb, LANES), jnp.transpose(state_pool[0], (1, 0, 2)),
        jnp.transpose(state_conv[0], (1, 0, 2)), pool_w_b, ps, cw, cbias, dtb, alog,
        dskip_row, ehot, ghot, sb)

    xp2 = x_prompt.reshape(bsz * seq, D_MODEL)
    proj_p, dt_p = _inproj(xp2, g_in, w_t)
    oa_p, npool_p = _pool_prompt(proj_p, pool_w_b, ps, bsz, seq)
    nconv_p, nssm_p, nssm_s, yo_s, part_p = _ssd_prompt_sample_state(
        proj_p, dt_p, cw, cbias, dtb, alog, dskip_row, ng, bsz, seq,
        cdec_s[:, :HEADS].reshape(db * HEADS), state_ssm[0], xw_s, bs_s, cs_s, sb, w_b)
    y_p = _outproj_tail(oa_p, part_p, w_a, xp2, g_fin).reshape(bsz, seq, D_MODEL)

    ob_s = _sample3(yo_s, ecum_s, ydx_s, pj3, ng)
    y_s = _outproj(oa_s.reshape(db * nl, D_POOL), ob_s.reshape(db * nl, D_SSM), w_a, w_b, xs2,
                   g_fin)
    y_s = y_s.reshape(nblk, nl, sb, D_MODEL).transpose(0, 2, 1, 3).reshape(db, nl, D_MODEL)

    return (y_p, y_s,
            npool_p[None], nconv_p[None], nssm_p[None],
            jnp.transpose(npool_s, (1, 0, 2))[None],
            jnp.transpose(nconv_s, (1, 0, 2))[None],
            nssm_s[None])
```

```python
import functools

import jax
import jax.numpy as jnp
from jax import lax
from jax.experimental import pallas as pl
from jax.experimental.pallas import tpu as pltpu

F32 = jnp.float32
BF16 = jnp.bfloat16

D_MODEL = 2048
D_POOL = 1024
POOL_WINDOWS = (2, 4, 8, 16)
POOL_GROUP = 256
POOL_HIST = 15
D_SSM = 3072
HEAD_DIM = 64
HEADS = 48
GROUPS = 8
HEADS_PER_GROUP = 6
D_STATE = 128
D_BC = GROUPS * D_STATE
CONV_WIDTH = 4
D_CONV = D_SSM + 2 * D_BC
D_MAIN = 2 * D_POOL + D_SSM + D_CONV
PAST_LEN = 16384
DEC_SEQ = 4
EPS = 1e-5

LANES = 128
SUBLANES = 8
VMEM_LIMIT = 56 * 1024 * 1024

OFF_Z = 0
OFF_XS = D_SSM
OFF_U = 2 * D_SSM
OFF_GATE = OFF_U + D_POOL
OFF_B = OFF_GATE + D_POOL
OFF_C = OFF_B + D_BC

PROMPT_TN = 2048
CHUNK = 128
SAMPLE_BLOCK = 32
NEG_BIG = -1e30
LOG2E = 1.4426950408889634


def _silu(v):
    h = 0.5 * v
    return h + h * jnp.tanh(h)


def _softplus(v):
    y = jnp.exp(-jnp.abs(v))
    u = 1.0 + y
    d = u - 1.0
    l1p = jnp.where(d == 0.0, y, jnp.log(u) * (y / jnp.where(d == 0.0, 1.0, d)))
    return jnp.maximum(v, 0.0) + l1p


def _split2(v):
    hi = v.astype(BF16)
    lo = (v - hi.astype(F32)).astype(BF16)
    return hi, lo


def _dot(a, b):
    return jnp.dot(a, b, preferred_element_type=F32)


def _dot_nt(a, b):
    return lax.dot_general(a, b, (((1,), (1,)), ((), ())), preferred_element_type=F32)


def _dot2(v, onehot):
    hi, lo = _split2(v)
    return _dot(hi, onehot) + _dot(lo, onehot)


def _inproj_kernel(x_ref, g_ref, w_ref, wdt_ref, o_ref, dt_ref, *rest):
    from_f32 = w_ref.dtype == F32
    (wb_out_ref, wdt_out_ref, h_ref) = rest if from_f32 else (None, None, *rest)

    @pl.when(pl.program_id(1) == 0)
    def _():
        x = x_ref[...]
        ms = jnp.mean(x * x, axis=-1, keepdims=True)
        h = (x * lax.rsqrt(ms + EPS) * g_ref[...]).astype(BF16)
        h_ref[...] = h
        if from_f32:
            wrow = lax.broadcasted_iota(jnp.int32, wdt_ref.shape, 0)
            wdt = jnp.where(wrow < HEADS, wdt_ref[...], 0.0).astype(BF16)
            wdt_out_ref[...] = wdt
        else:
            wdt = wdt_ref[...]
        dt_ref[...] = _dot_nt(h, wdt)

    if from_f32:
        wb = w_ref[...].astype(BF16)
        wb_out_ref[...] = wb
    else:
        wb = w_ref[...]
    o_ref[...] = _dot_nt(h_ref[...], wb).astype(BF16)


def _src_block(j):
    nz = (D_SSM + D_SSM) // 1024
    npool = 2 * D_POOL // 1024
    return jnp.where(j < nz, j + npool, jnp.where(j < nz + npool, j - nz, j))


def _inproj(x2d, norm_g, w, w_dt, tn):
    m = x2d.shape[0]
    tm = min(1024, m)
    from_f32 = w.dtype == F32
    assert tn == 1024 or not from_f32
    w_map = (lambda i, j: (_src_block(j), 0)) if from_f32 else (lambda i, j: (j, 0))
    wdt_map = (lambda i, j: (D_MAIN // LANES, 0)) if from_f32 else (lambda i, j: (0, 0))
    out_specs = [
        pl.BlockSpec((tm, tn), lambda i, j: (i, j)),
        pl.BlockSpec((tm, LANES), lambda i, j: (i, 0)),
    ]
    out_shape = [
        jax.ShapeDtypeStruct((m, D_MAIN), BF16),
        jax.ShapeDtypeStruct((m, LANES), F32),
    ]
    if from_f32:
        assert m == tm
        out_specs += [pl.BlockSpec((tn, D_MODEL), lambda i, j: (j, 0)),
                      pl.BlockSpec((LANES, D_MODEL), lambda i, j: (0, 0))]
        out_shape += [jax.ShapeDtypeStruct((D_MAIN, D_MODEL), BF16),
                      jax.ShapeDtypeStruct((LANES, D_MODEL), BF16)]
    return pl.pallas_call(
        _inproj_kernel,
        grid=(m // tm, D_MAIN // tn),
        in_specs=[
            pl.BlockSpec((tm, D_MODEL), lambda i, j: (i, 0)),
            pl.BlockSpec((1, D_MODEL), lambda i, j: (0, 0)),
            pl.BlockSpec((tn, D_MODEL), w_map),
            pl.BlockSpec((LANES, D_MODEL), wdt_map),
        ],
        out_specs=out_specs,
        out_shape=out_shape,
        scratch_shapes=[pltpu.VMEM((tm, D_MODEL), BF16)],
        compiler_params=pltpu.CompilerParams(
            dimension_semantics=("arbitrary", "arbitrary"),
            vmem_limit_bytes=VMEM_LIMIT),
        name="inproj",
    )(x2d, norm_g, w, w_dt)


def _outproj_kernel(a_ref, b_ref, wa_ref, wb_ref, x_ref, g_ref, y_ref):
    acc = _dot(a_ref[...], wa_ref[...]) + _dot(b_ref[...], wb_ref[...])
    r = x_ref[...] + acc
    ms = jnp.mean(r * r, axis=-1, keepdims=True)
    y_ref[...] = r * lax.rsqrt(ms + EPS) * g_ref[...]


def _outproj(out_a, out_b, w_a, w_b, x2d, final_g):
    m = x2d.shape[0]
    tm = min(512, m)
    resident = pl.Buffered(1)
    return pl.pallas_call(
        _outproj_kernel,
        grid=(m // tm,),
        in_specs=[
            pl.BlockSpec((tm, D_POOL), lambda i: (i, 0)),
            pl.BlockSpec((tm, D_SSM), lambda i: (i, 0)),
            pl.BlockSpec((D_POOL, D_MODEL), lambda i: (0, 0), pipeline_mode=resident),
            pl.BlockSpec((D_SSM, D_MODEL), lambda i: (0, 0), pipeline_mode=resident),
            pl.BlockSpec((tm, D_MODEL), lambda i: (i, 0)),
            pl.BlockSpec((1, D_MODEL), lambda i: (0, 0)),
        ],
        out_specs=pl.BlockSpec((tm, D_MODEL), lambda i: (i, 0)),
        out_shape=jax.ShapeDtypeStruct((m, D_MODEL), F32),
        compiler_params=pltpu.CompilerParams(
            dimension_semantics=("arbitrary",),
            vmem_limit_bytes=VMEM_LIMIT),
        name="outproj",
    )(out_a, out_b, w_a, w_b, x2d, final_g)


def _pool_kernel(u_ref, gate_ref, pw_ref, ps_ref, oa_ref, np_ref, ubuf, *, tl, nt):
    t = pl.program_id(1)
    hist = POOL_HIST + 1

    @pl.when(t == 0)
    def _():
        ubuf[0:hist, :] = jnp.zeros((hist, D_POOL), F32)

    u = u_ref[...].astype(F32)
    ubuf[hist:hist + tl, :] = u
    pos = t * tl + lax.broadcasted_iota(jnp.int32, (tl, 1), 0)
    for gi, w in enumerate(POOL_WINDOWS):
        cs = slice(gi * POOL_GROUP, (gi + 1) * POOL_GROUP)
        ug = u[:, cs]
        acc = ubuf[0:hist + tl, cs]
        span = 1
        while span < w:
            acc = acc + pltpu.roll(acc, span, 0)
            span *= 2
        cnt = jnp.minimum(w, pos + 1).astype(F32)
        pooled = acc[hist:hist + tl] / cnt - ug
        mixed = _dot(pooled.astype(BF16), pw_ref[gi])
        gt = gate_ref[:, cs].astype(F32)
        oa_ref[:, cs] = (mixed * ps_ref[:, cs] * _silu(gt)).astype(BF16)

    ubuf[0:hist, :] = ubuf[tl:tl + hist, :]

    @pl.when(t == nt - 1)
    def _():
        np_ref[...] = ubuf[tl + 1:tl + hist, :]


def _pool_prompt(proj, pool_w, pool_scale, bsz, seq):
    tl = min(512, seq)
    nt = seq // tl
    return pl.pallas_call(
        functools.partial(_pool_kernel, tl=tl, nt=nt),
        grid=(bsz, nt),
        in_specs=[
            pl.BlockSpec((tl, D_POOL), lambda b, t: (b * nt + t, OFF_U // D_POOL)),
            pl.BlockSpec((tl, D_POOL), lambda b, t: (b * nt + t, OFF_GATE // D_POOL)),
            pl.BlockSpec((len(POOL_WINDOWS), POOL_GROUP, POOL_GROUP), lambda b, t: (0, 0, 0)),
            pl.BlockSpec((1, D_POOL), lambda b, t: (0, 0)),
        ],
        out_specs=[
            pl.BlockSpec((tl, D_POOL), lambda b, t: (b * nt + t, 0)),
            pl.BlockSpec((None, POOL_HIST, D_POOL), lambda b, t: (b, 0, 0)),
        ],
        out_shape=[
            jax.ShapeDtypeStruct((bsz * seq, D_POOL), BF16),
            jax.ShapeDtypeStruct((bsz, POOL_HIST, D_POOL), F32),
        ],
        scratch_shapes=[pltpu.VMEM((tl + POOL_HIST + 1, D_POOL), F32)],
        compiler_params=pltpu.CompilerParams(
            dimension_semantics=("arbitrary", "arbitrary"),
            vmem_limit_bytes=VMEM_LIMIT),
        name="pool_prompt",
    )(proj, proj, pool_w, pool_scale)


def _ssd_body(z_ref, xs_ref, b_ref, c_ref, dtr_ref, cw_ref, cbias_ref, dtb_ref, alog_ref,
              dskip_ref, ng_ref,
              ob_ref, nconv_ref, nssm_ref,
              cbuf, st_ref, cvx_ref, cvb_ref, cvc_ref, y_ref, *, nc, after_group=None):
    q = CHUNK
    c_idx = pl.program_id(1)
    halo = SUBLANES

    nbx, nbb = D_SSM // LANES, D_BC // LANES

    @pl.when(c_idx == 0)
    def _():
        cbuf[:, 0:halo, :] = jnp.zeros((cbuf.shape[0], halo, LANES), F32)
        st_ref[...] = jnp.zeros(st_ref.shape, F32)

    nv = (halo + q) // SUBLANES
    for blk in range(D_CONV // LANES):
        ls = slice(blk * LANES, (blk + 1) * LANES)
        if blk < nbx:
            src_ref, off, dst = xs_ref, blk * LANES, cvx_ref.at[blk]
        elif blk < nbx + nbb:
            src_ref, off, dst = b_ref, (blk - nbx) * LANES, cvb_ref.at[blk - nbx]
        else:
            src_ref, off, dst = c_ref, (blk - nbx - nbb) * LANES, cvc_ref.at[blk - nbx - nbb]
        cbuf[blk, halo:halo + q, :] = src_ref[:, off:off + LANES].astype(F32)
        xv = [cbuf[blk, pl.ds(a, SUBLANES, stride=nv), :] for a in range(nv)]
        wrap = [pltpu.roll(xv[nv - k], 1, 0) for k in range(1, CONV_WIDTH)]
        taps = [cw_ref[k:k + 1, ls] for k in range(CONV_WIDTH)]
        bias = cbias_ref[:, ls]
        for a in range(nv):
            acc = bias + taps[CONV_WIDTH - 1] * xv[a]
            for k in range(1, CONV_WIDTH):
                src = xv[a - k] if a >= k else wrap[k - a - 1]
                acc = acc + taps[CONV_WIDTH - 1 - k] * src
            dst[pl.ds(a, SUBLANES, stride=nv), :] = _silu(acc)
    rows = slice(halo, halo + q)

    dt = _softplus(dtr_ref[...] + dtb_ref[...])
    a_neg = -jnp.exp(alog_ref[...])
    da = dt * (a_neg * LOG2E)
    row = lax.broadcasted_iota(jnp.int32, (q, LANES), 0)
    a2 = da
    shift = 1
    while shift < q:
        a2 = a2 + jnp.where(row >= shift, pltpu.roll(a2, shift, 0), 0.0)
        shift *= 2
    a2_t = jnp.transpose(a2)
    ldt_t = jnp.log2(jnp.transpose(dt))
    a2_end_t = a2_t[:, q - 1:q]
    w_t = jnp.exp2(a2_end_t - a2_t + ldt_t)
    cdec_t = jnp.exp2(a2_end_t)
    srow_t = a2_t - ldt_t
    ea = jnp.exp2(a2)

    li = lax.broadcasted_iota(jnp.int32, (q, q), 0)
    si = lax.broadcasted_iota(jnp.int32, (q, q), 1)
    tri = li >= si
    lane = lax.broadcasted_iota(jnp.int32, (q, LANES), 1)
    lo_half = lane < HEAD_DIM

    for g in range(GROUPS):
        gs = slice(g * D_STATE, (g + 1) * D_STATE)
        c_gb = cvc_ref[g, rows, :].astype(BF16)
        b_g = cvb_ref[g, rows, :]
        cb = _dot_nt(c_gb, b_g.astype(BF16))
        b_t = jnp.transpose(b_g)
        ppg = HEADS_PER_GROUP // 2
        gl = slice(g * ppg * LANES, (g + 1) * ppg * LANES)
        y_off = _dot(c_gb, st_ref[:, gl].astype(BF16))
        for j in range(ppg):
            blk = g * ppg + j
            ls = slice(blk * LANES, (blk + 1) * LANES)
            sc, bw, ecol, dec = [], [], [], []
            for h in (2 * blk, 2 * blk + 1):
                a_col = jnp.broadcast_to(a2[:, h:h + 1], (q, q))
                decay_dt = jnp.exp2(jnp.where(tri, a_col - srow_t[h:h + 1, :], NEG_BIG))
                sc.append((cb * decay_dt).astype(BF16))
                bw.append((b_t * w_t[h:h + 1, :]).astype(BF16))
                ecol.append(jnp.broadcast_to(ea[:, h:h + 1], (q, LANES)))
                dec.append(jnp.broadcast_to(cdec_t[h:h + 1, :], (q, LANES)))
            x_p = cvx_ref[blk, rows, :].astype(BF16)
            zero_b = jnp.zeros_like(x_p)
            x_bd = jnp.concatenate([jnp.where(lo_half, x_p, zero_b),
                                    jnp.where(lo_half, zero_b, x_p)], axis=0)
            lhs1 = jnp.concatenate([jnp.concatenate(sc, axis=1),
                                    jnp.concatenate(bw, axis=1)], axis=0)
            r1 = _dot(lhs1, x_bd)
            y_ref[:, ls] = (r1[0:q] + y_off[:, j * LANES:(j + 1) * LANES]
                            * jnp.where(lo_half, ecol[0], ecol[1]))
            st_ref[:, ls] = (st_ref[:, ls] * jnp.where(lo_half, dec[0], dec[1])
                             + r1[q:2 * q])
        if after_group is not None:
            after_group(g)

    gw = D_SSM // GROUPS
    bpg = gw // LANES
    for g in range(GROUPS):
        yz, ssq = [], None
        for blk in range(g * bpg, (g + 1) * bpg):
            ls = slice(blk * LANES, (blk + 1) * LANES)
            y = y_ref[:, ls] + cvx_ref[blk, rows, :] * dskip_ref[:, ls]
            v = y * _silu(z_ref[:, ls].astype(F32))
            yz.append(v)
            ssq = v * v if ssq is None else ssq + v * v
        scale = lax.rsqrt(jnp.sum(ssq, axis=-1, keepdims=True) * (1.0 / gw) + EPS)
        for i, blk in enumerate(range(g * bpg, (g + 1) * bpg)):
            ls = slice(blk * LANES, (blk + 1) * LANES)
            ob_ref[:, ls] = (yz[i] * scale * ng_ref[:, ls]).astype(BF16)

    cbuf[:, 0:halo, :] = cbuf[:, q:q + halo, :]

    @pl.when(c_idx == nc - 1)
    def _():
        for blk in range(D_CONV // LANES):
            nconv_ref[:, blk * LANES:(blk + 1) * LANES] = cbuf[blk, q + halo - 3:q + halo, :]
        for blk in range(D_SSM // LANES):
            t = jnp.transpose(st_ref[:, blk * LANES:(blk + 1) * LANES])
            nssm_ref[2 * blk:2 * blk + 2] = t.reshape(2, HEAD_DIM, D_STATE)


def _sample1_kernel(pj_ref, dtr_ref, sp_ref, scv_ref, pw_ref, ps_ref, cw_ref, cbias_ref, dtb_ref,
                    alog_ref, dskip_ref, ehot_ref, ghot_ref,
                    oa_ref, npool_ref, nconv_ref, cs_ref, bs_ref, xw_ref, ydx_ref, ecum_ref,
                    cdec_ref, *, sb):
    nl = DEC_SEQ

    def rows(l):
        return slice(l * sb, (l + 1) * sb)

    def tok(l, off, width):
        return pj_ref[rows(l), off:off + width].astype(F32)

    for gi, w in enumerate(POOL_WINDOWS):
        c0 = gi * POOL_GROUP
        ext = [sp_ref[j, :, c0:c0 + POOL_GROUP] for j in range(POOL_HIST)]
        ext += [tok(l, OFF_U + c0, POOL_GROUP) for l in range(nl)]
        for l in range(nl):
            acc = ext[POOL_HIST + l]
            for k in range(1, w):
                acc = acc + ext[POOL_HIST + l - k]
            cnt = float(min(w, PAST_LEN + l + 1))
            pooled = acc / cnt - ext[POOL_HIST + l]
            mixed = _dot(pooled.astype(BF16), pw_ref[gi])
            gt = tok(l, OFF_GATE + c0, POOL_GROUP)
            oa_ref[rows(l), c0:c0 + POOL_GROUP] = (
                mixed * ps_ref[:, c0:c0 + POOL_GROUP] * _silu(gt)).astype(BF16)
    for j in range(POOL_HIST):
        src = j + nl
        if src < POOL_HIST:
            npool_ref[j] = sp_ref[src]
        else:
            npool_ref[j] = tok(src - POOL_HIST, OFF_U, D_POOL)

    hist = CONV_WIDTH - 1
    conv_out = {}
    for name, poff, coff, width in (("x", OFF_XS, 0, D_SSM), ("b", OFF_B, D_SSM, D_BC),
                                    ("c", OFF_C, D_SSM + D_BC, D_BC)):
        ext = [scv_ref[j, :, coff:coff + width] for j in range(hist)]
        ext += [tok(l, poff, width) for l in range(nl)]
        outs = []
        for l in range(nl):
            acc = cbias_ref[:, coff:coff + width]
            for k in range(CONV_WIDTH):
                acc = acc + cw_ref[k:k + 1, coff:coff + width] * ext[l + k]
            outs.append(_silu(acc))
        conv_out[name] = outs
        for j in range(hist):
            nconv_ref[j, :, coff:coff + width] = ext[j + nl]
    xs, bs, cs = conv_out["x"], conv_out["b"], conv_out["c"]
    for l in range(nl):
        cs_ref[rows(l), :] = cs[l]
        bs_ref[rows(l), :] = bs[l].astype(BF16)

    a_neg = -jnp.exp(alog_ref[...])
    dt, a_cum = [], []
    run = None
    for l in range(nl):
        d = _softplus(dtr_ref[rows(l), :] + dtb_ref[...])
        dt.append(d)
        run = d * a_neg if run is None else run + d * a_neg
        a_cum.append(run)
    cdec_ref[...] = jnp.exp(a_cum[nl - 1])

    ehot = ehot_ref[...]
    ghot = ghot_ref[...]
    for l in range(nl):
        ydiag = None
        for s in range(l + 1):
            cbh = _dot2(cs[l] * bs[s], ghot)
            gls = cbh * jnp.exp(a_cum[l] - a_cum[s]) * dt[s]
            term = _dot2(gls, ehot) * xs[s]
            ydiag = term if ydiag is None else ydiag + term
        ydx_ref[rows(l), :] = ydiag + dskip_ref[...] * xs[l]
        w_end = jnp.exp(a_cum[nl - 1] - a_cum[l]) * dt[l]
        xw_ref[rows(l), :] = _dot2(w_end, ehot) * xs[l]
        ecum_ref[rows(l), :] = _dot2(jnp.exp(a_cum[l]), ehot)


def _sample1(pj3, dtr3, sp3, scv3, pool_w, pool_scale, conv_w, conv_b, dt_bias_p, a_log_p,
             dskip_row, ehot, ghot, sb):
    nblk = pj3.shape[0]
    db = nblk * sb
    rb = DEC_SEQ * sb
    blk2 = lambda width: pl.BlockSpec((None, rb, width), lambda i: (i, 0, 0))
    hist3 = lambda n, width: pl.BlockSpec((n, sb, width), lambda i: (0, i, 0))
    const2 = lambda shape: pl.BlockSpec(shape, lambda i: (0, 0))
    blk_shape = lambda width, dt: jax.ShapeDtypeStruct((nblk, rb, width), dt)
    return pl.pallas_call(
        functools.partial(_sample1_kernel, sb=sb),
        grid=(nblk,),
        in_specs=[
            blk2(D_MAIN), blk2(LANES), hist3(POOL_HIST, D_POOL), hist3(CONV_WIDTH - 1, D_CONV),
            pl.BlockSpec((len(POOL_WINDOWS), POOL_GROUP, POOL_GROUP), lambda i: (0, 0, 0)),
            const2((1, D_POOL)), const2((CONV_WIDTH, D_CONV)), const2((1, D_CONV)),
            const2((1, LANES)), const2((1, LANES)), const2((1, D_SSM)),
            const2((LANES, D_SSM)), const2((D_BC, LANES)),
        ],
        out_specs=[
            blk2(D_POOL), hist3(POOL_HIST, D_POOL), hist3(CONV_WIDTH - 1, D_CONV),
            blk2(D_BC), blk2(D_BC), blk2(D_SSM), blk2(D_SSM), blk2(D_SSM),
            pl.BlockSpec((sb, LANES), lambda i: (i, 0)),
        ],
        out_shape=[
            blk_shape(D_POOL, BF16),
            jax.ShapeDtypeStruct((POOL_HIST, db, D_POOL), F32),
            jax.ShapeDtypeStruct((CONV_WIDTH - 1, db, D_CONV), F32),
            blk_shape(D_BC, F32), blk_shape(D_BC, BF16),
            blk_shape(D_SSM, F32), blk_shape(D_SSM, F32), blk_shape(D_SSM, F32),
            jax.ShapeDtypeStruct((db, LANES), F32),
        ],
        compiler_params=pltpu.CompilerParams(
            dimension_semantics=("arbitrary",),
            vmem_limit_bytes=VMEM_LIMIT),
        name="sample_elementwise",
    )(pj3, dtr3, sp3, scv3, pool_w, pool_scale, conv_w, conv_b, dt_bias_p, a_log_p, dskip_row,
      ehot, ghot)


def _sample2_setup(xw_ref, cs_ref, xwt_ref, *, sb, per):
    j = pl.program_id(1)

    @pl.when(j == 0)
    def _():
        for blk in range(D_SSM // LANES):
            xwt_ref[blk * LANES:(blk + 1) * LANES, :] = jnp.transpose(
                xw_ref[:, blk * LANES:(blk + 1) * LANES]).astype(BF16)

    return [jnp.concatenate([cs_ref[pl.ds(l * sb + per * j + 2 * pr + bi, 1), :]
                             for bi in range(2) for l in range(DEC_SEQ)], axis=0).astype(BF16)
            for pr in range(per // 2)]


def _sample2_group(g, refs, c_rows, *, sb, per):
    cdec_ref, st_ref, _, bs_ref, _, nst_ref, yo_ref, xwt_ref = refs
    i = pl.program_id(0)
    j = pl.program_id(1)
    nl = DEC_SEQ
    rows = sb * nl
    seq_of_row = lax.broadcasted_iota(jnp.int32, (rows, 1), 0) & (sb - 1)
    gw = HEADS_PER_GROUP * HEAD_DIM
    gs = slice(g * D_STATE, (g + 1) * D_STATE)
    for pr in range(per // 2):
        q0 = per * j + 2 * pr
        c8 = c_rows[pr][:, gs]
        b_blk = bs_ref[:, gs]
        zero_b = jnp.zeros_like(b_blk)
        w2 = jnp.concatenate([jnp.where(seq_of_row == q0, b_blk, zero_b),
                              jnp.where(seq_of_row == q0 + 1, b_blk, zero_b)], axis=1)
        u2 = _dot(xwt_ref[g * gw:(g + 1) * gw, :], w2)
        for bi in range(2):
            sq = 2 * pr + bi
            s0 = st_ref[sq, g * HEADS_PER_GROUP:(g + 1) * HEADS_PER_GROUP].reshape(gw, D_STATE)
            yo = _dot_nt(c8, s0.astype(BF16))
            base = (i * sb + q0 + bi) * HEADS + g * HEADS_PER_GROUP
            for r in range(HEADS_PER_GROUP):
                dec = cdec_ref[base + r]
                rs = slice(r * HEAD_DIM, (r + 1) * HEAD_DIM)
                nst_ref[sq, g * HEADS_PER_GROUP + r] = (
                    s0[rs] * dec + u2[rs, bi * D_STATE:(bi + 1) * D_STATE])
            for l in range(nl):
                yo_ref[pl.ds(l * sb + q0 + bi, 1), g * gw:(g + 1) * gw] = (
                    yo[bi * nl + l:bi * nl + l + 1])


N_SSD_IN, N_SSD_OUT, N_SSD_SCRATCH = 11, 3, 6
N_ST_IN, N_ST_OUT = 5, 2


def _ssd_state_kernel(*refs, nc, sb, per):
    ins, rest = refs[:N_SSD_IN + N_ST_IN], refs[N_SSD_IN + N_ST_IN:]
    outs, scratch = rest[:N_SSD_OUT + N_ST_OUT], rest[N_SSD_OUT + N_ST_OUT:]
    st_refs = (*ins[N_SSD_IN:], *outs[N_SSD_OUT:], *scratch[N_SSD_SCRATCH:])
    c_rows = _sample2_setup(st_refs[2], st_refs[4], st_refs[7], sb=sb, per=per)
    _ssd_body(*ins[:N_SSD_IN], *outs[:N_SSD_OUT], *scratch[:N_SSD_SCRATCH], nc=nc,
              after_group=functools.partial(_sample2_group, refs=st_refs, c_rows=c_rows,
                                            sb=sb, per=per))


def _ssd_prompt_sample_state(proj, dt_raw, conv_w, conv_b, dt_bias_p, a_log_p, dskip_row, norm_g,
                             bsz, seq, cdec_flat, state, xw3, bs3, cs3, sb):
    q = CHUNK
    nc = seq // q
    db = state.shape[0]
    rb = DEC_SEQ * sb
    per = sb // nc
    assert rb == LANES and db // sb == bsz and per * nc == sb and per % 2 == 0
    row = lambda b, c: b * nc + c
    const = lambda b, c: (0, 0)
    blk2 = lambda width: pl.BlockSpec((None, rb, width), lambda i, j: (i, 0, 0))
    st_spec = pl.BlockSpec((per, HEADS, HEAD_DIM, D_STATE), lambda i, j: (i * nc + j, 0, 0, 0))
    return pl.pallas_call(
        functools.partial(_ssd_state_kernel, nc=nc, sb=sb, per=per),
        grid=(bsz, nc),
        in_specs=[
            pl.BlockSpec((q, D_SSM), lambda b, c: (row(b, c), OFF_Z // D_SSM)),
            pl.BlockSpec((q, D_SSM), lambda b, c: (row(b, c), OFF_XS // D_SSM)),
            pl.BlockSpec((q, D_BC), lambda b, c: (row(b, c), OFF_B // D_BC)),
            pl.BlockSpec((q, D_BC), lambda b, c: (row(b, c), OFF_C // D_BC)),
            pl.BlockSpec((q, LANES), lambda b, c: (row(b, c), 0)),
            pl.BlockSpec((CONV_WIDTH, D_CONV), const),
            pl.BlockSpec((1, D_CONV), const),
            pl.BlockSpec((1, LANES), const),
            pl.BlockSpec((1, LANES), const),
            pl.BlockSpec((1, D_SSM), const),
            pl.BlockSpec((1, D_SSM), const),
            pl.BlockSpec(memory_space=pltpu.SMEM),
            st_spec, blk2(D_SSM), blk2(D_BC), blk2(D_BC),
        ],
        out_specs=[
            pl.BlockSpec((q, D_SSM), lambda b, c: (row(b, c), 0)),
            pl.BlockSpec((None, CONV_WIDTH - 1, D_CONV), lambda b, c: (b, 0, 0)),
            pl.BlockSpec((None, HEADS, HEAD_DIM, D_STATE), lambda b, c: (b, 0, 0, 0)),
            st_spec, blk2(D_SSM),
        ],
        out_shape=[
            jax.ShapeDtypeStruct((bsz * seq, D_SSM), BF16),
            jax.ShapeDtypeStruct((bsz, CONV_WIDTH - 1, D_CONV), F32),
            jax.ShapeDtypeStruct((bsz, HEADS, HEAD_DIM, D_STATE), F32),
            jax.ShapeDtypeStruct(state.shape, F32),
            jax.ShapeDtypeStruct((db // sb, rb, D_SSM), F32),
        ],
        scratch_shapes=[
            pltpu.VMEM((D_CONV // LANES, q + SUBLANES, LANES), F32),
            pltpu.VMEM((D_STATE, D_SSM), F32),
            pltpu.VMEM((D_SSM // LANES, q + SUBLANES, LANES), F32),
            pltpu.VMEM((D_BC // LANES, q + SUBLANES, LANES), F32),
            pltpu.VMEM((D_BC // LANES, q + SUBLANES, LANES), F32),
            pltpu.VMEM((q, D_SSM), F32),
            pltpu.VMEM((D_SSM, rb), BF16),
        ],
        compiler_params=pltpu.CompilerParams(
            dimension_semantics=("arbitrary", "arbitrary"),
            vmem_limit_bytes=VMEM_LIMIT),
        name="ssd_prompt_sample_state",
    )(proj, proj, proj, proj, dt_raw, conv_w, conv_b, dt_bias_p, a_log_p, dskip_row, norm_g,
      cdec_flat, state, xw3, bs3, cs3)


def _sample3_kernel(yo_ref, ecum_ref, ydx_ref, z_ref, ng_ref, ob_ref):
    gw = D_SSM // GROUPS
    for g in range(GROUPS):
        cs = slice(g * gw, (g + 1) * gw)
        y = ydx_ref[:, cs] + ecum_ref[:, cs] * yo_ref[:, cs]
        yz = y * _silu(z_ref[:, cs].astype(F32))
        ms = jnp.sum(yz * yz, axis=-1, keepdims=True) * (1.0 / gw)
        ob_ref[:, cs] = (yz * lax.rsqrt(ms + EPS) * ng_ref[:, cs]).astype(BF16)


def _sample3(yo3, ecum3, ydx3, pj3, norm_g):
    nblk, rb, _ = pj3.shape
    blk2 = lambda width: pl.BlockSpec((None, rb, width), lambda i: (i, 0, 0))
    return pl.pallas_call(
        _sample3_kernel,
        grid=(nblk,),
        in_specs=[blk2(D_SSM), blk2(D_SSM), blk2(D_SSM),
                  pl.BlockSpec((None, rb, D_SSM), lambda i: (i, 0, OFF_Z // D_SSM)),
                  pl.BlockSpec((1, D_SSM), lambda i: (0, 0))],
        out_specs=blk2(D_SSM),
        out_shape=jax.ShapeDtypeStruct((nblk, rb, D_SSM), BF16),
        compiler_params=pltpu.CompilerParams(
            dimension_semantics=("arbitrary",),
            vmem_limit_bytes=VMEM_LIMIT),
        name="sample_gate_norm",
    )(yo3, ecum3, ydx3, pj3, norm_g)


def kernel(x_prompt, x_sample, state_pool, state_conv, state_ssm, norm_g, w_in, conv_w, conv_b,
           dt_bias, a_log, d_skip, ssm_norm_g, pool_w, pool_scale, w_out, final_g):
    bsz, seq, _ = x_prompt.shape
    db, nl, _ = x_sample.shape
    assert nl == DEC_SEQ and seq % CHUNK == 0 and w_in.shape[0] == 1

    w_t = jnp.transpose(w_in[0])
    w_a = w_out[0, :D_POOL].astype(BF16)
    w_b = w_out[0, D_POOL:].astype(BF16)
    pool_w_b = pool_w[0].astype(BF16)
    g_in = norm_g[0][None, :]
    g_fin = final_g[None, :]
    ps = pool_scale[0][None, :]
    cw = conv_w[0]
    cbias = conv_b[0][None, :]
    pad_h = lambda v: jnp.pad(v, (0, LANES - HEADS))[None, :]
    dtb = pad_h(dt_bias[0])
    alog = pad_h(a_log[0])
    dskip_row = jnp.repeat(d_skip[0], HEAD_DIM)[None, :]
    ng = ssm_norm_g[0][None, :]

    sb = SAMPLE_BLOCK
    nblk = db // sb
    rb = nl * sb
    xs2 = x_sample.reshape(nblk, sb, nl, D_MODEL).transpose(0, 2, 1, 3).reshape(db * nl, D_MODEL)
    proj_s, dt_s, w_bf, wdt_bf = _inproj(xs2, g_in, w_t, w_t, 1024)
    pj3 = proj_s.reshape(nblk, rb, D_MAIN)
    head_of_ch = jnp.arange(D_SSM, dtype=jnp.int32) // HEAD_DIM
    ehot = (jnp.arange(LANES, dtype=jnp.int32)[:, None] == head_of_ch[None, :]).astype(BF16)
    grp_of_row = jnp.arange(D_BC, dtype=jnp.int32) // D_STATE
    head_id = jnp.arange(LANES, dtype=jnp.int32)
    ghot = ((head_id[None, :] // HEADS_PER_GROUP == grp_of_row[:, None])
            & (head_id[None, :] < HEADS)).astype(BF16)
    (oa_s, npool_s, nconv_s, cs_s, bs_s, xw_s, ydx_s, ecum_s, cdec_s) = _sample1(
        pj3, dt_s.reshape(nblk, rb, LANES), jnp.transpose(state_pool[0], (1, 0, 2)),
        jnp.transpose(state_conv[0], (1, 0, 2)), pool_w_b, ps, cw, cbias, dtb, alog,
        dskip_row, ehot, ghot, sb)

    xp2 = x_prompt.reshape(bsz * seq, D_MODEL)
    proj_p, dt_p = _inproj(xp2, g_in, w_bf, wdt_bf, PROMPT_TN)
    oa_p, npool_p = _pool_prompt(proj_p, pool_w_b, ps, bsz, seq)
    ob_p, nconv_p, nssm_p, nssm_s, yo_s = _ssd_prompt_sample_state(
        proj_p, dt_p, cw, cbias, dtb, alog, dskip_row, ng, bsz, seq,
        cdec_s[:, :HEADS].reshape(db * HEADS), state_ssm[0], xw_s, bs_s, cs_s, sb)
    y_p = _outproj(oa_p, ob_p, w_a, w_b, xp2, g_fin).reshape(bsz, seq, D_MODEL)

    ob_s = _sample3(yo_s, ecum_s, ydx_s, pj3, ng)
    y_s = _outproj(oa_s.reshape(db * nl, D_POOL), ob_s.reshape(db * nl, D_SSM), w_a, w_b, xs2,
                   g_fin)
    y_s = y_s.reshape(nblk, nl, sb, D_MODEL).transpose(0, 2, 1, 3).reshape(db, nl, D_MODEL)

    return (y_p, y_s,
            npool_p[None], nconv_p[None], nssm_p[None],
            jnp.transpose(npool_s, (1, 0, 2))[None],
            jnp.transpose(nconv_s, (1, 0, 2))[None],
            nssm_s[None])
```

```python
import functools

import jax
import jax.numpy as jnp
from jax import lax
from jax.experimental import pallas as pl
from jax.experimental.pallas import tpu as pltpu

F32 = jnp.float32
BF16 = jnp.bfloat16

D_MODEL = 2048
D_POOL = 1024
POOL_WINDOWS = (2, 4, 8, 16)
POOL_GROUP = 256
POOL_HIST = 15
D_SSM = 3072
HEAD_DIM = 64
HEADS = 48
GROUPS = 8
HEADS_PER_GROUP = 6
D_STATE = 128
D_BC = GROUPS * D_STATE
CONV_WIDTH = 4
D_CONV = D_SSM + 2 * D_BC
D_MAIN = 2 * D_POOL + D_SSM + D_CONV
PAST_LEN = 16384
DEC_SEQ = 4
EPS = 1e-5

LANES = 128
SUBLANES = 8
VMEM_LIMIT = 56 * 1024 * 1024

OFF_Z = 0
OFF_XS = D_SSM
OFF_U = 2 * D_SSM
OFF_GATE = OFF_U + D_POOL
OFF_B = OFF_GATE + D_POOL
OFF_C = OFF_B + D_BC

PROMPT_TN = 2048
CHUNK = 128
SAMPLE_BLOCK = 32
NEG_BIG = -1e30
LOG2E = 1.4426950408889634


def _silu(v):
    h = 0.5 * v
    return h + h * jnp.tanh(h)


def _softplus(v):
    y = jnp.exp(-jnp.abs(v))
    u = 1.0 + y
    d = u - 1.0
    l1p = jnp.where(d == 0.0, y, jnp.log(u) * (y / jnp.where(d == 0.0, 1.0, d)))
    return jnp.maximum(v, 0.0) + l1p


def _split2(v):
    hi = v.astype(BF16)
    lo = (v - hi.astype(F32)).astype(BF16)
    return hi, lo


def _dot(a, b):
    return jnp.dot(a, b, preferred_element_type=F32)


def _dot_nt(a, b):
    return lax.dot_general(a, b, (((1,), (1,)), ((), ())), preferred_element_type=F32)


def _dot2(v, onehot):
    hi, lo = _split2(v)
    return _dot(hi, onehot) + _dot(lo, onehot)


def _inproj_kernel(x_ref, g_ref, w_ref, wdt_ref, o_ref, dt_ref, *rest):
    from_f32 = w_ref.dtype == F32
    (wb_out_ref, wdt_out_ref, h_ref) = rest if from_f32 else (None, None, *rest)

    @pl.when(pl.program_id(1) == 0)
    def _():
        x = x_ref[...]
        ms = jnp.mean(x * x, axis=-1, keepdims=True)
        h = (x * lax.rsqrt(ms + EPS) * g_ref[...]).astype(BF16)
        h_ref[...] = h
        if from_f32:
            wrow = lax.broadcasted_iota(jnp.int32, wdt_ref.shape, 0)
            wdt = jnp.where(wrow < HEADS, wdt_ref[...], 0.0).astype(BF16)
            wdt_out_ref[...] = wdt
        else:
            wdt = wdt_ref[...]
        dt_ref[...] = _dot_nt(h, wdt)

    if from_f32:
        wb = w_ref[...].astype(BF16)
        wb_out_ref[...] = wb
    else:
        wb = w_ref[...]
    o_ref[...] = _dot_nt(h_ref[...], wb).astype(BF16)


def _src_block(j):
    nz = (D_SSM + D_SSM) // 1024
    npool = 2 * D_POOL // 1024
    return jnp.where(j < nz, j + npool, jnp.where(j < nz + npool, j - nz, j))


def _inproj(x2d, norm_g, w, w_dt, tn):
    m = x2d.shape[0]
    tm = min(1024, m)
    from_f32 = w.dtype == F32
    assert tn == 1024 or not from_f32
    w_map = (lambda i, j: (_src_block(j), 0)) if from_f32 else (lambda i, j: (j, 0))
    wdt_map = (lambda i, j: (D_MAIN // LANES, 0)) if from_f32 else (lambda i, j: (0, 0))
    out_specs = [
        pl.BlockSpec((tm, tn), lambda i, j: (i, j)),
        pl.BlockSpec((tm, LANES), lambda i, j: (i, 0)),
    ]
    out_shape = [
        jax.ShapeDtypeStruct((m, D_MAIN), BF16),
        jax.ShapeDtypeStruct((m, LANES), F32),
    ]
    if from_f32:
        assert m == tm
        out_specs += [pl.BlockSpec((tn, D_MODEL), lambda i, j: (j, 0)),
                      pl.BlockSpec((LANES, D_MODEL), lambda i, j: (0, 0))]
        out_shape += [jax.ShapeDtypeStruct((D_MAIN, D_MODEL), BF16),
                      jax.ShapeDtypeStruct((LANES, D_MODEL), BF16)]
    return pl.pallas_call(
        _inproj_kernel,
        grid=(m // tm, D_MAIN // tn),
        in_specs=[
            pl.BlockSpec((tm, D_MODEL), lambda i, j: (i, 0)),
            pl.BlockSpec((1, D_MODEL), lambda i, j: (0, 0)),
            pl.BlockSpec((tn, D_MODEL), w_map),
            pl.BlockSpec((LANES, D_MODEL), wdt_map),
        ],
        out_specs=out_specs,
        out_shape=out_shape,
        scratch_shapes=[pltpu.VMEM((tm, D_MODEL), BF16)],
        compiler_params=pltpu.CompilerParams(
            dimension_semantics=("arbitrary", "arbitrary"),
            vmem_limit_bytes=VMEM_LIMIT),
        name="inproj",
    )(x2d, norm_g, w, w_dt)


def _outproj_kernel(a_ref, b_ref, wa_ref, wb_ref, x_ref, g_ref, y_ref):
    acc = _dot(a_ref[...], wa_ref[...]) + _dot(b_ref[...], wb_ref[...])
    r = x_ref[...] + acc
    ms = jnp.mean(r * r, axis=-1, keepdims=True)
    y_ref[...] = r * lax.rsqrt(ms + EPS) * g_ref[...]


def _outproj(out_a, out_b, w_a, w_b, x2d, final_g):
    m = x2d.shape[0]
    tm = min(512, m)
    resident = pl.Buffered(1)
    return pl.pallas_call(
        _outproj_kernel,
        grid=(m // tm,),
        in_specs=[
            pl.BlockSpec((tm, D_POOL), lambda i: (i, 0)),
            pl.BlockSpec((tm, D_SSM), lambda i: (i, 0)),
            pl.BlockSpec((D_POOL, D_MODEL), lambda i: (0, 0), pipeline_mode=resident),
            pl.BlockSpec((D_SSM, D_MODEL), lambda i: (0, 0), pipeline_mode=resident),
            pl.BlockSpec((tm, D_MODEL), lambda i: (i, 0)),
            pl.BlockSpec((1, D_MODEL), lambda i: (0, 0)),
        ],
        out_specs=pl.BlockSpec((tm, D_MODEL), lambda i: (i, 0)),
        out_shape=jax.ShapeDtypeStruct((m, D_MODEL), F32),
        compiler_params=pltpu.CompilerParams(
            dimension_semantics=("arbitrary",),
            vmem_limit_bytes=VMEM_LIMIT),
        name="outproj",
    )(out_a, out_b, w_a, w_b, x2d, final_g)


def _pool_kernel(u_ref, gate_ref, pw_ref, ps_ref, oa_ref, np_ref, ubuf, *, tl, nt):
    t = pl.program_id(1)
    hist = POOL_HIST + 1

    @pl.when(t == 0)
    def _():
        ubuf[0:hist, :] = jnp.zeros((hist, D_POOL), F32)

    u = u_ref[...].astype(F32)
    ubuf[hist:hist + tl, :] = u
    pos = t * tl + lax.broadcasted_iota(jnp.int32, (tl, 1), 0)
    for gi, w in enumerate(POOL_WINDOWS):
        cs = slice(gi * POOL_GROUP, (gi + 1) * POOL_GROUP)
        ug = u[:, cs]
        acc = ubuf[0:hist + tl, cs]
        span = 1
        while span < w:
            acc = acc + pltpu.roll(acc, span, 0)
            span *= 2
        cnt = jnp.minimum(w, pos + 1).astype(F32)
        pooled = acc[hist:hist + tl] / cnt - ug
        mixed = _dot(pooled.astype(BF16), pw_ref[gi])
        gt = gate_ref[:, cs].astype(F32)
        oa_ref[:, cs] = (mixed * ps_ref[:, cs] * _silu(gt)).astype(BF16)

    ubuf[0:hist, :] = ubuf[tl:tl + hist, :]

    @pl.when(t == nt - 1)
    def _():
        np_ref[...] = ubuf[tl + 1:tl + hist, :]


def _pool_prompt(proj, pool_w, pool_scale, bsz, seq):
    tl = min(512, seq)
    nt = seq // tl
    return pl.pallas_call(
        functools.partial(_pool_kernel, tl=tl, nt=nt),
        grid=(bsz, nt),
        in_specs=[
            pl.BlockSpec((tl, D_POOL), lambda b, t: (b * nt + t, OFF_U // D_POOL)),
            pl.BlockSpec((tl, D_POOL), lambda b, t: (b * nt + t, OFF_GATE // D_POOL)),
            pl.BlockSpec((len(POOL_WINDOWS), POOL_GROUP, POOL_GROUP), lambda b, t: (0, 0, 0)),
            pl.BlockSpec((1, D_POOL), lambda b, t: (0, 0)),
        ],
        out_specs=[
            pl.BlockSpec((tl, D_POOL), lambda b, t: (b * nt + t, 0)),
            pl.BlockSpec((None, POOL_HIST, D_POOL), lambda b, t: (b, 0, 0)),
        ],
        out_shape=[
            jax.ShapeDtypeStruct((bsz * seq, D_POOL), BF16),
            jax.ShapeDtypeStruct((bsz, POOL_HIST, D_POOL), F32),
        ],
        scratch_shapes=[pltpu.VMEM((tl + POOL_HIST + 1, D_POOL), F32)],
        compiler_params=pltpu.CompilerParams(
            dimension_semantics=("arbitrary", "arbitrary"),
            vmem_limit_bytes=VMEM_LIMIT),
        name="pool_prompt",
    )(proj, proj, pool_w, pool_scale)


def _ssd_body(z_ref, xs_ref, b_ref, c_ref, dtr_ref, cw_ref, cbias_ref, dtb_ref, alog_ref,
              dskip_ref, ng_ref,
              ob_ref, nconv_ref, nssm_ref,
              cbuf, st_ref, cvx_ref, cvb_ref, cvc_ref, y_ref, *, nc, after_group=None):
    q = CHUNK
    c_idx = pl.program_id(1)
    halo = SUBLANES

    nbx, nbb = D_SSM // LANES, D_BC // LANES

    @pl.when(c_idx == 0)
    def _():
        cbuf[:, 0:halo, :] = jnp.zeros((cbuf.shape[0], halo, LANES), F32)
        st_ref[...] = jnp.zeros(st_ref.shape, F32)

    nv = (halo + q) // SUBLANES
    for blk in range(D_CONV // LANES):
        ls = slice(blk * LANES, (blk + 1) * LANES)
        if blk < nbx:
            src_ref, off, dst = xs_ref, blk * LANES, cvx_ref.at[blk]
        elif blk < nbx + nbb:
            src_ref, off, dst = b_ref, (blk - nbx) * LANES, cvb_ref.at[blk - nbx]
        else:
            src_ref, off, dst = c_ref, (blk - nbx - nbb) * LANES, cvc_ref.at[blk - nbx - nbb]
        cbuf[blk, halo:halo + q, :] = src_ref[:, off:off + LANES].astype(F32)
        xv = [cbuf[blk, pl.ds(a, SUBLANES, stride=nv), :] for a in range(nv)]
        wrap = [pltpu.roll(xv[nv - k], 1, 0) for k in range(1, CONV_WIDTH)]
        taps = [cw_ref[k:k + 1, ls] for k in range(CONV_WIDTH)]
        bias = cbias_ref[:, ls]
        for a in range(nv):
            acc = bias + taps[CONV_WIDTH - 1] * xv[a]
            for k in range(1, CONV_WIDTH):
                src = xv[a - k] if a >= k else wrap[k - a - 1]
                acc = acc + taps[CONV_WIDTH - 1 - k] * src
            dst[pl.ds(a, SUBLANES, stride=nv), :] = _silu(acc)
    rows = slice(halo, halo + q)

    dt = _softplus(dtr_ref[...] + dtb_ref[...])
    a_neg = -jnp.exp(alog_ref[...])
    da = dt * (a_neg * LOG2E)
    row = lax.broadcasted_iota(jnp.int32, (q, LANES), 0)
    a2 = da
    shift = 1
    while shift < q:
        a2 = a2 + jnp.where(row >= shift, pltpu.roll(a2, shift, 0), 0.0)
        shift *= 2
    a2_t = jnp.transpose(a2)
    ldt_t = jnp.log2(jnp.transpose(dt))
    a2_end_t = a2_t[:, q - 1:q]
    w_t = jnp.exp2(a2_end_t - a2_t + ldt_t)
    cdec_t = jnp.exp2(a2_end_t)
    srow_t = a2_t - ldt_t
    ea = jnp.exp2(a2)

    li = lax.broadcasted_iota(jnp.int32, (q, q), 0)
    si = lax.broadcasted_iota(jnp.int32, (q, q), 1)
    tri = li >= si
    lane = lax.broadcasted_iota(jnp.int32, (q, LANES), 1)
    lo_half = lane < HEAD_DIM

    for g in range(GROUPS):
        gs = slice(g * D_STATE, (g + 1) * D_STATE)
        c_gb = cvc_ref[g, rows, :].astype(BF16)
        b_g = cvb_ref[g, rows, :]
        cb = _dot_nt(c_gb, b_g.astype(BF16))
        b_t = jnp.transpose(b_g)
        ppg = HEADS_PER_GROUP // 2
        gl = slice(g * ppg * LANES, (g + 1) * ppg * LANES)
        y_off = _dot(c_gb, st_ref[:, gl].astype(BF16))
        for j in range(ppg):
            blk = g * ppg + j
            ls = slice(blk * LANES, (blk + 1) * LANES)
            sc, bw, ecol, dec = [], [], [], []
            for h in (2 * blk, 2 * blk + 1):
                a_col = jnp.broadcast_to(a2[:, h:h + 1], (q, q))
                decay_dt = jnp.exp2(jnp.where(tri, a_col - srow_t[h:h + 1, :], NEG_BIG))
                sc.append((cb * decay_dt).astype(BF16))
                bw.append((b_t * w_t[h:h + 1, :]).astype(BF16))
                ecol.append(jnp.broadcast_to(ea[:, h:h + 1], (q, LANES)))
                dec.append(jnp.broadcast_to(cdec_t[h:h + 1, :], (q, LANES)))
            x_p = cvx_ref[blk, rows, :].astype(BF16)
            zero_b = jnp.zeros_like(x_p)
            x_bd = jnp.concatenate([jnp.where(lo_half, x_p, zero_b),
                                    jnp.where(lo_half, zero_b, x_p)], axis=0)
            lhs1 = jnp.concatenate([jnp.concatenate(sc, axis=1),
                                    jnp.concatenate(bw, axis=1)], axis=0)
            r1 = _dot(lhs1, x_bd)
            y_ref[:, ls] = (r1[0:q] + y_off[:, j * LANES:(j + 1) * LANES]
                            * jnp.where(lo_half, ecol[0], ecol[1]))
            st_ref[:, ls] = (st_ref[:, ls] * jnp.where(lo_half, dec[0], dec[1])
                             + r1[q:2 * q])
        if after_group is not None:
            after_group(g)

    gw = D_SSM // GROUPS
    bpg = gw // LANES
    for g in range(GROUPS):
        yz, ssq = [], None
        for blk in range(g * bpg, (g + 1) * bpg):
            ls = slice(blk * LANES, (blk + 1) * LANES)
            y = y_ref[:, ls] + cvx_ref[blk, rows, :] * dskip_ref[:, ls]
            v = y * _silu(z_ref[:, ls].astype(F32))
            yz.append(v)
            ssq = v * v if ssq is None else ssq + v * v
        scale = lax.rsqrt(jnp.sum(ssq, axis=-1, keepdims=True) * (1.0 / gw) + EPS)
        for i, blk in enumerate(range(g * bpg, (g + 1) * bpg)):
            ls = slice(blk * LANES, (blk + 1) * LANES)
            ob_ref[:, ls] = (yz[i] * scale * ng_ref[:, ls]).astype(BF16)

    cbuf[:, 0:halo, :] = cbuf[:, q:q + halo, :]

    @pl.when(c_idx == nc - 1)
    def _():
        for blk in range(D_CONV // LANES):
            nconv_ref[:, blk * LANES:(blk + 1) * LANES] = cbuf[blk, q + halo - 3:q + halo, :]
        for blk in range(D_SSM // LANES):
            t = jnp.transpose(st_ref[:, blk * LANES:(blk + 1) * LANES])
            nssm_ref[2 * blk:2 * blk + 2] = t.reshape(2, HEAD_DIM, D_STATE)


def _sample1_kernel(pj_ref, dtr_ref, sp_ref, scv_ref, pw_ref, ps_ref, cw_ref, cbias_ref, dtb_ref,
                    alog_ref, dskip_ref, ehot_ref, ghot_ref,
                    oa_ref, npool_ref, nconv_ref, cs_ref, bs_ref, xw_ref, ydx_ref, ecum_ref,
                    cdec_ref, *, sb):
    nl = DEC_SEQ

    def rows(l):
        return slice(l * sb, (l + 1) * sb)

    def tok(l, off, width):
        return pj_ref[rows(l), off:off + width].astype(F32)

    for gi, w in enumerate(POOL_WINDOWS):
        c0 = gi * POOL_GROUP
        ext = [sp_ref[j, :, c0:c0 + POOL_GROUP] for j in range(POOL_HIST)]
        ext += [tok(l, OFF_U + c0, POOL_GROUP) for l in range(nl)]
        for l in range(nl):
            acc = ext[POOL_HIST + l]
            for k in range(1, w):
                acc = acc + ext[POOL_HIST + l - k]
            cnt = float(min(w, PAST_LEN + l + 1))
            pooled = acc / cnt - ext[POOL_HIST + l]
            mixed = _dot(pooled.astype(BF16), pw_ref[gi])
            gt = tok(l, OFF_GATE + c0, POOL_GROUP)
            oa_ref[rows(l), c0:c0 + POOL_GROUP] = (
                mixed * ps_ref[:, c0:c0 + POOL_GROUP] * _silu(gt)).astype(BF16)
    for j in range(POOL_HIST):
        src = j + nl
        if src < POOL_HIST:
            npool_ref[j] = sp_ref[src]
        else:
            npool_ref[j] = tok(src - POOL_HIST, OFF_U, D_POOL)

    hist = CONV_WIDTH - 1
    conv_out = {}
    for name, poff, coff, width in (("x", OFF_XS, 0, D_SSM), ("b", OFF_B, D_SSM, D_BC),
                                    ("c", OFF_C, D_SSM + D_BC, D_BC)):
        ext = [scv_ref[j, :, coff:coff + width] for j in range(hist)]
        ext += [tok(l, poff, width) for l in range(nl)]
        outs = []
        for l in range(nl):
            acc = cbias_ref[:, coff:coff + width]
            for k in range(CONV_WIDTH):
                acc = acc + cw_ref[k:k + 1, coff:coff + width] * ext[l + k]
            outs.append(_silu(acc))
        conv_out[name] = outs
        for j in range(hist):
            nconv_ref[j, :, coff:coff + width] = ext[j + nl]
    xs, bs, cs = conv_out["x"], conv_out["b"], conv_out["c"]
    for l in range(nl):
        cs_ref[rows(l), :] = cs[l]
        bs_ref[rows(l), :] = bs[l].astype(BF16)

    a_neg = -jnp.exp(alog_ref[...])
    dt, a_cum = [], []
    run = None
    for l in range(nl):
        d = _softplus(dtr_ref[rows(l), :] + dtb_ref[...])
        dt.append(d)
        run = d * a_neg if run is None else run + d * a_neg
        a_cum.append(run)
    cdec_ref[...] = jnp.exp(a_cum[nl - 1])

    def onehot_rows(mats, onehot):
        parts = [_split2(m) for m in mats]
        stack = jnp.concatenate([p[0] for p in parts] + [p[1] for p in parts], axis=0)
        res = _dot(stack, onehot)
        n = len(mats)
        return [res[k * sb:(k + 1) * sb] + res[(n + k) * sb:(n + k + 1) * sb] for k in range(n)]

    pairs = [(l, s) for l in range(nl) for s in range(l + 1)]
    cbh = onehot_rows([cs[l] * bs[s] for l, s in pairs], ghot_ref[...])
    gls = [c * jnp.exp(a_cum[l] - a_cum[s]) * dt[s] for c, (l, s) in zip(cbh, pairs)]
    w_end = [jnp.exp(a_cum[nl - 1] - a_cum[l]) * dt[l] for l in range(nl)]
    e_cum = [jnp.exp(a_cum[l]) for l in range(nl)]
    chunk = 4 * LANES
    for c0 in range(0, D_SSM, chunk):
        cl = slice(c0, c0 + chunk)
        ex = onehot_rows(gls + w_end + e_cum, ehot_ref[:, cl])
        g_ex, w_ex, e_ex = ex[:len(pairs)], ex[len(pairs):len(pairs) + nl], ex[len(pairs) + nl:]
        xc = [x[:, cl] for x in xs]
        for l in range(nl):
            ydiag = dskip_ref[:, cl] * xc[l]
            for k, (pl_, ps_) in enumerate(pairs):
                if pl_ == l:
                    ydiag = ydiag + g_ex[k] * xc[ps_]
            ydx_ref[rows(l), cl] = ydiag
            xw_ref[rows(l), cl] = w_ex[l] * xc[l]
            ecum_ref[rows(l), cl] = e_ex[l]


def _sample1(pj3, dtr3, sp3, scv3, pool_w, pool_scale, conv_w, conv_b, dt_bias_p, a_log_p,
             dskip_row, ehot, ghot, sb):
    nblk = pj3.shape[0]
    db = nblk * sb
    rb = DEC_SEQ * sb
    blk2 = lambda width: pl.BlockSpec((None, rb, width), lambda i: (i, 0, 0))
    hist3 = lambda n, width: pl.BlockSpec((n, sb, width), lambda i: (0, i, 0))
    const2 = lambda shape: pl.BlockSpec(shape, lambda i: (0, 0))
    blk_shape = lambda width, dt: jax.ShapeDtypeStruct((nblk, rb, width), dt)
    return pl.pallas_call(
        functools.partial(_sample1_kernel, sb=sb),
        grid=(nblk,),
        in_specs=[
            blk2(D_MAIN), blk2(LANES), hist3(POOL_HIST, D_POOL), hist3(CONV_WIDTH - 1, D_CONV),
            pl.BlockSpec((len(POOL_WINDOWS), POOL_GROUP, POOL_GROUP), lambda i: (0, 0, 0)),
            const2((1, D_POOL)), const2((CONV_WIDTH, D_CONV)), const2((1, D_CONV)),
            const2((1, LANES)), const2((1, LANES)), const2((1, D_SSM)),
            const2((LANES, D_SSM)), const2((D_BC, LANES)),
        ],
        out_specs=[
            blk2(D_POOL), hist3(POOL_HIST, D_POOL), hist3(CONV_WIDTH - 1, D_CONV),
            blk2(D_BC), blk2(D_BC), blk2(D_SSM), blk2(D_SSM), blk2(D_SSM),
            pl.BlockSpec((sb, LANES), lambda i: (i, 0)),
        ],
        out_shape=[
            blk_shape(D_POOL, BF16),
            jax.ShapeDtypeStruct((POOL_HIST, db, D_POOL), F32),
            jax.ShapeDtypeStruct((CONV_WIDTH - 1, db, D_CONV), F32),
            blk_shape(D_BC, F32), blk_shape(D_BC, BF16),
            blk_shape(D_SSM, F32), blk_shape(D_SSM, F32), blk_shape(D_SSM, F32),
            jax.ShapeDtypeStruct((db, LANES), F32),
        ],
        compiler_params=pltpu.CompilerParams(
            dimension_semantics=("arbitrary",),
            vmem_limit_bytes=VMEM_LIMIT),
        name="sample_elementwise",
    )(pj3, dtr3, sp3, scv3, pool_w, pool_scale, conv_w, conv_b, dt_bias_p, a_log_p, dskip_row,
      ehot, ghot)


def _sample2_setup(xw_ref, cs_ref, xwt_ref, *, sb, per):
    j = pl.program_id(1)

    @pl.when(j == 0)
    def _():
        for blk in range(D_SSM // LANES):
            xwt_ref[blk * LANES:(blk + 1) * LANES, :] = jnp.transpose(
                xw_ref[:, blk * LANES:(blk + 1) * LANES]).astype(BF16)

    return [jnp.concatenate([cs_ref[pl.ds(l * sb + per * j + 2 * pr + bi, 1), :]
                             for bi in range(2) for l in range(DEC_SEQ)], axis=0).astype(BF16)
            for pr in range(per // 2)]


def _sample2_group(g, refs, c_rows, *, sb, per):
    cdec_ref, st_ref, _, bs_ref, _, nst_ref, yo_ref, xwt_ref = refs
    i = pl.program_id(0)
    j = pl.program_id(1)
    nl = DEC_SEQ
    rows = sb * nl
    seq_of_row = lax.broadcasted_iota(jnp.int32, (rows, 1), 0) & (sb - 1)
    gw = HEADS_PER_GROUP * HEAD_DIM
    gs = slice(g * D_STATE, (g + 1) * D_STATE)
    for pr in range(per // 2):
        q0 = per * j + 2 * pr
        c8 = c_rows[pr][:, gs]
        b_blk = bs_ref[:, gs]
        zero_b = jnp.zeros_like(b_blk)
        w2 = jnp.concatenate([jnp.where(seq_of_row == q0, b_blk, zero_b),
                              jnp.where(seq_of_row == q0 + 1, b_blk, zero_b)], axis=1)
        u2 = _dot(xwt_ref[g * gw:(g + 1) * gw, :], w2)
        for bi in range(2):
            sq = 2 * pr + bi
            s0 = st_ref[sq, g * HEADS_PER_GROUP:(g + 1) * HEADS_PER_GROUP].reshape(gw, D_STATE)
            yo = _dot_nt(c8, s0.astype(BF16))
            base = (i * sb + q0 + bi) * HEADS + g * HEADS_PER_GROUP
            for r in range(HEADS_PER_GROUP):
                dec = cdec_ref[base + r]
                rs = slice(r * HEAD_DIM, (r + 1) * HEAD_DIM)
                nst_ref[sq, g * HEADS_PER_GROUP + r] = (
                    s0[rs] * dec + u2[rs, bi * D_STATE:(bi + 1) * D_STATE])
            for l in range(nl):
                yo_ref[pl.ds(l * sb + q0 + bi, 1), g * gw:(g + 1) * gw] = (
                    yo[bi * nl + l:bi * nl + l + 1])


N_SSD_IN, N_SSD_OUT, N_SSD_SCRATCH = 11, 3, 6
N_ST_IN, N_ST_OUT = 5, 2


def _ssd_state_kernel(*refs, nc, sb, per):
    ins, rest = refs[:N_SSD_IN + N_ST_IN], refs[N_SSD_IN + N_ST_IN:]
    outs, scratch = rest[:N_SSD_OUT + N_ST_OUT], rest[N_SSD_OUT + N_ST_OUT:]
    st_refs = (*ins[N_SSD_IN:], *outs[N_SSD_OUT:], *scratch[N_SSD_SCRATCH:])
    c_rows = _sample2_setup(st_refs[2], st_refs[4], st_refs[7], sb=sb, per=per)
    _ssd_body(*ins[:N_SSD_IN], *outs[:N_SSD_OUT], *scratch[:N_SSD_SCRATCH], nc=nc,
              after_group=functools.partial(_sample2_group, refs=st_refs, c_rows=c_rows,
                                            sb=sb, per=per))


def _ssd_prompt_sample_state(proj, dt_raw, conv_w, conv_b, dt_bias_p, a_log_p, dskip_row, norm_g,
                             bsz, seq, cdec_flat, state, xw3, bs3, cs3, sb):
    q = CHUNK
    nc = seq // q
    db = state.shape[0]
    rb = DEC_SEQ * sb
    per = sb // nc
    assert rb == LANES and db // sb == bsz and per * nc == sb and per % 2 == 0
    row = lambda b, c: b * nc + c
    const = lambda b, c: (0, 0)
    blk2 = lambda width: pl.BlockSpec((None, rb, width), lambda i, j: (i, 0, 0))
    st_spec = pl.BlockSpec((per, HEADS, HEAD_DIM, D_STATE), lambda i, j: (i * nc + j, 0, 0, 0))
    return pl.pallas_call(
        functools.partial(_ssd_state_kernel, nc=nc, sb=sb, per=per),
        grid=(bsz, nc),
        in_specs=[
            pl.BlockSpec((q, D_SSM), lambda b, c: (row(b, c), OFF_Z // D_SSM)),
            pl.BlockSpec((q, D_SSM), lambda b, c: (row(b, c), OFF_XS // D_SSM)),
            pl.BlockSpec((q, D_BC), lambda b, c: (row(b, c), OFF_B // D_BC)),
            pl.BlockSpec((q, D_BC), lambda b, c: (row(b, c), OFF_C // D_BC)),
            pl.BlockSpec((q, LANES), lambda b, c: (row(b, c), 0)),
            pl.BlockSpec((CONV_WIDTH, D_CONV), const),
            pl.BlockSpec((1, D_CONV), const),
            pl.BlockSpec((1, LANES), const),
            pl.BlockSpec((1, LANES), const),
            pl.BlockSpec((1, D_SSM), const),
            pl.BlockSpec((1, D_SSM), const),
            pl.BlockSpec(memory_space=pltpu.SMEM),
            st_spec, blk2(D_SSM), blk2(D_BC), blk2(D_BC),
        ],
        out_specs=[
            pl.BlockSpec((q, D_SSM), lambda b, c: (row(b, c), 0)),
            pl.BlockSpec((None, CONV_WIDTH - 1, D_CONV), lambda b, c: (b, 0, 0)),
            pl.BlockSpec((None, HEADS, HEAD_DIM, D_STATE), lambda b, c: (b, 0, 0, 0)),
            st_spec, blk2(D_SSM),
        ],
        out_shape=[
            jax.ShapeDtypeStruct((bsz * seq, D_SSM), BF16),
            jax.ShapeDtypeStruct((bsz, CONV_WIDTH - 1, D_CONV), F32),
            jax.ShapeDtypeStruct((bsz, HEADS, HEAD_DIM, D_STATE), F32),
            jax.ShapeDtypeStruct(state.shape, F32),
            jax.ShapeDtypeStruct((db // sb, rb, D_SSM), F32),
        ],
        scratch_shapes=[
            pltpu.VMEM((D_CONV // LANES, q + SUBLANES, LANES), F32),
            pltpu.VMEM((D_STATE, D_SSM), F32),
            pltpu.VMEM((D_SSM // LANES, q + SUBLANES, LANES), F32),
            pltpu.VMEM((D_BC // LANES, q + SUBLANES, LANES), F32),
            pltpu.VMEM((D_BC // LANES, q + SUBLANES, LANES), F32),
            pltpu.VMEM((q, D_SSM), F32),
            pltpu.VMEM((D_SSM, rb), BF16),
        ],
        compiler_params=pltpu.CompilerParams(
            dimension_semantics=("arbitrary", "arbitrary"),
            vmem_limit_bytes=VMEM_LIMIT),
        name="ssd_prompt_sample_state",
    )(proj, proj, proj, proj, dt_raw, conv_w, conv_b, dt_bias_p, a_log_p, dskip_row, norm_g,
      cdec_flat, state, xw3, bs3, cs3)


def _outproj_sample_kernel(a_ref, yo_ref, ecum_ref, ydx_ref, z_ref, ng_ref, wa_ref, wb_ref, x_ref,
                           g_ref, y_ref, b_ref):
    gw = D_SSM // GROUPS
    for g in range(GROUPS):
        cs = slice(g * gw, (g + 1) * gw)
        y = ydx_ref[:, cs] + ecum_ref[:, cs] * yo_ref[:, cs]
        yz = y * _silu(z_ref[:, cs].astype(F32))
        ms = jnp.sum(yz * yz, axis=-1, keepdims=True) * (1.0 / gw)
        b_ref[:, cs] = (yz * lax.rsqrt(ms + EPS) * ng_ref[:, cs]).astype(BF16)
    _outproj_kernel(a_ref, b_ref, wa_ref, wb_ref, x_ref, g_ref, y_ref)


def _outproj_sample(out_a, yo, ecum, ydx, proj, norm_g, w_a, w_b, x2d, final_g):
    m = x2d.shape[0]
    tm = min(256, m)
    rowblk = lambda width: pl.BlockSpec((tm, width), lambda i: (i, 0))
    const = lambda shape, **kw: pl.BlockSpec(shape, lambda i: (0, 0), **kw)
    return pl.pallas_call(
        _outproj_sample_kernel,
        grid=(m // tm,),
        in_specs=[
            rowblk(D_POOL), rowblk(D_SSM), rowblk(D_SSM), rowblk(D_SSM),
            pl.BlockSpec((tm, D_SSM), lambda i: (i, OFF_Z // D_SSM)),
            const((1, D_SSM)),
            const((D_POOL, D_MODEL), pipeline_mode=pl.Buffered(1)),
            const((D_SSM, D_MODEL), pipeline_mode=pl.Buffered(1)),
            rowblk(D_MODEL), const((1, D_MODEL)),
        ],
        out_specs=rowblk(D_MODEL),
        out_shape=jax.ShapeDtypeStruct((m, D_MODEL), F32),
        scratch_shapes=[pltpu.VMEM((tm, D_SSM), BF16)],
        compiler_params=pltpu.CompilerParams(
            dimension_semantics=("arbitrary",),
            vmem_limit_bytes=VMEM_LIMIT),
        name="outproj_sample",
    )(out_a, yo, ecum, ydx, proj, norm_g, w_a, w_b, x2d, final_g)


def kernel(x_prompt, x_sample, state_pool, state_conv, state_ssm, norm_g, w_in, conv_w, conv_b,
           dt_bias, a_log, d_skip, ssm_norm_g, pool_w, pool_scale, w_out, final_g):
    bsz, seq, _ = x_prompt.shape
    db, nl, _ = x_sample.shape
    assert nl == DEC_SEQ and seq % CHUNK == 0 and w_in.shape[0] == 1

    w_t = jnp.transpose(w_in[0])
    w_a = w_out[0, :D_POOL].astype(BF16)
    w_b = w_out[0, D_POOL:].astype(BF16)
    pool_w_b = pool_w[0].astype(BF16)
    g_in = norm_g[0][None, :]
    g_fin = final_g[None, :]
    ps = pool_scale[0][None, :]
    cw = conv_w[0]
    cbias = conv_b[0][None, :]
    pad_h = lambda v: jnp.pad(v, (0, LANES - HEADS))[None, :]
    dtb = pad_h(dt_bias[0])
    alog = pad_h(a_log[0])
    dskip_row = jnp.repeat(d_skip[0], HEAD_DIM)[None, :]
    ng = ssm_norm_g[0][None, :]

    sb = SAMPLE_BLOCK
    nblk = db // sb
    rb = nl * sb
    xs2 = x_sample.reshape(nblk, sb, nl, D_MODEL).transpose(0, 2, 1, 3).reshape(db * nl, D_MODEL)
    proj_s, dt_s, w_bf, wdt_bf = _inproj(xs2, g_in, w_t, w_t, 1024)
    pj3 = proj_s.reshape(nblk, rb, D_MAIN)
    head_of_ch = jnp.arange(D_SSM, dtype=jnp.int32) // HEAD_DIM
    ehot = (jnp.arange(LANES, dtype=jnp.int32)[:, None] == head_of_ch[None, :]).astype(BF16)
    grp_of_row = jnp.arange(D_BC, dtype=jnp.int32) // D_STATE
    head_id = jnp.arange(LANES, dtype=jnp.int32)
    ghot = ((head_id[None, :] // HEADS_PER_GROUP == grp_of_row[:, None])
            & (head_id[None, :] < HEADS)).astype(BF16)
    (oa_s, npool_s, nconv_s, cs_s, bs_s, xw_s, ydx_s, ecum_s, cdec_s) = _sample1(
        pj3, dt_s.reshape(nblk, rb, LANES), jnp.transpose(state_pool[0], (1, 0, 2)),
        jnp.transpose(state_conv[0], (1, 0, 2)), pool_w_b, ps, cw, cbias, dtb, alog,
        dskip_row, ehot, ghot, sb)

    xp2 = x_prompt.reshape(bsz * seq, D_MODEL)
    proj_p, dt_p = _inproj(xp2, g_in, w_bf, wdt_bf, PROMPT_TN)
    oa_p, npool_p = _pool_prompt(proj_p, pool_w_b, ps, bsz, seq)
    ob_p, nconv_p, nssm_p, nssm_s, yo_s = _ssd_prompt_sample_state(
        proj_p, dt_p, cw, cbias, dtb, alog, dskip_row, ng, bsz, seq,
        cdec_s[:, :HEADS].reshape(db * HEADS), state_ssm[0], xw_s, bs_s, cs_s, sb)
    y_p = _outproj(oa_p, ob_p, w_a, w_b, xp2, g_fin).reshape(bsz, seq, D_MODEL)

    flat = lambda t: t.reshape(db * nl, t.shape[-1])
    y_s = _outproj_sample(flat(oa_s), flat(yo_s), flat(ecum_s), flat(ydx_s), proj_s, ng, w_a, w_b,
                          xs2, g_fin)
    y_s = y_s.reshape(nblk, nl, sb, D_MODEL).transpose(0, 2, 1, 3).reshape(db, nl, D_MODEL)

    return (y_p, y_s,
            npool_p[None], nconv_p[None], nssm_p[None],
            jnp.transpose(npool_s, (1, 0, 2))[None],
            jnp.transpose(nconv_s, (1, 0, 2))[None],
            nssm_s[None])
```

```python
import functools

import jax
import jax.numpy as jnp
from jax import lax
from jax.experimental import pallas as pl
from jax.experimental.pallas import tpu as pltpu

F32 = jnp.float32
BF16 = jnp.bfloat16

D_MODEL = 2048
D_POOL = 1024
POOL_WINDOWS = (2, 4, 8, 16)
POOL_GROUP = 256
POOL_HIST = 15
D_SSM = 3072
HEAD_DIM = 64
HEADS = 48
GROUPS = 8
HEADS_PER_GROUP = 6
D_STATE = 128
D_BC = GROUPS * D_STATE
CONV_WIDTH = 4
D_CONV = D_SSM + 2 * D_BC
D_MAIN = 2 * D_POOL + D_SSM + D_CONV
PAST_LEN = 16384
DEC_SEQ = 4
EPS = 1e-5

LANES = 128
SUBLANES = 8
VMEM_LIMIT = 56 * 1024 * 1024

OFF_Z = 0
OFF_XS = D_SSM
OFF_U = 2 * D_SSM
OFF_GATE = OFF_U + D_POOL
OFF_B = OFF_GATE + D_POOL
OFF_C = OFF_B + D_BC

PROMPT_TN = 2048
CHUNK = 128
SAMPLE_BLOCK = 32
NEG_BIG = -1e30
LOG2E = 1.4426950408889634


def _silu(v):
    h = 0.5 * v
    return h + h * jnp.tanh(h)


def _softplus(v):
    y = jnp.exp(-jnp.abs(v))
    u = 1.0 + y
    d = u - 1.0
    l1p = jnp.where(d == 0.0, y, jnp.log(u) * (y / jnp.where(d == 0.0, 1.0, d)))
    return jnp.maximum(v, 0.0) + l1p


def _split2(v):
    hi = v.astype(BF16)
    lo = (v - hi.astype(F32)).astype(BF16)
    return hi, lo


def _dot(a, b):
    return jnp.dot(a, b, preferred_element_type=F32)


def _dot_nt(a, b):
    return lax.dot_general(a, b, (((1,), (1,)), ((), ())), preferred_element_type=F32)


def _dot2(v, onehot):
    hi, lo = _split2(v)
    return _dot(hi, onehot) + _dot(lo, onehot)


def _inproj_kernel(x_ref, g_ref, w_ref, wdt_ref, o_ref, dt_ref, *rest):
    from_f32 = w_ref.dtype == F32
    (wb_out_ref, wdt_out_ref, h_ref) = rest if from_f32 else (None, None, *rest)

    @pl.when(pl.program_id(1) == 0)
    def _():
        x = x_ref[...]
        ms = jnp.mean(x * x, axis=-1, keepdims=True)
        h = (x * lax.rsqrt(ms + EPS) * g_ref[...]).astype(BF16)
        h_ref[...] = h
        if from_f32:
            wrow = lax.broadcasted_iota(jnp.int32, wdt_ref.shape, 0)
            wdt = jnp.where(wrow < HEADS, wdt_ref[...], 0.0).astype(BF16)
            wdt_out_ref[...] = wdt
        else:
            wdt = wdt_ref[...]
        dt_ref[...] = _dot_nt(h, wdt)

    if from_f32:
        wb = w_ref[...].astype(BF16)
        wb_out_ref[...] = wb
    else:
        wb = w_ref[...]
    o_ref[...] = _dot_nt(h_ref[...], wb).astype(BF16)


def _src_block(j):
    nz = (D_SSM + D_SSM) // 1024
    npool = 2 * D_POOL // 1024
    return jnp.where(j < nz, j + npool, jnp.where(j < nz + npool, j - nz, j))


def _inproj(x2d, norm_g, w, w_dt, tn):
    m = x2d.shape[0]
    tm = min(1024, m)
    from_f32 = w.dtype == F32
    assert tn == 1024 or not from_f32
    w_map = (lambda i, j: (_src_block(j), 0)) if from_f32 else (lambda i, j: (j, 0))
    wdt_map = (lambda i, j: (D_MAIN // LANES, 0)) if from_f32 else (lambda i, j: (0, 0))
    out_specs = [
        pl.BlockSpec((tm, tn), lambda i, j: (i, j)),
        pl.BlockSpec((tm, LANES), lambda i, j: (i, 0)),
    ]
    out_shape = [
        jax.ShapeDtypeStruct((m, D_MAIN), BF16),
        jax.ShapeDtypeStruct((m, LANES), F32),
    ]
    if from_f32:
        assert m == tm
        out_specs += [pl.BlockSpec((tn, D_MODEL), lambda i, j: (j, 0)),
                      pl.BlockSpec((LANES, D_MODEL), lambda i, j: (0, 0))]
        out_shape += [jax.ShapeDtypeStruct((D_MAIN, D_MODEL), BF16),
                      jax.ShapeDtypeStruct((LANES, D_MODEL), BF16)]
    return pl.pallas_call(
        _inproj_kernel,
        grid=(m // tm, D_MAIN // tn),
        in_specs=[
            pl.BlockSpec((tm, D_MODEL), lambda i, j: (i, 0)),
            pl.BlockSpec((1, D_MODEL), lambda i, j: (0, 0)),
            pl.BlockSpec((tn, D_MODEL), w_map),
            pl.BlockSpec((LANES, D_MODEL), wdt_map),
        ],
        out_specs=out_specs,
        out_shape=out_shape,
        scratch_shapes=[pltpu.VMEM((tm, D_MODEL), BF16)],
        compiler_params=pltpu.CompilerParams(
            dimension_semantics=("arbitrary", "arbitrary"),
            vmem_limit_bytes=VMEM_LIMIT),
        name="inproj",
    )(x2d, norm_g, w, w_dt)


def _outproj_kernel(a_ref, b_ref, w_ref, x_ref, g_ref, y_ref):
    acc = (_dot(a_ref[...], w_ref[0:D_POOL, :])
           + _dot(b_ref[...], w_ref[D_POOL:D_POOL + D_SSM, :]))
    r = x_ref[...] + acc
    ms = jnp.mean(r * r, axis=-1, keepdims=True)
    y_ref[...] = r * lax.rsqrt(ms + EPS) * g_ref[...]


def _outproj(out_a, out_b, w_bf, x2d, final_g):
    m = x2d.shape[0]
    tm = min(512, m)
    resident = pl.Buffered(1)
    return pl.pallas_call(
        _outproj_kernel,
        grid=(m // tm,),
        in_specs=[
            pl.BlockSpec((tm, D_POOL), lambda i: (i, 0)),
            pl.BlockSpec((tm, D_SSM), lambda i: (i, 0)),
            pl.BlockSpec((D_POOL + D_SSM, D_MODEL), lambda i: (0, 0), pipeline_mode=resident),
            pl.BlockSpec((tm, D_MODEL), lambda i: (i, 0)),
            pl.BlockSpec((1, D_MODEL), lambda i: (0, 0)),
        ],
        out_specs=pl.BlockSpec((tm, D_MODEL), lambda i: (i, 0)),
        out_shape=jax.ShapeDtypeStruct((m, D_MODEL), F32),
        compiler_params=pltpu.CompilerParams(
            dimension_semantics=("arbitrary",),
            vmem_limit_bytes=VMEM_LIMIT),
        name="outproj",
    )(out_a, out_b, w_bf, x2d, final_g)


def _pool_kernel(u_ref, gate_ref, pw_ref, ps_ref, wo_ref, oa_ref, np_ref, wob_ref, ubuf, *, tl, nt):
    t = pl.program_id(1)
    hist = POOL_HIST + 1
    wob_ref[...] = wo_ref[...].astype(BF16)

    @pl.when(t == 0)
    def _():
        ubuf[0:hist, :] = jnp.zeros((hist, D_POOL), F32)

    u = u_ref[...].astype(F32)
    ubuf[hist:hist + tl, :] = u
    pos = t * tl + lax.broadcasted_iota(jnp.int32, (tl, 1), 0)
    for gi, w in enumerate(POOL_WINDOWS):
        cs = slice(gi * POOL_GROUP, (gi + 1) * POOL_GROUP)
        ug = u[:, cs]
        acc = ubuf[0:hist + tl, cs]
        span = 1
        while span < w:
            acc = acc + pltpu.roll(acc, span, 0)
            span *= 2
        cnt = jnp.minimum(w, pos + 1).astype(F32)
        pooled = acc[hist:hist + tl] / cnt - ug
        mixed = _dot(pooled.astype(BF16), pw_ref[gi].astype(BF16))
        gt = gate_ref[:, cs].astype(F32)
        oa_ref[:, cs] = (mixed * ps_ref[:, cs] * _silu(gt)).astype(BF16)

    ubuf[0:hist, :] = ubuf[tl:tl + hist, :]

    @pl.when(t == nt - 1)
    def _():
        np_ref[...] = ubuf[tl + 1:tl + hist, :]


def _pool_prompt(proj, pool_w, pool_scale, w_out2d, bsz, seq):
    tl = min(512, seq)
    nt = seq // tl
    d_mix = w_out2d.shape[0]
    slab = d_mix // (bsz * nt)
    assert slab * bsz * nt == d_mix and slab % (2 * SUBLANES) == 0
    wslab = pl.BlockSpec((slab, D_MODEL), lambda b, t: (b * nt + t, 0))
    return pl.pallas_call(
        functools.partial(_pool_kernel, tl=tl, nt=nt),
        grid=(bsz, nt),
        in_specs=[
            pl.BlockSpec((tl, D_POOL), lambda b, t: (b * nt + t, OFF_U // D_POOL)),
            pl.BlockSpec((tl, D_POOL), lambda b, t: (b * nt + t, OFF_GATE // D_POOL)),
            pl.BlockSpec((len(POOL_WINDOWS), POOL_GROUP, POOL_GROUP), lambda b, t: (0, 0, 0)),
            pl.BlockSpec((1, D_POOL), lambda b, t: (0, 0)),
            wslab,
        ],
        out_specs=[
            pl.BlockSpec((tl, D_POOL), lambda b, t: (b * nt + t, 0)),
            pl.BlockSpec((None, POOL_HIST, D_POOL), lambda b, t: (b, 0, 0)),
            wslab,
        ],
        out_shape=[
            jax.ShapeDtypeStruct((bsz * seq, D_POOL), BF16),
            jax.ShapeDtypeStruct((bsz, POOL_HIST, D_POOL), F32),
            jax.ShapeDtypeStruct(w_out2d.shape, BF16),
        ],
        scratch_shapes=[pltpu.VMEM((tl + POOL_HIST + 1, D_POOL), F32)],
        compiler_params=pltpu.CompilerParams(
            dimension_semantics=("arbitrary", "arbitrary"),
            vmem_limit_bytes=VMEM_LIMIT),
        name="pool_prompt",
    )(proj, proj, pool_w, pool_scale, w_out2d)


def _ssd_body(z_ref, xs_ref, b_ref, c_ref, dtr_ref, cw_ref, cbias_ref, dtb_ref, alog_ref,
              dskip_ref, ng_ref,
              ob_ref, nconv_ref, nssm_ref,
              cbuf, st_ref, cvx_ref, cvb_ref, cvc_ref, y_ref, *, nc, after_group=None):
    q = CHUNK
    c_idx = pl.program_id(1)
    halo = SUBLANES

    nbx, nbb = D_SSM // LANES, D_BC // LANES

    @pl.when(c_idx == 0)
    def _():
        cbuf[:, 0:halo, :] = jnp.zeros((cbuf.shape[0], halo, LANES), F32)
        st_ref[...] = jnp.zeros(st_ref.shape, F32)

    nv = (halo + q) // SUBLANES
    for blk in range(D_CONV // LANES):
        ls = slice(blk * LANES, (blk + 1) * LANES)
        if blk < nbx:
            src_ref, off, dst = xs_ref, blk * LANES, cvx_ref.at[blk]
        elif blk < nbx + nbb:
            src_ref, off, dst = b_ref, (blk - nbx) * LANES, cvb_ref.at[blk - nbx]
        else:
            src_ref, off, dst = c_ref, (blk - nbx - nbb) * LANES, cvc_ref.at[blk - nbx - nbb]
        cbuf[blk, halo:halo + q, :] = src_ref[:, off:off + LANES].astype(F32)
        xv = [cbuf[blk, pl.ds(a, SUBLANES, stride=nv), :] for a in range(nv)]
        wrap = [pltpu.roll(xv[nv - k], 1, 0) for k in range(1, CONV_WIDTH)]
        taps = [cw_ref[k:k + 1, ls] for k in range(CONV_WIDTH)]
        bias = cbias_ref[:, ls]
        for a in range(nv):
            acc = bias + taps[CONV_WIDTH - 1] * xv[a]
            for k in range(1, CONV_WIDTH):
                src = xv[a - k] if a >= k else wrap[k - a - 1]
                acc = acc + taps[CONV_WIDTH - 1 - k] * src
            dst[pl.ds(a, SUBLANES, stride=nv), :] = _silu(acc)
    rows = slice(halo, halo + q)

    dt = _softplus(dtr_ref[...] + dtb_ref[...])
    a_neg = -jnp.exp(alog_ref[...])
    da = dt * (a_neg * LOG2E)
    row = lax.broadcasted_iota(jnp.int32, (q, LANES), 0)
    a2 = da
    shift = 1
    while shift < q:
        a2 = a2 + jnp.where(row >= shift, pltpu.roll(a2, shift, 0), 0.0)
        shift *= 2
    a2_t = jnp.transpose(a2)
    ldt_t = jnp.log2(jnp.transpose(dt))
    a2_end_t = a2_t[:, q - 1:q]
    w_t = jnp.exp2(a2_end_t - a2_t + ldt_t)
    cdec_t = jnp.exp2(a2_end_t)
    srow_t = a2_t - ldt_t
    ea = jnp.exp2(a2)

    li = lax.broadcasted_iota(jnp.int32, (q, q), 0)
    si = lax.broadcasted_iota(jnp.int32, (q, q), 1)
    tri = li >= si
    lane = lax.broadcasted_iota(jnp.int32, (q, LANES), 1)
    lo_half = lane < HEAD_DIM

    for g in range(GROUPS):
        gs = slice(g * D_STATE, (g + 1) * D_STATE)
        c_gb = cvc_ref[g, rows, :].astype(BF16)
        b_g = cvb_ref[g, rows, :]
        cb = _dot_nt(c_gb, b_g.astype(BF16))
        b_t = jnp.transpose(b_g)
        ppg = HEADS_PER_GROUP // 2
        gl = slice(g * ppg * LANES, (g + 1) * ppg * LANES)
        y_off = _dot(c_gb, st_ref[:, gl].astype(BF16))
        for j in range(ppg):
            blk = g * ppg + j
            ls = slice(blk * LANES, (blk + 1) * LANES)
            sc, bw, ecol, dec = [], [], [], []
            for h in (2 * blk, 2 * blk + 1):
                a_col = jnp.broadcast_to(a2[:, h:h + 1], (q, q))
                decay_dt = jnp.exp2(jnp.where(tri, a_col - srow_t[h:h + 1, :], NEG_BIG))
                sc.append((cb * decay_dt).astype(BF16))
                bw.append((b_t * w_t[h:h + 1, :]).astype(BF16))
                ecol.append(jnp.broadcast_to(ea[:, h:h + 1], (q, LANES)))
                dec.append(jnp.broadcast_to(cdec_t[h:h + 1, :], (q, LANES)))
            x_p = cvx_ref[blk, rows, :].astype(BF16)
            zero_b = jnp.zeros_like(x_p)
            x_bd = jnp.concatenate([jnp.where(lo_half, x_p, zero_b),
                                    jnp.where(lo_half, zero_b, x_p)], axis=0)
            lhs1 = jnp.concatenate([jnp.concatenate(sc, axis=1),
                                    jnp.concatenate(bw, axis=1)], axis=0)
            r1 = _dot(lhs1, x_bd)
            y_ref[:, ls] = (r1[0:q] + y_off[:, j * LANES:(j + 1) * LANES]
                            * jnp.where(lo_half, ecol[0], ecol[1]))
            st_ref[:, ls] = (st_ref[:, ls] * jnp.where(lo_half, dec[0], dec[1])
                             + r1[q:2 * q])
        if after_group is not None:
            after_group(g)

    gw = D_SSM // GROUPS
    bpg = gw // LANES
    for g in range(GROUPS):
        yz, ssq = [], None
        for blk in range(g * bpg, (g + 1) * bpg):
            ls = slice(blk * LANES, (blk + 1) * LANES)
            y = y_ref[:, ls] + cvx_ref[blk, rows, :] * dskip_ref[:, ls]
            v = y * _silu(z_ref[:, ls].astype(F32))
            yz.append(v)
            ssq = v * v if ssq is None else ssq + v * v
        scale = lax.rsqrt(jnp.sum(ssq, axis=-1, keepdims=True) * (1.0 / gw) + EPS)
        for i, blk in enumerate(range(g * bpg, (g + 1) * bpg)):
            ls = slice(blk * LANES, (blk + 1) * LANES)
            ob_ref[:, ls] = (yz[i] * scale * ng_ref[:, ls]).astype(BF16)

    cbuf[:, 0:halo, :] = cbuf[:, q:q + halo, :]

    @pl.when(c_idx == nc - 1)
    def _():
        for blk in range(D_CONV // LANES):
            nconv_ref[:, blk * LANES:(blk + 1) * LANES] = cbuf[blk, q + halo - 3:q + halo, :]
        for blk in range(D_SSM // LANES):
            t = jnp.transpose(st_ref[:, blk * LANES:(blk + 1) * LANES])
            nssm_ref[2 * blk:2 * blk + 2] = t.reshape(2, HEAD_DIM, D_STATE)


def _sample1_kernel(pj_ref, dtr_ref, sp_ref, scv_ref, pw_ref, ps_ref, cw_ref, cbias_ref, dtb_ref,
                    alog_ref, dskip_ref, ehot_ref, ghot_ref,
                    oa_ref, npool_ref, nconv_ref, cs_ref, bs_ref, xw_ref, ydx_ref, ecum_ref,
                    cdec_ref, *, sb):
    nl = DEC_SEQ

    def rows(l):
        return slice(l * sb, (l + 1) * sb)

    def tok(l, off, width):
        return pj_ref[rows(l), off:off + width].astype(F32)

    for gi, w in enumerate(POOL_WINDOWS):
        c0 = gi * POOL_GROUP
        ext = [sp_ref[j, :, c0:c0 + POOL_GROUP] for j in range(POOL_HIST)]
        ext += [tok(l, OFF_U + c0, POOL_GROUP) for l in range(nl)]
        for l in range(nl):
            acc = ext[POOL_HIST + l]
            for k in range(1, w):
                acc = acc + ext[POOL_HIST + l - k]
            cnt = float(min(w, PAST_LEN + l + 1))
            pooled = acc / cnt - ext[POOL_HIST + l]
            mixed = _dot(pooled.astype(BF16), pw_ref[gi].astype(BF16))
            gt = tok(l, OFF_GATE + c0, POOL_GROUP)
            oa_ref[rows(l), c0:c0 + POOL_GROUP] = (
                mixed * ps_ref[:, c0:c0 + POOL_GROUP] * _silu(gt)).astype(BF16)
    for j in range(POOL_HIST):
        src = j + nl
        if src < POOL_HIST:
            npool_ref[j] = sp_ref[src]
        else:
            npool_ref[j] = tok(src - POOL_HIST, OFF_U, D_POOL)

    hist = CONV_WIDTH - 1
    conv_out = {}
    for name, poff, coff, width in (("x", OFF_XS, 0, D_SSM), ("b", OFF_B, D_SSM, D_BC),
                                    ("c", OFF_C, D_SSM + D_BC, D_BC)):
        ext = [scv_ref[j, :, coff:coff + width] for j in range(hist)]
        ext += [tok(l, poff, width) for l in range(nl)]
        outs = []
        for l in range(nl):
            acc = cbias_ref[:, coff:coff + width]
            for k in range(CONV_WIDTH):
                acc = acc + cw_ref[k:k + 1, coff:coff + width] * ext[l + k]
            outs.append(_silu(acc))
        conv_out[name] = outs
        for j in range(hist):
            nconv_ref[j, :, coff:coff + width] = ext[j + nl]
    xs, bs, cs = conv_out["x"], conv_out["b"], conv_out["c"]
    for l in range(nl):
        cs_ref[rows(l), :] = cs[l]
        bs_ref[rows(l), :] = bs[l].astype(BF16)

    a_neg = -jnp.exp(alog_ref[...])
    dt, a_cum = [], []
    run = None
    for l in range(nl):
        d = _softplus(dtr_ref[rows(l), :] + dtb_ref[...])
        dt.append(d)
        run = d * a_neg if run is None else run + d * a_neg
        a_cum.append(run)
    cdec_ref[...] = jnp.exp(a_cum[nl - 1])

    def onehot_rows(mats, onehot):
        parts = [_split2(m) for m in mats]
        stack = jnp.concatenate([p[0] for p in parts] + [p[1] for p in parts], axis=0)
        res = _dot(stack, onehot)
        n = len(mats)
        return [res[k * sb:(k + 1) * sb] + res[(n + k) * sb:(n + k + 1) * sb] for k in range(n)]

    pairs = [(l, s) for l in range(nl) for s in range(l + 1)]
    cbh = onehot_rows([cs[l] * bs[s] for l, s in pairs], ghot_ref[...])
    gls = [c * jnp.exp(a_cum[l] - a_cum[s]) * dt[s] for c, (l, s) in zip(cbh, pairs)]
    w_end = [jnp.exp(a_cum[nl - 1] - a_cum[l]) * dt[l] for l in range(nl)]
    e_cum = [jnp.exp(a_cum[l]) for l in range(nl)]
    chunk = 4 * LANES
    for c0 in range(0, D_SSM, chunk):
        cl = slice(c0, c0 + chunk)
        ex = onehot_rows(gls + w_end + e_cum, ehot_ref[:, cl])
        g_ex, w_ex, e_ex = ex[:len(pairs)], ex[len(pairs):len(pairs) + nl], ex[len(pairs) + nl:]
        xc = [x[:, cl] for x in xs]
        for l in range(nl):
            ydiag = dskip_ref[:, cl] * xc[l]
            for k, (pl_, ps_) in enumerate(pairs):
                if pl_ == l:
                    ydiag = ydiag + g_ex[k] * xc[ps_]
            ydx_ref[rows(l), cl] = ydiag
            xw_ref[rows(l), cl] = w_ex[l] * xc[l]
            ecum_ref[rows(l), cl] = e_ex[l]


def _sample1(pj3, dtr3, sp3, scv3, pool_w, pool_scale, conv_w, conv_b, dt_bias_p, a_log_p,
             dskip_row, ehot, ghot, sb):
    nblk = pj3.shape[0]
    db = nblk * sb
    rb = DEC_SEQ * sb
    blk2 = lambda width: pl.BlockSpec((None, rb, width), lambda i: (i, 0, 0))
    hist3 = lambda n, width: pl.BlockSpec((n, sb, width), lambda i: (0, i, 0))
    const2 = lambda shape: pl.BlockSpec(shape, lambda i: (0, 0))
    blk_shape = lambda width, dt: jax.ShapeDtypeStruct((nblk, rb, width), dt)
    return pl.pallas_call(
        functools.partial(_sample1_kernel, sb=sb),
        grid=(nblk,),
        in_specs=[
            blk2(D_MAIN), blk2(LANES), hist3(POOL_HIST, D_POOL), hist3(CONV_WIDTH - 1, D_CONV),
            pl.BlockSpec((len(POOL_WINDOWS), POOL_GROUP, POOL_GROUP), lambda i: (0, 0, 0)),
            const2((1, D_POOL)), const2((CONV_WIDTH, D_CONV)), const2((1, D_CONV)),
            const2((1, LANES)), const2((1, LANES)), const2((1, D_SSM)),
            const2((LANES, D_SSM)), const2((D_BC, LANES)),
        ],
        out_specs=[
            blk2(D_POOL), hist3(POOL_HIST, D_POOL), hist3(CONV_WIDTH - 1, D_CONV),
            blk2(D_BC), blk2(D_BC), blk2(D_SSM), blk2(D_SSM), blk2(D_SSM),
            pl.BlockSpec((sb, LANES), lambda i: (i, 0)),
        ],
        out_shape=[
            blk_shape(D_POOL, BF16),
            jax.ShapeDtypeStruct((POOL_HIST, db, D_POOL), F32),
            jax.ShapeDtypeStruct((CONV_WIDTH - 1, db, D_CONV), F32),
            blk_shape(D_BC, F32), blk_shape(D_BC, BF16),
            blk_shape(D_SSM, F32), blk_shape(D_SSM, F32), blk_shape(D_SSM, F32),
            jax.ShapeDtypeStruct((db, LANES), F32),
        ],
        compiler_params=pltpu.CompilerParams(
            dimension_semantics=("arbitrary",),
            vmem_limit_bytes=VMEM_LIMIT),
        name="sample_elementwise",
    )(pj3, dtr3, sp3, scv3, pool_w, pool_scale, conv_w, conv_b, dt_bias_p, a_log_p, dskip_row,
      ehot, ghot)


def _sample2_setup(xw_ref, cs_ref, xwt_ref, *, sb, per):
    j = pl.program_id(1)

    @pl.when(j == 0)
    def _():
        for blk in range(D_SSM // LANES):
            xwt_ref[blk * LANES:(blk + 1) * LANES, :] = jnp.transpose(
                xw_ref[:, blk * LANES:(blk + 1) * LANES]).astype(BF16)

    return [jnp.concatenate([cs_ref[pl.ds(l * sb + per * j + 2 * pr + bi, 1), :]
                             for bi in range(2) for l in range(DEC_SEQ)], axis=0).astype(BF16)
            for pr in range(per // 2)]


def _sample2_group(g, refs, c_rows, *, sb, per):
    cdec_ref, st_ref, _, bs_ref, _, nst_ref, yo_ref, xwt_ref = refs
    i = pl.program_id(0)
    j = pl.program_id(1)
    nl = DEC_SEQ
    rows = sb * nl
    seq_of_row = lax.broadcasted_iota(jnp.int32, (rows, 1), 0) & (sb - 1)
    gw = HEADS_PER_GROUP * HEAD_DIM
    gs = slice(g * D_STATE, (g + 1) * D_STATE)
    for pr in range(per // 2):
        q0 = per * j + 2 * pr
        c8 = c_rows[pr][:, gs]
        b_blk = bs_ref[:, gs]
        zero_b = jnp.zeros_like(b_blk)
        w2 = jnp.concatenate([jnp.where(seq_of_row == q0, b_blk, zero_b),
                              jnp.where(seq_of_row == q0 + 1, b_blk, zero_b)], axis=1)
        u2 = _dot(xwt_ref[g * gw:(g + 1) * gw, :], w2)
        for bi in range(2):
            sq = 2 * pr + bi
            s0 = st_ref[sq, g * HEADS_PER_GROUP:(g + 1) * HEADS_PER_GROUP].reshape(gw, D_STATE)
            yo = _dot_nt(c8, s0.astype(BF16))
            base = (i * sb + q0 + bi) * HEADS + g * HEADS_PER_GROUP
            for r in range(HEADS_PER_GROUP):
                dec = cdec_ref[base + r]
                rs = slice(r * HEAD_DIM, (r + 1) * HEAD_DIM)
                nst_ref[sq, g * HEADS_PER_GROUP + r] = (
                    s0[rs] * dec + u2[rs, bi * D_STATE:(bi + 1) * D_STATE])
            for l in range(nl):
                yo_ref[pl.ds(l * sb + q0 + bi, 1), g * gw:(g + 1) * gw] = (
                    yo[bi * nl + l:bi * nl + l + 1])


N_SSD_IN, N_SSD_OUT, N_SSD_SCRATCH = 11, 3, 6
N_ST_IN, N_ST_OUT = 5, 2


def _ssd_state_kernel(*refs, nc, sb, per):
    ins, rest = refs[:N_SSD_IN + N_ST_IN], refs[N_SSD_IN + N_ST_IN:]
    outs, scratch = rest[:N_SSD_OUT + N_ST_OUT], rest[N_SSD_OUT + N_ST_OUT:]
    st_refs = (*ins[N_SSD_IN:], *outs[N_SSD_OUT:], *scratch[N_SSD_SCRATCH:])
    c_rows = _sample2_setup(st_refs[2], st_refs[4], st_refs[7], sb=sb, per=per)
    _ssd_body(*ins[:N_SSD_IN], *outs[:N_SSD_OUT], *scratch[:N_SSD_SCRATCH], nc=nc,
              after_group=functools.partial(_sample2_group, refs=st_refs, c_rows=c_rows,
                                            sb=sb, per=per))


def _ssd_prompt_sample_state(proj, dt_raw, conv_w, conv_b, dt_bias_p, a_log_p, dskip_row, norm_g,
                             bsz, seq, cdec_flat, state, xw3, bs3, cs3, sb):
    q = CHUNK
    nc = seq // q
    db = state.shape[0]
    rb = DEC_SEQ * sb
    per = sb // nc
    assert rb == LANES and db // sb == bsz and per * nc == sb and per % 2 == 0
    row = lambda b, c: b * nc + c
    const = lambda b, c: (0, 0)
    blk2 = lambda width: pl.BlockSpec((None, rb, width), lambda i, j: (i, 0, 0))
    st_spec = pl.BlockSpec((per, HEADS, HEAD_DIM, D_STATE), lambda i, j: (i * nc + j, 0, 0, 0))
    return pl.pallas_call(
        functools.partial(_ssd_state_kernel, nc=nc, sb=sb, per=per),
        grid=(bsz, nc),
        in_specs=[
            pl.BlockSpec((q, D_SSM), lambda b, c: (row(b, c), OFF_Z // D_SSM)),
            pl.BlockSpec((q, D_SSM), lambda b, c: (row(b, c), OFF_XS // D_SSM)),
            pl.BlockSpec((q, D_BC), lambda b, c: (row(b, c), OFF_B // D_BC)),
            pl.BlockSpec((q, D_BC), lambda b, c: (row(b, c), OFF_C // D_BC)),
            pl.BlockSpec((q, LANES), lambda b, c: (row(b, c), 0)),
            pl.BlockSpec((CONV_WIDTH, D_CONV), const),
            pl.BlockSpec((1, D_CONV), const),
            pl.BlockSpec((1, LANES), const),
            pl.BlockSpec((1, LANES), const),
            pl.BlockSpec((1, D_SSM), const),
            pl.BlockSpec((1, D_SSM), const),
            pl.BlockSpec(memory_space=pltpu.SMEM),
            st_spec, blk2(D_SSM), blk2(D_BC), blk2(D_BC),
        ],
        out_specs=[
            pl.BlockSpec((q, D_SSM), lambda b, c: (row(b, c), 0)),
            pl.BlockSpec((None, CONV_WIDTH - 1, D_CONV), lambda b, c: (b, 0, 0)),
            pl.BlockSpec((None, HEADS, HEAD_DIM, D_STATE), lambda b, c: (b, 0, 0, 0)),
            st_spec, blk2(D_SSM),
        ],
        out_shape=[
            jax.ShapeDtypeStruct((bsz * seq, D_SSM), BF16),
            jax.ShapeDtypeStruct((bsz, CONV_WIDTH - 1, D_CONV), F32),
            jax.ShapeDtypeStruct((bsz, HEADS, HEAD_DIM, D_STATE), F32),
            jax.ShapeDtypeStruct(state.shape, F32),
            jax.ShapeDtypeStruct((db // sb, rb, D_SSM), F32),
        ],
        scratch_shapes=[
            pltpu.VMEM((D_CONV // LANES, q + SUBLANES, LANES), F32),
            pltpu.VMEM((D_STATE, D_SSM), F32),
            pltpu.VMEM((D_SSM // LANES, q + SUBLANES, LANES), F32),
            pltpu.VMEM((D_BC // LANES, q + SUBLANES, LANES), F32),
            pltpu.VMEM((D_BC // LANES, q + SUBLANES, LANES), F32),
            pltpu.VMEM((q, D_SSM), F32),
            pltpu.VMEM((D_SSM, rb), BF16),
        ],
        compiler_params=pltpu.CompilerParams(
            dimension_semantics=("arbitrary", "arbitrary"),
            vmem_limit_bytes=VMEM_LIMIT),
        name="ssd_prompt_sample_state",
    )(proj, proj, proj, proj, dt_raw, conv_w, conv_b, dt_bias_p, a_log_p, dskip_row, norm_g,
      cdec_flat, state, xw3, bs3, cs3)


def _outproj_sample_kernel(a_ref, yo_ref, ecum_ref, ydx_ref, z_ref, ng_ref, w_ref, x_ref,
                           g_ref, y_ref, b_ref):
    gw = D_SSM // GROUPS
    for g in range(GROUPS):
        cs = slice(g * gw, (g + 1) * gw)
        y = ydx_ref[:, cs] + ecum_ref[:, cs] * yo_ref[:, cs]
        yz = y * _silu(z_ref[:, cs].astype(F32))
        ms = jnp.sum(yz * yz, axis=-1, keepdims=True) * (1.0 / gw)
        b_ref[:, cs] = (yz * lax.rsqrt(ms + EPS) * ng_ref[:, cs]).astype(BF16)
    _outproj_kernel(a_ref, b_ref, w_ref, x_ref, g_ref, y_ref)


def _outproj_sample(out_a, yo, ecum, ydx, proj, norm_g, w_bf, x2d, final_g):
    m = x2d.shape[0]
    tm = min(256, m)
    rowblk = lambda width: pl.BlockSpec((tm, width), lambda i: (i, 0))
    const = lambda shape, **kw: pl.BlockSpec(shape, lambda i: (0, 0), **kw)
    return pl.pallas_call(
        _outproj_sample_kernel,
        grid=(m // tm,),
        in_specs=[
            rowblk(D_POOL), rowblk(D_SSM), rowblk(D_SSM), rowblk(D_SSM),
            pl.BlockSpec((tm, D_SSM), lambda i: (i, OFF_Z // D_SSM)),
            const((1, D_SSM)),
            const((D_POOL + D_SSM, D_MODEL), pipeline_mode=pl.Buffered(1)),
            rowblk(D_MODEL), const((1, D_MODEL)),
        ],
        out_specs=rowblk(D_MODEL),
        out_shape=jax.ShapeDtypeStruct((m, D_MODEL), F32),
        scratch_shapes=[pltpu.VMEM((tm, D_SSM), BF16)],
        compiler_params=pltpu.CompilerParams(
            dimension_semantics=("arbitrary",),
            vmem_limit_bytes=VMEM_LIMIT),
        name="outproj_sample",
    )(out_a, yo, ecum, ydx, proj, norm_g, w_bf, x2d, final_g)


def kernel(x_prompt, x_sample, state_pool, state_conv, state_ssm, norm_g, w_in, conv_w, conv_b,
           dt_bias, a_log, d_skip, ssm_norm_g, pool_w, pool_scale, w_out, final_g):
    bsz, seq, _ = x_prompt.shape
    db, nl, _ = x_sample.shape
    assert nl == DEC_SEQ and seq % CHUNK == 0 and w_in.shape[0] == 1

    w_t = jnp.transpose(w_in[0])
    pool_w_b = pool_w[0]
    g_in = norm_g[0][None, :]
    g_fin = final_g[None, :]
    ps = pool_scale[0][None, :]
    cw = conv_w[0]
    cbias = conv_b[0][None, :]
    pad_h = lambda v: jnp.pad(v, (0, LANES - HEADS))[None, :]
    dtb = pad_h(dt_bias[0])
    alog = pad_h(a_log[0])
    dskip_row = jnp.repeat(d_skip[0], HEAD_DIM)[None, :]
    ng = ssm_norm_g[0][None, :]

    sb = SAMPLE_BLOCK
    nblk = db // sb
    rb = nl * sb
    xs2 = x_sample.reshape(nblk, sb, nl, D_MODEL).transpose(0, 2, 1, 3).reshape(db * nl, D_MODEL)
    proj_s, dt_s, w_bf, wdt_bf = _inproj(xs2, g_in, w_t, w_t, 1024)
    pj3 = proj_s.reshape(nblk, rb, D_MAIN)
    head_of_ch = jnp.arange(D_SSM, dtype=jnp.int32) // HEAD_DIM
    ehot = (jnp.arange(LANES, dtype=jnp.int32)[:, None] == head_of_ch[None, :]).astype(BF16)
    grp_of_row = jnp.arange(D_BC, dtype=jnp.int32) // D_STATE
    head_id = jnp.arange(LANES, dtype=jnp.int32)
    ghot = ((head_id[None, :] // HEADS_PER_GROUP == grp_of_row[:, None])
            & (head_id[None, :] < HEADS)).astype(BF16)
    (oa_s, npool_s, nconv_s, cs_s, bs_s, xw_s, ydx_s, ecum_s, cdec_s) = _sample1(
        pj3, dt_s.reshape(nblk, rb, LANES), jnp.transpose(state_pool[0], (1, 0, 2)),
        jnp.transpose(state_conv[0], (1, 0, 2)), pool_w_b, ps, cw, cbias, dtb, alog,
        dskip_row, ehot, ghot, sb)

    xp2 = x_prompt.reshape(bsz * seq, D_MODEL)
    proj_p, dt_p = _inproj(xp2, g_in, w_bf, wdt_bf, PROMPT_TN)
    oa_p, npool_p, w_out_bf = _pool_prompt(proj_p, pool_w_b, ps, w_out[0], bsz, seq)
    ob_p, nconv_p, nssm_p, nssm_s, yo_s = _ssd_prompt_sample_state(
        proj_p, dt_p, cw, cbias, dtb, alog, dskip_row, ng, bsz, seq,
        cdec_s[:, :HEADS].reshape(db * HEADS), state_ssm[0], xw_s, bs_s, cs_s, sb)
    y_p = _outproj(oa_p, ob_p, w_out_bf, xp2, g_fin).reshape(bsz, seq, D_MODEL)

    flat = lambda t: t.reshape(db * nl, t.shape[-1])
    y_s = _outproj_sample(flat(oa_s), flat(yo_s), flat(ecum_s), flat(ydx_s), proj_s, ng, w_out_bf,
                          xs2, g_fin)
    y_s = y_s.reshape(nblk, nl, sb, D_MODEL).transpose(0, 2, 1, 3).reshape(db, nl, D_MODEL)

    return (y_p, y_s,
            npool_p[None], nconv_p[None], nssm_p[None],
            jnp.transpose(npool_s, (1, 0, 2))[None],
            jnp.transpose(nconv_s, (1, 0, 2))[None],
            nssm_s[None])
```

```python
import functools

import jax
import jax.numpy as jnp
from jax import lax
from jax.experimental import pallas as pl
from jax.experimental.pallas import tpu as pltpu

F32 = jnp.float32
BF16 = jnp.bfloat16

D_MODEL = 2048
D_POOL = 1024
POOL_WINDOWS = (2, 4, 8, 16)
POOL_GROUP = 256
POOL_HIST = 15
D_SSM = 3072
HEAD_DIM = 64
HEADS = 48
GROUPS = 8
HEADS_PER_GROUP = 6
D_STATE = 128
D_BC = GROUPS * D_STATE
CONV_WIDTH = 4
D_CONV = D_SSM + 2 * D_BC
D_MAIN = 2 * D_POOL + D_SSM + D_CONV
PAST_LEN = 16384
DEC_SEQ = 4
EPS = 1e-5

LANES = 128
SUBLANES = 8
VMEM_LIMIT = 56 * 1024 * 1024

OFF_Z = 0
OFF_XS = D_SSM
OFF_U = 2 * D_SSM
OFF_GATE = OFF_U + D_POOL
OFF_B = OFF_GATE + D_POOL
OFF_C = OFF_B + D_BC

PROMPT_TN = 2048
CHUNK = 128
SAMPLE_BLOCK = 32
NEG_BIG = -1e30
LOG2E = 1.4426950408889634


def _silu(v):
    h = 0.5 * v
    return h + h * jnp.tanh(h)


def _softplus(v):
    y = jnp.exp(-jnp.abs(v))
    u = 1.0 + y
    d = u - 1.0
    l1p = jnp.where(d == 0.0, y, jnp.log(u) * (y / jnp.where(d == 0.0, 1.0, d)))
    return jnp.maximum(v, 0.0) + l1p


def _split2(v):
    hi = v.astype(BF16)
    lo = (v - hi.astype(F32)).astype(BF16)
    return hi, lo


def _dot(a, b):
    return jnp.dot(a, b, preferred_element_type=F32)


def _dot_nt(a, b):
    return lax.dot_general(a, b, (((1,), (1,)), ((), ())), preferred_element_type=F32)


def _dot2(v, onehot):
    hi, lo = _split2(v)
    return _dot(hi, onehot) + _dot(lo, onehot)


def _inproj_kernel(x_ref, g_ref, w_ref, wdt_ref, o_ref, dt_ref, *rest):
    from_f32 = w_ref.dtype == F32
    (wb_out_ref, wdt_out_ref, h_ref) = rest if from_f32 else (None, None, *rest)

    @pl.when(pl.program_id(1) == 0)
    def _():
        x = x_ref[...]
        ms = jnp.mean(x * x, axis=-1, keepdims=True)
        h = (x * lax.rsqrt(ms + EPS) * g_ref[...]).astype(BF16)
        h_ref[...] = h
        if from_f32:
            wrow = lax.broadcasted_iota(jnp.int32, wdt_ref.shape, 0)
            wdt = jnp.where(wrow < HEADS, wdt_ref[...], 0.0).astype(BF16)
            wdt_out_ref[...] = wdt
        else:
            wdt = wdt_ref[...]
        dt_ref[...] = _dot_nt(h, wdt)

    if from_f32:
        wb = w_ref[...].astype(BF16)
        wb_out_ref[...] = wb
    else:
        wb = w_ref[...]
    o_ref[...] = _dot_nt(h_ref[...], wb).astype(BF16)


def _src_block(j):
    nz = (D_SSM + D_SSM) // 1024
    npool = 2 * D_POOL // 1024
    return jnp.where(j < nz, j + npool, jnp.where(j < nz + npool, j - nz, j))


def _inproj(x2d, norm_g, w, w_dt, tn):
    m = x2d.shape[0]
    tm = min(1024, m)
    from_f32 = w.dtype == F32
    assert tn == 1024 or not from_f32
    w_map = (lambda i, j: (_src_block(j), 0)) if from_f32 else (lambda i, j: (j, 0))
    wdt_map = (lambda i, j: (D_MAIN // LANES, 0)) if from_f32 else (lambda i, j: (0, 0))
    out_specs = [
        pl.BlockSpec((tm, tn), lambda i, j: (i, j)),
        pl.BlockSpec((tm, LANES), lambda i, j: (i, 0)),
    ]
    out_shape = [
        jax.ShapeDtypeStruct((m, D_MAIN), BF16),
        jax.ShapeDtypeStruct((m, LANES), F32),
    ]
    if from_f32:
        assert m == tm
        out_specs += [pl.BlockSpec((tn, D_MODEL), lambda i, j: (j, 0)),
                      pl.BlockSpec((LANES, D_MODEL), lambda i, j: (0, 0))]
        out_shape += [jax.ShapeDtypeStruct((D_MAIN, D_MODEL), BF16),
                      jax.ShapeDtypeStruct((LANES, D_MODEL), BF16)]
    return pl.pallas_call(
        _inproj_kernel,
        grid=(m // tm, D_MAIN // tn),
        in_specs=[
            pl.BlockSpec((tm, D_MODEL), lambda i, j: (i, 0)),
            pl.BlockSpec((1, D_MODEL), lambda i, j: (0, 0)),
            pl.BlockSpec((tn, D_MODEL), w_map),
            pl.BlockSpec((LANES, D_MODEL), wdt_map),
        ],
        out_specs=out_specs,
        out_shape=out_shape,
        scratch_shapes=[pltpu.VMEM((tm, D_MODEL), BF16)],
        compiler_params=pltpu.CompilerParams(
            dimension_semantics=("arbitrary", "arbitrary"),
            vmem_limit_bytes=VMEM_LIMIT),
        name="inproj",
    )(x2d, norm_g, w, w_dt)


def _outproj_kernel(a_ref, b_ref, w_ref, x_ref, g_ref, y_ref):
    acc = (_dot(a_ref[...], w_ref[0:D_POOL, :])
           + _dot(b_ref[...], w_ref[D_POOL:D_POOL + D_SSM, :]))
    r = x_ref[...] + acc
    ms = jnp.mean(r * r, axis=-1, keepdims=True)
    y_ref[...] = r * lax.rsqrt(ms + EPS) * g_ref[...]


def _outproj(out_a, out_b, w_bf, x2d, final_g):
    m = x2d.shape[0]
    tm = min(512, m)
    resident = pl.Buffered(1)
    return pl.pallas_call(
        _outproj_kernel,
        grid=(m // tm,),
        in_specs=[
            pl.BlockSpec((tm, D_POOL), lambda i: (i, 0)),
            pl.BlockSpec((tm, D_SSM), lambda i: (i, 0)),
            pl.BlockSpec((D_POOL + D_SSM, D_MODEL), lambda i: (0, 0), pipeline_mode=resident),
            pl.BlockSpec((tm, D_MODEL), lambda i: (i, 0)),
            pl.BlockSpec((1, D_MODEL), lambda i: (0, 0)),
        ],
        out_specs=pl.BlockSpec((tm, D_MODEL), lambda i: (i, 0)),
        out_shape=jax.ShapeDtypeStruct((m, D_MODEL), F32),
        compiler_params=pltpu.CompilerParams(
            dimension_semantics=("arbitrary",),
            vmem_limit_bytes=VMEM_LIMIT),
        name="outproj",
    )(out_a, out_b, w_bf, x2d, final_g)


def _pool_kernel(u_ref, gate_ref, pw_ref, ps_ref, oa_ref, np_ref, ubuf, *, tl, nt):
    t = pl.program_id(1)
    hist = POOL_HIST + 1

    @pl.when(t == 0)
    def _():
        ubuf[0:hist, :] = jnp.zeros((hist, D_POOL), F32)

    u = u_ref[...].astype(F32)
    ubuf[hist:hist + tl, :] = u
    pos = t * tl + lax.broadcasted_iota(jnp.int32, (tl, 1), 0)
    for gi, w in enumerate(POOL_WINDOWS):
        cs = slice(gi * POOL_GROUP, (gi + 1) * POOL_GROUP)
        ug = u[:, cs]
        acc = ubuf[0:hist + tl, cs]
        span = 1
        while span < w:
            acc = acc + pltpu.roll(acc, span, 0)
            span *= 2
        cnt = jnp.minimum(w, pos + 1).astype(F32)
        pooled = acc[hist:hist + tl] / cnt - ug
        mixed = _dot(pooled.astype(BF16), pw_ref[gi].astype(BF16))
        gt = gate_ref[:, cs].astype(F32)
        oa_ref[:, cs] = (mixed * ps_ref[:, cs] * _silu(gt)).astype(BF16)

    ubuf[0:hist, :] = ubuf[tl:tl + hist, :]

    @pl.when(t == nt - 1)
    def _():
        np_ref[...] = ubuf[tl + 1:tl + hist, :]


def _pool_prompt(proj, pool_w, pool_scale, bsz, seq):
    tl = min(512, seq)
    nt = seq // tl
    return pl.pallas_call(
        functools.partial(_pool_kernel, tl=tl, nt=nt),
        grid=(bsz, nt),
        in_specs=[
            pl.BlockSpec((tl, D_POOL), lambda b, t: (b * nt + t, OFF_U // D_POOL)),
            pl.BlockSpec((tl, D_POOL), lambda b, t: (b * nt + t, OFF_GATE // D_POOL)),
            pl.BlockSpec((len(POOL_WINDOWS), POOL_GROUP, POOL_GROUP), lambda b, t: (0, 0, 0)),
            pl.BlockSpec((1, D_POOL), lambda b, t: (0, 0)),
        ],
        out_specs=[
            pl.BlockSpec((tl, D_POOL), lambda b, t: (b * nt + t, 0)),
            pl.BlockSpec((None, POOL_HIST, D_POOL), lambda b, t: (b, 0, 0)),
        ],
        out_shape=[
            jax.ShapeDtypeStruct((bsz * seq, D_POOL), BF16),
            jax.ShapeDtypeStruct((bsz, POOL_HIST, D_POOL), F32),
        ],
        scratch_shapes=[pltpu.VMEM((tl + POOL_HIST + 1, D_POOL), F32)],
        compiler_params=pltpu.CompilerParams(
            dimension_semantics=("arbitrary", "arbitrary"),
            vmem_limit_bytes=VMEM_LIMIT),
        name="pool_prompt",
    )(proj, proj, pool_w, pool_scale)


def _ssd_body(z_ref, xs_ref, b_ref, c_ref, dtr_ref, cw_ref, cbias_ref, dtb_ref, alog_ref,
              dskip_ref, ng_ref,
              ob_ref, nconv_ref, nssm_ref,
              cbuf, st_ref, cvx_ref, cvb_ref, cvc_ref, y_ref, *, nc, after_group=None):
    q = CHUNK
    c_idx = pl.program_id(1)
    halo = SUBLANES

    nbx, nbb = D_SSM // LANES, D_BC // LANES

    @pl.when(c_idx == 0)
    def _():
        cbuf[:, 0:halo, :] = jnp.zeros((cbuf.shape[0], halo, LANES), F32)
        st_ref[...] = jnp.zeros(st_ref.shape, F32)

    nv = (halo + q) // SUBLANES
    for blk in range(D_CONV // LANES):
        ls = slice(blk * LANES, (blk + 1) * LANES)
        if blk < nbx:
            src_ref, off, dst = xs_ref, blk * LANES, cvx_ref.at[blk]
        elif blk < nbx + nbb:
            src_ref, off, dst = b_ref, (blk - nbx) * LANES, cvb_ref.at[blk - nbx]
        else:
            src_ref, off, dst = c_ref, (blk - nbx - nbb) * LANES, cvc_ref.at[blk - nbx - nbb]
        cbuf[blk, halo:halo + q, :] = src_ref[:, off:off + LANES].astype(F32)
        xv = [cbuf[blk, pl.ds(a, SUBLANES, stride=nv), :] for a in range(nv)]
        wrap = [pltpu.roll(xv[nv - k], 1, 0) for k in range(1, CONV_WIDTH)]
        taps = [cw_ref[k:k + 1, ls] for k in range(CONV_WIDTH)]
        bias = cbias_ref[:, ls]
        for a in range(nv):
            acc = bias + taps[CONV_WIDTH - 1] * xv[a]
            for k in range(1, CONV_WIDTH):
                src = xv[a - k] if a >= k else wrap[k - a - 1]
                acc = acc + taps[CONV_WIDTH - 1 - k] * src
            dst[pl.ds(a, SUBLANES, stride=nv), :] = _silu(acc)
    rows = slice(halo, halo + q)

    dt = _softplus(dtr_ref[...] + dtb_ref[...])
    a_neg = -jnp.exp(alog_ref[...])
    da = dt * (a_neg * LOG2E)
    row = lax.broadcasted_iota(jnp.int32, (q, LANES), 0)
    a2 = da
    shift = 1
    while shift < q:
        a2 = a2 + jnp.where(row >= shift, pltpu.roll(a2, shift, 0), 0.0)
        shift *= 2
    a2_t = jnp.transpose(a2)
    ldt_t = jnp.log2(jnp.transpose(dt))
    a2_end_t = a2_t[:, q - 1:q]
    w_t = jnp.exp2(a2_end_t - a2_t + ldt_t)
    cdec_t = jnp.exp2(a2_end_t)
    srow_t = a2_t - ldt_t
    ea = jnp.exp2(a2)

    li = lax.broadcasted_iota(jnp.int32, (q, q), 0)
    si = lax.broadcasted_iota(jnp.int32, (q, q), 1)
    tri = li >= si
    lane = lax.broadcasted_iota(jnp.int32, (q, LANES), 1)
    lo_half = lane < HEAD_DIM

    for g in range(GROUPS):
        gs = slice(g * D_STATE, (g + 1) * D_STATE)
        c_gb = cvc_ref[g, rows, :].astype(BF16)
        b_g = cvb_ref[g, rows, :]
        cb = _dot_nt(c_gb, b_g.astype(BF16))
        b_t = jnp.transpose(b_g)
        ppg = HEADS_PER_GROUP // 2
        gl = slice(g * ppg * LANES, (g + 1) * ppg * LANES)
        y_off = _dot(c_gb, st_ref[:, gl].astype(BF16))
        for j in range(ppg):
            blk = g * ppg + j
            ls = slice(blk * LANES, (blk + 1) * LANES)
            sc, bw, ecol, dec = [], [], [], []
            for h in (2 * blk, 2 * blk + 1):
                a_col = jnp.broadcast_to(a2[:, h:h + 1], (q, q))
                decay_dt = jnp.exp2(jnp.where(tri, a_col - srow_t[h:h + 1, :], NEG_BIG))
                sc.append((cb * decay_dt).astype(BF16))
                bw.append((b_t * w_t[h:h + 1, :]).astype(BF16))
                ecol.append(jnp.broadcast_to(ea[:, h:h + 1], (q, LANES)))
                dec.append(jnp.broadcast_to(cdec_t[h:h + 1, :], (q, LANES)))
            x_p = cvx_ref[blk, rows, :].astype(BF16)
            zero_b = jnp.zeros_like(x_p)
            x_bd = jnp.concatenate([jnp.where(lo_half, x_p, zero_b),
                                    jnp.where(lo_half, zero_b, x_p)], axis=0)
            lhs1 = jnp.concatenate([jnp.concatenate(sc, axis=1),
                                    jnp.concatenate(bw, axis=1)], axis=0)
            r1 = _dot(lhs1, x_bd)
            y_ref[:, ls] = (r1[0:q] + y_off[:, j * LANES:(j + 1) * LANES]
                            * jnp.where(lo_half, ecol[0], ecol[1]))
            st_ref[:, ls] = (st_ref[:, ls] * jnp.where(lo_half, dec[0], dec[1])
                             + r1[q:2 * q])
        if after_group is not None:
            after_group(g)

    gw = D_SSM // GROUPS
    bpg = gw // LANES
    for g in range(GROUPS):
        yz, ssq = [], None
        for blk in range(g * bpg, (g + 1) * bpg):
            ls = slice(blk * LANES, (blk + 1) * LANES)
            y = y_ref[:, ls] + cvx_ref[blk, rows, :] * dskip_ref[:, ls]
            v = y * _silu(z_ref[:, ls].astype(F32))
            yz.append(v)
            ssq = v * v if ssq is None else ssq + v * v
        scale = lax.rsqrt(jnp.sum(ssq, axis=-1, keepdims=True) * (1.0 / gw) + EPS)
        for i, blk in enumerate(range(g * bpg, (g + 1) * bpg)):
            ls = slice(blk * LANES, (blk + 1) * LANES)
            ob_ref[:, ls] = (yz[i] * scale * ng_ref[:, ls]).astype(BF16)

    cbuf[:, 0:halo, :] = cbuf[:, q:q + halo, :]

    @pl.when(c_idx == nc - 1)
    def _():
        for blk in range(D_CONV // LANES):
            nconv_ref[:, blk * LANES:(blk + 1) * LANES] = cbuf[blk, q + halo - 3:q + halo, :]
        for blk in range(D_SSM // LANES):
            t = jnp.transpose(st_ref[:, blk * LANES:(blk + 1) * LANES])
            nssm_ref[2 * blk:2 * blk + 2] = t.reshape(2, HEAD_DIM, D_STATE)


def _sample1_kernel(pj_ref, dtr_ref, sp_ref, scv_ref, pw_ref, ps_ref, cw_ref, cbias_ref, dtb_ref,
                    alog_ref, dskip_ref, ehot_ref, ghot_ref,
                    oa_ref, npool_ref, nconv_ref, cs_ref, bs_ref, xw_ref, ydx_ref, ecum_ref,
                    cdec_ref, *, sb):
    nl = DEC_SEQ

    def rows(l):
        return slice(l * sb, (l + 1) * sb)

    def tok(l, off, width):
        return pj_ref[rows(l), off:off + width].astype(F32)

    for gi, w in enumerate(POOL_WINDOWS):
        c0 = gi * POOL_GROUP
        ext = [sp_ref[j, :, c0:c0 + POOL_GROUP] for j in range(POOL_HIST)]
        ext += [tok(l, OFF_U + c0, POOL_GROUP) for l in range(nl)]
        for l in range(nl):
            acc = ext[POOL_HIST + l]
            for k in range(1, w):
                acc = acc + ext[POOL_HIST + l - k]
            cnt = float(min(w, PAST_LEN + l + 1))
            pooled = acc / cnt - ext[POOL_HIST + l]
            mixed = _dot(pooled.astype(BF16), pw_ref[gi].astype(BF16))
            gt = tok(l, OFF_GATE + c0, POOL_GROUP)
            oa_ref[rows(l), c0:c0 + POOL_GROUP] = (
                mixed * ps_ref[:, c0:c0 + POOL_GROUP] * _silu(gt)).astype(BF16)
    for j in range(POOL_HIST):
        src = j + nl
        if src < POOL_HIST:
            npool_ref[j] = sp_ref[src]
        else:
            npool_ref[j] = tok(src - POOL_HIST, OFF_U, D_POOL)

    hist = CONV_WIDTH - 1
    conv_out = {}
    for name, poff, coff, width in (("x", OFF_XS, 0, D_SSM), ("b", OFF_B, D_SSM, D_BC),
                                    ("c", OFF_C, D_SSM + D_BC, D_BC)):
        ext = [scv_ref[j, :, coff:coff + width] for j in range(hist)]
        ext += [tok(l, poff, width) for l in range(nl)]
        outs = []
        for l in range(nl):
            acc = cbias_ref[:, coff:coff + width]
            for k in range(CONV_WIDTH):
                acc = acc + cw_ref[k:k + 1, coff:coff + width] * ext[l + k]
            outs.append(_silu(acc))
        conv_out[name] = outs
        for j in range(hist):
            nconv_ref[j, :, coff:coff + width] = ext[j + nl]
    xs, bs, cs = conv_out["x"], conv_out["b"], conv_out["c"]
    for l in range(nl):
        cs_ref[rows(l), :] = cs[l]
        bs_ref[rows(l), :] = bs[l].astype(BF16)

    a_neg = -jnp.exp(alog_ref[...])
    dt, a_cum = [], []
    run = None
    for l in range(nl):
        d = _softplus(dtr_ref[rows(l), :] + dtb_ref[...])
        dt.append(d)
        run = d * a_neg if run is None else run + d * a_neg
        a_cum.append(run)
    cdec_ref[...] = jnp.exp(a_cum[nl - 1])

    def onehot_rows(mats, onehot):
        parts = [_split2(m) for m in mats]
        stack = jnp.concatenate([p[0] for p in parts] + [p[1] for p in parts], axis=0)
        res = _dot(stack, onehot)
        n = len(mats)
        return [res[k * sb:(k + 1) * sb] + res[(n + k) * sb:(n + k + 1) * sb] for k in range(n)]

    pairs = [(l, s) for l in range(nl) for s in range(l + 1)]
    cbh = onehot_rows([cs[l] * bs[s] for l, s in pairs], ghot_ref[...])
    gls = [c * jnp.exp(a_cum[l] - a_cum[s]) * dt[s] for c, (l, s) in zip(cbh, pairs)]
    w_end = [jnp.exp(a_cum[nl - 1] - a_cum[l]) * dt[l] for l in range(nl)]
    e_cum = [jnp.exp(a_cum[l]) for l in range(nl)]
    chunk = 4 * LANES
    for c0 in range(0, D_SSM, chunk):
        cl = slice(c0, c0 + chunk)
        ex = onehot_rows(gls + w_end + e_cum, ehot_ref[:, cl])
        g_ex, w_ex, e_ex = ex[:len(pairs)], ex[len(pairs):len(pairs) + nl], ex[len(pairs) + nl:]
        xc = [x[:, cl] for x in xs]
        for l in range(nl):
            ydiag = dskip_ref[:, cl] * xc[l]
            for k, (pl_, ps_) in enumerate(pairs):
                if pl_ == l:
                    ydiag = ydiag + g_ex[k] * xc[ps_]
            ydx_ref[rows(l), cl] = ydiag
            xw_ref[rows(l), cl] = w_ex[l] * xc[l]
            ecum_ref[rows(l), cl] = e_ex[l]


def _sample1(pj3, dtr3, sp3, scv3, pool_w, pool_scale, conv_w, conv_b, dt_bias_p, a_log_p,
             dskip_row, ehot, ghot, sb):
    nblk = pj3.shape[0]
    db = nblk * sb
    rb = DEC_SEQ * sb
    blk2 = lambda width: pl.BlockSpec((None, rb, width), lambda i: (i, 0, 0))
    hist3 = lambda n, width: pl.BlockSpec((n, sb, width), lambda i: (0, i, 0))
    const2 = lambda shape: pl.BlockSpec(shape, lambda i: (0, 0))
    blk_shape = lambda width, dt: jax.ShapeDtypeStruct((nblk, rb, width), dt)
    return pl.pallas_call(
        functools.partial(_sample1_kernel, sb=sb),
        grid=(nblk,),
        in_specs=[
            blk2(D_MAIN), blk2(LANES), hist3(POOL_HIST, D_POOL), hist3(CONV_WIDTH - 1, D_CONV),
            pl.BlockSpec((len(POOL_WINDOWS), POOL_GROUP, POOL_GROUP), lambda i: (0, 0, 0)),
            const2((1, D_POOL)), const2((CONV_WIDTH, D_CONV)), const2((1, D_CONV)),
            const2((1, LANES)), const2((1, LANES)), const2((1, D_SSM)),
            const2((LANES, D_SSM)), const2((D_BC, LANES)),
        ],
        out_specs=[
            blk2(D_POOL), hist3(POOL_HIST, D_POOL), hist3(CONV_WIDTH - 1, D_CONV),
            blk2(D_BC), blk2(D_BC), blk2(D_SSM), blk2(D_SSM), blk2(D_SSM),
            pl.BlockSpec((sb, LANES), lambda i: (i, 0)),
        ],
        out_shape=[
            blk_shape(D_POOL, BF16),
            jax.ShapeDtypeStruct((POOL_HIST, db, D_POOL), F32),
            jax.ShapeDtypeStruct((CONV_WIDTH - 1, db, D_CONV), F32),
            blk_shape(D_BC, F32), blk_shape(D_BC, BF16),
            blk_shape(D_SSM, F32), blk_shape(D_SSM, F32), blk_shape(D_SSM, F32),
            jax.ShapeDtypeStruct((db, LANES), F32),
        ],
        compiler_params=pltpu.CompilerParams(
            dimension_semantics=("arbitrary",),
            vmem_limit_bytes=VMEM_LIMIT),
        name="sample_elementwise",
    )(pj3, dtr3, sp3, scv3, pool_w, pool_scale, conv_w, conv_b, dt_bias_p, a_log_p, dskip_row,
      ehot, ghot)


def _sample2_setup(xw_ref, cs_ref, xwt_ref, *, sb, per):
    j = pl.program_id(1)

    @pl.when(j == 0)
    def _():
        for blk in range(D_SSM // LANES):
            xwt_ref[blk * LANES:(blk + 1) * LANES, :] = jnp.transpose(
                xw_ref[:, blk * LANES:(blk + 1) * LANES]).astype(BF16)

    return [jnp.concatenate([cs_ref[pl.ds(l * sb + per * j + 2 * pr + bi, 1), :]
                             for bi in range(2) for l in range(DEC_SEQ)], axis=0).astype(BF16)
            for pr in range(per // 2)]


def _sample2_group(g, refs, c_rows, *, sb, per):
    cdec_ref, st_ref, _, bs_ref, _, nst_ref, yo_ref, xwt_ref = refs
    i = pl.program_id(0)
    j = pl.program_id(1)
    nl = DEC_SEQ
    rows = sb * nl
    seq_of_row = lax.broadcasted_iota(jnp.int32, (rows, 1), 0) & (sb - 1)
    gw = HEADS_PER_GROUP * HEAD_DIM
    gs = slice(g * D_STATE, (g + 1) * D_STATE)
    for pr in range(per // 2):
        q0 = per * j + 2 * pr
        c8 = c_rows[pr][:, gs]
        b_blk = bs_ref[:, gs]
        zero_b = jnp.zeros_like(b_blk)
        w2 = jnp.concatenate([jnp.where(seq_of_row == q0, b_blk, zero_b),
                              jnp.where(seq_of_row == q0 + 1, b_blk, zero_b)], axis=1)
        u2 = _dot(xwt_ref[g * gw:(g + 1) * gw, :], w2)
        for bi in range(2):
            sq = 2 * pr + bi
            s0 = st_ref[sq, g * HEADS_PER_GROUP:(g + 1) * HEADS_PER_GROUP].reshape(gw, D_STATE)
            yo = _dot_nt(c8, s0.astype(BF16))
            base = (i * sb + q0 + bi) * HEADS + g * HEADS_PER_GROUP
            for r in range(HEADS_PER_GROUP):
                dec = cdec_ref[base + r]
                rs = slice(r * HEAD_DIM, (r + 1) * HEAD_DIM)
                nst_ref[sq, g * HEADS_PER_GROUP + r] = (
                    s0[rs] * dec + u2[rs, bi * D_STATE:(bi + 1) * D_STATE])
            for l in range(nl):
                yo_ref[pl.ds(l * sb + q0 + bi, 1), g * gw:(g + 1) * gw] = (
                    yo[bi * nl + l:bi * nl + l + 1])


N_SSD_IN, N_SSD_OUT, N_SSD_SCRATCH = 11, 3, 6
N_ST_IN, N_ST_OUT = 5, 2


def _ssd_state_kernel(*refs, nc, sb, per):
    n_in = N_SSD_IN + N_ST_IN + 1
    ins, rest = refs[:n_in], refs[n_in:]
    n_out = N_SSD_OUT + N_ST_OUT + 1
    outs, scratch = rest[:n_out], rest[n_out:]
    outs[-1][...] = ins[-1][...].astype(BF16)
    st_refs = (*ins[N_SSD_IN:N_SSD_IN + N_ST_IN], *outs[N_SSD_OUT:N_SSD_OUT + N_ST_OUT],
               *scratch[N_SSD_SCRATCH:])
    c_rows = _sample2_setup(st_refs[2], st_refs[4], st_refs[7], sb=sb, per=per)
    _ssd_body(*ins[:N_SSD_IN], *outs[:N_SSD_OUT], *scratch[:N_SSD_SCRATCH], nc=nc,
              after_group=functools.partial(_sample2_group, refs=st_refs, c_rows=c_rows,
                                            sb=sb, per=per))


def _ssd_prompt_sample_state(proj, dt_raw, conv_w, conv_b, dt_bias_p, a_log_p, dskip_row, norm_g,
                             bsz, seq, cdec_flat, state, xw3, bs3, cs3, sb, w_out2d):
    q = CHUNK
    nc = seq // q
    db = state.shape[0]
    rb = DEC_SEQ * sb
    per = sb // nc
    assert rb == LANES and db // sb == bsz and per * nc == sb and per % 2 == 0
    d_mix = w_out2d.shape[0]
    slab = d_mix // (bsz * nc)
    assert slab * bsz * nc == d_mix and slab % (2 * SUBLANES) == 0
    wslab = pl.BlockSpec((slab, D_MODEL), lambda b, c: (b * nc + c, 0))
    row = lambda b, c: b * nc + c
    const = lambda b, c: (0, 0)
    blk2 = lambda width: pl.BlockSpec((None, rb, width), lambda i, j: (i, 0, 0))
    st_spec = pl.BlockSpec((per, HEADS, HEAD_DIM, D_STATE), lambda i, j: (i * nc + j, 0, 0, 0))
    return pl.pallas_call(
        functools.partial(_ssd_state_kernel, nc=nc, sb=sb, per=per),
        grid=(bsz, nc),
        in_specs=[
            pl.BlockSpec((q, D_SSM), lambda b, c: (row(b, c), OFF_Z // D_SSM)),
            pl.BlockSpec((q, D_SSM), lambda b, c: (row(b, c), OFF_XS // D_SSM)),
            pl.BlockSpec((q, D_BC), lambda b, c: (row(b, c), OFF_B // D_BC)),
            pl.BlockSpec((q, D_BC), lambda b, c: (row(b, c), OFF_C // D_BC)),
            pl.BlockSpec((q, LANES), lambda b, c: (row(b, c), 0)),
            pl.BlockSpec((CONV_WIDTH, D_CONV), const),
            pl.BlockSpec((1, D_CONV), const),
            pl.BlockSpec((1, LANES), const),
            pl.BlockSpec((1, LANES), const),
            pl.BlockSpec((1, D_SSM), const),
            pl.BlockSpec((1, D_SSM), const),
            pl.BlockSpec(memory_space=pltpu.SMEM),
            st_spec, blk2(D_SSM), blk2(D_BC), blk2(D_BC),
            wslab,
        ],
        out_specs=[
            pl.BlockSpec((q, D_SSM), lambda b, c: (row(b, c), 0)),
            pl.BlockSpec((None, CONV_WIDTH - 1, D_CONV), lambda b, c: (b, 0, 0)),
            pl.BlockSpec((None, HEADS, HEAD_DIM, D_STATE), lambda b, c: (b, 0, 0, 0)),
            st_spec, blk2(D_SSM),
            wslab,
        ],
        out_shape=[
            jax.ShapeDtypeStruct((bsz * seq, D_SSM), BF16),
            jax.ShapeDtypeStruct((bsz, CONV_WIDTH - 1, D_CONV), F32),
            jax.ShapeDtypeStruct((bsz, HEADS, HEAD_DIM, D_STATE), F32),
            jax.ShapeDtypeStruct(state.shape, F32),
            jax.ShapeDtypeStruct((db // sb, rb, D_SSM), F32),
            jax.ShapeDtypeStruct(w_out2d.shape, BF16),
        ],
        scratch_shapes=[
            pltpu.VMEM((D_CONV // LANES, q + SUBLANES, LANES), F32),
            pltpu.VMEM((D_STATE, D_SSM), F32),
            pltpu.VMEM((D_SSM // LANES, q + SUBLANES, LANES), F32),
            pltpu.VMEM((D_BC // LANES, q + SUBLANES, LANES), F32),
            pltpu.VMEM((D_BC // LANES, q + SUBLANES, LANES), F32),
            pltpu.VMEM((q, D_SSM), F32),
            pltpu.VMEM((D_SSM, rb), BF16),
        ],
        compiler_params=pltpu.CompilerParams(
            dimension_semantics=("arbitrary", "arbitrary"),
            vmem_limit_bytes=VMEM_LIMIT),
        name="ssd_prompt_sample_state",
    )(proj, proj, proj, proj, dt_raw, conv_w, conv_b, dt_bias_p, a_log_p, dskip_row, norm_g,
      cdec_flat, state, xw3, bs3, cs3, w_out2d)


def _outproj_sample_kernel(a_ref, yo_ref, ecum_ref, ydx_ref, z_ref, ng_ref, w_ref, x_ref,
                           g_ref, y_ref, b_ref):
    gw = D_SSM // GROUPS
    for g in range(GROUPS):
        cs = slice(g * gw, (g + 1) * gw)
        y = ydx_ref[:, cs] + ecum_ref[:, cs] * yo_ref[:, cs]
        yz = y * _silu(z_ref[:, cs].astype(F32))
        ms = jnp.sum(yz * yz, axis=-1, keepdims=True) * (1.0 / gw)
        b_ref[:, cs] = (yz * lax.rsqrt(ms + EPS) * ng_ref[:, cs]).astype(BF16)
    _outproj_kernel(a_ref, b_ref, w_ref, x_ref, g_ref, y_ref)


def _outproj_sample(out_a, yo, ecum, ydx, proj, norm_g, w_bf, x2d, final_g):
    m = x2d.shape[0]
    tm = min(256, m)
    rowblk = lambda width: pl.BlockSpec((tm, width), lambda i: (i, 0))
    const = lambda shape, **kw: pl.BlockSpec(shape, lambda i: (0, 0), **kw)
    return pl.pallas_call(
        _outproj_sample_kernel,
        grid=(m // tm,),
        in_specs=[
            rowblk(D_POOL), rowblk(D_SSM), rowblk(D_SSM), rowblk(D_SSM),
            pl.BlockSpec((tm, D_SSM), lambda i: (i, OFF_Z // D_SSM)),
            const((1, D_SSM)),
            const((D_POOL + D_SSM, D_MODEL), pipeline_mode=pl.Buffered(1)),
            rowblk(D_MODEL), const((1, D_MODEL)),
        ],
        out_specs=rowblk(D_MODEL),
        out_shape=jax.ShapeDtypeStruct((m, D_MODEL), F32),
        scratch_shapes=[pltpu.VMEM((tm, D_SSM), BF16)],
        compiler_params=pltpu.CompilerParams(
            dimension_semantics=("arbitrary",),
            vmem_limit_bytes=VMEM_LIMIT),
        name="outproj_sample",
    )(out_a, yo, ecum, ydx, proj, norm_g, w_bf, x2d, final_g)


def kernel(x_prompt, x_sample, state_pool, state_conv, state_ssm, norm_g, w_in, conv_w, conv_b,
           dt_bias, a_log, d_skip, ssm_norm_g, pool_w, pool_scale, w_out, final_g):
    bsz, seq, _ = x_prompt.shape
    db, nl, _ = x_sample.shape
    assert nl == DEC_SEQ and seq % CHUNK == 0 and w_in.shape[0] == 1

    w_t = jnp.transpose(w_in[0])
    pool_w_b = pool_w[0]
    g_in = norm_g[0][None, :]
    g_fin = final_g[None, :]
    ps = pool_scale[0][None, :]
    cw = conv_w[0]
    cbias = conv_b[0][None, :]
    pad_h = lambda v: jnp.pad(v, (0, LANES - HEADS))[None, :]
    dtb = pad_h(dt_bias[0])
    alog = pad_h(a_log[0])
    dskip_row = jnp.repeat(d_skip[0], HEAD_DIM)[None, :]
    ng = ssm_norm_g[0][None, :]

    sb = SAMPLE_BLOCK
    nblk = db // sb
    rb = nl * sb
    xs2 = x_sample.reshape(nblk, sb, nl, D_MODEL).transpose(0, 2, 1, 3).reshape(db * nl, D_MODEL)
    proj_s, dt_s, w_bf, wdt_bf = _inproj(xs2, g_in, w_t, w_t, 1024)
    pj3 = proj_s.reshape(nblk, rb, D_MAIN)
    head_of_ch = jnp.arange(D_SSM, dtype=jnp.int32) // HEAD_DIM
    ehot = (jnp.arange(LANES, dtype=jnp.int32)[:, None] == head_of_ch[None, :]).astype(BF16)
    grp_of_row = jnp.arange(D_BC, dtype=jnp.int32) // D_STATE
    head_id = jnp.arange(LANES, dtype=jnp.int32)
    ghot = ((head_id[None, :] // HEADS_PER_GROUP == grp_of_row[:, None])
            & (head_id[None, :] < HEADS)).astype(BF16)
    (oa_s, npool_s, nconv_s, cs_s, bs_s, xw_s, ydx_s, ecum_s, cdec_s) = _sample1(
        pj3, dt_s.reshape(nblk, rb, LANES), jnp.transpose(state_pool[0], (1, 0, 2)),
        jnp.transpose(state_conv[0], (1, 0, 2)), pool_w_b, ps, cw, cbias, dtb, alog,
        dskip_row, ehot, ghot, sb)

    xp2 = x_prompt.reshape(bsz * seq, D_MODEL)
    proj_p, dt_p = _inproj(xp2, g_in, w_bf, wdt_bf, PROMPT_TN)
    oa_p, npool_p = _pool_prompt(proj_p, pool_w_b, ps, bsz, seq)
    ob_p, nconv_p, nssm_p, nssm_s, yo_s, w_out_bf = _ssd_prompt_sample_state(
        proj_p, dt_p, cw, cbias, dtb, alog, dskip_row, ng, bsz, seq,
        cdec_s[:, :HEADS].reshape(db * HEADS), state_ssm[0], xw_s, bs_s, cs_s, sb, w_out[0])
    y_p = _outproj(oa_p, ob_p, w_out_bf, xp2, g_fin).reshape(bsz, seq, D_MODEL)

    flat = lambda t: t.reshape(db * nl, t.shape[-1])
    y_s = _outproj_sample(flat(oa_s), flat(yo_s), flat(ecum_s), flat(ydx_s), proj_s, ng, w_out_bf,
                          xs2, g_fin)
    y_s = y_s.reshape(nblk, nl, sb, D_MODEL).transpose(0, 2, 1, 3).reshape(db, nl, D_MODEL)

    return (y_p, y_s,
            npool_p[None], nconv_p[None], nssm_p[None],
            jnp.transpose(npool_s, (1, 0, 2))[None],
            jnp.transpose(nconv_s, (1, 0, 2))[None],
            nssm_s[None])
```

```python
import functools

import jax
import jax.numpy as jnp
from jax import lax
from jax.experimental import pallas as pl
from jax.experimental.pallas import tpu as pltpu

F32 = jnp.float32
BF16 = jnp.bfloat16

D_MODEL = 2048
D_POOL = 1024
POOL_WINDOWS = (2, 4, 8, 16)
POOL_GROUP = 256
POOL_HIST = 15
D_SSM = 3072
HEAD_DIM = 64
HEADS = 48
GROUPS = 8
HEADS_PER_GROUP = 6
D_STATE = 128
D_BC = GROUPS * D_STATE
CONV_WIDTH = 4
D_CONV = D_SSM + 2 * D_BC
D_MAIN = 2 * D_POOL + D_SSM + D_CONV
PAST_LEN = 16384
DEC_SEQ = 4
EPS = 1e-5

LANES = 128
SUBLANES = 8
VMEM_LIMIT = 56 * 1024 * 1024

OFF_Z = 0
OFF_XS = D_SSM
OFF_U = 2 * D_SSM
OFF_GATE = OFF_U + D_POOL
OFF_B = OFF_GATE + D_POOL
OFF_C = OFF_B + D_BC

PROMPT_TN = 2048
CHUNK = 128
SAMPLE_BLOCK = 32
NEG_BIG = -1e30
LOG2E = 1.4426950408889634


def _silu(v):
    h = 0.5 * v
    return h + h * jnp.tanh(h)


def _softplus(v):
    y = jnp.exp(-jnp.abs(v))
    u = 1.0 + y
    d = u - 1.0
    l1p = jnp.where(d == 0.0, y, jnp.log(u) * (y / jnp.where(d == 0.0, 1.0, d)))
    return jnp.maximum(v, 0.0) + l1p


def _split2(v):
    hi = v.astype(BF16)
    lo = (v - hi.astype(F32)).astype(BF16)
    return hi, lo


def _dot(a, b):
    return jnp.dot(a, b, preferred_element_type=F32)


def _dot_nt(a, b):
    return lax.dot_general(a, b, (((1,), (1,)), ((), ())), preferred_element_type=F32)


def _dot2(v, onehot):
    hi, lo = _split2(v)
    return _dot(hi, onehot) + _dot(lo, onehot)


def _inproj_kernel(x_ref, g_ref, w_ref, wdt_ref, o_ref, dt_ref, *rest):
    from_f32 = w_ref.dtype == F32
    (wb_out_ref, wdt_out_ref, h_ref) = rest if from_f32 else (None, None, *rest)

    @pl.when(pl.program_id(1) == 0)
    def _():
        x = x_ref[...]
        ms = jnp.mean(x * x, axis=-1, keepdims=True)
        h = (x * lax.rsqrt(ms + EPS) * g_ref[...]).astype(BF16)
        h_ref[...] = h
        if from_f32:
            wrow = lax.broadcasted_iota(jnp.int32, wdt_ref.shape, 0)
            wdt = jnp.where(wrow < HEADS, wdt_ref[...], 0.0).astype(BF16)
            wdt_out_ref[...] = wdt
        else:
            wdt = wdt_ref[...]
        dt_ref[...] = _dot_nt(h, wdt)

    if from_f32:
        wb = w_ref[...].astype(BF16)
        wb_out_ref[...] = wb
    else:
        wb = w_ref[...]
    o_ref[...] = _dot_nt(h_ref[...], wb).astype(BF16)


def _src_block(j):
    nz = (D_SSM + D_SSM) // 1024
    npool = 2 * D_POOL // 1024
    return jnp.where(j < nz, j + npool, jnp.where(j < nz + npool, j - nz, j))


def _inproj(x2d, norm_g, w, w_dt, tn):
    m = x2d.shape[0]
    tm = min(1024, m)
    from_f32 = w.dtype == F32
    assert tn == 1024 or not from_f32
    w_map = (lambda i, j: (_src_block(j), 0)) if from_f32 else (lambda i, j: (j, 0))
    wdt_map = (lambda i, j: (D_MAIN // LANES, 0)) if from_f32 else (lambda i, j: (0, 0))
    out_specs = [
        pl.BlockSpec((tm, tn), lambda i, j: (i, j)),
        pl.BlockSpec((tm, LANES), lambda i, j: (i, 0)),
    ]
    out_shape = [
        jax.ShapeDtypeStruct((m, D_MAIN), BF16),
        jax.ShapeDtypeStruct((m, LANES), F32),
    ]
    if from_f32:
        assert m == tm
        out_specs += [pl.BlockSpec((tn, D_MODEL), lambda i, j: (j, 0)),
                      pl.BlockSpec((LANES, D_MODEL), lambda i, j: (0, 0))]
        out_shape += [jax.ShapeDtypeStruct((D_MAIN, D_MODEL), BF16),
                      jax.ShapeDtypeStruct((LANES, D_MODEL), BF16)]
    return pl.pallas_call(
        _inproj_kernel,
        grid=(m // tm, D_MAIN // tn),
        in_specs=[
            pl.BlockSpec((tm, D_MODEL), lambda i, j: (i, 0)),
            pl.BlockSpec((1, D_MODEL), lambda i, j: (0, 0)),
            pl.BlockSpec((tn, D_MODEL), w_map),
            pl.BlockSpec((LANES, D_MODEL), wdt_map),
        ],
        out_specs=out_specs,
        out_shape=out_shape,
        scratch_shapes=[pltpu.VMEM((tm, D_MODEL), BF16)],
        compiler_params=pltpu.CompilerParams(
            dimension_semantics=("arbitrary", "arbitrary"),
            vmem_limit_bytes=VMEM_LIMIT),
        name="inproj",
    )(x2d, norm_g, w, w_dt)


def _outproj_kernel(a_ref, b_ref, w_ref, x_ref, g_ref, y_ref):
    acc = (_dot(a_ref[...], w_ref[0:D_POOL, :])
           + _dot(b_ref[...], w_ref[D_POOL:D_POOL + D_SSM, :]))
    r = x_ref[...] + acc
    ms = jnp.mean(r * r, axis=-1, keepdims=True)
    y_ref[...] = r * lax.rsqrt(ms + EPS) * g_ref[...]


def _outproj(out_a, out_b, w_bf, x2d, final_g):
    m = x2d.shape[0]
    tm = min(512, m)
    resident = pl.Buffered(1)
    return pl.pallas_call(
        _outproj_kernel,
        grid=(m // tm,),
        in_specs=[
            pl.BlockSpec((tm, D_POOL), lambda i: (i, 0)),
            pl.BlockSpec((tm, D_SSM), lambda i: (i, 0)),
            pl.BlockSpec((D_POOL + D_SSM, D_MODEL), lambda i: (0, 0), pipeline_mode=resident),
            pl.BlockSpec((tm, D_MODEL), lambda i: (i, 0)),
            pl.BlockSpec((1, D_MODEL), lambda i: (0, 0)),
        ],
        out_specs=pl.BlockSpec((tm, D_MODEL), lambda i: (i, 0)),
        out_shape=jax.ShapeDtypeStruct((m, D_MODEL), F32),
        compiler_params=pltpu.CompilerParams(
            dimension_semantics=("arbitrary",),
            vmem_limit_bytes=VMEM_LIMIT),
        name="outproj",
    )(out_a, out_b, w_bf, x2d, final_g)


def _pool_kernel(u_ref, gate_ref, pw_ref, ps_ref, oa_ref, np_ref, ubuf, wbuf, *, tl, nt):
    t = pl.program_id(1)
    hist = POOL_HIST + 1
    nrows = hist + tl
    nv = nrows // SUBLANES
    bpg = POOL_GROUP // LANES

    @pl.when(t == 0)
    def _():
        ubuf[:, 0:hist, :] = jnp.zeros((ubuf.shape[0], hist, LANES), F32)

    pos = t * tl + lax.broadcasted_iota(jnp.int32, (tl, 1), 0)
    for gi, w in enumerate(POOL_WINDOWS):
        win = []
        for lb in range(gi * bpg, (gi + 1) * bpg):
            ubuf[lb, hist:nrows, :] = u_ref[:, lb * LANES:(lb + 1) * LANES].astype(F32)
            acc = [ubuf[lb, pl.ds(a, SUBLANES, stride=nv), :] for a in range(nv)]
            span = 1
            while span < w:
                wrap = [pltpu.roll(acc[nv - span + a], 1, 0) for a in range(span)]
                acc = [acc[a] + (acc[a - span] if a >= span else wrap[a]) for a in range(nv)]
                span *= 2
            for a in range(nv):
                wbuf[lb, pl.ds(a, SUBLANES, stride=nv), :] = acc[a]
            win.append(wbuf[lb, hist:nrows, :])
        cs = slice(gi * POOL_GROUP, (gi + 1) * POOL_GROUP)
        cnt = jnp.minimum(w, pos + 1).astype(F32)
        ug = jnp.concatenate([ubuf[lb, hist:nrows, :] for lb in range(gi * bpg, (gi + 1) * bpg)],
                             axis=1)
        pooled = jnp.concatenate(win, axis=1) / cnt - ug
        mixed = _dot(pooled.astype(BF16), pw_ref[gi].astype(BF16))
        gt = gate_ref[:, cs].astype(F32)
        oa_ref[:, cs] = (mixed * ps_ref[:, cs] * _silu(gt)).astype(BF16)

    @pl.when(t == nt - 1)
    def _():
        for lb in range(D_POOL // LANES):
            np_ref[:, lb * LANES:(lb + 1) * LANES] = ubuf[lb, tl + 1:nrows, :]

    ubuf[:, 0:hist, :] = ubuf[:, tl:nrows, :]


def _pool_prompt(proj, pool_w, pool_scale, bsz, seq):
    tl = min(512, seq)
    nt = seq // tl
    return pl.pallas_call(
        functools.partial(_pool_kernel, tl=tl, nt=nt),
        grid=(bsz, nt),
        in_specs=[
            pl.BlockSpec((tl, D_POOL), lambda b, t: (b * nt + t, OFF_U // D_POOL)),
            pl.BlockSpec((tl, D_POOL), lambda b, t: (b * nt + t, OFF_GATE // D_POOL)),
            pl.BlockSpec((len(POOL_WINDOWS), POOL_GROUP, POOL_GROUP), lambda b, t: (0, 0, 0)),
            pl.BlockSpec((1, D_POOL), lambda b, t: (0, 0)),
        ],
        out_specs=[
            pl.BlockSpec((tl, D_POOL), lambda b, t: (b * nt + t, 0)),
            pl.BlockSpec((None, POOL_HIST, D_POOL), lambda b, t: (b, 0, 0)),
        ],
        out_shape=[
            jax.ShapeDtypeStruct((bsz * seq, D_POOL), BF16),
            jax.ShapeDtypeStruct((bsz, POOL_HIST, D_POOL), F32),
        ],
        scratch_shapes=[pltpu.VMEM((D_POOL // LANES, tl + POOL_HIST + 1, LANES), F32),
                        pltpu.VMEM((D_POOL // LANES, tl + POOL_HIST + 1, LANES), F32)],
        compiler_params=pltpu.CompilerParams(
            dimension_semantics=("arbitrary", "arbitrary"),
            vmem_limit_bytes=VMEM_LIMIT),
        name="pool_prompt",
    )(proj, proj, pool_w, pool_scale)


def _ssd_body(z_ref, xs_ref, b_ref, c_ref, dtr_ref, cw_ref, cbias_ref, dtb_ref, alog_ref,
              dskip_ref, ng_ref,
              ob_ref, nconv_ref, nssm_ref,
              cbuf, st_ref, cvx_ref, cvb_ref, cvc_ref, y_ref, *, nc, after_group=None):
    q = CHUNK
    c_idx = pl.program_id(1)
    halo = SUBLANES

    nbx, nbb = D_SSM // LANES, D_BC // LANES

    @pl.when(c_idx == 0)
    def _():
        cbuf[:, 0:halo, :] = jnp.zeros((cbuf.shape[0], halo, LANES), F32)
        st_ref[...] = jnp.zeros(st_ref.shape, F32)

    nv = (halo + q) // SUBLANES
    for blk in range(D_CONV // LANES):
        ls = slice(blk * LANES, (blk + 1) * LANES)
        if blk < nbx:
            src_ref, off, dst = xs_ref, blk * LANES, cvx_ref.at[blk]
        elif blk < nbx + nbb:
            src_ref, off, dst = b_ref, (blk - nbx) * LANES, cvb_ref.at[blk - nbx]
        else:
            src_ref, off, dst = c_ref, (blk - nbx - nbb) * LANES, cvc_ref.at[blk - nbx - nbb]
        cbuf[blk, halo:halo + q, :] = src_ref[:, off:off + LANES].astype(F32)
        xv = [cbuf[blk, pl.ds(a, SUBLANES, stride=nv), :] for a in range(nv)]
        wrap = [pltpu.roll(xv[nv - k], 1, 0) for k in range(1, CONV_WIDTH)]
        taps = [0.5 * cw_ref[k:k + 1, ls] for k in range(CONV_WIDTH)]
        bias = 0.5 * cbias_ref[:, ls]
        for a in range(nv):
            h = bias + taps[CONV_WIDTH - 1] * xv[a]
            for k in range(1, CONV_WIDTH):
                src = xv[a - k] if a >= k else wrap[k - a - 1]
                h = h + taps[CONV_WIDTH - 1 - k] * src
            dst[pl.ds(a, SUBLANES, stride=nv), :] = h + h * jnp.tanh(h)
    rows = slice(halo, halo + q)

    dt = _softplus(dtr_ref[...] + dtb_ref[...])
    a_neg = -jnp.exp(alog_ref[...])
    da = dt * (a_neg * LOG2E)
    row = lax.broadcasted_iota(jnp.int32, (q, LANES), 0)
    a2 = da
    shift = 1
    while shift < q:
        a2 = a2 + jnp.where(row >= shift, pltpu.roll(a2, shift, 0), 0.0)
        shift *= 2
    a2_t = jnp.transpose(a2)
    ldt_t = jnp.log2(jnp.transpose(dt))
    a2_end_t = a2_t[:, q - 1:q]
    w_t = jnp.exp2(a2_end_t - a2_t + ldt_t)
    cdec_t = jnp.exp2(a2_end_t)
    srow_t = a2_t - ldt_t
    ea = jnp.exp2(a2)

    li = lax.broadcasted_iota(jnp.int32, (q, q), 0)
    si = lax.broadcasted_iota(jnp.int32, (q, q), 1)
    tri = li >= si
    lane = lax.broadcasted_iota(jnp.int32, (q, LANES), 1)
    lo_half = lane < HEAD_DIM

    for g in range(GROUPS):
        gs = slice(g * D_STATE, (g + 1) * D_STATE)
        c_gb = cvc_ref[g, rows, :].astype(BF16)
        b_g = cvb_ref[g, rows, :]
        cb = _dot_nt(c_gb, b_g.astype(BF16))
        b_t = jnp.transpose(b_g)
        ppg = HEADS_PER_GROUP // 2
        gl = slice(g * ppg * LANES, (g + 1) * ppg * LANES)
        y_off = _dot(c_gb, st_ref[:, gl].astype(BF16))
        for j in range(ppg):
            blk = g * ppg + j
            ls = slice(blk * LANES, (blk + 1) * LANES)
            sc, bw, ecol, dec = [], [], [], []
            for h in (2 * blk, 2 * blk + 1):
                a_col = jnp.broadcast_to(a2[:, h:h + 1], (q, q))
                decay_dt = jnp.exp2(jnp.where(tri, a_col - srow_t[h:h + 1, :], NEG_BIG))
                sc.append((cb * decay_dt).astype(BF16))
                bw.append((b_t * w_t[h:h + 1, :]).astype(BF16))
                ecol.append(jnp.broadcast_to(ea[:, h:h + 1], (q, LANES)))
                dec.append(jnp.broadcast_to(cdec_t[h:h + 1, :], (q, LANES)))
            x_p = cvx_ref[blk, rows, :].astype(BF16)
            zero_b = jnp.zeros_like(x_p)
            x_bd = jnp.concatenate([jnp.where(lo_half, x_p, zero_b),
                                    jnp.where(lo_half, zero_b, x_p)], axis=0)
            lhs1 = jnp.concatenate([jnp.concatenate(sc, axis=1),
                                    jnp.concatenate(bw, axis=1)], axis=0)
            r1 = _dot(lhs1, x_bd)
            y_ref[:, ls] = (r1[0:q] + y_off[:, j * LANES:(j + 1) * LANES]
                            * jnp.where(lo_half, ecol[0], ecol[1]))
            st_ref[:, ls] = (st_ref[:, ls] * jnp.where(lo_half, dec[0], dec[1])
                             + r1[q:2 * q])
        if after_group is not None:
            after_group(g)

    gw = D_SSM // GROUPS
    bpg = gw // LANES
    for g in range(GROUPS):
        yz, ssq = [], None
        for blk in range(g * bpg, (g + 1) * bpg):
            ls = slice(blk * LANES, (blk + 1) * LANES)
            y = y_ref[:, ls] + cvx_ref[blk, rows, :] * dskip_ref[:, ls]
            v = y * _silu(z_ref[:, ls].astype(F32))
            yz.append(v)
            ssq = v * v if ssq is None else ssq + v * v
        scale = lax.rsqrt(jnp.sum(ssq, axis=-1, keepdims=True) * (1.0 / gw) + EPS)
        for i, blk in enumerate(range(g * bpg, (g + 1) * bpg)):
            ls = slice(blk * LANES, (blk + 1) * LANES)
            ob_ref[:, ls] = (yz[i] * scale * ng_ref[:, ls]).astype(BF16)

    cbuf[:, 0:halo, :] = cbuf[:, q:q + halo, :]

    @pl.when(c_idx == nc - 1)
    def _():
        for blk in range(D_CONV // LANES):
            nconv_ref[:, blk * LANES:(blk + 1) * LANES] = cbuf[blk, q + halo - 3:q + halo, :]
        for blk in range(D_SSM // LANES):
            t = jnp.transpose(st_ref[:, blk * LANES:(blk + 1) * LANES])
            nssm_ref[2 * blk:2 * blk + 2] = t.reshape(2, HEAD_DIM, D_STATE)


def _sample1_kernel(pj_ref, dtr_ref, sp_ref, scv_ref, pw_ref, ps_ref, cw_ref, cbias_ref, dtb_ref,
                    alog_ref, dskip_ref, ehot_ref, ghot_ref,
                    oa_ref, npool_ref, nconv_ref, cs_ref, bs_ref, xw_ref, ydx_ref, ecum_ref,
                    cdec_ref, *, sb):
    nl = DEC_SEQ

    def rows(l):
        return slice(l * sb, (l + 1) * sb)

    def tok(l, off, width):
        return pj_ref[rows(l), off:off + width].astype(F32)

    for gi, w in enumerate(POOL_WINDOWS):
        c0 = gi * POOL_GROUP
        ext = [sp_ref[j, :, c0:c0 + POOL_GROUP] for j in range(POOL_HIST)]
        ext += [tok(l, OFF_U + c0, POOL_GROUP) for l in range(nl)]
        for l in range(nl):
            acc = ext[POOL_HIST + l]
            for k in range(1, w):
                acc = acc + ext[POOL_HIST + l - k]
            cnt = float(min(w, PAST_LEN + l + 1))
            pooled = acc / cnt - ext[POOL_HIST + l]
            mixed = _dot(pooled.astype(BF16), pw_ref[gi].astype(BF16))
            gt = tok(l, OFF_GATE + c0, POOL_GROUP)
            oa_ref[rows(l), c0:c0 + POOL_GROUP] = (
                mixed * ps_ref[:, c0:c0 + POOL_GROUP] * _silu(gt)).astype(BF16)
    for j in range(POOL_HIST):
        src = j + nl
        if src < POOL_HIST:
            npool_ref[j] = sp_ref[src]
        else:
            npool_ref[j] = tok(src - POOL_HIST, OFF_U, D_POOL)

    hist = CONV_WIDTH - 1
    conv_out = {}
    for name, poff, coff, width in (("x", OFF_XS, 0, D_SSM), ("b", OFF_B, D_SSM, D_BC),
                                    ("c", OFF_C, D_SSM + D_BC, D_BC)):
        ext = [scv_ref[j, :, coff:coff + width] for j in range(hist)]
        ext += [tok(l, poff, width) for l in range(nl)]
        outs = []
        for l in range(nl):
            acc = cbias_ref[:, coff:coff + width]
            for k in range(CONV_WIDTH):
                acc = acc + cw_ref[k:k + 1, coff:coff + width] * ext[l + k]
            outs.append(_silu(acc))
        conv_out[name] = outs
        for j in range(hist):
            nconv_ref[j, :, coff:coff + width] = ext[j + nl]
    xs, bs, cs = conv_out["x"], conv_out["b"], conv_out["c"]
    for l in range(nl):
        cs_ref[rows(l), :] = cs[l]
        bs_ref[rows(l), :] = bs[l].astype(BF16)

    a_neg = -jnp.exp(alog_ref[...])
    dt, a_cum = [], []
    run = None
    for l in range(nl):
        d = _softplus(dtr_ref[rows(l), :] + dtb_ref[...])
        dt.append(d)
        run = d * a_neg if run is None else run + d * a_neg
        a_cum.append(run)
    cdec_ref[...] = jnp.exp(a_cum[nl - 1])

    def onehot_rows(mats, onehot):
        parts = [_split2(m) for m in mats]
        stack = jnp.concatenate([p[0] for p in parts] + [p[1] for p in parts], axis=0)
        res = _dot(stack, onehot)
        n = len(mats)
        return [res[k * sb:(k + 1) * sb] + res[(n + k) * sb:(n + k + 1) * sb] for k in range(n)]

    pairs = [(l, s) for l in range(nl) for s in range(l + 1)]
    cbh = onehot_rows([cs[l] * bs[s] for l, s in pairs], ghot_ref[...])
    gls = [c * jnp.exp(a_cum[l] - a_cum[s]) * dt[s] for c, (l, s) in zip(cbh, pairs)]
    w_end = [jnp.exp(a_cum[nl - 1] - a_cum[l]) * dt[l] for l in range(nl)]
    e_cum = [jnp.exp(a_cum[l]) for l in range(nl)]
    chunk = 4 * LANES
    for c0 in range(0, D_SSM, chunk):
        cl = slice(c0, c0 + chunk)
        ex = onehot_rows(gls + w_end + e_cum, ehot_ref[:, cl])
        g_ex, w_ex, e_ex = ex[:len(pairs)], ex[len(pairs):len(pairs) + nl], ex[len(pairs) + nl:]
        xc = [x[:, cl] for x in xs]
        for l in range(nl):
            ydiag = dskip_ref[:, cl] * xc[l]
            for k, (pl_, ps_) in enumerate(pairs):
                if pl_ == l:
                    ydiag = ydiag + g_ex[k] * xc[ps_]
            ydx_ref[rows(l), cl] = ydiag
            xw_ref[rows(l), cl] = w_ex[l] * xc[l]
            ecum_ref[rows(l), cl] = e_ex[l]


def _sample1(pj3, dtr3, sp3, scv3, pool_w, pool_scale, conv_w, conv_b, dt_bias_p, a_log_p,
             dskip_row, ehot, ghot, sb):
    nblk = pj3.shape[0]
    db = nblk * sb
    rb = DEC_SEQ * sb
    blk2 = lambda width: pl.BlockSpec((None, rb, width), lambda i: (i, 0, 0))
    hist3 = lambda n, width: pl.BlockSpec((n, sb, width), lambda i: (0, i, 0))
    const2 = lambda shape: pl.BlockSpec(shape, lambda i: (0, 0))
    blk_shape = lambda width, dt: jax.ShapeDtypeStruct((nblk, rb, width), dt)
    return pl.pallas_call(
        functools.partial(_sample1_kernel, sb=sb),
        grid=(nblk,),
        in_specs=[
            blk2(D_MAIN), blk2(LANES), hist3(POOL_HIST, D_POOL), hist3(CONV_WIDTH - 1, D_CONV),
            pl.BlockSpec((len(POOL_WINDOWS), POOL_GROUP, POOL_GROUP), lambda i: (0, 0, 0)),
            const2((1, D_POOL)), const2((CONV_WIDTH, D_CONV)), const2((1, D_CONV)),
            const2((1, LANES)), const2((1, LANES)), const2((1, D_SSM)),
            const2((LANES, D_SSM)), const2((D_BC, LANES)),
        ],
        out_specs=[
            blk2(D_POOL), hist3(POOL_HIST, D_POOL), hist3(CONV_WIDTH - 1, D_CONV),
            blk2(D_BC), blk2(D_BC), blk2(D_SSM), blk2(D_SSM), blk2(D_SSM),
            pl.BlockSpec((sb, LANES), lambda i: (i, 0)),
        ],
        out_shape=[
            blk_shape(D_POOL, BF16),
            jax.ShapeDtypeStruct((POOL_HIST, db, D_POOL), F32),
            jax.ShapeDtypeStruct((CONV_WIDTH - 1, db, D_CONV), F32),
            blk_shape(D_BC, F32), blk_shape(D_BC, BF16),
            blk_shape(D_SSM, F32), blk_shape(D_SSM, F32), blk_shape(D_SSM, F32),
            jax.ShapeDtypeStruct((db, LANES), F32),
        ],
        compiler_params=pltpu.CompilerParams(
            dimension_semantics=("arbitrary",),
            vmem_limit_bytes=VMEM_LIMIT),
        name="sample_elementwise",
    )(pj3, dtr3, sp3, scv3, pool_w, pool_scale, conv_w, conv_b, dt_bias_p, a_log_p, dskip_row,
      ehot, ghot)


def _sample2_setup(xw_ref, cs_ref, xwt_ref, *, sb, per):
    j = pl.program_id(1)

    @pl.when(j == 0)
    def _():
        for blk in range(D_SSM // LANES):
            xwt_ref[blk * LANES:(blk + 1) * LANES, :] = jnp.transpose(
                xw_ref[:, blk * LANES:(blk + 1) * LANES]).astype(BF16)

    return [jnp.concatenate([cs_ref[pl.ds(l * sb + per * j + 2 * pr + bi, 1), :]
                             for bi in range(2) for l in range(DEC_SEQ)], axis=0).astype(BF16)
            for pr in range(per // 2)]


def _sample2_group(g, refs, c_rows, *, sb, per):
    cdec_ref, st_ref, _, bs_ref, _, nst_ref, yo_ref, xwt_ref = refs
    i = pl.program_id(0)
    j = pl.program_id(1)
    nl = DEC_SEQ
    rows = sb * nl
    seq_of_row = lax.broadcasted_iota(jnp.int32, (rows, 1), 0) & (sb - 1)
    gw = HEADS_PER_GROUP * HEAD_DIM
    gs = slice(g * D_STATE, (g + 1) * D_STATE)
    for pr in range(per // 2):
        q0 = per * j + 2 * pr
        c8 = c_rows[pr][:, gs]
        b_blk = bs_ref[:, gs]
        zero_b = jnp.zeros_like(b_blk)
        w2 = jnp.concatenate([jnp.where(seq_of_row == q0, b_blk, zero_b),
                              jnp.where(seq_of_row == q0 + 1, b_blk, zero_b)], axis=1)
        u2 = _dot(xwt_ref[g * gw:(g + 1) * gw, :], w2)
        for bi in range(2):
            sq = 2 * pr + bi
            s0 = st_ref[sq, g * HEADS_PER_GROUP:(g + 1) * HEADS_PER_GROUP].reshape(gw, D_STATE)
            yo = _dot_nt(c8, s0.astype(BF16))
            base = (i * sb + q0 + bi) * HEADS + g * HEADS_PER_GROUP
            for r in range(HEADS_PER_GROUP):
                dec = cdec_ref[base + r]
                rs = slice(r * HEAD_DIM, (r + 1) * HEAD_DIM)
                nst_ref[sq, g * HEADS_PER_GROUP + r] = (
                    s0[rs] * dec + u2[rs, bi * D_STATE:(bi + 1) * D_STATE])
            for l in range(nl):
                yo_ref[pl.ds(l * sb + q0 + bi, 1), g * gw:(g + 1) * gw] = (
                    yo[bi * nl + l:bi * nl + l + 1])


N_SSD_IN, N_SSD_OUT, N_SSD_SCRATCH = 11, 3, 6
N_ST_IN, N_ST_OUT = 5, 2


def _ssd_state_kernel(*refs, nc, sb, per):
    n_in = N_SSD_IN + N_ST_IN + 1
    ins, rest = refs[:n_in], refs[n_in:]
    n_out = N_SSD_OUT + N_ST_OUT + 1
    outs, scratch = rest[:n_out], rest[n_out:]
    outs[-1][...] = ins[-1][...].astype(BF16)
    st_refs = (*ins[N_SSD_IN:N_SSD_IN + N_ST_IN], *outs[N_SSD_OUT:N_SSD_OUT + N_ST_OUT],
               *scratch[N_SSD_SCRATCH:])
    c_rows = _sample2_setup(st_refs[2], st_refs[4], st_refs[7], sb=sb, per=per)
    _ssd_body(*ins[:N_SSD_IN], *outs[:N_SSD_OUT], *scratch[:N_SSD_SCRATCH], nc=nc,
              after_group=functools.partial(_sample2_group, refs=st_refs, c_rows=c_rows,
                                            sb=sb, per=per))


def _ssd_prompt_sample_state(proj, dt_raw, conv_w, conv_b, dt_bias_p, a_log_p, dskip_row, norm_g,
                             bsz, seq, cdec_flat, state, xw3, bs3, cs3, sb, w_out2d):
    q = CHUNK
    nc = seq // q
    db = state.shape[0]
    rb = DEC_SEQ * sb
    per = sb // nc
    assert rb == LANES and db // sb == bsz and per * nc == sb and per % 2 == 0
    d_mix = w_out2d.shape[0]
    slab = d_mix // (bsz * nc)
    assert slab * bsz * nc == d_mix and slab % (2 * SUBLANES) == 0
    wslab = pl.BlockSpec((slab, D_MODEL), lambda b, c: (b * nc + c, 0))
    row = lambda b, c: b * nc + c
    const = lambda b, c: (0, 0)
    blk2 = lambda width: pl.BlockSpec((None, rb, width), lambda i, j: (i, 0, 0))
    st_spec = pl.BlockSpec((per, HEADS, HEAD_DIM, D_STATE), lambda i, j: (i * nc + j, 0, 0, 0))
    return pl.pallas_call(
        functools.partial(_ssd_state_kernel, nc=nc, sb=sb, per=per),
        grid=(bsz, nc),
        in_specs=[
            pl.BlockSpec((q, D_SSM), lambda b, c: (row(b, c), OFF_Z // D_SSM)),
            pl.BlockSpec((q, D_SSM), lambda b, c: (row(b, c), OFF_XS // D_SSM)),
            pl.BlockSpec((q, D_BC), lambda b, c: (row(b, c), OFF_B // D_BC)),
            pl.BlockSpec((q, D_BC), lambda b, c: (row(b, c), OFF_C // D_BC)),
            pl.BlockSpec((q, LANES), lambda b, c: (row(b, c), 0)),
            pl.BlockSpec((CONV_WIDTH, D_CONV), const),
            pl.BlockSpec((1, D_CONV), const),
            pl.BlockSpec((1, LANES), const),
            pl.BlockSpec((1, LANES), const),
            pl.BlockSpec((1, D_SSM), const),
            pl.BlockSpec((1, D_SSM), const),
            pl.BlockSpec(memory_space=pltpu.SMEM),
            st_spec, blk2(D_SSM), blk2(D_BC), blk2(D_BC),
            wslab,
        ],
        out_specs=[
            pl.BlockSpec((q, D_SSM), lambda b, c: (row(b, c), 0)),
            pl.BlockSpec((None, CONV_WIDTH - 1, D_CONV), lambda b, c: (b, 0, 0)),
            pl.BlockSpec((None, HEADS, HEAD_DIM, D_STATE), lambda b, c: (b, 0, 0, 0)),
            st_spec, blk2(D_SSM),
            wslab,
        ],
        out_shape=[
            jax.ShapeDtypeStruct((bsz * seq, D_SSM), BF16),
            jax.ShapeDtypeStruct((bsz, CONV_WIDTH - 1, D_CONV), F32),
            jax.ShapeDtypeStruct((bsz, HEADS, HEAD_DIM, D_STATE), F32),
            jax.ShapeDtypeStruct(state.shape, F32),
            jax.ShapeDtypeStruct((db // sb, rb, D_SSM), F32),
            jax.ShapeDtypeStruct(w_out2d.shape, BF16),
        ],
        scratch_shapes=[
            pltpu.VMEM((D_CONV // LANES, q + SUBLANES, LANES), F32),
            pltpu.VMEM((D_STATE, D_SSM), F32),
            pltpu.VMEM((D_SSM // LANES, q + SUBLANES, LANES), F32),
            pltpu.VMEM((D_BC // LANES, q + SUBLANES, LANES), F32),
            pltpu.VMEM((D_BC // LANES, q + SUBLANES, LANES), F32),
            pltpu.VMEM((q, D_SSM), F32),
            pltpu.VMEM((D_SSM, rb), BF16),
        ],
        compiler_params=pltpu.CompilerParams(
            dimension_semantics=("arbitrary", "arbitrary"),
            vmem_limit_bytes=VMEM_LIMIT),
        name="ssd_prompt_sample_state",
    )(proj, proj, proj, proj, dt_raw, conv_w, conv_b, dt_bias_p, a_log_p, dskip_row, norm_g,
      cdec_flat, state, xw3, bs3, cs3, w_out2d)


def _outproj_sample_kernel(a_ref, yo_ref, ecum_ref, ydx_ref, z_ref, ng_ref, w_ref, x_ref,
                           g_ref, y_ref, b_ref):
    gw = D_SSM // GROUPS
    for g in range(GROUPS):
        cs = slice(g * gw, (g + 1) * gw)
        y = ydx_ref[:, cs] + ecum_ref[:, cs] * yo_ref[:, cs]
        yz = y * _silu(z_ref[:, cs].astype(F32))
        ms = jnp.sum(yz * yz, axis=-1, keepdims=True) * (1.0 / gw)
        b_ref[:, cs] = (yz * lax.rsqrt(ms + EPS) * ng_ref[:, cs]).astype(BF16)
    _outproj_kernel(a_ref, b_ref, w_ref, x_ref, g_ref, y_ref)


def _outproj_sample(out_a, yo, ecum, ydx, proj, norm_g, w_bf, x2d, final_g):
    m = x2d.shape[0]
    tm = min(256, m)
    rowblk = lambda width: pl.BlockSpec((tm, width), lambda i: (i, 0))
    const = lambda shape, **kw: pl.BlockSpec(shape, lambda i: (0, 0), **kw)
    return pl.pallas_call(
        _outproj_sample_kernel,
        grid=(m // tm,),
        in_specs=[
            rowblk(D_POOL), rowblk(D_SSM), rowblk(D_SSM), rowblk(D_SSM),
            pl.BlockSpec((tm, D_SSM), lambda i: (i, OFF_Z // D_SSM)),
            const((1, D_SSM)),
            const((D_POOL + D_SSM, D_MODEL), pipeline_mode=pl.Buffered(1)),
            rowblk(D_MODEL), const((1, D_MODEL)),
        ],
        out_specs=rowblk(D_MODEL),
        out_shape=jax.ShapeDtypeStruct((m, D_MODEL), F32),
        scratch_shapes=[pltpu.VMEM((tm, D_SSM), BF16)],
        compiler_params=pltpu.CompilerParams(
            dimension_semantics=("arbitrary",),
            vmem_limit_bytes=VMEM_LIMIT),
        name="outproj_sample",
    )(out_a, yo, ecum, ydx, proj, norm_g, w_bf, x2d, final_g)


def kernel(x_prompt, x_sample, state_pool, state_conv, state_ssm, norm_g, w_in, conv_w, conv_b,
           dt_bias, a_log, d_skip, ssm_norm_g, pool_w, pool_scale, w_out, final_g):
    bsz, seq, _ = x_prompt.shape
    db, nl, _ = x_sample.shape
    assert nl == DEC_SEQ and seq % CHUNK == 0 and w_in.shape[0] == 1

    w_t = jnp.transpose(w_in[0])
    pool_w_b = pool_w[0]
    g_in = norm_g[0][None, :]
    g_fin = final_g[None, :]
    ps = pool_scale[0][None, :]
    cw = conv_w[0]
    cbias = conv_b[0][None, :]
    pad_h = lambda v: jnp.pad(v, (0, LANES - HEADS))[None, :]
    dtb = pad_h(dt_bias[0])
    alog = pad_h(a_log[0])
    dskip_row = jnp.repeat(d_skip[0], HEAD_DIM)[None, :]
    ng = ssm_norm_g[0][None, :]

    sb = SAMPLE_BLOCK
    nblk = db // sb
    rb = nl * sb
    xs2 = x_sample.reshape(nblk, sb, nl, D_MODEL).transpose(0, 2, 1, 3).reshape(db * nl, D_MODEL)
    proj_s, dt_s, w_bf, wdt_bf = _inproj(xs2, g_in, w_t, w_t, 1024)
    pj3 = proj_s.reshape(nblk, rb, D_MAIN)
    head_of_ch = jnp.arange(D_SSM, dtype=jnp.int32) // HEAD_DIM
    ehot = (jnp.arange(LANES, dtype=jnp.int32)[:, None] == head_of_ch[None, :]).astype(BF16)
    grp_of_row = jnp.arange(D_BC, dtype=jnp.int32) // D_STATE
    head_id = jnp.arange(LANES, dtype=jnp.int32)
    ghot = ((head_id[None, :] // HEADS_PER_GROUP == grp_of_row[:, None])
            & (head_id[None, :] < HEADS)).astype(BF16)
    (oa_s, npool_s, nconv_s, cs_s, bs_s, xw_s, ydx_s, ecum_s, cdec_s) = _sample1(
        pj3, dt_s.reshape(nblk, rb, LANES), jnp.transpose(state_pool[0], (1, 0, 2)),
        jnp.transpose(state_conv[0], (1, 0, 2)), pool_w_b, ps, cw, cbias, dtb, alog,
        dskip_row, ehot, ghot, sb)

    xp2 = x_prompt.reshape(bsz * seq, D_MODEL)
    proj_p, dt_p = _inproj(xp2, g_in, w_bf, wdt_bf, PROMPT_TN)
    oa_p, npool_p = _pool_prompt(proj_p, pool_w_b, ps, bsz, seq)
    ob_p, nconv_p, nssm_p, nssm_s, yo_s, w_out_bf = _ssd_prompt_sample_state(
        proj_p, dt_p, cw, cbias, dtb, alog, dskip_row, ng, bsz, seq,
        cdec_s[:, :HEADS].reshape(db * HEADS), state_ssm[0], xw_s, bs_s, cs_s, sb, w_out[0])
    y_p = _outproj(oa_p, ob_p, w_out_bf, xp2, g_fin).reshape(bsz, seq, D_MODEL)

    flat = lambda t: t.reshape(db * nl, t.shape[-1])
    y_s = _outproj_sample(flat(oa_s), flat(yo_s), flat(ecum_s), flat(ydx_s), proj_s, ng, w_out_bf,
                          xs2, g_fin)
    y_s = y_s.reshape(nblk, nl, sb, D_MODEL).transpose(0, 2, 1, 3).reshape(db, nl, D_MODEL)

    return (y_p, y_s,
            npool_p[None], nconv_p[None], nssm_p[None],
            jnp.transpose(npool_s, (1, 0, 2))[None],
            jnp.transpose(nconv_s, (1, 0, 2))[None],
            nssm_s[None])
```

```python
import functools

import jax
import jax.numpy as jnp
from jax import lax
from jax.experimental import pallas as pl
from jax.experimental.pallas import tpu as pltpu

F32 = jnp.float32
BF16 = jnp.bfloat16

D_MODEL = 2048
D_POOL = 1024
POOL_WINDOWS = (2, 4, 8, 16)
POOL_GROUP = 256
POOL_HIST = 15
D_SSM = 3072
HEAD_DIM = 64
HEADS = 48
GROUPS = 8
HEADS_PER_GROUP = 6
D_STATE = 128
D_BC = GROUPS * D_STATE
CONV_WIDTH = 4
D_CONV = D_SSM + 2 * D_BC
D_MAIN = 2 * D_POOL + D_SSM + D_CONV
PAST_LEN = 16384
DEC_SEQ = 4
EPS = 1e-5

LANES = 128
SUBLANES = 8
VMEM_LIMIT = 56 * 1024 * 1024

OFF_Z = 0
OFF_XS = D_SSM
OFF_U = 2 * D_SSM
OFF_GATE = OFF_U + D_POOL
OFF_B = OFF_GATE + D_POOL
OFF_C = OFF_B + D_BC

PROMPT_TN = 2048
CHUNK = 128
SAMPLE_BLOCK = 32
NEG_BIG = -1e30
LOG2E = 1.4426950408889634


def _silu(v):
    h = 0.5 * v
    return h + h * jnp.tanh(h)


def _softplus(v):
    y = jnp.exp(-jnp.abs(v))
    u = 1.0 + y
    d = u - 1.0
    l1p = jnp.where(d == 0.0, y, jnp.log(u) * (y / jnp.where(d == 0.0, 1.0, d)))
    return jnp.maximum(v, 0.0) + l1p


def _split2(v):
    hi = v.astype(BF16)
    lo = (v - hi.astype(F32)).astype(BF16)
    return hi, lo


def _dot(a, b):
    return jnp.dot(a, b, preferred_element_type=F32)


def _dot_nt(a, b):
    return lax.dot_general(a, b, (((1,), (1,)), ((), ())), preferred_element_type=F32)


def _dot2(v, onehot):
    hi, lo = _split2(v)
    return _dot(hi, onehot) + _dot(lo, onehot)


def _inproj_kernel(x_ref, g_ref, w_ref, wdt_ref, o_ref, dt_ref, *rest):
    from_f32 = w_ref.dtype == F32
    (wb_out_ref, wdt_out_ref, h_ref) = rest if from_f32 else (None, None, *rest)

    @pl.when(pl.program_id(1) == 0)
    def _():
        x = x_ref[...]
        ms = jnp.mean(x * x, axis=-1, keepdims=True)
        h = (x * lax.rsqrt(ms + EPS) * g_ref[...]).astype(BF16)
        h_ref[...] = h
        if from_f32:
            wrow = lax.broadcasted_iota(jnp.int32, wdt_ref.shape, 0)
            wdt = jnp.where(wrow < HEADS, wdt_ref[...], 0.0).astype(BF16)
            wdt_out_ref[...] = wdt
        else:
            wdt = wdt_ref[...]
        dt_ref[...] = _dot_nt(h, wdt)

    if from_f32:
        wb = w_ref[...].astype(BF16)
        wb_out_ref[...] = wb
    else:
        wb = w_ref[...]
    o_ref[...] = _dot_nt(h_ref[...], wb).astype(BF16)


def _src_block(j):
    nz = (D_SSM + D_SSM) // 1024
    npool = 2 * D_POOL // 1024
    return jnp.where(j < nz, j + npool, jnp.where(j < nz + npool, j - nz, j))


def _inproj(x2d, norm_g, w, w_dt, tn):
    m = x2d.shape[0]
    tm = min(1024, m)
    from_f32 = w.dtype == F32
    assert tn == 1024 or not from_f32
    w_map = (lambda i, j: (_src_block(j), 0)) if from_f32 else (lambda i, j: (j, 0))
    wdt_map = (lambda i, j: (D_MAIN // LANES, 0)) if from_f32 else (lambda i, j: (0, 0))
    out_specs = [
        pl.BlockSpec((tm, tn), lambda i, j: (i, j)),
        pl.BlockSpec((tm, LANES), lambda i, j: (i, 0)),
    ]
    out_shape = [
        jax.ShapeDtypeStruct((m, D_MAIN), BF16),
        jax.ShapeDtypeStruct((m, LANES), F32),
    ]
    if from_f32:
        assert m == tm
        out_specs += [pl.BlockSpec((tn, D_MODEL), lambda i, j: (j, 0)),
                      pl.BlockSpec((LANES, D_MODEL), lambda i, j: (0, 0))]
        out_shape += [jax.ShapeDtypeStruct((D_MAIN, D_MODEL), BF16),
                      jax.ShapeDtypeStruct((LANES, D_MODEL), BF16)]
    return pl.pallas_call(
        _inproj_kernel,
        grid=(m // tm, D_MAIN // tn),
        in_specs=[
            pl.BlockSpec((tm, D_MODEL), lambda i, j: (i, 0)),
            pl.BlockSpec((1, D_MODEL), lambda i, j: (0, 0)),
            pl.BlockSpec((tn, D_MODEL), w_map),
            pl.BlockSpec((LANES, D_MODEL), wdt_map),
        ],
        out_specs=out_specs,
        out_shape=out_shape,
        scratch_shapes=[pltpu.VMEM((tm, D_MODEL), BF16)],
        compiler_params=pltpu.CompilerParams(
            dimension_semantics=("arbitrary", "arbitrary"),
            vmem_limit_bytes=VMEM_LIMIT),
        name="inproj",
    )(x2d, norm_g, w, w_dt)


def _outproj_kernel(a_ref, b_ref, w_ref, x_ref, g_ref, y_ref):
    acc = (_dot(a_ref[...], w_ref[0:D_POOL, :])
           + _dot(b_ref[...], w_ref[D_POOL:D_POOL + D_SSM, :]))
    r = x_ref[...] + acc
    ms = jnp.mean(r * r, axis=-1, keepdims=True)
    y_ref[...] = r * lax.rsqrt(ms + EPS) * g_ref[...]


def _pool_kernel(u_ref, gate_ref, pw_ref, ps_ref, oa_ref, np_ref, ubuf, wbuf, *, tl, nt):
    t = pl.program_id(1)
    hist = POOL_HIST + 1
    nrows = hist + tl
    nv = nrows // SUBLANES
    bpg = POOL_GROUP // LANES

    @pl.when(t == 0)
    def _():
        ubuf[:, 0:hist, :] = jnp.zeros((ubuf.shape[0], hist, LANES), F32)

    pos = t * tl + lax.broadcasted_iota(jnp.int32, (tl, 1), 0)
    for gi, w in enumerate(POOL_WINDOWS):
        win = []
        for lb in range(gi * bpg, (gi + 1) * bpg):
            ubuf[lb, hist:nrows, :] = u_ref[:, lb * LANES:(lb + 1) * LANES].astype(F32)
            acc = [ubuf[lb, pl.ds(a, SUBLANES, stride=nv), :] for a in range(nv)]
            span = 1
            while span < w:
                wrap = [pltpu.roll(acc[nv - span + a], 1, 0) for a in range(span)]
                acc = [acc[a] + (acc[a - span] if a >= span else wrap[a]) for a in range(nv)]
                span *= 2
            for a in range(nv):
                wbuf[lb, pl.ds(a, SUBLANES, stride=nv), :] = acc[a]
            win.append(wbuf[lb, hist:nrows, :])
        cs = slice(gi * POOL_GROUP, (gi + 1) * POOL_GROUP)
        cnt = jnp.minimum(w, pos + 1).astype(F32)
        ug = jnp.concatenate([ubuf[lb, hist:nrows, :] for lb in range(gi * bpg, (gi + 1) * bpg)],
                             axis=1)
        pooled = jnp.concatenate(win, axis=1) / cnt - ug
        mixed = _dot(pooled.astype(BF16), pw_ref[gi].astype(BF16))
        gt = gate_ref[:, cs].astype(F32)
        oa_ref[:, cs] = (mixed * ps_ref[:, cs] * _silu(gt)).astype(BF16)

    @pl.when(t == nt - 1)
    def _():
        for lb in range(D_POOL // LANES):
            np_ref[:, lb * LANES:(lb + 1) * LANES] = ubuf[lb, tl + 1:nrows, :]

    ubuf[:, 0:hist, :] = ubuf[:, tl:nrows, :]


def _pool_outproj_kernel(u_ref, gate_ref, pw_ref, ps_ref, b_ref, w_ref, x_ref, g_ref,
                         y_ref, np_ref, ubuf, wbuf, a_ref, *, tl, nt):
    _pool_kernel(u_ref, gate_ref, pw_ref, ps_ref, a_ref, np_ref, ubuf, wbuf, tl=tl, nt=nt)
    _outproj_kernel(a_ref, b_ref, w_ref, x_ref, g_ref, y_ref)


def _pool_outproj_prompt(proj, pool_w, pool_scale, out_b, w_bf, x2d, final_g, bsz, seq):
    tl = min(512, seq)
    nt = seq // tl
    row = lambda b, t: b * nt + t
    const = lambda b, t: (0, 0)
    return pl.pallas_call(
        functools.partial(_pool_outproj_kernel, tl=tl, nt=nt),
        grid=(bsz, nt),
        in_specs=[
            pl.BlockSpec((tl, D_POOL), lambda b, t: (row(b, t), OFF_U // D_POOL)),
            pl.BlockSpec((tl, D_POOL), lambda b, t: (row(b, t), OFF_GATE // D_POOL)),
            pl.BlockSpec((len(POOL_WINDOWS), POOL_GROUP, POOL_GROUP), lambda b, t: (0, 0, 0)),
            pl.BlockSpec((1, D_POOL), const),
            pl.BlockSpec((tl, D_SSM), lambda b, t: (row(b, t), 0)),
            pl.BlockSpec((D_POOL + D_SSM, D_MODEL), const, pipeline_mode=pl.Buffered(1)),
            pl.BlockSpec((tl, D_MODEL), lambda b, t: (row(b, t), 0)),
            pl.BlockSpec((1, D_MODEL), const),
        ],
        out_specs=[
            pl.BlockSpec((tl, D_MODEL), lambda b, t: (row(b, t), 0)),
            pl.BlockSpec((None, POOL_HIST, D_POOL), lambda b, t: (b, 0, 0)),
        ],
        out_shape=[
            jax.ShapeDtypeStruct((bsz * seq, D_MODEL), F32),
            jax.ShapeDtypeStruct((bsz, POOL_HIST, D_POOL), F32),
        ],
        scratch_shapes=[pltpu.VMEM((D_POOL // LANES, tl + POOL_HIST + 1, LANES), F32),
                        pltpu.VMEM((D_POOL // LANES, tl + POOL_HIST + 1, LANES), F32),
                        pltpu.VMEM((tl, D_POOL), BF16)],
        compiler_params=pltpu.CompilerParams(
            dimension_semantics=("arbitrary", "arbitrary"),
            vmem_limit_bytes=VMEM_LIMIT),
        name="pool_outproj",
    )(proj, proj, pool_w, pool_scale, out_b, w_bf, x2d, final_g)


def _ssd_body(z_ref, xs_ref, b_ref, c_ref, dtr_ref, cw_ref, cbias_ref, dtb_ref, alog_ref,
              dskip_ref, ng_ref,
              ob_ref, nconv_ref, nssm_ref,
              cbuf, st_ref, cvx_ref, cvb_ref, cvc_ref, y_ref, *, nc, after_group=None):
    q = CHUNK
    c_idx = pl.program_id(1)
    halo = SUBLANES

    nbx, nbb = D_SSM // LANES, D_BC // LANES

    @pl.when(c_idx == 0)
    def _():
        cbuf[:, 0:halo, :] = jnp.zeros((cbuf.shape[0], halo, LANES), F32)
        st_ref[...] = jnp.zeros(st_ref.shape, F32)

    nv = (halo + q) // SUBLANES
    for blk in range(D_CONV // LANES):
        ls = slice(blk * LANES, (blk + 1) * LANES)
        if blk < nbx:
            src_ref, off, dst = xs_ref, blk * LANES, cvx_ref.at[blk]
        elif blk < nbx + nbb:
            src_ref, off, dst = b_ref, (blk - nbx) * LANES, cvb_ref.at[blk - nbx]
        else:
            src_ref, off, dst = c_ref, (blk - nbx - nbb) * LANES, cvc_ref.at[blk - nbx - nbb]
        cbuf[blk, halo:halo + q, :] = src_ref[:, off:off + LANES].astype(F32)
        xv = [cbuf[blk, pl.ds(a, SUBLANES, stride=nv), :] for a in range(nv)]
        wrap = [pltpu.roll(xv[nv - k], 1, 0) for k in range(1, CONV_WIDTH)]
        taps = [0.5 * cw_ref[k:k + 1, ls] for k in range(CONV_WIDTH)]
        bias = 0.5 * cbias_ref[:, ls]
        for a in range(nv):
            h = bias + taps[CONV_WIDTH - 1] * xv[a]
            for k in range(1, CONV_WIDTH):
                src = xv[a - k] if a >= k else wrap[k - a - 1]
                h = h + taps[CONV_WIDTH - 1 - k] * src
            dst[pl.ds(a, SUBLANES, stride=nv), :] = h + h * jnp.tanh(h)
    rows = slice(halo, halo + q)

    dt = _softplus(dtr_ref[...] + dtb_ref[...])
    a_neg = -jnp.exp(alog_ref[...])
    da = dt * (a_neg * LOG2E)
    row = lax.broadcasted_iota(jnp.int32, (q, LANES), 0)
    a2 = da
    shift = 1
    while shift < q:
        a2 = a2 + jnp.where(row >= shift, pltpu.roll(a2, shift, 0), 0.0)
        shift *= 2
    a2_t = jnp.transpose(a2)
    ldt_t = jnp.log2(jnp.transpose(dt))
    a2_end_t = a2_t[:, q - 1:q]
    w_t = jnp.exp2(a2_end_t - a2_t + ldt_t)
    cdec_t = jnp.exp2(a2_end_t)
    srow_t = a2_t - ldt_t
    ea = jnp.exp2(a2)

    li = lax.broadcasted_iota(jnp.int32, (q, q), 0)
    si = lax.broadcasted_iota(jnp.int32, (q, q), 1)
    tri = li >= si
    lane = lax.broadcasted_iota(jnp.int32, (q, LANES), 1)
    lo_half = lane < HEAD_DIM

    for g in range(GROUPS):
        gs = slice(g * D_STATE, (g + 1) * D_STATE)
        c_gb = cvc_ref[g, rows, :].astype(BF16)
        b_g = cvb_ref[g, rows, :]
        cb = _dot_nt(c_gb, b_g.astype(BF16))
        b_t = jnp.transpose(b_g)
        ppg = HEADS_PER_GROUP // 2
        gl = slice(g * ppg * LANES, (g + 1) * ppg * LANES)
        y_off = _dot(c_gb, st_ref[:, gl].astype(BF16))
        for j in range(ppg):
            blk = g * ppg + j
            ls = slice(blk * LANES, (blk + 1) * LANES)
            sc, bw, ecol, dec = [], [], [], []
            for h in (2 * blk, 2 * blk + 1):
                a_col = jnp.broadcast_to(a2[:, h:h + 1], (q, q))
                decay_dt = jnp.exp2(jnp.where(tri, a_col - srow_t[h:h + 1, :], NEG_BIG))
                sc.append((cb * decay_dt).astype(BF16))
                bw.append((b_t * w_t[h:h + 1, :]).astype(BF16))
                ecol.append(jnp.broadcast_to(ea[:, h:h + 1], (q, LANES)))
                dec.append(jnp.broadcast_to(cdec_t[h:h + 1, :], (q, LANES)))
            x_p = cvx_ref[blk, rows, :].astype(BF16)
            zero_b = jnp.zeros_like(x_p)
            x_bd = jnp.concatenate([jnp.where(lo_half, x_p, zero_b),
                                    jnp.where(lo_half, zero_b, x_p)], axis=0)
            lhs1 = jnp.concatenate([jnp.concatenate(sc, axis=1),
                                    jnp.concatenate(bw, axis=1)], axis=0)
            r1 = _dot(lhs1, x_bd)
            y_ref[:, ls] = (r1[0:q] + y_off[:, j * LANES:(j + 1) * LANES]
                            * jnp.where(lo_half, ecol[0], ecol[1]))
            st_ref[:, ls] = (st_ref[:, ls] * jnp.where(lo_half, dec[0], dec[1])
                             + r1[q:2 * q])
        if after_group is not None:
            after_group(g)

    gw = D_SSM // GROUPS
    bpg = gw // LANES
    for g in range(GROUPS):
        yz, ssq = [], None
        for blk in range(g * bpg, (g + 1) * bpg):
            ls = slice(blk * LANES, (blk + 1) * LANES)
            y = y_ref[:, ls] + cvx_ref[blk, rows, :] * dskip_ref[:, ls]
            v = y * _silu(z_ref[:, ls].astype(F32))
            yz.append(v)
            ssq = v * v if ssq is None else ssq + v * v
        scale = lax.rsqrt(jnp.sum(ssq, axis=-1, keepdims=True) * (1.0 / gw) + EPS)
        for i, blk in enumerate(range(g * bpg, (g + 1) * bpg)):
            ls = slice(blk * LANES, (blk + 1) * LANES)
            ob_ref[:, ls] = (yz[i] * scale * ng_ref[:, ls]).astype(BF16)

    cbuf[:, 0:halo, :] = cbuf[:, q:q + halo, :]

    @pl.when(c_idx == nc - 1)
    def _():
        for blk in range(D_CONV // LANES):
            nconv_ref[:, blk * LANES:(blk + 1) * LANES] = cbuf[blk, q + halo - 3:q + halo, :]
        for blk in range(D_SSM // LANES):
            t = jnp.transpose(st_ref[:, blk * LANES:(blk + 1) * LANES])
            nssm_ref[2 * blk:2 * blk + 2] = t.reshape(2, HEAD_DIM, D_STATE)


def _sample1_kernel(pj_ref, dtr_ref, sp_ref, scv_ref, pw_ref, ps_ref, cw_ref, cbias_ref, dtb_ref,
                    alog_ref, dskip_ref, ehot_ref, ghot_ref,
                    oa_ref, npool_ref, nconv_ref, cs_ref, bs_ref, xw_ref, ydx_ref, ecum_ref,
                    cdec_ref, *, sb):
    nl = DEC_SEQ

    def rows(l):
        return slice(l * sb, (l + 1) * sb)

    def tok(l, off, width):
        return pj_ref[rows(l), off:off + width].astype(F32)

    for gi, w in enumerate(POOL_WINDOWS):
        c0 = gi * POOL_GROUP
        ext = [sp_ref[j, :, c0:c0 + POOL_GROUP] for j in range(POOL_HIST)]
        ext += [tok(l, OFF_U + c0, POOL_GROUP) for l in range(nl)]
        for l in range(nl):
            acc = ext[POOL_HIST + l]
            for k in range(1, w):
                acc = acc + ext[POOL_HIST + l - k]
            cnt = float(min(w, PAST_LEN + l + 1))
            pooled = acc / cnt - ext[POOL_HIST + l]
            mixed = _dot(pooled.astype(BF16), pw_ref[gi].astype(BF16))
            gt = tok(l, OFF_GATE + c0, POOL_GROUP)
            oa_ref[rows(l), c0:c0 + POOL_GROUP] = (
                mixed * ps_ref[:, c0:c0 + POOL_GROUP] * _silu(gt)).astype(BF16)
    for j in range(POOL_HIST):
        src = j + nl
        if src < POOL_HIST:
            npool_ref[j] = sp_ref[src]
        else:
            npool_ref[j] = tok(src - POOL_HIST, OFF_U, D_POOL)

    hist = CONV_WIDTH - 1
    conv_out = {}
    for name, poff, coff, width in (("x", OFF_XS, 0, D_SSM), ("b", OFF_B, D_SSM, D_BC),
                                    ("c", OFF_C, D_SSM + D_BC, D_BC)):
        ext = [scv_ref[j, :, coff:coff + width] for j in range(hist)]
        ext += [tok(l, poff, width) for l in range(nl)]
        outs = []
        for l in range(nl):
            acc = cbias_ref[:, coff:coff + width]
            for k in range(CONV_WIDTH):
                acc = acc + cw_ref[k:k + 1, coff:coff + width] * ext[l + k]
            outs.append(_silu(acc))
        conv_out[name] = outs
        for j in range(hist):
            nconv_ref[j, :, coff:coff + width] = ext[j + nl]
    xs, bs, cs = conv_out["x"], conv_out["b"], conv_out["c"]
    for l in range(nl):
        cs_ref[rows(l), :] = cs[l]
        bs_ref[rows(l), :] = bs[l].astype(BF16)

    a_neg = -jnp.exp(alog_ref[...])
    dt, a_cum = [], []
    run = None
    for l in range(nl):
        d = _softplus(dtr_ref[rows(l), :] + dtb_ref[...])
        dt.append(d)
        run = d * a_neg if run is None else run + d * a_neg
        a_cum.append(run)
    cdec_ref[...] = jnp.exp(a_cum[nl - 1])

    def onehot_rows(mats, onehot):
        parts = [_split2(m) for m in mats]
        stack = jnp.concatenate([p[0] for p in parts] + [p[1] for p in parts], axis=0)
        res = _dot(stack, onehot)
        n = len(mats)
        return [res[k * sb:(k + 1) * sb] + res[(n + k) * sb:(n + k + 1) * sb] for k in range(n)]

    pairs = [(l, s) for l in range(nl) for s in range(l + 1)]
    cbh = onehot_rows([cs[l] * bs[s] for l, s in pairs], ghot_ref[...])
    gls = [c * jnp.exp(a_cum[l] - a_cum[s]) * dt[s] for c, (l, s) in zip(cbh, pairs)]
    w_end = [jnp.exp(a_cum[nl - 1] - a_cum[l]) * dt[l] for l in range(nl)]
    e_cum = [jnp.exp(a_cum[l]) for l in range(nl)]
    chunk = 4 * LANES
    for c0 in range(0, D_SSM, chunk):
        cl = slice(c0, c0 + chunk)
        ex = onehot_rows(gls + w_end + e_cum, ehot_ref[:, cl])
        g_ex, w_ex, e_ex = ex[:len(pairs)], ex[len(pairs):len(pairs) + nl], ex[len(pairs) + nl:]
        xc = [x[:, cl] for x in xs]
        for l in range(nl):
            ydiag = dskip_ref[:, cl] * xc[l]
            for k, (pl_, ps_) in enumerate(pairs):
                if pl_ == l:
                    ydiag = ydiag + g_ex[k] * xc[ps_]
            ydx_ref[rows(l), cl] = ydiag
            xw_ref[rows(l), cl] = w_ex[l] * xc[l]
            ecum_ref[rows(l), cl] = e_ex[l]


def _sample1(pj3, dtr3, sp3, scv3, pool_w, pool_scale, conv_w, conv_b, dt_bias_p, a_log_p,
             dskip_row, ehot, ghot, sb):
    nblk = pj3.shape[0]
    db = nblk * sb
    rb = DEC_SEQ * sb
    blk2 = lambda width: pl.BlockSpec((None, rb, width), lambda i: (i, 0, 0))
    hist3 = lambda n, width: pl.BlockSpec((n, sb, width), lambda i: (0, i, 0))
    const2 = lambda shape: pl.BlockSpec(shape, lambda i: (0, 0))
    blk_shape = lambda width, dt: jax.ShapeDtypeStruct((nblk, rb, width), dt)
    return pl.pallas_call(
        functools.partial(_sample1_kernel, sb=sb),
        grid=(nblk,),
        in_specs=[
            blk2(D_MAIN), blk2(LANES), hist3(POOL_HIST, D_POOL), hist3(CONV_WIDTH - 1, D_CONV),
            pl.BlockSpec((len(POOL_WINDOWS), POOL_GROUP, POOL_GROUP), lambda i: (0, 0, 0)),
            const2((1, D_POOL)), const2((CONV_WIDTH, D_CONV)), const2((1, D_CONV)),
            const2((1, LANES)), const2((1, LANES)), const2((1, D_SSM)),
            const2((LANES, D_SSM)), const2((D_BC, LANES)),
        ],
        out_specs=[
            blk2(D_POOL), hist3(POOL_HIST, D_POOL), hist3(CONV_WIDTH - 1, D_CONV),
            blk2(D_BC), blk2(D_BC), blk2(D_SSM), blk2(D_SSM), blk2(D_SSM),
            pl.BlockSpec((sb, LANES), lambda i: (i, 0)),
        ],
        out_shape=[
            blk_shape(D_POOL, BF16),
            jax.ShapeDtypeStruct((POOL_HIST, db, D_POOL), F32),
            jax.ShapeDtypeStruct((CONV_WIDTH - 1, db, D_CONV), F32),
            blk_shape(D_BC, F32), blk_shape(D_BC, BF16),
            blk_shape(D_SSM, F32), blk_shape(D_SSM, F32), blk_shape(D_SSM, F32),
            jax.ShapeDtypeStruct((db, LANES), F32),
        ],
        compiler_params=pltpu.CompilerParams(
            dimension_semantics=("arbitrary",),
            vmem_limit_bytes=VMEM_LIMIT),
        name="sample_elementwise",
    )(pj3, dtr3, sp3, scv3, pool_w, pool_scale, conv_w, conv_b, dt_bias_p, a_log_p, dskip_row,
      ehot, ghot)


def _sample2_setup(xw_ref, cs_ref, xwt_ref, *, sb, per):
    j = pl.program_id(1)

    @pl.when(j == 0)
    def _():
        for blk in range(D_SSM // LANES):
            xwt_ref[blk * LANES:(blk + 1) * LANES, :] = jnp.transpose(
                xw_ref[:, blk * LANES:(blk + 1) * LANES]).astype(BF16)

    return [jnp.concatenate([cs_ref[pl.ds(l * sb + per * j + 2 * pr + bi, 1), :]
                             for bi in range(2) for l in range(DEC_SEQ)], axis=0).astype(BF16)
            for pr in range(per // 2)]


def _sample2_group(g, refs, c_rows, *, sb, per):
    cdec_ref, st_ref, _, bs_ref, _, nst_ref, yo_ref, xwt_ref = refs
    i = pl.program_id(0)
    j = pl.program_id(1)
    nl = DEC_SEQ
    rows = sb * nl
    seq_of_row = lax.broadcasted_iota(jnp.int32, (rows, 1), 0) & (sb - 1)
    gw = HEADS_PER_GROUP * HEAD_DIM
    gs = slice(g * D_STATE, (g + 1) * D_STATE)
    for pr in range(per // 2):
        q0 = per * j + 2 * pr
        c8 = c_rows[pr][:, gs]
        b_blk = bs_ref[:, gs]
        zero_b = jnp.zeros_like(b_blk)
        w2 = jnp.concatenate([jnp.where(seq_of_row == q0, b_blk, zero_b),
                              jnp.where(seq_of_row == q0 + 1, b_blk, zero_b)], axis=1)
        u2 = _dot(xwt_ref[g * gw:(g + 1) * gw, :], w2)
        for bi in range(2):
            sq = 2 * pr + bi
            s0 = st_ref[sq, g * HEADS_PER_GROUP:(g + 1) * HEADS_PER_GROUP].reshape(gw, D_STATE)
            yo = _dot_nt(c8, s0.astype(BF16))
            base = (i * sb + q0 + bi) * HEADS + g * HEADS_PER_GROUP
            for r in range(HEADS_PER_GROUP):
                dec = cdec_ref[base + r]
                rs = slice(r * HEAD_DIM, (r + 1) * HEAD_DIM)
                nst_ref[sq, g * HEADS_PER_GROUP + r] = (
                    s0[rs] * dec + u2[rs, bi * D_STATE:(bi + 1) * D_STATE])
            for l in range(nl):
                yo_ref[pl.ds(l * sb + q0 + bi, 1), g * gw:(g + 1) * gw] = (
                    yo[bi * nl + l:bi * nl + l + 1])


N_SSD_IN, N_SSD_OUT, N_SSD_SCRATCH = 11, 3, 6
N_ST_IN, N_ST_OUT = 5, 2


def _ssd_state_kernel(*refs, nc, sb, per):
    n_in = N_SSD_IN + N_ST_IN + 1
    ins, rest = refs[:n_in], refs[n_in:]
    n_out = N_SSD_OUT + N_ST_OUT + 1
    outs, scratch = rest[:n_out], rest[n_out:]
    outs[-1][...] = ins[-1][...].astype(BF16)
    st_refs = (*ins[N_SSD_IN:N_SSD_IN + N_ST_IN], *outs[N_SSD_OUT:N_SSD_OUT + N_ST_OUT],
               *scratch[N_SSD_SCRATCH:])
    c_rows = _sample2_setup(st_refs[2], st_refs[4], st_refs[7], sb=sb, per=per)
    _ssd_body(*ins[:N_SSD_IN], *outs[:N_SSD_OUT], *scratch[:N_SSD_SCRATCH], nc=nc,
              after_group=functools.partial(_sample2_group, refs=st_refs, c_rows=c_rows,
                                            sb=sb, per=per))


def _ssd_prompt_sample_state(proj, dt_raw, conv_w, conv_b, dt_bias_p, a_log_p, dskip_row, norm_g,
                             bsz, seq, cdec_flat, state, xw3, bs3, cs3, sb, w_out2d):
    q = CHUNK
    nc = seq // q
    db = state.shape[0]
    rb = DEC_SEQ * sb
    per = sb // nc
    assert rb == LANES and db // sb == bsz and per * nc == sb and per % 2 == 0
    d_mix = w_out2d.shape[0]
    slab = d_mix // (bsz * nc)
    assert slab * bsz * nc == d_mix and slab % (2 * SUBLANES) == 0
    wslab = pl.BlockSpec((slab, D_MODEL), lambda b, c: (b * nc + c, 0))
    row = lambda b, c: b * nc + c
    const = lambda b, c: (0, 0)
    blk2 = lambda width: pl.BlockSpec((None, rb, width), lambda i, j: (i, 0, 0))
    st_spec = pl.BlockSpec((per, HEADS, HEAD_DIM, D_STATE), lambda i, j: (i * nc + j, 0, 0, 0))
    return pl.pallas_call(
        functools.partial(_ssd_state_kernel, nc=nc, sb=sb, per=per),
        grid=(bsz, nc),
        in_specs=[
            pl.BlockSpec((q, D_SSM), lambda b, c: (row(b, c), OFF_Z // D_SSM)),
            pl.BlockSpec((q, D_SSM), lambda b, c: (row(b, c), OFF_XS // D_SSM)),
            pl.BlockSpec((q, D_BC), lambda b, c: (row(b, c), OFF_B // D_BC)),
            pl.BlockSpec((q, D_BC), lambda b, c: (row(b, c), OFF_C // D_BC)),
            pl.BlockSpec((q, LANES), lambda b, c: (row(b, c), 0)),
            pl.BlockSpec((CONV_WIDTH, D_CONV), const),
            pl.BlockSpec((1, D_CONV), const),
            pl.BlockSpec((1, LANES), const),
            pl.BlockSpec((1, LANES), const),
            pl.BlockSpec((1, D_SSM), const),
            pl.BlockSpec((1, D_SSM), const),
            pl.BlockSpec(memory_space=pltpu.SMEM),
            st_spec, blk2(D_SSM), blk2(D_BC), blk2(D_BC),
            wslab,
        ],
        out_specs=[
            pl.BlockSpec((q, D_SSM), lambda b, c: (row(b, c), 0)),
            pl.BlockSpec((None, CONV_WIDTH - 1, D_CONV), lambda b, c: (b, 0, 0)),
            pl.BlockSpec((None, HEADS, HEAD_DIM, D_STATE), lambda b, c: (b, 0, 0, 0)),
            st_spec, blk2(D_SSM),
            wslab,
        ],
        out_shape=[
            jax.ShapeDtypeStruct((bsz * seq, D_SSM), BF16),
            jax.ShapeDtypeStruct((bsz, CONV_WIDTH - 1, D_CONV), F32),
            jax.ShapeDtypeStruct((bsz, HEADS, HEAD_DIM, D_STATE), F32),
            jax.ShapeDtypeStruct(state.shape, F32),
            jax.ShapeDtypeStruct((db // sb, rb, D_SSM), F32),
            jax.ShapeDtypeStruct(w_out2d.shape, BF16),
        ],
        scratch_shapes=[
            pltpu.VMEM((D_CONV // LANES, q + SUBLANES, LANES), F32),
            pltpu.VMEM((D_STATE, D_SSM), F32),
            pltpu.VMEM((D_SSM // LANES, q + SUBLANES, LANES), F32),
            pltpu.VMEM((D_BC // LANES, q + SUBLANES, LANES), F32),
            pltpu.VMEM((D_BC // LANES, q + SUBLANES, LANES), F32),
            pltpu.VMEM((q, D_SSM), F32),
            pltpu.VMEM((D_SSM, rb), BF16),
        ],
        compiler_params=pltpu.CompilerParams(
            dimension_semantics=("arbitrary", "arbitrary"),
            vmem_limit_bytes=VMEM_LIMIT),
        name="ssd_prompt_sample_state",
    )(proj, proj, proj, proj, dt_raw, conv_w, conv_b, dt_bias_p, a_log_p, dskip_row, norm_g,
      cdec_flat, state, xw3, bs3, cs3, w_out2d)


def _outproj_sample_kernel(a_ref, yo_ref, ecum_ref, ydx_ref, z_ref, ng_ref, w_ref, x_ref,
                           g_ref, y_ref, b_ref):
    gw = D_SSM // GROUPS
    for g in range(GROUPS):
        cs = slice(g * gw, (g + 1) * gw)
        y = ydx_ref[:, cs] + ecum_ref[:, cs] * yo_ref[:, cs]
        yz = y * _silu(z_ref[:, cs].astype(F32))
        ms = jnp.sum(yz * yz, axis=-1, keepdims=True) * (1.0 / gw)
        b_ref[:, cs] = (yz * lax.rsqrt(ms + EPS) * ng_ref[:, cs]).astype(BF16)
    _outproj_kernel(a_ref, b_ref, w_ref, x_ref, g_ref, y_ref)


def _outproj_sample(out_a, yo, ecum, ydx, proj, norm_g, w_bf, x2d, final_g):
    m = x2d.shape[0]
    tm = min(256, m)
    rowblk = lambda width: pl.BlockSpec((tm, width), lambda i: (i, 0))
    const = lambda shape, **kw: pl.BlockSpec(shape, lambda i: (0, 0), **kw)
    return pl.pallas_call(
        _outproj_sample_kernel,
        grid=(m // tm,),
        in_specs=[
            rowblk(D_POOL), rowblk(D_SSM), rowblk(D_SSM), rowblk(D_SSM),
            pl.BlockSpec((tm, D_SSM), lambda i: (i, OFF_Z // D_SSM)),
            const((1, D_SSM)),
            const((D_POOL + D_SSM, D_MODEL), pipeline_mode=pl.Buffered(1)),
            rowblk(D_MODEL), const((1, D_MODEL)),
        ],
        out_specs=rowblk(D_MODEL),
        out_shape=jax.ShapeDtypeStruct((m, D_MODEL), F32),
        scratch_shapes=[pltpu.VMEM((tm, D_SSM), BF16)],
        compiler_params=pltpu.CompilerParams(
            dimension_semantics=("arbitrary",),
            vmem_limit_bytes=VMEM_LIMIT),
        name="outproj_sample",
    )(out_a, yo, ecum, ydx, proj, norm_g, w_bf, x2d, final_g)


def kernel(x_prompt, x_sample, state_pool, state_conv, state_ssm, norm_g, w_in, conv_w, conv_b,
           dt_bias, a_log, d_skip, ssm_norm_g, pool_w, pool_scale, w_out, final_g):
    bsz, seq, _ = x_prompt.shape
    db, nl, _ = x_sample.shape
    assert nl == DEC_SEQ and seq % CHUNK == 0 and w_in.shape[0] == 1

    w_t = jnp.transpose(w_in[0])
    pool_w_b = pool_w[0]
    g_in = norm_g[0][None, :]
    g_fin = final_g[None, :]
    ps = pool_scale[0][None, :]
    cw = conv_w[0]
    cbias = conv_b[0][None, :]
    pad_h = lambda v: jnp.pad(v, (0, LANES - HEADS))[None, :]
    dtb = pad_h(dt_bias[0])
    alog = pad_h(a_log[0])
    dskip_row = jnp.repeat(d_skip[0], HEAD_DIM)[None, :]
    ng = ssm_norm_g[0][None, :]

    sb = SAMPLE_BLOCK
    nblk = db // sb
    rb = nl * sb
    xs2 = x_sample.reshape(nblk, sb, nl, D_MODEL).transpose(0, 2, 1, 3).reshape(db * nl, D_MODEL)
    proj_s, dt_s, w_bf, wdt_bf = _inproj(xs2, g_in, w_t, w_t, 1024)
    pj3 = proj_s.reshape(nblk, rb, D_MAIN)
    head_of_ch = jnp.arange(D_SSM, dtype=jnp.int32) // HEAD_DIM
    ehot = (jnp.arange(LANES, dtype=jnp.int32)[:, None] == head_of_ch[None, :]).astype(BF16)
    grp_of_row = jnp.arange(D_BC, dtype=jnp.int32) // D_STATE
    head_id = jnp.arange(LANES, dtype=jnp.int32)
    ghot = ((head_id[None, :] // HEADS_PER_GROUP == grp_of_row[:, None])
            & (head_id[None, :] < HEADS)).astype(BF16)
    (oa_s, npool_s, nconv_s, cs_s, bs_s, xw_s, ydx_s, ecum_s, cdec_s) = _sample1(
        pj3, dt_s.reshape(nblk, rb, LANES), jnp.transpose(state_pool[0], (1, 0, 2)),
        jnp.transpose(state_conv[0], (1, 0, 2)), pool_w_b, ps, cw, cbias, dtb, alog,
        dskip_row, ehot, ghot, sb)

    xp2 = x_prompt.reshape(bsz * seq, D_MODEL)
    proj_p, dt_p = _inproj(xp2, g_in, w_bf, wdt_bf, PROMPT_TN)
    ob_p, nconv_p, nssm_p, nssm_s, yo_s, w_out_bf = _ssd_prompt_sample_state(
        proj_p, dt_p, cw, cbias, dtb, alog, dskip_row, ng, bsz, seq,
        cdec_s[:, :HEADS].reshape(db * HEADS), state_ssm[0], xw_s, bs_s, cs_s, sb, w_out[0])
    y_p, npool_p = _pool_outproj_prompt(proj_p, pool_w_b, ps, ob_p, w_out_bf, xp2, g_fin, bsz, seq)
    y_p = y_p.reshape(bsz, seq, D_MODEL)

    flat = lambda t: t.reshape(db * nl, t.shape[-1])
    y_s = _outproj_sample(flat(oa_s), flat(yo_s), flat(ecum_s), flat(ydx_s), proj_s, ng, w_out_bf,
                          xs2, g_fin)
    y_s = y_s.reshape(nblk, nl, sb, D_MODEL).transpose(0, 2, 1, 3).reshape(db, nl, D_MODEL)

    return (y_p, y_s,
            npool_p[None], nconv_p[None], nssm_p[None],
            jnp.transpose(npool_s, (1, 0, 2))[None],
            jnp.transpose(nconv_s, (1, 0, 2))[None],
            nssm_s[None])
```

```python
import functools

import jax
import jax.numpy as jnp
from jax import lax
from jax.experimental import pallas as pl
from jax.experimental.pallas import tpu as pltpu

F32 = jnp.float32
BF16 = jnp.bfloat16

D_MODEL = 2048
D_POOL = 1024
POOL_WINDOWS = (2, 4, 8, 16)
POOL_GROUP = 256
POOL_HIST = 15
D_SSM = 3072
HEAD_DIM = 64
HEADS = 48
GROUPS = 8
HEADS_PER_GROUP = 6
D_STATE = 128
D_BC = GROUPS * D_STATE
CONV_WIDTH = 4
D_CONV = D_SSM + 2 * D_BC
D_MAIN = 2 * D_POOL + D_SSM + D_CONV
PAST_LEN = 16384
DEC_SEQ = 4
EPS = 1e-5

LANES = 128
SUBLANES = 8
VMEM_LIMIT = 56 * 1024 * 1024

OFF_Z = 0
OFF_XS = D_SSM
OFF_U = 2 * D_SSM
OFF_GATE = OFF_U + D_POOL
OFF_B = OFF_GATE + D_POOL
OFF_C = OFF_B + D_BC

PROMPT_TN = 2048
CHUNK = 128
SAMPLE_BLOCK = 32
NEG_BIG = -1e30
LOG2E = 1.4426950408889634


def _silu(v):
    h = 0.5 * v
    return h + h * jnp.tanh(h)


def _softplus(v):
    y = jnp.exp(-jnp.abs(v))
    u = 1.0 + y
    d = u - 1.0
    l1p = jnp.where(d == 0.0, y, jnp.log(u) * (y / jnp.where(d == 0.0, 1.0, d)))
    return jnp.maximum(v, 0.0) + l1p


def _split2(v):
    hi = v.astype(BF16)
    lo = (v - hi.astype(F32)).astype(BF16)
    return hi, lo


def _dot(a, b):
    return jnp.dot(a, b, preferred_element_type=F32)


def _dot_nt(a, b):
    return lax.dot_general(a, b, (((1,), (1,)), ((), ())), preferred_element_type=F32)


def _dot2(v, onehot):
    hi, lo = _split2(v)
    return _dot(hi, onehot) + _dot(lo, onehot)


def _inproj_kernel(x_ref, g_ref, w_ref, wdt_ref, *rest):
    from_f32 = w_ref.dtype == F32
    if from_f32:
        perm_ref, o_ref, dt_ref, wb_out_ref, wdt_out_ref, h_ref = rest
    else:
        o_ref, dt_ref, h_ref = rest

    @pl.when(pl.program_id(1) == 0)
    def _():
        x = x_ref[...]
        ms = jnp.mean(x * x, axis=-1, keepdims=True)
        h = (x * lax.rsqrt(ms + EPS) * g_ref[...]).astype(BF16)
        if from_f32:
            pr = perm_ref.shape[0]
            for k in range(h.shape[0] // pr):
                h_ref[k * pr:(k + 1) * pr, :] = _dot(perm_ref[...], h[k * pr:(k + 1) * pr]).astype(BF16)
            h = h_ref[...]
        else:
            h_ref[...] = h
        if from_f32:
            wrow = lax.broadcasted_iota(jnp.int32, wdt_ref.shape, 0)
            wdt = jnp.where(wrow < HEADS, wdt_ref[...], 0.0).astype(BF16)
            wdt_out_ref[...] = wdt
        else:
            wdt = wdt_ref[...]
        dt_ref[...] = _dot_nt(h, wdt)

    if from_f32:
        wb = w_ref[...].astype(BF16)
        wb_out_ref[...] = wb
    else:
        wb = w_ref[...]
    o_ref[...] = _dot_nt(h_ref[...], wb).astype(BF16)


def _src_block(j):
    nz = (D_SSM + D_SSM) // 1024
    npool = 2 * D_POOL // 1024
    return jnp.where(j < nz, j + npool, jnp.where(j < nz + npool, j - nz, j))


def _inproj(x2d, norm_g, w, w_dt, tn, row_perm=None):
    m = x2d.shape[0]
    tm = min(1024, m)
    from_f32 = w.dtype == F32
    assert from_f32 == (row_perm is not None)
    assert tn == 1024 or not from_f32
    w_map = (lambda i, j: (_src_block(j), 0)) if from_f32 else (lambda i, j: (j, 0))
    wdt_map = (lambda i, j: (D_MAIN // LANES, 0)) if from_f32 else (lambda i, j: (0, 0))
    out_specs = [
        pl.BlockSpec((tm, tn), lambda i, j: (i, j)),
        pl.BlockSpec((tm, LANES), lambda i, j: (i, 0)),
    ]
    out_shape = [
        jax.ShapeDtypeStruct((m, D_MAIN), BF16),
        jax.ShapeDtypeStruct((m, LANES), F32),
    ]
    if from_f32:
        assert m == tm
        out_specs += [pl.BlockSpec((tn, D_MODEL), lambda i, j: (j, 0)),
                      pl.BlockSpec((LANES, D_MODEL), lambda i, j: (0, 0))]
        out_shape += [jax.ShapeDtypeStruct((D_MAIN, D_MODEL), BF16),
                      jax.ShapeDtypeStruct((LANES, D_MODEL), BF16)]
    in_specs = [
        pl.BlockSpec((tm, D_MODEL), lambda i, j: (i, 0)),
        pl.BlockSpec((1, D_MODEL), lambda i, j: (0, 0)),
        pl.BlockSpec((tn, D_MODEL), w_map),
        pl.BlockSpec((LANES, D_MODEL), wdt_map),
    ]
    operands = [x2d, norm_g, w, w_dt]
    if from_f32:
        in_specs.append(pl.BlockSpec(row_perm.shape, lambda i, j: (0, 0)))
        operands.append(row_perm)
    return pl.pallas_call(
        _inproj_kernel,
        grid=(m // tm, D_MAIN // tn),
        in_specs=in_specs,
        out_specs=out_specs,
        out_shape=out_shape,
        scratch_shapes=[pltpu.VMEM((tm, D_MODEL), BF16)],
        compiler_params=pltpu.CompilerParams(
            dimension_semantics=("arbitrary", "arbitrary"),
            vmem_limit_bytes=VMEM_LIMIT),
        name="inproj",
    )(*operands)


def _outproj_kernel(a_ref, b_ref, w_ref, x_ref, g_ref, y_ref):
    acc = (_dot(a_ref[...], w_ref[0:D_POOL, :])
           + _dot(b_ref[...], w_ref[D_POOL:D_POOL + D_SSM, :]))
    r = x_ref[...] + acc
    ms = jnp.mean(r * r, axis=-1, keepdims=True)
    y_ref[...] = r * lax.rsqrt(ms + EPS) * g_ref[...]


def _pool_kernel(u_ref, gate_ref, pw_ref, ps_ref, oa_ref, np_ref, ubuf, wbuf, *, tl, nt):
    t = pl.program_id(1)
    hist = POOL_HIST + 1
    nrows = hist + tl
    nv = nrows // SUBLANES
    bpg = POOL_GROUP // LANES

    @pl.when(t == 0)
    def _():
        ubuf[:, 0:hist, :] = jnp.zeros((ubuf.shape[0], hist, LANES), F32)

    pos = t * tl + lax.broadcasted_iota(jnp.int32, (tl, 1), 0)
    for gi, w in enumerate(POOL_WINDOWS):
        win = []
        for lb in range(gi * bpg, (gi + 1) * bpg):
            ubuf[lb, hist:nrows, :] = u_ref[:, lb * LANES:(lb + 1) * LANES].astype(F32)
            acc = [ubuf[lb, pl.ds(a, SUBLANES, stride=nv), :] for a in range(nv)]
            span = 1
            while span < w:
                wrap = [pltpu.roll(acc[nv - span + a], 1, 0) for a in range(span)]
                acc = [acc[a] + (acc[a - span] if a >= span else wrap[a]) for a in range(nv)]
                span *= 2
            for a in range(nv):
                wbuf[lb, pl.ds(a, SUBLANES, stride=nv), :] = acc[a]
            win.append(wbuf[lb, hist:nrows, :])
        cs = slice(gi * POOL_GROUP, (gi + 1) * POOL_GROUP)
        cnt = jnp.minimum(w, pos + 1).astype(F32)
        ug = jnp.concatenate([ubuf[lb, hist:nrows, :] for lb in range(gi * bpg, (gi + 1) * bpg)],
                             axis=1)
        pooled = jnp.concatenate(win, axis=1) / cnt - ug
        mixed = _dot(pooled.astype(BF16), pw_ref[gi].astype(BF16))
        gt = gate_ref[:, cs].astype(F32)
        oa_ref[:, cs] = (mixed * ps_ref[:, cs] * _silu(gt)).astype(BF16)

    @pl.when(t == nt - 1)
    def _():
        for lb in range(D_POOL // LANES):
            np_ref[:, lb * LANES:(lb + 1) * LANES] = ubuf[lb, tl + 1:nrows, :]

    ubuf[:, 0:hist, :] = ubuf[:, tl:nrows, :]


def _pool_outproj_kernel(u_ref, gate_ref, pw_ref, ps_ref, b_ref, w_ref, x_ref, g_ref,
                         y_ref, np_ref, ubuf, wbuf, a_ref, *, tl, nt):
    _pool_kernel(u_ref, gate_ref, pw_ref, ps_ref, a_ref, np_ref, ubuf, wbuf, tl=tl, nt=nt)
    _outproj_kernel(a_ref, b_ref, w_ref, x_ref, g_ref, y_ref)


def _pool_outproj_prompt(proj, pool_w, pool_scale, out_b, w_bf, x2d, final_g, bsz, seq):
    tl = min(512, seq)
    nt = seq // tl
    row = lambda b, t: b * nt + t
    const = lambda b, t: (0, 0)
    return pl.pallas_call(
        functools.partial(_pool_outproj_kernel, tl=tl, nt=nt),
        grid=(bsz, nt),
        in_specs=[
            pl.BlockSpec((tl, D_POOL), lambda b, t: (row(b, t), OFF_U // D_POOL)),
            pl.BlockSpec((tl, D_POOL), lambda b, t: (row(b, t), OFF_GATE // D_POOL)),
            pl.BlockSpec((len(POOL_WINDOWS), POOL_GROUP, POOL_GROUP), lambda b, t: (0, 0, 0)),
            pl.BlockSpec((1, D_POOL), const),
            pl.BlockSpec((tl, D_SSM), lambda b, t: (row(b, t), 0)),
            pl.BlockSpec((D_POOL + D_SSM, D_MODEL), const, pipeline_mode=pl.Buffered(1)),
            pl.BlockSpec((tl, D_MODEL), lambda b, t: (row(b, t), 0)),
            pl.BlockSpec((1, D_MODEL), const),
        ],
        out_specs=[
            pl.BlockSpec((tl, D_MODEL), lambda b, t: (row(b, t), 0)),
            pl.BlockSpec((None, POOL_HIST, D_POOL), lambda b, t: (b, 0, 0)),
        ],
        out_shape=[
            jax.ShapeDtypeStruct((bsz * seq, D_MODEL), F32),
            jax.ShapeDtypeStruct((bsz, POOL_HIST, D_POOL), F32),
        ],
        scratch_shapes=[pltpu.VMEM((D_POOL // LANES, tl + POOL_HIST + 1, LANES), F32),
                        pltpu.VMEM((D_POOL // LANES, tl + POOL_HIST + 1, LANES), F32),
                        pltpu.VMEM((tl, D_POOL), BF16)],
        compiler_params=pltpu.CompilerParams(
            dimension_semantics=("arbitrary", "arbitrary"),
            vmem_limit_bytes=VMEM_LIMIT),
        name="pool_outproj",
    )(proj, proj, pool_w, pool_scale, out_b, w_bf, x2d, final_g)


def _ssd_body(z_ref, xs_ref, b_ref, c_ref, dtr_ref, cw_ref, cbias_ref, dtb_ref, alog_ref,
              dskip_ref, ng_ref,
              ob_ref, nconv_ref, nssm_ref,
              cbuf, st_ref, cvx_ref, cvb_ref, cvc_ref, y_ref, *, nc, after_group=None):
    q = CHUNK
    c_idx = pl.program_id(1)
    halo = SUBLANES

    nbx, nbb = D_SSM // LANES, D_BC // LANES

    @pl.when(c_idx == 0)
    def _():
        cbuf[:, 0:halo, :] = jnp.zeros((cbuf.shape[0], halo, LANES), F32)
        st_ref[...] = jnp.zeros(st_ref.shape, F32)

    nv = (halo + q) // SUBLANES
    for blk in range(D_CONV // LANES):
        ls = slice(blk * LANES, (blk + 1) * LANES)
        if blk < nbx:
            src_ref, off, dst = xs_ref, blk * LANES, cvx_ref.at[blk]
        elif blk < nbx + nbb:
            src_ref, off, dst = b_ref, (blk - nbx) * LANES, cvb_ref.at[blk - nbx]
        else:
            src_ref, off, dst = c_ref, (blk - nbx - nbb) * LANES, cvc_ref.at[blk - nbx - nbb]
        cbuf[blk, halo:halo + q, :] = src_ref[:, off:off + LANES].astype(F32)
        xv = [cbuf[blk, pl.ds(a, SUBLANES, stride=nv), :] for a in range(nv)]
        wrap = [pltpu.roll(xv[nv - k], 1, 0) for k in range(1, CONV_WIDTH)]
        taps = [0.5 * cw_ref[k:k + 1, ls] for k in range(CONV_WIDTH)]
        bias = 0.5 * cbias_ref[:, ls]
        for a in range(nv):
            h = bias + taps[CONV_WIDTH - 1] * xv[a]
            for k in range(1, CONV_WIDTH):
                src = xv[a - k] if a >= k else wrap[k - a - 1]
                h = h + taps[CONV_WIDTH - 1 - k] * src
            dst[pl.ds(a, SUBLANES, stride=nv), :] = h + h * jnp.tanh(h)
    rows = slice(halo, halo + q)

    dt = _softplus(dtr_ref[...] + dtb_ref[...])
    a_neg = -jnp.exp(alog_ref[...])
    da = dt * (a_neg * LOG2E)
    row = lax.broadcasted_iota(jnp.int32, (q, LANES), 0)
    a2 = da
    shift = 1
    while shift < q:
        a2 = a2 + jnp.where(row >= shift, pltpu.roll(a2, shift, 0), 0.0)
        shift *= 2
    a2_t = jnp.transpose(a2)
    ldt_t = jnp.log2(jnp.transpose(dt))
    a2_end_t = a2_t[:, q - 1:q]
    w_t = jnp.exp2(a2_end_t - a2_t + ldt_t)
    cdec_t = jnp.exp2(a2_end_t)
    srow_t = a2_t - ldt_t
    ea = jnp.exp2(a2)

    li = lax.broadcasted_iota(jnp.int32, (q, q), 0)
    si = lax.broadcasted_iota(jnp.int32, (q, q), 1)
    tri = li >= si
    lane = lax.broadcasted_iota(jnp.int32, (q, LANES), 1)
    lo_half = lane < HEAD_DIM

    for g in range(GROUPS):
        gs = slice(g * D_STATE, (g + 1) * D_STATE)
        c_gb = cvc_ref[g, rows, :].astype(BF16)
        b_g = cvb_ref[g, rows, :]
        cb = _dot_nt(c_gb, b_g.astype(BF16))
        b_t = jnp.transpose(b_g)
        ppg = HEADS_PER_GROUP // 2
        gl = slice(g * ppg * LANES, (g + 1) * ppg * LANES)
        y_off = _dot(c_gb, st_ref[:, gl].astype(BF16))
        for j in range(ppg):
            blk = g * ppg + j
            ls = slice(blk * LANES, (blk + 1) * LANES)
            sc, bw, ecol, dec = [], [], [], []
            for h in (2 * blk, 2 * blk + 1):
                a_col = jnp.broadcast_to(a2[:, h:h + 1], (q, q))
                decay_dt = jnp.exp2(jnp.where(tri, a_col - srow_t[h:h + 1, :], NEG_BIG))
                sc.append((cb * decay_dt).astype(BF16))
                bw.append((b_t * w_t[h:h + 1, :]).astype(BF16))
                ecol.append(jnp.broadcast_to(ea[:, h:h + 1], (q, LANES)))
                dec.append(jnp.broadcast_to(cdec_t[h:h + 1, :], (q, LANES)))
            x_p = cvx_ref[blk, rows, :].astype(BF16)
            zero_b = jnp.zeros_like(x_p)
            x_bd = jnp.concatenate([jnp.where(lo_half, x_p, zero_b),
                                    jnp.where(lo_half, zero_b, x_p)], axis=0)
            lhs1 = jnp.concatenate([jnp.concatenate(sc, axis=1),
                                    jnp.concatenate(bw, axis=1)], axis=0)
            r1 = _dot(lhs1, x_bd)
            y_ref[:, ls] = (r1[0:q] + y_off[:, j * LANES:(j + 1) * LANES]
                            * jnp.where(lo_half, ecol[0], ecol[1]))
            st_ref[:, ls] = (st_ref[:, ls] * jnp.where(lo_half, dec[0], dec[1])
                             + r1[q:2 * q])
        if after_group is not None:
            after_group(g)

    gw = D_SSM // GROUPS
    bpg = gw // LANES
    for g in range(GROUPS):
        yz, ssq = [], None
        for blk in range(g * bpg, (g + 1) * bpg):
            ls = slice(blk * LANES, (blk + 1) * LANES)
            y = y_ref[:, ls] + cvx_ref[blk, rows, :] * dskip_ref[:, ls]
            v = y * _silu(z_ref[:, ls].astype(F32))
            yz.append(v)
            ssq = v * v if ssq is None else ssq + v * v
        scale = lax.rsqrt(jnp.sum(ssq, axis=-1, keepdims=True) * (1.0 / gw) + EPS)
        for i, blk in enumerate(range(g * bpg, (g + 1) * bpg)):
            ls = slice(blk * LANES, (blk + 1) * LANES)
            ob_ref[:, ls] = (yz[i] * scale * ng_ref[:, ls]).astype(BF16)

    cbuf[:, 0:halo, :] = cbuf[:, q:q + halo, :]

    @pl.when(c_idx == nc - 1)
    def _():
        for blk in range(D_CONV // LANES):
            nconv_ref[:, blk * LANES:(blk + 1) * LANES] = cbuf[blk, q + halo - 3:q + halo, :]
        for blk in range(D_SSM // LANES):
            t = jnp.transpose(st_ref[:, blk * LANES:(blk + 1) * LANES])
            nssm_ref[2 * blk:2 * blk + 2] = t.reshape(2, HEAD_DIM, D_STATE)


def _sample1_kernel(pj_ref, dtr_ref, sp_ref, scv_ref, pw_ref, ps_ref, cw_ref, cbias_ref, dtb_ref,
                    alog_ref, dskip_ref, ehot_ref, ghot_ref,
                    oa_ref, npool_ref, nconv_ref, cs_ref, bs_ref, xw_ref, ydx_ref, ecum_ref,
                    cdec_ref, *, sb):
    nl = DEC_SEQ

    def rows(l):
        return slice(l * sb, (l + 1) * sb)

    def tok(l, off, width):
        return pj_ref[rows(l), off:off + width].astype(F32)

    for gi, w in enumerate(POOL_WINDOWS):
        c0 = gi * POOL_GROUP
        ext = [sp_ref[j, :, c0:c0 + POOL_GROUP] for j in range(POOL_HIST)]
        ext += [tok(l, OFF_U + c0, POOL_GROUP) for l in range(nl)]
        for l in range(nl):
            acc = ext[POOL_HIST + l]
            for k in range(1, w):
                acc = acc + ext[POOL_HIST + l - k]
            cnt = float(min(w, PAST_LEN + l + 1))
            pooled = acc / cnt - ext[POOL_HIST + l]
            mixed = _dot(pooled.astype(BF16), pw_ref[gi].astype(BF16))
            gt = tok(l, OFF_GATE + c0, POOL_GROUP)
            oa_ref[rows(l), c0:c0 + POOL_GROUP] = (
                mixed * ps_ref[:, c0:c0 + POOL_GROUP] * _silu(gt)).astype(BF16)
    for j in range(POOL_HIST):
        src = j + nl
        if src < POOL_HIST:
            npool_ref[j] = sp_ref[src]
        else:
            npool_ref[j] = tok(src - POOL_HIST, OFF_U, D_POOL)

    hist = CONV_WIDTH - 1
    conv_out = {}
    for name, poff, coff, width in (("x", OFF_XS, 0, D_SSM), ("b", OFF_B, D_SSM, D_BC),
                                    ("c", OFF_C, D_SSM + D_BC, D_BC)):
        ext = [scv_ref[j, :, coff:coff + width] for j in range(hist)]
        ext += [tok(l, poff, width) for l in range(nl)]
        outs = []
        for l in range(nl):
            acc = cbias_ref[:, coff:coff + width]
            for k in range(CONV_WIDTH):
                acc = acc + cw_ref[k:k + 1, coff:coff + width] * ext[l + k]
            outs.append(_silu(acc))
        conv_out[name] = outs
        for j in range(hist):
            nconv_ref[j, :, coff:coff + width] = ext[j + nl]
    xs, bs, cs = conv_out["x"], conv_out["b"], conv_out["c"]
    for l in range(nl):
        cs_ref[rows(l), :] = cs[l]
        bs_ref[rows(l), :] = bs[l].astype(BF16)

    a_neg = -jnp.exp(alog_ref[...])
    dt, a_cum = [], []
    run = None
    for l in range(nl):
        d = _softplus(dtr_ref[rows(l), :] + dtb_ref[...])
        dt.append(d)
        run = d * a_neg if run is None else run + d * a_neg
        a_cum.append(run)
    cdec_ref[...] = jnp.exp(a_cum[nl - 1])

    def onehot_rows(mats, onehot):
        parts = [_split2(m) for m in mats]
        stack = jnp.concatenate([p[0] for p in parts] + [p[1] for p in parts], axis=0)
        res = _dot(stack, onehot)
        n = len(mats)
        return [res[k * sb:(k + 1) * sb] + res[(n + k) * sb:(n + k + 1) * sb] for k in range(n)]

    pairs = [(l, s) for l in range(nl) for s in range(l + 1)]
    cbh = onehot_rows([cs[l] * bs[s] for l, s in pairs], ghot_ref[...])
    gls = [c * jnp.exp(a_cum[l] - a_cum[s]) * dt[s] for c, (l, s) in zip(cbh, pairs)]
    w_end = [jnp.exp(a_cum[nl - 1] - a_cum[l]) * dt[l] for l in range(nl)]
    e_cum = [jnp.exp(a_cum[l]) for l in range(nl)]
    chunk = 4 * LANES
    for c0 in range(0, D_SSM, chunk):
        cl = slice(c0, c0 + chunk)
        ex = onehot_rows(gls + w_end + e_cum, ehot_ref[:, cl])
        g_ex, w_ex, e_ex = ex[:len(pairs)], ex[len(pairs):len(pairs) + nl], ex[len(pairs) + nl:]
        xc = [x[:, cl] for x in xs]
        for l in range(nl):
            ydiag = dskip_ref[:, cl] * xc[l]
            for k, (pl_, ps_) in enumerate(pairs):
                if pl_ == l:
                    ydiag = ydiag + g_ex[k] * xc[ps_]
            ydx_ref[rows(l), cl] = ydiag
            xw_ref[rows(l), cl] = w_ex[l] * xc[l]
            ecum_ref[rows(l), cl] = e_ex[l]


def _sample1(pj3, dtr3, sp3, scv3, pool_w, pool_scale, conv_w, conv_b, dt_bias_p, a_log_p,
             dskip_row, ehot, ghot, sb):
    nblk = pj3.shape[0]
    db = nblk * sb
    rb = DEC_SEQ * sb
    blk2 = lambda width: pl.BlockSpec((None, rb, width), lambda i: (i, 0, 0))
    hist3 = lambda n, width: pl.BlockSpec((n, sb, width), lambda i: (0, i, 0))
    const2 = lambda shape: pl.BlockSpec(shape, lambda i: (0, 0))
    blk_shape = lambda width, dt: jax.ShapeDtypeStruct((nblk, rb, width), dt)
    return pl.pallas_call(
        functools.partial(_sample1_kernel, sb=sb),
        grid=(nblk,),
        in_specs=[
            blk2(D_MAIN), blk2(LANES), hist3(POOL_HIST, D_POOL), hist3(CONV_WIDTH - 1, D_CONV),
            pl.BlockSpec((len(POOL_WINDOWS), POOL_GROUP, POOL_GROUP), lambda i: (0, 0, 0)),
            const2((1, D_POOL)), const2((CONV_WIDTH, D_CONV)), const2((1, D_CONV)),
            const2((1, LANES)), const2((1, LANES)), const2((1, D_SSM)),
            const2((LANES, D_SSM)), const2((D_BC, LANES)),
        ],
        out_specs=[
            blk2(D_POOL), hist3(POOL_HIST, D_POOL), hist3(CONV_WIDTH - 1, D_CONV),
            blk2(D_BC), blk2(D_BC), blk2(D_SSM), blk2(D_SSM), blk2(D_SSM),
            pl.BlockSpec((sb, LANES), lambda i: (i, 0)),
        ],
        out_shape=[
            blk_shape(D_POOL, BF16),
            jax.ShapeDtypeStruct((POOL_HIST, db, D_POOL), F32),
            jax.ShapeDtypeStruct((CONV_WIDTH - 1, db, D_CONV), F32),
            blk_shape(D_BC, F32), blk_shape(D_BC, BF16),
            blk_shape(D_SSM, F32), blk_shape(D_SSM, F32), blk_shape(D_SSM, F32),
            jax.ShapeDtypeStruct((db, LANES), F32),
        ],
        compiler_params=pltpu.CompilerParams(
            dimension_semantics=("arbitrary",),
            vmem_limit_bytes=VMEM_LIMIT),
        name="sample_elementwise",
    )(pj3, dtr3, sp3, scv3, pool_w, pool_scale, conv_w, conv_b, dt_bias_p, a_log_p, dskip_row,
      ehot, ghot)


def _sample2_setup(xw_ref, cs_ref, xwt_ref, *, sb, per):
    j = pl.program_id(1)

    @pl.when(j == 0)
    def _():
        for blk in range(D_SSM // LANES):
            xwt_ref[blk * LANES:(blk + 1) * LANES, :] = jnp.transpose(
                xw_ref[:, blk * LANES:(blk + 1) * LANES]).astype(BF16)

    return [jnp.concatenate([cs_ref[pl.ds(l * sb + per * j + 2 * pr + bi, 1), :]
                             for bi in range(2) for l in range(DEC_SEQ)], axis=0).astype(BF16)
            for pr in range(per // 2)]


def _sample2_group(g, refs, c_rows, *, sb, per):
    cdec_ref, st_ref, _, bs_ref, _, nst_ref, yo_ref, xwt_ref = refs
    i = pl.program_id(0)
    j = pl.program_id(1)
    nl = DEC_SEQ
    rows = sb * nl
    seq_of_row = lax.broadcasted_iota(jnp.int32, (rows, 1), 0) & (sb - 1)
    gw = HEADS_PER_GROUP * HEAD_DIM
    gs = slice(g * D_STATE, (g + 1) * D_STATE)
    for pr in range(per // 2):
        q0 = per * j + 2 * pr
        c8 = c_rows[pr][:, gs]
        b_blk = bs_ref[:, gs]
        zero_b = jnp.zeros_like(b_blk)
        w2 = jnp.concatenate([jnp.where(seq_of_row == q0, b_blk, zero_b),
                              jnp.where(seq_of_row == q0 + 1, b_blk, zero_b)], axis=1)
        u2 = _dot(xwt_ref[g * gw:(g + 1) * gw, :], w2)
        for bi in range(2):
            sq = 2 * pr + bi
            s0 = st_ref[sq, g * HEADS_PER_GROUP:(g + 1) * HEADS_PER_GROUP].reshape(gw, D_STATE)
            yo = _dot_nt(c8, s0.astype(BF16))
            base = (i * sb + q0 + bi) * HEADS + g * HEADS_PER_GROUP
            for r in range(HEADS_PER_GROUP):
                dec = cdec_ref[base + r]
                rs = slice(r * HEAD_DIM, (r + 1) * HEAD_DIM)
                nst_ref[sq, g * HEADS_PER_GROUP + r] = (
                    s0[rs] * dec + u2[rs, bi * D_STATE:(bi + 1) * D_STATE])
            for l in range(nl):
                yo_ref[pl.ds(l * sb + q0 + bi, 1), g * gw:(g + 1) * gw] = (
                    yo[bi * nl + l:bi * nl + l + 1])


N_SSD_IN, N_SSD_OUT, N_SSD_SCRATCH = 11, 3, 6
N_ST_IN, N_ST_OUT = 5, 2


def _ssd_state_kernel(*refs, nc, sb, per):
    n_in = N_SSD_IN + N_ST_IN + 1
    ins, rest = refs[:n_in], refs[n_in:]
    n_out = N_SSD_OUT + N_ST_OUT + 1
    outs, scratch = rest[:n_out], rest[n_out:]
    outs[-1][...] = ins[-1][...].astype(BF16)
    st_refs = (*ins[N_SSD_IN:N_SSD_IN + N_ST_IN], *outs[N_SSD_OUT:N_SSD_OUT + N_ST_OUT],
               *scratch[N_SSD_SCRATCH:])
    c_rows = _sample2_setup(st_refs[2], st_refs[4], st_refs[7], sb=sb, per=per)
    _ssd_body(*ins[:N_SSD_IN], *outs[:N_SSD_OUT], *scratch[:N_SSD_SCRATCH], nc=nc,
              after_group=functools.partial(_sample2_group, refs=st_refs, c_rows=c_rows,
                                            sb=sb, per=per))


def _ssd_prompt_sample_state(proj, dt_raw, conv_w, conv_b, dt_bias_p, a_log_p, dskip_row, norm_g,
                             bsz, seq, cdec_flat, state, xw3, bs3, cs3, sb, w_out2d):
    q = CHUNK
    nc = seq // q
    db = state.shape[0]
    rb = DEC_SEQ * sb
    per = sb // nc
    assert rb == LANES and db // sb == bsz and per * nc == sb and per % 2 == 0
    d_mix = w_out2d.shape[0]
    slab = d_mix // (bsz * nc)
    assert slab * bsz * nc == d_mix and slab % (2 * SUBLANES) == 0
    wslab = pl.BlockSpec((slab, D_MODEL), lambda b, c: (b * nc + c, 0))
    row = lambda b, c: b * nc + c
    const = lambda b, c: (0, 0)
    blk2 = lambda width: pl.BlockSpec((None, rb, width), lambda i, j: (i, 0, 0))
    st_spec = pl.BlockSpec((per, HEADS, HEAD_DIM, D_STATE), lambda i, j: (i * nc + j, 0, 0, 0))
    return pl.pallas_call(
        functools.partial(_ssd_state_kernel, nc=nc, sb=sb, per=per),
        grid=(bsz, nc),
        in_specs=[
            pl.BlockSpec((q, D_SSM), lambda b, c: (row(b, c), OFF_Z // D_SSM)),
            pl.BlockSpec((q, D_SSM), lambda b, c: (row(b, c), OFF_XS // D_SSM)),
            pl.BlockSpec((q, D_BC), lambda b, c: (row(b, c), OFF_B // D_BC)),
            pl.BlockSpec((q, D_BC), lambda b, c: (row(b, c), OFF_C // D_BC)),
            pl.BlockSpec((q, LANES), lambda b, c: (row(b, c), 0)),
            pl.BlockSpec((CONV_WIDTH, D_CONV), const),
            pl.BlockSpec((1, D_CONV), const),
            pl.BlockSpec((1, LANES), const),
            pl.BlockSpec((1, LANES), const),
            pl.BlockSpec((1, D_SSM), const),
            pl.BlockSpec((1, D_SSM), const),
            pl.BlockSpec(memory_space=pltpu.SMEM),
            st_spec, blk2(D_SSM), blk2(D_BC), blk2(D_BC),
            wslab,
        ],
        out_specs=[
            pl.BlockSpec((q, D_SSM), lambda b, c: (row(b, c), 0)),
            pl.BlockSpec((None, CONV_WIDTH - 1, D_CONV), lambda b, c: (b, 0, 0)),
            pl.BlockSpec((None, HEADS, HEAD_DIM, D_STATE), lambda b, c: (b, 0, 0, 0)),
            st_spec, blk2(D_SSM),
            wslab,
        ],
        out_shape=[
            jax.ShapeDtypeStruct((bsz * seq, D_SSM), BF16),
            jax.ShapeDtypeStruct((bsz, CONV_WIDTH - 1, D_CONV), F32),
            jax.ShapeDtypeStruct((bsz, HEADS, HEAD_DIM, D_STATE), F32),
            jax.ShapeDtypeStruct(state.shape, F32),
            jax.ShapeDtypeStruct((db // sb, rb, D_SSM), F32),
            jax.ShapeDtypeStruct(w_out2d.shape, BF16),
        ],
        scratch_shapes=[
            pltpu.VMEM((D_CONV // LANES, q + SUBLANES, LANES), F32),
            pltpu.VMEM((D_STATE, D_SSM), F32),
            pltpu.VMEM((D_SSM // LANES, q + SUBLANES, LANES), F32),
            pltpu.VMEM((D_BC // LANES, q + SUBLANES, LANES), F32),
            pltpu.VMEM((D_BC // LANES, q + SUBLANES, LANES), F32),
            pltpu.VMEM((q, D_SSM), F32),
            pltpu.VMEM((D_SSM, rb), BF16),
        ],
        compiler_params=pltpu.CompilerParams(
            dimension_semantics=("arbitrary", "arbitrary"),
            vmem_limit_bytes=VMEM_LIMIT),
        name="ssd_prompt_sample_state",
    )(proj, proj, proj, proj, dt_raw, conv_w, conv_b, dt_bias_p, a_log_p, dskip_row, norm_g,
      cdec_flat, state, xw3, bs3, cs3, w_out2d)


def _outproj_sample_kernel(a_ref, yo_ref, ecum_ref, ydx_ref, z_ref, ng_ref, perm_ref, w_ref, x_ref,
                           g_ref, y_ref, an_ref, bn_ref):
    gw = D_SSM // GROUPS
    pr = perm_ref.shape[0]
    for k in range(a_ref.shape[0] // pr):
        rows = slice(k * pr, (k + 1) * pr)
        an_ref[rows, :] = _dot(perm_ref[...], a_ref[rows, :]).astype(BF16)
        for g in range(GROUPS):
            cs = slice(g * gw, (g + 1) * gw)
            y = ydx_ref[rows, cs] + ecum_ref[rows, cs] * yo_ref[rows, cs]
            yz = y * _silu(z_ref[rows, cs].astype(F32))
            ms = jnp.sum(yz * yz, axis=-1, keepdims=True) * (1.0 / gw)
            b_blk = (yz * lax.rsqrt(ms + EPS) * ng_ref[:, cs]).astype(BF16)
            bn_ref[rows, cs] = _dot(perm_ref[...], b_blk).astype(BF16)
    _outproj_kernel(an_ref, bn_ref, w_ref, x_ref, g_ref, y_ref)


def _outproj_sample(out_a, yo, ecum, ydx, proj, norm_g, unperm, w_bf, x2d, final_g):
    m = x2d.shape[0]
    tm = min(256, m)
    assert tm % unperm.shape[0] == 0
    rowblk = lambda width: pl.BlockSpec((tm, width), lambda i: (i, 0))
    const = lambda shape, **kw: pl.BlockSpec(shape, lambda i: (0, 0), **kw)
    return pl.pallas_call(
        _outproj_sample_kernel,
        grid=(m // tm,),
        in_specs=[
            rowblk(D_POOL), rowblk(D_SSM), rowblk(D_SSM), rowblk(D_SSM),
            pl.BlockSpec((tm, D_SSM), lambda i: (i, OFF_Z // D_SSM)),
            const((1, D_SSM)), const(unperm.shape),
            const((D_POOL + D_SSM, D_MODEL), pipeline_mode=pl.Buffered(1)),
            rowblk(D_MODEL), const((1, D_MODEL)),
        ],
        out_specs=rowblk(D_MODEL),
        out_shape=jax.ShapeDtypeStruct((m, D_MODEL), F32),
        scratch_shapes=[pltpu.VMEM((tm, D_POOL), BF16), pltpu.VMEM((tm, D_SSM), BF16)],
        compiler_params=pltpu.CompilerParams(
            dimension_semantics=("arbitrary",),
            vmem_limit_bytes=VMEM_LIMIT),
        name="outproj_sample",
    )(out_a, yo, ecum, ydx, proj, norm_g, unperm, w_bf, x2d, final_g)


def kernel(x_prompt, x_sample, state_pool, state_conv, state_ssm, norm_g, w_in, conv_w, conv_b,
           dt_bias, a_log, d_skip, ssm_norm_g, pool_w, pool_scale, w_out, final_g):
    bsz, seq, _ = x_prompt.shape
    db, nl, _ = x_sample.shape
    assert nl == DEC_SEQ and seq % CHUNK == 0 and w_in.shape[0] == 1

    w_t = jnp.transpose(w_in[0])
    pool_w_b = pool_w[0]
    g_in = norm_g[0][None, :]
    g_fin = final_g[None, :]
    ps = pool_scale[0][None, :]
    cw = conv_w[0]
    cbias = conv_b[0][None, :]
    pad_h = lambda v: jnp.pad(v, (0, LANES - HEADS))[None, :]
    dtb = pad_h(dt_bias[0])
    alog = pad_h(a_log[0])
    dskip_row = jnp.repeat(d_skip[0], HEAD_DIM)[None, :]
    ng = ssm_norm_g[0][None, :]

    sb = SAMPLE_BLOCK
    nblk = db // sb
    rb = nl * sb
    nat = jnp.arange(rb, dtype=jnp.int32)
    blk_row = (nat % nl) * sb + nat // nl
    to_blk = (blk_row[None, :] == jnp.arange(rb, dtype=jnp.int32)[:, None]).astype(BF16)
    xs2 = x_sample.reshape(db * nl, D_MODEL)
    proj_s, dt_s, w_bf, wdt_bf = _inproj(xs2, g_in, w_t, w_t, 1024, row_perm=to_blk)
    pj3 = proj_s.reshape(nblk, rb, D_MAIN)
    head_of_ch = jnp.arange(D_SSM, dtype=jnp.int32) // HEAD_DIM
    ehot = (jnp.arange(LANES, dtype=jnp.int32)[:, None] == head_of_ch[None, :]).astype(BF16)
    grp_of_row = jnp.arange(D_BC, dtype=jnp.int32) // D_STATE
    head_id = jnp.arange(LANES, dtype=jnp.int32)
    ghot = ((head_id[None, :] // HEADS_PER_GROUP == grp_of_row[:, None])
            & (head_id[None, :] < HEADS)).astype(BF16)
    (oa_s, npool_s, nconv_s, cs_s, bs_s, xw_s, ydx_s, ecum_s, cdec_s) = _sample1(
        pj3, dt_s.reshape(nblk, rb, LANES), jnp.transpose(state_pool[0], (1, 0, 2)),
        jnp.transpose(state_conv[0], (1, 0, 2)), pool_w_b, ps, cw, cbias, dtb, alog,
        dskip_row, ehot, ghot, sb)

    xp2 = x_prompt.reshape(bsz * seq, D_MODEL)
    proj_p, dt_p = _inproj(xp2, g_in, w_bf, wdt_bf, PROMPT_TN)
    ob_p, nconv_p, nssm_p, nssm_s, yo_s, w_out_bf = _ssd_prompt_sample_state(
        proj_p, dt_p, cw, cbias, dtb, alog, dskip_row, ng, bsz, seq,
        cdec_s[:, :HEADS].reshape(db * HEADS), state_ssm[0], xw_s, bs_s, cs_s, sb, w_out[0])
    y_p, npool_p = _pool_outproj_prompt(proj_p, pool_w_b, ps, ob_p, w_out_bf, xp2, g_fin, bsz, seq)
    y_p = y_p.reshape(bsz, seq, D_MODEL)

    flat = lambda t: t.reshape(db * nl, t.shape[-1])
    y_s = _outproj_sample(flat(oa_s), flat(yo_s), flat(ecum_s), flat(ydx_s), proj_s, ng,
                          jnp.transpose(to_blk), w_out_bf, xs2, g_fin).reshape(db, nl, D_MODEL)

    return (y_p, y_s,
            npool_p[None], nconv_p[None], nssm_p[None],
            jnp.transpose(npool_s, (1, 0, 2))[None],
            jnp.transpose(nconv_s, (1, 0, 2))[None],
            nssm_s[None])
```

```python
import functools

import jax
import jax.numpy as jnp
from jax import lax
from jax.experimental import pallas as pl
from jax.experimental.pallas import tpu as pltpu

F32 = jnp.float32
BF16 = jnp.bfloat16

D_MODEL = 2048
D_POOL = 1024
POOL_WINDOWS = (2, 4, 8, 16)
POOL_GROUP = 256
POOL_HIST = 15
D_SSM = 3072
HEAD_DIM = 64
HEADS = 48
GROUPS = 8
HEADS_PER_GROUP = 6
D_STATE = 128
D_BC = GROUPS * D_STATE
CONV_WIDTH = 4
D_CONV = D_SSM + 2 * D_BC
D_MAIN = 2 * D_POOL + D_SSM + D_CONV
PAST_LEN = 16384
DEC_SEQ = 4
EPS = 1e-5

LANES = 128
SUBLANES = 8
VMEM_LIMIT = 56 * 1024 * 1024

OFF_Z = 0
OFF_XS = D_SSM
OFF_U = 2 * D_SSM
OFF_GATE = OFF_U + D_POOL
OFF_B = OFF_GATE + D_POOL
OFF_C = OFF_B + D_BC

PERM_BLOCK = 1024
INPROJ_TM = 1024
OUTPROJ_TM = 512
OUTPROJ_SAMPLE_TM = 256
PROMPT_TN = 2048
CHUNK = 128
SAMPLE_BLOCK = 32
NEG_BIG = -1e30
LOG2E = 1.4426950408889634


def _silu(v):
    h = 0.5 * v
    return h + h * jnp.tanh(h)


def _softplus(v):
    y = jnp.exp(-jnp.abs(v))
    u = 1.0 + y
    d = u - 1.0
    l1p = jnp.where(d == 0.0, y, jnp.log(u) * (y / jnp.where(d == 0.0, 1.0, d)))
    return jnp.maximum(v, 0.0) + l1p


def _split2(v):
    hi = v.astype(BF16)
    lo = (v - hi.astype(F32)).astype(BF16)
    return hi, lo


def _dot(a, b):
    return jnp.dot(a, b, preferred_element_type=F32)


def _dot_nt(a, b):
    return lax.dot_general(a, b, (((1,), (1,)), ((), ())), preferred_element_type=F32)


def _dot2(v, onehot):
    hi, lo = _split2(v)
    return _dot(hi, onehot) + _dot(lo, onehot)


def _inproj_kernel(x_ref, g_ref, w_ref, wdt_ref, o_ref, dt_ref, *rest):
    from_f32 = w_ref.dtype == F32
    (wb_out_ref, wdt_out_ref, h_ref) = rest if from_f32 else (None, None, *rest)

    @pl.when(pl.program_id(1) == 0)
    def _():
        x = x_ref[...]
        ms = jnp.mean(x * x, axis=-1, keepdims=True)
        h = (x * lax.rsqrt(ms + EPS) * g_ref[...]).astype(BF16)
        h_ref[...] = h
        if from_f32:
            wrow = lax.broadcasted_iota(jnp.int32, wdt_ref.shape, 0)
            wdt = jnp.where(wrow < HEADS, wdt_ref[...], 0.0).astype(BF16)
            wdt_out_ref[...] = wdt
        else:
            wdt = wdt_ref[...]
        dt_ref[...] = _dot_nt(h, wdt)

    if from_f32:
        wb = w_ref[...].astype(BF16)
        wb_out_ref[...] = wb
    else:
        wb = w_ref[...]
    o_ref[...] = _dot_nt(h_ref[...], wb).astype(BF16)


def _src_block(j):
    nz = (D_SSM + D_SSM) // PERM_BLOCK
    npool = 2 * D_POOL // PERM_BLOCK
    return jnp.where(j < nz, j + npool, jnp.where(j < nz + npool, j - nz, j))


def _inproj(x2d, norm_g, w, w_dt, tn):
    m = x2d.shape[0]
    tm = min(INPROJ_TM, m)
    from_f32 = w.dtype == F32
    assert tn == PERM_BLOCK or not from_f32
    w_map = (lambda i, j: (_src_block(j), 0)) if from_f32 else (lambda i, j: (j, 0))
    wdt_map = (lambda i, j: (D_MAIN // LANES, 0)) if from_f32 else (lambda i, j: (0, 0))
    out_specs = [
        pl.BlockSpec((tm, tn), lambda i, j: (i, j)),
        pl.BlockSpec((tm, LANES), lambda i, j: (i, 0)),
    ]
    out_shape = [
        jax.ShapeDtypeStruct((m, D_MAIN), BF16),
        jax.ShapeDtypeStruct((m, LANES), F32),
    ]
    if from_f32:
        assert m == tm
        out_specs += [pl.BlockSpec((tn, D_MODEL), lambda i, j: (j, 0)),
                      pl.BlockSpec((LANES, D_MODEL), lambda i, j: (0, 0))]
        out_shape += [jax.ShapeDtypeStruct((D_MAIN, D_MODEL), BF16),
                      jax.ShapeDtypeStruct((LANES, D_MODEL), BF16)]
    return pl.pallas_call(
        _inproj_kernel,
        grid=(m // tm, D_MAIN // tn),
        in_specs=[
            pl.BlockSpec((tm, D_MODEL), lambda i, j: (i, 0)),
            pl.BlockSpec((1, D_MODEL), lambda i, j: (0, 0)),
            pl.BlockSpec((tn, D_MODEL), w_map),
            pl.BlockSpec((LANES, D_MODEL), wdt_map),
        ],
        out_specs=out_specs,
        out_shape=out_shape,
        scratch_shapes=[pltpu.VMEM((tm, D_MODEL), BF16)],
        compiler_params=pltpu.CompilerParams(
            dimension_semantics=("arbitrary", "arbitrary"),
            vmem_limit_bytes=VMEM_LIMIT),
        name="inproj",
    )(x2d, norm_g, w, w_dt)


def _outproj_kernel(a_ref, b_ref, w_ref, x_ref, g_ref, y_ref):
    acc = (_dot(a_ref[...], w_ref[0:D_POOL, :])
           + _dot(b_ref[...], w_ref[D_POOL:D_POOL + D_SSM, :]))
    r = x_ref[...] + acc
    ms = jnp.mean(r * r, axis=-1, keepdims=True)
    y_ref[...] = r * lax.rsqrt(ms + EPS) * g_ref[...]


def _pool_kernel(u_ref, gate_ref, pw_ref, ps_ref, oa_ref, np_ref, ubuf, wbuf, *, tl, nt):
    t = pl.program_id(1)
    hist = POOL_HIST + 1
    nrows = hist + tl
    nv = nrows // SUBLANES
    bpg = POOL_GROUP // LANES

    @pl.when(t == 0)
    def _():
        ubuf[:, 0:hist, :] = jnp.zeros((ubuf.shape[0], hist, LANES), F32)

    pos = t * tl + lax.broadcasted_iota(jnp.int32, (tl, 1), 0)
    for gi, w in enumerate(POOL_WINDOWS):
        win = []
        for lb in range(gi * bpg, (gi + 1) * bpg):
            ubuf[lb, hist:nrows, :] = u_ref[:, lb * LANES:(lb + 1) * LANES].astype(F32)
            acc = [ubuf[lb, pl.ds(a, SUBLANES, stride=nv), :] for a in range(nv)]
            span = 1
            while span < w:
                wrap = [pltpu.roll(acc[nv - span + a], 1, 0) for a in range(span)]
                acc = [acc[a] + (acc[a - span] if a >= span else wrap[a]) for a in range(nv)]
                span *= 2
            for a in range(nv):
                wbuf[lb, pl.ds(a, SUBLANES, stride=nv), :] = acc[a]
            win.append(wbuf[lb, hist:nrows, :])
        cs = slice(gi * POOL_GROUP, (gi + 1) * POOL_GROUP)
        cnt = jnp.minimum(w, pos + 1).astype(F32)
        ug = jnp.concatenate([ubuf[lb, hist:nrows, :] for lb in range(gi * bpg, (gi + 1) * bpg)],
                             axis=1)
        pooled = jnp.concatenate(win, axis=1) / cnt - ug
        mixed = _dot(pooled.astype(BF16), pw_ref[gi].astype(BF16))
        gt = gate_ref[:, cs].astype(F32)
        oa_ref[:, cs] = (mixed * ps_ref[:, cs] * _silu(gt)).astype(BF16)

    @pl.when(t == nt - 1)
    def _():
        for lb in range(D_POOL // LANES):
            np_ref[:, lb * LANES:(lb + 1) * LANES] = ubuf[lb, tl + 1:nrows, :]

    ubuf[:, 0:hist, :] = ubuf[:, tl:nrows, :]


def _pool_outproj_kernel(u_ref, gate_ref, pw_ref, ps_ref, b_ref, w_ref, x_ref, g_ref,
                         y_ref, np_ref, ubuf, wbuf, a_ref, *, tl, nt):
    _pool_kernel(u_ref, gate_ref, pw_ref, ps_ref, a_ref, np_ref, ubuf, wbuf, tl=tl, nt=nt)
    _outproj_kernel(a_ref, b_ref, w_ref, x_ref, g_ref, y_ref)


def _pool_outproj_prompt(proj, pool_w, pool_scale, out_b, w_bf, x2d, final_g, bsz, seq):
    tl = min(OUTPROJ_TM, seq)
    nt = seq // tl
    row = lambda b, t: b * nt + t
    const = lambda b, t: (0, 0)
    return pl.pallas_call(
        functools.partial(_pool_outproj_kernel, tl=tl, nt=nt),
        grid=(bsz, nt),
        in_specs=[
            pl.BlockSpec((tl, D_POOL), lambda b, t: (row(b, t), OFF_U // D_POOL)),
            pl.BlockSpec((tl, D_POOL), lambda b, t: (row(b, t), OFF_GATE // D_POOL)),
            pl.BlockSpec((len(POOL_WINDOWS), POOL_GROUP, POOL_GROUP), lambda b, t: (0, 0, 0)),
            pl.BlockSpec((1, D_POOL), const),
            pl.BlockSpec((tl, D_SSM), lambda b, t: (row(b, t), 0)),
            pl.BlockSpec((D_POOL + D_SSM, D_MODEL), const, pipeline_mode=pl.Buffered(1)),
            pl.BlockSpec((tl, D_MODEL), lambda b, t: (row(b, t), 0)),
            pl.BlockSpec((1, D_MODEL), const),
        ],
        out_specs=[
            pl.BlockSpec((tl, D_MODEL), lambda b, t: (row(b, t), 0)),
            pl.BlockSpec((None, POOL_HIST, D_POOL), lambda b, t: (b, 0, 0)),
        ],
        out_shape=[
            jax.ShapeDtypeStruct((bsz * seq, D_MODEL), F32),
            jax.ShapeDtypeStruct((bsz, POOL_HIST, D_POOL), F32),
        ],
        scratch_shapes=[pltpu.VMEM((D_POOL // LANES, tl + POOL_HIST + 1, LANES), F32),
                        pltpu.VMEM((D_POOL // LANES, tl + POOL_HIST + 1, LANES), F32),
                        pltpu.VMEM((tl, D_POOL), BF16)],
        compiler_params=pltpu.CompilerParams(
            dimension_semantics=("arbitrary", "arbitrary"),
            vmem_limit_bytes=VMEM_LIMIT),
        name="pool_outproj",
    )(proj, proj, pool_w, pool_scale, out_b, w_bf, x2d, final_g)


def _ssd_body(z_ref, xs_ref, b_ref, c_ref, dtr_ref, cw_ref, cbias_ref, dtb_ref, alog_ref,
              dskip_ref, ng_ref,
              ob_ref, nconv_ref, nssm_ref,
              cbuf, st_ref, cvx_ref, cvb_ref, cvc_ref, y_ref, *, nc, after_group=None):
    q = CHUNK
    c_idx = pl.program_id(1)
    halo = SUBLANES

    nbx, nbb = D_SSM // LANES, D_BC // LANES

    @pl.when(c_idx == 0)
    def _():
        cbuf[:, 0:halo, :] = jnp.zeros((cbuf.shape[0], halo, LANES), F32)
        st_ref[...] = jnp.zeros(st_ref.shape, F32)

    nv = (halo + q) // SUBLANES
    for blk in range(D_CONV // LANES):
        ls = slice(blk * LANES, (blk + 1) * LANES)
        if blk < nbx:
            src_ref, off, dst = xs_ref, blk * LANES, cvx_ref.at[blk]
        elif blk < nbx + nbb:
            src_ref, off, dst = b_ref, (blk - nbx) * LANES, cvb_ref.at[blk - nbx]
        else:
            src_ref, off, dst = c_ref, (blk - nbx - nbb) * LANES, cvc_ref.at[blk - nbx - nbb]
        cbuf[blk, halo:halo + q, :] = src_ref[:, off:off + LANES].astype(F32)
        xv = [cbuf[blk, pl.ds(a, SUBLANES, stride=nv), :] for a in range(nv)]
        wrap = [pltpu.roll(xv[nv - k], 1, 0) for k in range(1, CONV_WIDTH)]
        taps = [0.5 * cw_ref[k:k + 1, ls] for k in range(CONV_WIDTH)]
        bias = 0.5 * cbias_ref[:, ls]
        for a in range(nv):
            h = bias + taps[CONV_WIDTH - 1] * xv[a]
            for k in range(1, CONV_WIDTH):
                src = xv[a - k] if a >= k else wrap[k - a - 1]
                h = h + taps[CONV_WIDTH - 1 - k] * src
            dst[pl.ds(a, SUBLANES, stride=nv), :] = h + h * jnp.tanh(h)
    rows = slice(halo, halo + q)

    dt = _softplus(dtr_ref[...] + dtb_ref[...])
    a_neg = -jnp.exp(alog_ref[...])
    da = dt * (a_neg * LOG2E)
    row = lax.broadcasted_iota(jnp.int32, (q, LANES), 0)
    a2 = da
    shift = 1
    while shift < q:
        a2 = a2 + jnp.where(row >= shift, pltpu.roll(a2, shift, 0), 0.0)
        shift *= 2
    a2_t = jnp.transpose(a2)
    ldt_t = jnp.log2(jnp.transpose(dt))
    a2_end_t = a2_t[:, q - 1:q]
    w_t = jnp.exp2(a2_end_t - a2_t + ldt_t)
    cdec_t = jnp.exp2(a2_end_t)
    srow_t = a2_t - ldt_t
    ea = jnp.exp2(a2)

    li = lax.broadcasted_iota(jnp.int32, (q, q), 0)
    si = lax.broadcasted_iota(jnp.int32, (q, q), 1)
    tri = li >= si
    lane = lax.broadcasted_iota(jnp.int32, (q, LANES), 1)
    lo_half = lane < HEAD_DIM

    for g in range(GROUPS):
        gs = slice(g * D_STATE, (g + 1) * D_STATE)
        c_gb = cvc_ref[g, rows, :].astype(BF16)
        b_g = cvb_ref[g, rows, :]
        cb = _dot_nt(c_gb, b_g.astype(BF16))
        b_t = jnp.transpose(b_g)
        ppg = HEADS_PER_GROUP // 2
        gl = slice(g * ppg * LANES, (g + 1) * ppg * LANES)
        y_off = _dot(c_gb, st_ref[:, gl].astype(BF16))
        for j in range(ppg):
            blk = g * ppg + j
            ls = slice(blk * LANES, (blk + 1) * LANES)
            sc, bw, ecol, dec = [], [], [], []
            for h in (2 * blk, 2 * blk + 1):
                a_col = jnp.broadcast_to(a2[:, h:h + 1], (q, q))
                decay_dt = jnp.exp2(jnp.where(tri, a_col - srow_t[h:h + 1, :], NEG_BIG))
                sc.append((cb * decay_dt).astype(BF16))
                bw.append((b_t * w_t[h:h + 1, :]).astype(BF16))
                ecol.append(jnp.broadcast_to(ea[:, h:h + 1], (q, LANES)))
                dec.append(jnp.broadcast_to(cdec_t[h:h + 1, :], (q, LANES)))
            x_p = cvx_ref[blk, rows, :].astype(BF16)
            zero_b = jnp.zeros_like(x_p)
            x_bd = jnp.concatenate([jnp.where(lo_half, x_p, zero_b),
                                    jnp.where(lo_half, zero_b, x_p)], axis=0)
            lhs1 = jnp.concatenate([jnp.concatenate(sc, axis=1),
                                    jnp.concatenate(bw, axis=1)], axis=0)
            r1 = _dot(lhs1, x_bd)
            y_ref[:, ls] = (r1[0:q] + y_off[:, j * LANES:(j + 1) * LANES]
                            * jnp.where(lo_half, ecol[0], ecol[1]))
            st_ref[:, ls] = (st_ref[:, ls] * jnp.where(lo_half, dec[0], dec[1])
                             + r1[q:2 * q])
        if after_group is not None:
            after_group(g)

    gw = D_SSM // GROUPS
    bpg = gw // LANES
    for g in range(GROUPS):
        yz, ssq = [], None
        for blk in range(g * bpg, (g + 1) * bpg):
            ls = slice(blk * LANES, (blk + 1) * LANES)
            y = y_ref[:, ls] + cvx_ref[blk, rows, :] * dskip_ref[:, ls]
            v = y * _silu(z_ref[:, ls].astype(F32))
            yz.append(v)
            ssq = v * v if ssq is None else ssq + v * v
        scale = lax.rsqrt(jnp.sum(ssq, axis=-1, keepdims=True) * (1.0 / gw) + EPS)
        for i, blk in enumerate(range(g * bpg, (g + 1) * bpg)):
            ls = slice(blk * LANES, (blk + 1) * LANES)
            ob_ref[:, ls] = (yz[i] * scale * ng_ref[:, ls]).astype(BF16)

    cbuf[:, 0:halo, :] = cbuf[:, q:q + halo, :]

    @pl.when(c_idx == nc - 1)
    def _():
        for blk in range(D_CONV // LANES):
            nconv_ref[:, blk * LANES:(blk + 1) * LANES] = cbuf[blk, q + halo - 3:q + halo, :]
        for blk in range(D_SSM // LANES):
            t = jnp.transpose(st_ref[:, blk * LANES:(blk + 1) * LANES])
            nssm_ref[2 * blk:2 * blk + 2] = t.reshape(2, HEAD_DIM, D_STATE)


def _sample1_kernel(pj_ref, dtr_ref, sp_ref, scv_ref, pw_ref, ps_ref, cw_ref, cbias_ref, dtb_ref,
                    alog_ref, dskip_ref, ehot_ref, ghot_ref,
                    oa_ref, npool_ref, nconv_ref, cs_ref, bs_ref, xw_ref, ydx_ref, ecum_ref,
                    cdec_ref, *, sb):
    nl = DEC_SEQ

    def rows(l):
        return slice(l * sb, (l + 1) * sb)

    def tok(l, off, width):
        return pj_ref[rows(l), off:off + width].astype(F32)

    for gi, w in enumerate(POOL_WINDOWS):
        c0 = gi * POOL_GROUP
        ext = [sp_ref[j, :, c0:c0 + POOL_GROUP] for j in range(POOL_HIST)]
        ext += [tok(l, OFF_U + c0, POOL_GROUP) for l in range(nl)]
        for l in range(nl):
            acc = ext[POOL_HIST + l]
            for k in range(1, w):
                acc = acc + ext[POOL_HIST + l - k]
            cnt = float(min(w, PAST_LEN + l + 1))
            pooled = acc / cnt - ext[POOL_HIST + l]
            mixed = _dot(pooled.astype(BF16), pw_ref[gi].astype(BF16))
            gt = tok(l, OFF_GATE + c0, POOL_GROUP)
            oa_ref[rows(l), c0:c0 + POOL_GROUP] = (
                mixed * ps_ref[:, c0:c0 + POOL_GROUP] * _silu(gt)).astype(BF16)
    for j in range(POOL_HIST):
        src = j + nl
        if src < POOL_HIST:
            npool_ref[j] = sp_ref[src]
        else:
            npool_ref[j] = tok(src - POOL_HIST, OFF_U, D_POOL)

    hist = CONV_WIDTH - 1
    conv_out = {}
    for name, poff, coff, width in (("x", OFF_XS, 0, D_SSM), ("b", OFF_B, D_SSM, D_BC),
                                    ("c", OFF_C, D_SSM + D_BC, D_BC)):
        ext = [scv_ref[j, :, coff:coff + width] for j in range(hist)]
        ext += [tok(l, poff, width) for l in range(nl)]
        outs = []
        for l in range(nl):
            acc = cbias_ref[:, coff:coff + width]
            for k in range(CONV_WIDTH):
                acc = acc + cw_ref[k:k + 1, coff:coff + width] * ext[l + k]
            outs.append(_silu(acc))
        conv_out[name] = outs
        for j in range(hist):
            nconv_ref[j, :, coff:coff + width] = ext[j + nl]
    xs, bs, cs = conv_out["x"], conv_out["b"], conv_out["c"]
    for l in range(nl):
        cs_ref[rows(l), :] = cs[l]
        bs_ref[rows(l), :] = bs[l].astype(BF16)

    a_neg = -jnp.exp(alog_ref[...])
    dt, a_cum = [], []
    run = None
    for l in range(nl):
        d = _softplus(dtr_ref[rows(l), :] + dtb_ref[...])
        dt.append(d)
        run = d * a_neg if run is None else run + d * a_neg
        a_cum.append(run)
    cdec_ref[...] = jnp.exp(a_cum[nl - 1])

    def onehot_rows(mats, onehot):
        parts = [_split2(m) for m in mats]
        stack = jnp.concatenate([p[0] for p in parts] + [p[1] for p in parts], axis=0)
        res = _dot(stack, onehot)
        n = len(mats)
        return [res[k * sb:(k + 1) * sb] + res[(n + k) * sb:(n + k + 1) * sb] for k in range(n)]

    pairs = [(l, s) for l in range(nl) for s in range(l + 1)]
    cbh = onehot_rows([cs[l] * bs[s] for l, s in pairs], ghot_ref[...])
    gls = [c * jnp.exp(a_cum[l] - a_cum[s]) * dt[s] for c, (l, s) in zip(cbh, pairs)]
    w_end = [jnp.exp(a_cum[nl - 1] - a_cum[l]) * dt[l] for l in range(nl)]
    e_cum = [jnp.exp(a_cum[l]) for l in range(nl)]
    chunk = 4 * LANES
    for c0 in range(0, D_SSM, chunk):
        cl = slice(c0, c0 + chunk)
        ex = onehot_rows(gls + w_end + e_cum, ehot_ref[:, cl])
        g_ex, w_ex, e_ex = ex[:len(pairs)], ex[len(pairs):len(pairs) + nl], ex[len(pairs) + nl:]
        xc = [x[:, cl] for x in xs]
        for l in range(nl):
            ydiag = dskip_ref[:, cl] * xc[l]
            for k, (pl_, ps_) in enumerate(pairs):
                if pl_ == l:
                    ydiag = ydiag + g_ex[k] * xc[ps_]
            ydx_ref[rows(l), cl] = ydiag
            xw_ref[rows(l), cl] = w_ex[l] * xc[l]
            ecum_ref[rows(l), cl] = e_ex[l]


def _sample1(pj3, dtr3, sp3, scv3, pool_w, pool_scale, conv_w, conv_b, dt_bias_p, a_log_p,
             dskip_row, ehot, ghot, sb):
    nblk = pj3.shape[0]
    db = nblk * sb
    rb = DEC_SEQ * sb
    blk2 = lambda width: pl.BlockSpec((None, rb, width), lambda i: (i, 0, 0))
    hist3 = lambda n, width: pl.BlockSpec((n, sb, width), lambda i: (0, i, 0))
    const2 = lambda shape: pl.BlockSpec(shape, lambda i: (0, 0))
    blk_shape = lambda width, dt: jax.ShapeDtypeStruct((nblk, rb, width), dt)
    return pl.pallas_call(
        functools.partial(_sample1_kernel, sb=sb),
        grid=(nblk,),
        in_specs=[
            blk2(D_MAIN), blk2(LANES), hist3(POOL_HIST, D_POOL), hist3(CONV_WIDTH - 1, D_CONV),
            pl.BlockSpec((len(POOL_WINDOWS), POOL_GROUP, POOL_GROUP), lambda i: (0, 0, 0)),
            const2((1, D_POOL)), const2((CONV_WIDTH, D_CONV)), const2((1, D_CONV)),
            const2((1, LANES)), const2((1, LANES)), const2((1, D_SSM)),
            const2((LANES, D_SSM)), const2((D_BC, LANES)),
        ],
        out_specs=[
            blk2(D_POOL), hist3(POOL_HIST, D_POOL), hist3(CONV_WIDTH - 1, D_CONV),
            blk2(D_BC), blk2(D_BC), blk2(D_SSM), blk2(D_SSM), blk2(D_SSM),
            pl.BlockSpec((sb, LANES), lambda i: (i, 0)),
        ],
        out_shape=[
            blk_shape(D_POOL, BF16),
            jax.ShapeDtypeStruct((POOL_HIST, db, D_POOL), F32),
            jax.ShapeDtypeStruct((CONV_WIDTH - 1, db, D_CONV), F32),
            blk_shape(D_BC, F32), blk_shape(D_BC, BF16),
            blk_shape(D_SSM, F32), blk_shape(D_SSM, F32), blk_shape(D_SSM, F32),
            jax.ShapeDtypeStruct((db, LANES), F32),
        ],
        compiler_params=pltpu.CompilerParams(
            dimension_semantics=("arbitrary",),
            vmem_limit_bytes=VMEM_LIMIT),
        name="sample_elementwise",
    )(pj3, dtr3, sp3, scv3, pool_w, pool_scale, conv_w, conv_b, dt_bias_p, a_log_p, dskip_row,
      ehot, ghot)


def _sample2_setup(xw_ref, cs_ref, xwt_ref, *, sb, per):
    j = pl.program_id(1)

    @pl.when(j == 0)
    def _():
        for blk in range(D_SSM // LANES):
            xwt_ref[blk * LANES:(blk + 1) * LANES, :] = jnp.transpose(
                xw_ref[:, blk * LANES:(blk + 1) * LANES]).astype(BF16)

    return [jnp.concatenate([cs_ref[pl.ds(l * sb + per * j + 2 * pr + bi, 1), :]
                             for bi in range(2) for l in range(DEC_SEQ)], axis=0).astype(BF16)
            for pr in range(per // 2)]


def _sample2_group(g, refs, c_rows, *, sb, per):
    cdec_ref, st_ref, _, bs_ref, _, nst_ref, yo_ref, xwt_ref = refs
    i = pl.program_id(0)
    j = pl.program_id(1)
    nl = DEC_SEQ
    rows = sb * nl
    seq_of_row = lax.broadcasted_iota(jnp.int32, (rows, 1), 0) & (sb - 1)
    gw = HEADS_PER_GROUP * HEAD_DIM
    gs = slice(g * D_STATE, (g + 1) * D_STATE)
    for pr in range(per // 2):
        q0 = per * j + 2 * pr
        c8 = c_rows[pr][:, gs]
        b_blk = bs_ref[:, gs]
        zero_b = jnp.zeros_like(b_blk)
        w2 = jnp.concatenate([jnp.where(seq_of_row == q0, b_blk, zero_b),
                              jnp.where(seq_of_row == q0 + 1, b_blk, zero_b)], axis=1)
        u2 = _dot(xwt_ref[g * gw:(g + 1) * gw, :], w2)
        for bi in range(2):
            sq = 2 * pr + bi
            s0 = st_ref[sq, g * HEADS_PER_GROUP:(g + 1) * HEADS_PER_GROUP].reshape(gw, D_STATE)
            yo = _dot_nt(c8, s0.astype(BF16))
            base = (i * sb + q0 + bi) * HEADS + g * HEADS_PER_GROUP
            for r in range(HEADS_PER_GROUP):
                dec = cdec_ref[base + r]
                rs = slice(r * HEAD_DIM, (r + 1) * HEAD_DIM)
                nst_ref[sq, g * HEADS_PER_GROUP + r] = (
                    s0[rs] * dec + u2[rs, bi * D_STATE:(bi + 1) * D_STATE])
            for l in range(nl):
                yo_ref[pl.ds(l * sb + q0 + bi, 1), g * gw:(g + 1) * gw] = (
                    yo[bi * nl + l:bi * nl + l + 1])


N_SSD_IN, N_SSD_OUT, N_SSD_SCRATCH = 11, 3, 6
N_ST_IN, N_ST_OUT = 5, 2


def _ssd_state_kernel(*refs, nc, sb, per):
    n_in = N_SSD_IN + N_ST_IN + 1
    ins, rest = refs[:n_in], refs[n_in:]
    n_out = N_SSD_OUT + N_ST_OUT + 1
    outs, scratch = rest[:n_out], rest[n_out:]
    outs[-1][...] = ins[-1][...].astype(BF16)
    st_refs = (*ins[N_SSD_IN:N_SSD_IN + N_ST_IN], *outs[N_SSD_OUT:N_SSD_OUT + N_ST_OUT],
               *scratch[N_SSD_SCRATCH:])
    c_rows = _sample2_setup(st_refs[2], st_refs[4], st_refs[7], sb=sb, per=per)
    _ssd_body(*ins[:N_SSD_IN], *outs[:N_SSD_OUT], *scratch[:N_SSD_SCRATCH], nc=nc,
              after_group=functools.partial(_sample2_group, refs=st_refs, c_rows=c_rows,
                                            sb=sb, per=per))


def _ssd_prompt_sample_state(proj, dt_raw, conv_w, conv_b, dt_bias_p, a_log_p, dskip_row, norm_g,
                             bsz, seq, cdec_flat, state, xw3, bs3, cs3, sb, w_out2d):
    q = CHUNK
    nc = seq // q
    db = state.shape[0]
    rb = DEC_SEQ * sb
    per = sb // nc
    assert rb == LANES and db // sb == bsz and per * nc == sb and per % 2 == 0
    d_mix = w_out2d.shape[0]
    slab = d_mix // (bsz * nc)
    assert slab * bsz * nc == d_mix and slab % (2 * SUBLANES) == 0
    wslab = pl.BlockSpec((slab, D_MODEL), lambda b, c: (b * nc + c, 0))
    row = lambda b, c: b * nc + c
    const = lambda b, c: (0, 0)
    blk2 = lambda width: pl.BlockSpec((None, rb, width), lambda i, j: (i, 0, 0))
    st_spec = pl.BlockSpec((per, HEADS, HEAD_DIM, D_STATE), lambda i, j: (i * nc + j, 0, 0, 0))
    return pl.pallas_call(
        functools.partial(_ssd_state_kernel, nc=nc, sb=sb, per=per),
        grid=(bsz, nc),
        in_specs=[
            pl.BlockSpec((q, D_SSM), lambda b, c: (row(b, c), OFF_Z // D_SSM)),
            pl.BlockSpec((q, D_SSM), lambda b, c: (row(b, c), OFF_XS // D_SSM)),
            pl.BlockSpec((q, D_BC), lambda b, c: (row(b, c), OFF_B // D_BC)),
            pl.BlockSpec((q, D_BC), lambda b, c: (row(b, c), OFF_C // D_BC)),
            pl.BlockSpec((q, LANES), lambda b, c: (row(b, c), 0)),
            pl.BlockSpec((CONV_WIDTH, D_CONV), const),
            pl.BlockSpec((1, D_CONV), const),
            pl.BlockSpec((1, LANES), const),
            pl.BlockSpec((1, LANES), const),
            pl.BlockSpec((1, D_SSM), const),
            pl.BlockSpec((1, D_SSM), const),
            pl.BlockSpec(memory_space=pltpu.SMEM),
            st_spec, blk2(D_SSM), blk2(D_BC), blk2(D_BC),
            wslab,
        ],
        out_specs=[
            pl.BlockSpec((q, D_SSM), lambda b, c: (row(b, c), 0)),
            pl.BlockSpec((None, CONV_WIDTH - 1, D_CONV), lambda b, c: (b, 0, 0)),
            pl.BlockSpec((None, HEADS, HEAD_DIM, D_STATE), lambda b, c: (b, 0, 0, 0)),
            st_spec, blk2(D_SSM),
            wslab,
        ],
        out_shape=[
            jax.ShapeDtypeStruct((bsz * seq, D_SSM), BF16),
            jax.ShapeDtypeStruct((bsz, CONV_WIDTH - 1, D_CONV), F32),
            jax.ShapeDtypeStruct((bsz, HEADS, HEAD_DIM, D_STATE), F32),
            jax.ShapeDtypeStruct(state.shape, F32),
            jax.ShapeDtypeStruct((db // sb, rb, D_SSM), F32),
            jax.ShapeDtypeStruct(w_out2d.shape, BF16),
        ],
        scratch_shapes=[
            pltpu.VMEM((D_CONV // LANES, q + SUBLANES, LANES), F32),
            pltpu.VMEM((D_STATE, D_SSM), F32),
            pltpu.VMEM((D_SSM // LANES, q + SUBLANES, LANES), F32),
            pltpu.VMEM((D_BC // LANES, q + SUBLANES, LANES), F32),
            pltpu.VMEM((D_BC // LANES, q + SUBLANES, LANES), F32),
            pltpu.VMEM((q, D_SSM), F32),
            pltpu.VMEM((D_SSM, rb), BF16),
        ],
        compiler_params=pltpu.CompilerParams(
            dimension_semantics=("arbitrary", "arbitrary"),
            vmem_limit_bytes=VMEM_LIMIT),
        name="ssd_prompt_sample_state",
    )(proj, proj, proj, proj, dt_raw, conv_w, conv_b, dt_bias_p, a_log_p, dskip_row, norm_g,
      cdec_flat, state, xw3, bs3, cs3, w_out2d)


def _outproj_sample_kernel(a_ref, yo_ref, ecum_ref, ydx_ref, z_ref, ng_ref, w_ref, x_ref,
                           g_ref, y_ref, b_ref):
    gw = D_SSM // GROUPS
    for g in range(GROUPS):
        cs = slice(g * gw, (g + 1) * gw)
        y = ydx_ref[:, cs] + ecum_ref[:, cs] * yo_ref[:, cs]
        yz = y * _silu(z_ref[:, cs].astype(F32))
        ms = jnp.sum(yz * yz, axis=-1, keepdims=True) * (1.0 / gw)
        b_ref[:, cs] = (yz * lax.rsqrt(ms + EPS) * ng_ref[:, cs]).astype(BF16)
    _outproj_kernel(a_ref, b_ref, w_ref, x_ref, g_ref, y_ref)


def _outproj_sample(out_a, yo, ecum, ydx, proj, norm_g, w_bf, x2d, final_g):
    m = x2d.shape[0]
    tm = min(OUTPROJ_SAMPLE_TM, m)
    rowblk = lambda width: pl.BlockSpec((tm, width), lambda i: (i, 0))
    const = lambda shape, **kw: pl.BlockSpec(shape, lambda i: (0, 0), **kw)
    return pl.pallas_call(
        _outproj_sample_kernel,
        grid=(m // tm,),
        in_specs=[
            rowblk(D_POOL), rowblk(D_SSM), rowblk(D_SSM), rowblk(D_SSM),
            pl.BlockSpec((tm, D_SSM), lambda i: (i, OFF_Z // D_SSM)),
            const((1, D_SSM)),
            const((D_POOL + D_SSM, D_MODEL), pipeline_mode=pl.Buffered(1)),
            rowblk(D_MODEL), const((1, D_MODEL)),
        ],
        out_specs=rowblk(D_MODEL),
        out_shape=jax.ShapeDtypeStruct((m, D_MODEL), F32),
        scratch_shapes=[pltpu.VMEM((tm, D_SSM), BF16)],
        compiler_params=pltpu.CompilerParams(
            dimension_semantics=("arbitrary",),
            vmem_limit_bytes=VMEM_LIMIT),
        name="outproj_sample",
    )(out_a, yo, ecum, ydx, proj, norm_g, w_bf, x2d, final_g)


def kernel(x_prompt, x_sample, state_pool, state_conv, state_ssm, norm_g, w_in, conv_w, conv_b,
           dt_bias, a_log, d_skip, ssm_norm_g, pool_w, pool_scale, w_out, final_g):
    bsz, seq, _ = x_prompt.shape
    db, nl, _ = x_sample.shape
    assert nl == DEC_SEQ and seq % CHUNK == 0 and w_in.shape[0] == 1

    w_t = jnp.transpose(w_in[0])
    pool_w_b = pool_w[0]
    g_in = norm_g[0][None, :]
    g_fin = final_g[None, :]
    ps = pool_scale[0][None, :]
    cw = conv_w[0]
    cbias = conv_b[0][None, :]
    pad_h = lambda v: jnp.pad(v, (0, LANES - HEADS))[None, :]
    dtb = pad_h(dt_bias[0])
    alog = pad_h(a_log[0])
    dskip_row = jnp.repeat(d_skip[0], HEAD_DIM)[None, :]
    ng = ssm_norm_g[0][None, :]

    sb = SAMPLE_BLOCK
    nblk = db // sb
    rb = nl * sb
    xs2 = x_sample.reshape(nblk, sb, nl, D_MODEL).transpose(0, 2, 1, 3).reshape(db * nl, D_MODEL)
    proj_s, dt_s, w_bf, wdt_bf = _inproj(xs2, g_in, w_t, w_t, PERM_BLOCK)
    pj3 = proj_s.reshape(nblk, rb, D_MAIN)
    head_of_ch = jnp.arange(D_SSM, dtype=jnp.int32) // HEAD_DIM
    ehot = (jnp.arange(LANES, dtype=jnp.int32)[:, None] == head_of_ch[None, :]).astype(BF16)
    grp_of_row = jnp.arange(D_BC, dtype=jnp.int32) // D_STATE
    head_id = jnp.arange(LANES, dtype=jnp.int32)
    ghot = ((head_id[None, :] // HEADS_PER_GROUP == grp_of_row[:, None])
            & (head_id[None, :] < HEADS)).astype(BF16)
    (oa_s, npool_s, nconv_s, cs_s, bs_s, xw_s, ydx_s, ecum_s, cdec_s) = _sample1(
        pj3, dt_s.reshape(nblk, rb, LANES), jnp.transpose(state_pool[0], (1, 0, 2)),
        jnp.transpose(state_conv[0], (1, 0, 2)), pool_w_b, ps, cw, cbias, dtb, alog,
        dskip_row, ehot, ghot, sb)

    xp2 = x_prompt.reshape(bsz * seq, D_MODEL)
    proj_p, dt_p = _inproj(xp2, g_in, w_bf, wdt_bf, PROMPT_TN)
    ob_p, nconv_p, nssm_p, nssm_s, yo_s, w_out_bf = _ssd_prompt_sample_state(
        proj_p, dt_p, cw, cbias, dtb, alog, dskip_row, ng, bsz, seq,
        cdec_s[:, :HEADS].reshape(db * HEADS), state_ssm[0], xw_s, bs_s, cs_s, sb, w_out[0])
    y_p, npool_p = _pool_outproj_prompt(proj_p, pool_w_b, ps, ob_p, w_out_bf, xp2, g_fin, bsz, seq)
    y_p = y_p.reshape(bsz, seq, D_MODEL)

    flat = lambda t: t.reshape(db * nl, t.shape[-1])
    y_s = _outproj_sample(flat(oa_s), flat(yo_s), flat(ecum_s), flat(ydx_s), proj_s, ng, w_out_bf,
                          xs2, g_fin)
    y_s = y_s.reshape(nblk, nl, sb, D_MODEL).transpose(0, 2, 1, 3).reshape(db, nl, D_MODEL)

    return (y_p, y_s,
            npool_p[None], nconv_p[None], nssm_p[None],
            jnp.transpose(npool_s, (1, 0, 2))[None],
            jnp.transpose(nconv_s, (1, 0, 2))[None],
            nssm_s[None])
```

```python
import functools

import jax
import jax.numpy as jnp
from jax import lax
from jax.experimental import pallas as pl
from jax.experimental.pallas import tpu as pltpu

F32 = jnp.float32
BF16 = jnp.bfloat16

D_MODEL = 2048
D_POOL = 1024
POOL_WINDOWS = (2, 4, 8, 16)
POOL_GROUP = 256
POOL_HIST = 15
D_SSM = 3072
HEAD_DIM = 64
HEADS = 48
GROUPS = 8
HEADS_PER_GROUP = 6
D_STATE = 128
D_BC = GROUPS * D_STATE
CONV_WIDTH = 4
D_CONV = D_SSM + 2 * D_BC
D_MAIN = 2 * D_POOL + D_SSM + D_CONV
PAST_LEN = 16384
DEC_SEQ = 4
EPS = 1e-5

LANES = 128
SUBLANES = 8
VMEM_LIMIT = 56 * 1024 * 1024

OFF_Z = 0
OFF_XS = D_SSM
OFF_U = 2 * D_SSM
OFF_GATE = OFF_U + D_POOL
OFF_B = OFF_GATE + D_POOL
OFF_C = OFF_B + D_BC

PERM_BLOCK = 1024
INPROJ_TM = 1024
OUTPROJ_TM = 512
OUTPROJ_SAMPLE_TM = 256
PROMPT_TN = 2048
CHUNK = 128
SAMPLE_BLOCK = 32
NEG_BIG = -1e30
LOG2E = 1.4426950408889634


def _silu(v):
    h = 0.5 * v
    return h + h * jnp.tanh(h)


def _softplus(v):
    y = jnp.exp(-jnp.abs(v))
    u = 1.0 + y
    d = u - 1.0
    l1p = jnp.where(d == 0.0, y, jnp.log(u) * (y / jnp.where(d == 0.0, 1.0, d)))
    return jnp.maximum(v, 0.0) + l1p


def _split2(v):
    hi = v.astype(BF16)
    lo = (v - hi.astype(F32)).astype(BF16)
    return hi, lo


def _dot(a, b):
    return jnp.dot(a, b, preferred_element_type=F32)


def _dot_nt(a, b):
    return lax.dot_general(a, b, (((1,), (1,)), ((), ())), preferred_element_type=F32)


def _dot2(v, onehot):
    hi, lo = _split2(v)
    return _dot(hi, onehot) + _dot(lo, onehot)


def _inproj_kernel(x_ref, g_ref, w_ref, wdt_ref, o_ref, dt_ref, *rest):
    from_f32 = w_ref.dtype == F32
    (wb_out_ref, wdt_out_ref), rest = (rest[:2], rest[2:]) if from_f32 else ((None, None), rest)
    h_ref = rest[0]

    @pl.when(pl.program_id(1) == 0)
    def _():
        if len(x_ref.shape) == 3:
            rows_ref = rest[1]
            sb = SAMPLE_BLOCK
            for b in range(x_ref.shape[0] // sb):
                for t in range(DEC_SEQ):
                    r0 = (b * DEC_SEQ + t) * sb
                    rows_ref[r0:r0 + sb, :] = x_ref[b * sb:(b + 1) * sb, t, :]
            x = rows_ref[...]
        else:
            x = x_ref[...]
        ms = jnp.mean(x * x, axis=-1, keepdims=True)
        h = (x * lax.rsqrt(ms + EPS) * g_ref[...]).astype(BF16)
        h_ref[...] = h
        if from_f32:
            wrow = lax.broadcasted_iota(jnp.int32, wdt_ref.shape, 0)
            wdt = jnp.where(wrow < HEADS, wdt_ref[...], 0.0).astype(BF16)
            wdt_out_ref[...] = wdt
        else:
            wdt = wdt_ref[...]
        dt_ref[...] = _dot_nt(h, wdt)

    if from_f32:
        wb = w_ref[...].astype(BF16)
        wb_out_ref[...] = wb
    else:
        wb = w_ref[...]
    o_ref[...] = _dot_nt(h_ref[...], wb).astype(BF16)


def _src_block(j):
    nz = (D_SSM + D_SSM) // PERM_BLOCK
    npool = 2 * D_POOL // PERM_BLOCK
    return jnp.where(j < nz, j + npool, jnp.where(j < nz + npool, j - nz, j))


def _inproj(x2d, norm_g, w, w_dt, tn):
    whole = x2d.ndim == 3
    m = x2d.shape[0] * x2d.shape[1] if whole else x2d.shape[0]
    tm = min(INPROJ_TM, m)
    assert m == tm or not whole
    x_spec = (pl.BlockSpec(x2d.shape, lambda i, j: (0, 0, 0), pipeline_mode=pl.Buffered(1))
              if whole else pl.BlockSpec((tm, D_MODEL), lambda i, j: (i, 0)))
    scratch = [pltpu.VMEM((tm, D_MODEL), BF16)]
    if whole:
        scratch.append(pltpu.VMEM((tm, D_MODEL), F32))
    from_f32 = w.dtype == F32
    assert tn == PERM_BLOCK or not from_f32
    w_map = (lambda i, j: (_src_block(j), 0)) if from_f32 else (lambda i, j: (j, 0))
    wdt_map = (lambda i, j: (D_MAIN // LANES, 0)) if from_f32 else (lambda i, j: (0, 0))
    out_specs = [
        pl.BlockSpec((tm, tn), lambda i, j: (i, j)),
        pl.BlockSpec((tm, LANES), lambda i, j: (i, 0)),
    ]
    out_shape = [
        jax.ShapeDtypeStruct((m, D_MAIN), BF16),
        jax.ShapeDtypeStruct((m, LANES), F32),
    ]
    if from_f32:
        assert m == tm
        out_specs += [pl.BlockSpec((tn, D_MODEL), lambda i, j: (j, 0)),
                      pl.BlockSpec((LANES, D_MODEL), lambda i, j: (0, 0))]
        out_shape += [jax.ShapeDtypeStruct((D_MAIN, D_MODEL), BF16),
                      jax.ShapeDtypeStruct((LANES, D_MODEL), BF16)]
    return pl.pallas_call(
        _inproj_kernel,
        grid=(m // tm, D_MAIN // tn),
        in_specs=[
            x_spec,
            pl.BlockSpec((1, D_MODEL), lambda i, j: (0, 0)),
            pl.BlockSpec((tn, D_MODEL), w_map),
            pl.BlockSpec((LANES, D_MODEL), wdt_map),
        ],
        out_specs=out_specs,
        out_shape=out_shape,
        scratch_shapes=scratch,
        compiler_params=pltpu.CompilerParams(
            dimension_semantics=("arbitrary", "arbitrary"),
            vmem_limit_bytes=VMEM_LIMIT),
        name="inproj",
    )(x2d, norm_g, w, w_dt)


def _outproj_kernel(a_ref, b_ref, w_ref, x_ref, g_ref, y_ref):
    acc = (_dot(a_ref[...], w_ref[0:D_POOL, :])
           + _dot(b_ref[...], w_ref[D_POOL:D_POOL + D_SSM, :]))

    def residual_norm(x, part):
        r = x + part
        ms = jnp.mean(r * r, axis=-1, keepdims=True)
        return r * lax.rsqrt(ms + EPS) * g_ref[...]

    if len(x_ref.shape) == 3:
        sb = SAMPLE_BLOCK
        for b in range(x_ref.shape[0] // sb):
            for t in range(DEC_SEQ):
                r0 = (b * DEC_SEQ + t) * sb
                y_ref[b * sb:(b + 1) * sb, t, :] = residual_norm(
                    x_ref[b * sb:(b + 1) * sb, t, :], acc[r0:r0 + sb])
    else:
        y_ref[...] = residual_norm(x_ref[...], acc)


def _pool_kernel(u_ref, gate_ref, pw_ref, ps_ref, oa_ref, np_ref, ubuf, wbuf, *, tl, nt):
    t = pl.program_id(1)
    hist = POOL_HIST + 1
    nrows = hist + tl
    nv = nrows // SUBLANES
    bpg = POOL_GROUP // LANES

    @pl.when(t == 0)
    def _():
        ubuf[:, 0:hist, :] = jnp.zeros((ubuf.shape[0], hist, LANES), F32)

    pos = t * tl + lax.broadcasted_iota(jnp.int32, (tl, 1), 0)
    for gi, w in enumerate(POOL_WINDOWS):
        win = []
        for lb in range(gi * bpg, (gi + 1) * bpg):
            ubuf[lb, hist:nrows, :] = u_ref[:, lb * LANES:(lb + 1) * LANES].astype(F32)
            acc = [ubuf[lb, pl.ds(a, SUBLANES, stride=nv), :] for a in range(nv)]
            span = 1
            while span < w:
                wrap = [pltpu.roll(acc[nv - span + a], 1, 0) for a in range(span)]
                acc = [acc[a] + (acc[a - span] if a >= span else wrap[a]) for a in range(nv)]
                span *= 2
            for a in range(nv):
                wbuf[lb, pl.ds(a, SUBLANES, stride=nv), :] = acc[a]
            win.append(wbuf[lb, hist:nrows, :])
        cs = slice(gi * POOL_GROUP, (gi + 1) * POOL_GROUP)
        cnt = jnp.minimum(w, pos + 1).astype(F32)
        ug = jnp.concatenate([ubuf[lb, hist:nrows, :] for lb in range(gi * bpg, (gi + 1) * bpg)],
                             axis=1)
        pooled = jnp.concatenate(win, axis=1) / cnt - ug
        mixed = _dot(pooled.astype(BF16), pw_ref[gi].astype(BF16))
        gt = gate_ref[:, cs].astype(F32)
        oa_ref[:, cs] = (mixed * ps_ref[:, cs] * _silu(gt)).astype(BF16)

    @pl.when(t == nt - 1)
    def _():
        for lb in range(D_POOL // LANES):
            np_ref[:, lb * LANES:(lb + 1) * LANES] = ubuf[lb, tl + 1:nrows, :]

    ubuf[:, 0:hist, :] = ubuf[:, tl:nrows, :]


def _pool_outproj_kernel(u_ref, gate_ref, pw_ref, ps_ref, b_ref, w_ref, x_ref, g_ref,
                         y_ref, np_ref, ubuf, wbuf, a_ref, *, tl, nt):
    _pool_kernel(u_ref, gate_ref, pw_ref, ps_ref, a_ref, np_ref, ubuf, wbuf, tl=tl, nt=nt)
    _outproj_kernel(a_ref, b_ref, w_ref, x_ref, g_ref, y_ref)


def _pool_outproj_prompt(proj, pool_w, pool_scale, out_b, w_bf, x2d, final_g, bsz, seq):
    tl = min(OUTPROJ_TM, seq)
    nt = seq // tl
    row = lambda b, t: b * nt + t
    const = lambda b, t: (0, 0)
    return pl.pallas_call(
        functools.partial(_pool_outproj_kernel, tl=tl, nt=nt),
        grid=(bsz, nt),
        in_specs=[
            pl.BlockSpec((tl, D_POOL), lambda b, t: (row(b, t), OFF_U // D_POOL)),
            pl.BlockSpec((tl, D_POOL), lambda b, t: (row(b, t), OFF_GATE // D_POOL)),
            pl.BlockSpec((len(POOL_WINDOWS), POOL_GROUP, POOL_GROUP), lambda b, t: (0, 0, 0)),
            pl.BlockSpec((1, D_POOL), const),
            pl.BlockSpec((tl, D_SSM), lambda b, t: (row(b, t), 0)),
            pl.BlockSpec((D_POOL + D_SSM, D_MODEL), const, pipeline_mode=pl.Buffered(1)),
            pl.BlockSpec((tl, D_MODEL), lambda b, t: (row(b, t), 0)),
            pl.BlockSpec((1, D_MODEL), const),
        ],
        out_specs=[
            pl.BlockSpec((tl, D_MODEL), lambda b, t: (row(b, t), 0)),
            pl.BlockSpec((None, POOL_HIST, D_POOL), lambda b, t: (b, 0, 0)),
        ],
        out_shape=[
            jax.ShapeDtypeStruct((bsz * seq, D_MODEL), F32),
            jax.ShapeDtypeStruct((bsz, POOL_HIST, D_POOL), F32),
        ],
        scratch_shapes=[pltpu.VMEM((D_POOL // LANES, tl + POOL_HIST + 1, LANES), F32),
                        pltpu.VMEM((D_POOL // LANES, tl + POOL_HIST + 1, LANES), F32),
                        pltpu.VMEM((tl, D_POOL), BF16)],
        compiler_params=pltpu.CompilerParams(
            dimension_semantics=("arbitrary", "arbitrary"),
            vmem_limit_bytes=VMEM_LIMIT),
        name="pool_outproj",
    )(proj, proj, pool_w, pool_scale, out_b, w_bf, x2d, final_g)


def _ssd_body(z_ref, xs_ref, b_ref, c_ref, dtr_ref, cw_ref, cbias_ref, dtb_ref, alog_ref,
              dskip_ref, ng_ref,
              ob_ref, nconv_ref, nssm_ref,
              cbuf, st_ref, cvx_ref, cvb_ref, cvc_ref, y_ref, *, nc, after_group=None):
    q = CHUNK
    c_idx = pl.program_id(1)
    halo = SUBLANES

    nbx, nbb = D_SSM // LANES, D_BC // LANES

    @pl.when(c_idx == 0)
    def _():
        cbuf[:, 0:halo, :] = jnp.zeros((cbuf.shape[0], halo, LANES), F32)
        st_ref[...] = jnp.zeros(st_ref.shape, F32)

    nv = (halo + q) // SUBLANES
    for blk in range(D_CONV // LANES):
        ls = slice(blk * LANES, (blk + 1) * LANES)
        if blk < nbx:
            src_ref, off, dst = xs_ref, blk * LANES, cvx_ref.at[blk]
        elif blk < nbx + nbb:
            src_ref, off, dst = b_ref, (blk - nbx) * LANES, cvb_ref.at[blk - nbx]
        else:
            src_ref, off, dst = c_ref, (blk - nbx - nbb) * LANES, cvc_ref.at[blk - nbx - nbb]
        cbuf[blk, halo:halo + q, :] = src_ref[:, off:off + LANES].astype(F32)
        xv = [cbuf[blk, pl.ds(a, SUBLANES, stride=nv), :] for a in range(nv)]
        wrap = [pltpu.roll(xv[nv - k], 1, 0) for k in range(1, CONV_WIDTH)]
        taps = [0.5 * cw_ref[k:k + 1, ls] for k in range(CONV_WIDTH)]
        bias = 0.5 * cbias_ref[:, ls]
        for a in range(nv):
            h = bias + taps[CONV_WIDTH - 1] * xv[a]
            for k in range(1, CONV_WIDTH):
                src = xv[a - k] if a >= k else wrap[k - a - 1]
                h = h + taps[CONV_WIDTH - 1 - k] * src
            dst[pl.ds(a, SUBLANES, stride=nv), :] = h + h * jnp.tanh(h)
    rows = slice(halo, halo + q)

    dt = _softplus(dtr_ref[...] + dtb_ref[...])
    a_neg = -jnp.exp(alog_ref[...])
    da = dt * (a_neg * LOG2E)
    row = lax.broadcasted_iota(jnp.int32, (q, LANES), 0)
    a2 = da
    shift = 1
    while shift < q:
        a2 = a2 + jnp.where(row >= shift, pltpu.roll(a2, shift, 0), 0.0)
        shift *= 2
    a2_t = jnp.transpose(a2)
    ldt_t = jnp.log2(jnp.transpose(dt))
    a2_end_t = a2_t[:, q - 1:q]
    w_t = jnp.exp2(a2_end_t - a2_t + ldt_t)
    cdec_t = jnp.exp2(a2_end_t)
    srow_t = a2_t - ldt_t
    ea = jnp.exp2(a2)

    li = lax.broadcasted_iota(jnp.int32, (q, q), 0)
    si = lax.broadcasted_iota(jnp.int32, (q, q), 1)
    tri = li >= si
    lane = lax.broadcasted_iota(jnp.int32, (q, LANES), 1)
    lo_half = lane < HEAD_DIM

    for g in range(GROUPS):
        gs = slice(g * D_STATE, (g + 1) * D_STATE)
        c_gb = cvc_ref[g, rows, :].astype(BF16)
        b_g = cvb_ref[g, rows, :]
        cb = _dot_nt(c_gb, b_g.astype(BF16))
        b_t = jnp.transpose(b_g)
        ppg = HEADS_PER_GROUP // 2
        gl = slice(g * ppg * LANES, (g + 1) * ppg * LANES)
        y_off = _dot(c_gb, st_ref[:, gl].astype(BF16))
        for j in range(ppg):
            blk = g * ppg + j
            ls = slice(blk * LANES, (blk + 1) * LANES)
            sc, bw, ecol, dec = [], [], [], []
            for h in (2 * blk, 2 * blk + 1):
                a_col = jnp.broadcast_to(a2[:, h:h + 1], (q, q))
                decay_dt = jnp.exp2(jnp.where(tri, a_col - srow_t[h:h + 1, :], NEG_BIG))
                sc.append((cb * decay_dt).astype(BF16))
                bw.append((b_t * w_t[h:h + 1, :]).astype(BF16))
                ecol.append(jnp.broadcast_to(ea[:, h:h + 1], (q, LANES)))
                dec.append(jnp.broadcast_to(cdec_t[h:h + 1, :], (q, LANES)))
            x_p = cvx_ref[blk, rows, :].astype(BF16)
            zero_b = jnp.zeros_like(x_p)
            x_bd = jnp.concatenate([jnp.where(lo_half, x_p, zero_b),
                                    jnp.where(lo_half, zero_b, x_p)], axis=0)
            lhs1 = jnp.concatenate([jnp.concatenate(sc, axis=1),
                                    jnp.concatenate(bw, axis=1)], axis=0)
            r1 = _dot(lhs1, x_bd)
            y_ref[:, ls] = (r1[0:q] + y_off[:, j * LANES:(j + 1) * LANES]
                            * jnp.where(lo_half, ecol[0], ecol[1]))
            st_ref[:, ls] = (st_ref[:, ls] * jnp.where(lo_half, dec[0], dec[1])
                             + r1[q:2 * q])
        if after_group is not None:
            after_group(g)

    gw = D_SSM // GROUPS
    bpg = gw // LANES
    for g in range(GROUPS):
        yz, ssq = [], None
        for blk in range(g * bpg, (g + 1) * bpg):
            ls = slice(blk * LANES, (blk + 1) * LANES)
            y = y_ref[:, ls] + cvx_ref[blk, rows, :] * dskip_ref[:, ls]
            v = y * _silu(z_ref[:, ls].astype(F32))
            yz.append(v)
            ssq = v * v if ssq is None else ssq + v * v
        scale = lax.rsqrt(jnp.sum(ssq, axis=-1, keepdims=True) * (1.0 / gw) + EPS)
        for i, blk in enumerate(range(g * bpg, (g + 1) * bpg)):
            ls = slice(blk * LANES, (blk + 1) * LANES)
            ob_ref[:, ls] = (yz[i] * scale * ng_ref[:, ls]).astype(BF16)

    cbuf[:, 0:halo, :] = cbuf[:, q:q + halo, :]

    @pl.when(c_idx == nc - 1)
    def _():
        for blk in range(D_CONV // LANES):
            nconv_ref[:, blk * LANES:(blk + 1) * LANES] = cbuf[blk, q + halo - 3:q + halo, :]
        for blk in range(D_SSM // LANES):
            t = jnp.transpose(st_ref[:, blk * LANES:(blk + 1) * LANES])
            nssm_ref[2 * blk:2 * blk + 2] = t.reshape(2, HEAD_DIM, D_STATE)


def _sample1_kernel(pj_ref, dtr_ref, sp_ref, scv_ref, pw_ref, ps_ref, cw_ref, cbias_ref, dtb_ref,
                    alog_ref, dskip_ref, ehot_ref, ghot_ref,
                    oa_ref, npool_ref, nconv_ref, cs_ref, bs_ref, xw_ref, ydx_ref, ecum_ref,
                    cdec_ref, *, sb):
    nl = DEC_SEQ

    def rows(l):
        return slice(l * sb, (l + 1) * sb)

    def tok(l, off, width):
        return pj_ref[rows(l), off:off + width].astype(F32)

    for gi, w in enumerate(POOL_WINDOWS):
        c0 = gi * POOL_GROUP
        ext = [sp_ref[j, :, c0:c0 + POOL_GROUP] for j in range(POOL_HIST)]
        ext += [tok(l, OFF_U + c0, POOL_GROUP) for l in range(nl)]
        for l in range(nl):
            acc = ext[POOL_HIST + l]
            for k in range(1, w):
                acc = acc + ext[POOL_HIST + l - k]
            cnt = float(min(w, PAST_LEN + l + 1))
            pooled = acc / cnt - ext[POOL_HIST + l]
            mixed = _dot(pooled.astype(BF16), pw_ref[gi].astype(BF16))
            gt = tok(l, OFF_GATE + c0, POOL_GROUP)
            oa_ref[rows(l), c0:c0 + POOL_GROUP] = (
                mixed * ps_ref[:, c0:c0 + POOL_GROUP] * _silu(gt)).astype(BF16)
    for j in range(POOL_HIST):
        src = j + nl
        if src < POOL_HIST:
            npool_ref[j] = sp_ref[src]
        else:
            npool_ref[j] = tok(src - POOL_HIST, OFF_U, D_POOL)

    hist = CONV_WIDTH - 1
    conv_out = {}
    for name, poff, coff, width in (("x", OFF_XS, 0, D_SSM), ("b", OFF_B, D_SSM, D_BC),
                                    ("c", OFF_C, D_SSM + D_BC, D_BC)):
        ext = [scv_ref[j, :, coff:coff + width] for j in range(hist)]
        ext += [tok(l, poff, width) for l in range(nl)]
        outs = []
        for l in range(nl):
            acc = cbias_ref[:, coff:coff + width]
            for k in range(CONV_WIDTH):
                acc = acc + cw_ref[k:k + 1, coff:coff + width] * ext[l + k]
            outs.append(_silu(acc))
        conv_out[name] = outs
        for j in range(hist):
            nconv_ref[j, :, coff:coff + width] = ext[j + nl]
    xs, bs, cs = conv_out["x"], conv_out["b"], conv_out["c"]
    for l in range(nl):
        cs_ref[rows(l), :] = cs[l]
        bs_ref[rows(l), :] = bs[l].astype(BF16)

    a_neg = -jnp.exp(alog_ref[...])
    dt, a_cum = [], []
    run = None
    for l in range(nl):
        d = _softplus(dtr_ref[rows(l), :] + dtb_ref[...])
        dt.append(d)
        run = d * a_neg if run is None else run + d * a_neg
        a_cum.append(run)
    cdec_ref[...] = jnp.exp(a_cum[nl - 1])

    def onehot_rows(mats, onehot):
        parts = [_split2(m) for m in mats]
        stack = jnp.concatenate([p[0] for p in parts] + [p[1] for p in parts], axis=0)
        res = _dot(stack, onehot)
        n = len(mats)
        return [res[k * sb:(k + 1) * sb] + res[(n + k) * sb:(n + k + 1) * sb] for k in range(n)]

    pairs = [(l, s) for l in range(nl) for s in range(l + 1)]
    cbh = onehot_rows([cs[l] * bs[s] for l, s in pairs], ghot_ref[...])
    gls = [c * jnp.exp(a_cum[l] - a_cum[s]) * dt[s] for c, (l, s) in zip(cbh, pairs)]
    w_end = [jnp.exp(a_cum[nl - 1] - a_cum[l]) * dt[l] for l in range(nl)]
    e_cum = [jnp.exp(a_cum[l]) for l in range(nl)]
    chunk = 4 * LANES
    for c0 in range(0, D_SSM, chunk):
        cl = slice(c0, c0 + chunk)
        ex = onehot_rows(gls + w_end + e_cum, ehot_ref[:, cl])
        g_ex, w_ex, e_ex = ex[:len(pairs)], ex[len(pairs):len(pairs) + nl], ex[len(pairs) + nl:]
        xc = [x[:, cl] for x in xs]
        for l in range(nl):
            ydiag = dskip_ref[:, cl] * xc[l]
            for k, (pl_, ps_) in enumerate(pairs):
                if pl_ == l:
                    ydiag = ydiag + g_ex[k] * xc[ps_]
            ydx_ref[rows(l), cl] = ydiag
            xw_ref[rows(l), cl] = w_ex[l] * xc[l]
            ecum_ref[rows(l), cl] = e_ex[l]


def _sample1(pj3, dtr3, sp3, scv3, pool_w, pool_scale, conv_w, conv_b, dt_bias_p, a_log_p,
             dskip_row, ehot, ghot, sb):
    nblk = pj3.shape[0]
    db = nblk * sb
    rb = DEC_SEQ * sb
    blk2 = lambda width: pl.BlockSpec((None, rb, width), lambda i: (i, 0, 0))
    hist3 = lambda n, width: pl.BlockSpec((n, sb, width), lambda i: (0, i, 0))
    const2 = lambda shape: pl.BlockSpec(shape, lambda i: (0, 0))
    blk_shape = lambda width, dt: jax.ShapeDtypeStruct((nblk, rb, width), dt)
    return pl.pallas_call(
        functools.partial(_sample1_kernel, sb=sb),
        grid=(nblk,),
        in_specs=[
            blk2(D_MAIN), blk2(LANES), hist3(POOL_HIST, D_POOL), hist3(CONV_WIDTH - 1, D_CONV),
            pl.BlockSpec((len(POOL_WINDOWS), POOL_GROUP, POOL_GROUP), lambda i: (0, 0, 0)),
            const2((1, D_POOL)), const2((CONV_WIDTH, D_CONV)), const2((1, D_CONV)),
            const2((1, LANES)), const2((1, LANES)), const2((1, D_SSM)),
            const2((LANES, D_SSM)), const2((D_BC, LANES)),
        ],
        out_specs=[
            blk2(D_POOL), hist3(POOL_HIST, D_POOL), hist3(CONV_WIDTH - 1, D_CONV),
            blk2(D_BC), blk2(D_BC), blk2(D_SSM), blk2(D_SSM), blk2(D_SSM),
            pl.BlockSpec((sb, LANES), lambda i: (i, 0)),
        ],
        out_shape=[
            blk_shape(D_POOL, BF16),
            jax.ShapeDtypeStruct((POOL_HIST, db, D_POOL), F32),
            jax.ShapeDtypeStruct((CONV_WIDTH - 1, db, D_CONV), F32),
            blk_shape(D_BC, F32), blk_shape(D_BC, BF16),
            blk_shape(D_SSM, F32), blk_shape(D_SSM, F32), blk_shape(D_SSM, F32),
            jax.ShapeDtypeStruct((db, LANES), F32),
        ],
        compiler_params=pltpu.CompilerParams(
            dimension_semantics=("arbitrary",),
            vmem_limit_bytes=VMEM_LIMIT),
        name="sample_elementwise",
    )(pj3, dtr3, sp3, scv3, pool_w, pool_scale, conv_w, conv_b, dt_bias_p, a_log_p, dskip_row,
      ehot, ghot)


def _sample2_setup(xw_ref, cs_ref, xwt_ref, *, sb, per):
    j = pl.program_id(1)

    @pl.when(j == 0)
    def _():
        for blk in range(D_SSM // LANES):
            xwt_ref[blk * LANES:(blk + 1) * LANES, :] = jnp.transpose(
                xw_ref[:, blk * LANES:(blk + 1) * LANES]).astype(BF16)

    return [jnp.concatenate([cs_ref[pl.ds(l * sb + per * j + 2 * pr + bi, 1), :]
                             for bi in range(2) for l in range(DEC_SEQ)], axis=0).astype(BF16)
            for pr in range(per // 2)]


def _sample2_group(g, refs, c_rows, *, sb, per):
    cdec_ref, st_ref, _, bs_ref, _, nst_ref, yo_ref, xwt_ref = refs
    i = pl.program_id(0)
    j = pl.program_id(1)
    nl = DEC_SEQ
    rows = sb * nl
    seq_of_row = lax.broadcasted_iota(jnp.int32, (rows, 1), 0) & (sb - 1)
    gw = HEADS_PER_GROUP * HEAD_DIM
    gs = slice(g * D_STATE, (g + 1) * D_STATE)
    for pr in range(per // 2):
        q0 = per * j + 2 * pr
        c8 = c_rows[pr][:, gs]
        b_blk = bs_ref[:, gs]
        zero_b = jnp.zeros_like(b_blk)
        w2 = jnp.concatenate([jnp.where(seq_of_row == q0, b_blk, zero_b),
                              jnp.where(seq_of_row == q0 + 1, b_blk, zero_b)], axis=1)
        u2 = _dot(xwt_ref[g * gw:(g + 1) * gw, :], w2)
        for bi in range(2):
            sq = 2 * pr + bi
            s0 = st_ref[sq, g * HEADS_PER_GROUP:(g + 1) * HEADS_PER_GROUP].reshape(gw, D_STATE)
            yo = _dot_nt(c8, s0.astype(BF16))
            base = (i * sb + q0 + bi) * HEADS + g * HEADS_PER_GROUP
            for r in range(HEADS_PER_GROUP):
                dec = cdec_ref[base + r]
                rs = slice(r * HEAD_DIM, (r + 1) * HEAD_DIM)
                nst_ref[sq, g * HEADS_PER_GROUP + r] = (
                    s0[rs] * dec + u2[rs, bi * D_STATE:(bi + 1) * D_STATE])
            for l in range(nl):
                yo_ref[pl.ds(l * sb + q0 + bi, 1), g * gw:(g + 1) * gw] = (
                    yo[bi * nl + l:bi * nl + l + 1])


N_SSD_IN, N_SSD_OUT, N_SSD_SCRATCH = 11, 3, 6
N_ST_IN, N_ST_OUT = 5, 2


def _ssd_state_kernel(*refs, nc, sb, per):
    n_in = N_SSD_IN + N_ST_IN + 1
    ins, rest = refs[:n_in], refs[n_in:]
    n_out = N_SSD_OUT + N_ST_OUT + 1
    outs, scratch = rest[:n_out], rest[n_out:]
    outs[-1][...] = ins[-1][...].astype(BF16)
    st_refs = (*ins[N_SSD_IN:N_SSD_IN + N_ST_IN], *outs[N_SSD_OUT:N_SSD_OUT + N_ST_OUT],
               *scratch[N_SSD_SCRATCH:])
    c_rows = _sample2_setup(st_refs[2], st_refs[4], st_refs[7], sb=sb, per=per)
    _ssd_body(*ins[:N_SSD_IN], *outs[:N_SSD_OUT], *scratch[:N_SSD_SCRATCH], nc=nc,
              after_group=functools.partial(_sample2_group, refs=st_refs, c_rows=c_rows,
                                            sb=sb, per=per))


def _ssd_prompt_sample_state(proj, dt_raw, conv_w, conv_b, dt_bias_p, a_log_p, dskip_row, norm_g,
                             bsz, seq, cdec_flat, state, xw3, bs3, cs3, sb, w_out2d):
    q = CHUNK
    nc = seq // q
    db = state.shape[0]
    rb = DEC_SEQ * sb
    per = sb // nc
    assert rb == LANES and db // sb == bsz and per * nc == sb and per % 2 == 0
    d_mix = w_out2d.shape[0]
    slab = d_mix // (bsz * nc)
    assert slab * bsz * nc == d_mix and slab % (2 * SUBLANES) == 0
    wslab = pl.BlockSpec((slab, D_MODEL), lambda b, c: (b * nc + c, 0))
    row = lambda b, c: b * nc + c
    const = lambda b, c: (0, 0)
    blk2 = lambda width: pl.BlockSpec((None, rb, width), lambda i, j: (i, 0, 0))
    st_spec = pl.BlockSpec((per, HEADS, HEAD_DIM, D_STATE), lambda i, j: (i * nc + j, 0, 0, 0))
    return pl.pallas_call(
        functools.partial(_ssd_state_kernel, nc=nc, sb=sb, per=per),
        grid=(bsz, nc),
        in_specs=[
            pl.BlockSpec((q, D_SSM), lambda b, c: (row(b, c), OFF_Z // D_SSM)),
            pl.BlockSpec((q, D_SSM), lambda b, c: (row(b, c), OFF_XS // D_SSM)),
            pl.BlockSpec((q, D_BC), lambda b, c: (row(b, c), OFF_B // D_BC)),
            pl.BlockSpec((q, D_BC), lambda b, c: (row(b, c), OFF_C // D_BC)),
            pl.BlockSpec((q, LANES), lambda b, c: (row(b, c), 0)),
            pl.BlockSpec((CONV_WIDTH, D_CONV), const),
            pl.BlockSpec((1, D_CONV), const),
            pl.BlockSpec((1, LANES), const),
            pl.BlockSpec((1, LANES), const),
            pl.BlockSpec((1, D_SSM), const),
            pl.BlockSpec((1, D_SSM), const),
            pl.BlockSpec(memory_space=pltpu.SMEM),
            st_spec, blk2(D_SSM), blk2(D_BC), blk2(D_BC),
            wslab,
        ],
        out_specs=[
            pl.BlockSpec((q, D_SSM), lambda b, c: (row(b, c), 0)),
            pl.BlockSpec((None, CONV_WIDTH - 1, D_CONV), lambda b, c: (b, 0, 0)),
            pl.BlockSpec((None, HEADS, HEAD_DIM, D_STATE), lambda b, c: (b, 0, 0, 0)),
            st_spec, blk2(D_SSM),
            wslab,
        ],
        out_shape=[
            jax.ShapeDtypeStruct((bsz * seq, D_SSM), BF16),
            jax.ShapeDtypeStruct((bsz, CONV_WIDTH - 1, D_CONV), F32),
            jax.ShapeDtypeStruct((bsz, HEADS, HEAD_DIM, D_STATE), F32),
            jax.ShapeDtypeStruct(state.shape, F32),
            jax.ShapeDtypeStruct((db // sb, rb, D_SSM), F32),
            jax.ShapeDtypeStruct(w_out2d.shape, BF16),
        ],
        scratch_shapes=[
            pltpu.VMEM((D_CONV // LANES, q + SUBLANES, LANES), F32),
            pltpu.VMEM((D_STATE, D_SSM), F32),
            pltpu.VMEM((D_SSM // LANES, q + SUBLANES, LANES), F32),
            pltpu.VMEM((D_BC // LANES, q + SUBLANES, LANES), F32),
            pltpu.VMEM((D_BC // LANES, q + SUBLANES, LANES), F32),
            pltpu.VMEM((q, D_SSM), F32),
            pltpu.VMEM((D_SSM, rb), BF16),
        ],
        compiler_params=pltpu.CompilerParams(
            dimension_semantics=("arbitrary", "arbitrary"),
            vmem_limit_bytes=VMEM_LIMIT),
        name="ssd_prompt_sample_state",
    )(proj, proj, proj, proj, dt_raw, conv_w, conv_b, dt_bias_p, a_log_p, dskip_row, norm_g,
      cdec_flat, state, xw3, bs3, cs3, w_out2d)


def _outproj_sample_kernel(a_ref, yo_ref, ecum_ref, ydx_ref, z_ref, ng_ref, w_ref, x_ref,
                           g_ref, y_ref, b_ref):
    gw = D_SSM // GROUPS
    for g in range(GROUPS):
        cs = slice(g * gw, (g + 1) * gw)
        y = ydx_ref[:, cs] + ecum_ref[:, cs] * yo_ref[:, cs]
        yz = y * _silu(z_ref[:, cs].astype(F32))
        ms = jnp.sum(yz * yz, axis=-1, keepdims=True) * (1.0 / gw)
        b_ref[:, cs] = (yz * lax.rsqrt(ms + EPS) * ng_ref[:, cs]).astype(BF16)
    _outproj_kernel(a_ref, b_ref, w_ref, x_ref, g_ref, y_ref)


def _outproj_sample(out_a, yo, ecum, ydx, proj, norm_g, w_bf, x3d, final_g):
    db, nl, _ = x3d.shape
    m = db * nl
    tm = min(OUTPROJ_SAMPLE_TM, m)
    assert tm % (nl * SAMPLE_BLOCK) == 0
    seqblk = pl.BlockSpec((tm // nl, nl, D_MODEL), lambda i: (i, 0, 0))
    rowblk = lambda width: pl.BlockSpec((tm, width), lambda i: (i, 0))
    const = lambda shape, **kw: pl.BlockSpec(shape, lambda i: (0, 0), **kw)
    return pl.pallas_call(
        _outproj_sample_kernel,
        grid=(m // tm,),
        in_specs=[
            rowblk(D_POOL), rowblk(D_SSM), rowblk(D_SSM), rowblk(D_SSM),
            pl.BlockSpec((tm, D_SSM), lambda i: (i, OFF_Z // D_SSM)),
            const((1, D_SSM)),
            const((D_POOL + D_SSM, D_MODEL), pipeline_mode=pl.Buffered(1)),
            seqblk, const((1, D_MODEL)),
        ],
        out_specs=seqblk,
        out_shape=jax.ShapeDtypeStruct((db, nl, D_MODEL), F32),
        scratch_shapes=[pltpu.VMEM((tm, D_SSM), BF16)],
        compiler_params=pltpu.CompilerParams(
            dimension_semantics=("arbitrary",),
            vmem_limit_bytes=VMEM_LIMIT),
        name="outproj_sample",
    )(out_a, yo, ecum, ydx, proj, norm_g, w_bf, x3d, final_g)


def kernel(x_prompt, x_sample, state_pool, state_conv, state_ssm, norm_g, w_in, conv_w, conv_b,
           dt_bias, a_log, d_skip, ssm_norm_g, pool_w, pool_scale, w_out, final_g):
    bsz, seq, _ = x_prompt.shape
    db, nl, _ = x_sample.shape
    assert nl == DEC_SEQ and seq % CHUNK == 0 and w_in.shape[0] == 1

    w_t = jnp.transpose(w_in[0])
    pool_w_b = pool_w[0]
    g_in = norm_g[0][None, :]
    g_fin = final_g[None, :]
    ps = pool_scale[0][None, :]
    cw = conv_w[0]
    cbias = conv_b[0][None, :]
    pad_h = lambda v: jnp.pad(v, (0, LANES - HEADS))[None, :]
    dtb = pad_h(dt_bias[0])
    alog = pad_h(a_log[0])
    dskip_row = jnp.repeat(d_skip[0], HEAD_DIM)[None, :]
    ng = ssm_norm_g[0][None, :]

    sb = SAMPLE_BLOCK
    nblk = db // sb
    rb = nl * sb
    proj_s, dt_s, w_bf, wdt_bf = _inproj(x_sample, g_in, w_t, w_t, PERM_BLOCK)
    pj3 = proj_s.reshape(nblk, rb, D_MAIN)
    head_of_ch = jnp.arange(D_SSM, dtype=jnp.int32) // HEAD_DIM
    ehot = (jnp.arange(LANES, dtype=jnp.int32)[:, None] == head_of_ch[None, :]).astype(BF16)
    grp_of_row = jnp.arange(D_BC, dtype=jnp.int32) // D_STATE
    head_id = jnp.arange(LANES, dtype=jnp.int32)
    ghot = ((head_id[None, :] // HEADS_PER_GROUP == grp_of_row[:, None])
            & (head_id[None, :] < HEADS)).astype(BF16)
    (oa_s, npool_s, nconv_s, cs_s, bs_s, xw_s, ydx_s, ecum_s, cdec_s) = _sample1(
        pj3, dt_s.reshape(nblk, rb, LANES), jnp.transpose(state_pool[0], (1, 0, 2)),
        jnp.transpose(state_conv[0], (1, 0, 2)), pool_w_b, ps, cw, cbias, dtb, alog,
        dskip_row, ehot, ghot, sb)

    xp2 = x_prompt.reshape(bsz * seq, D_MODEL)
    proj_p, dt_p = _inproj(xp2, g_in, w_bf, wdt_bf, PROMPT_TN)
    ob_p, nconv_p, nssm_p, nssm_s, yo_s, w_out_bf = _ssd_prompt_sample_state(
        proj_p, dt_p, cw, cbias, dtb, alog, dskip_row, ng, bsz, seq,
        cdec_s[:, :HEADS].reshape(db * HEADS), state_ssm[0], xw_s, bs_s, cs_s, sb, w_out[0])
    y_p, npool_p = _pool_outproj_prompt(proj_p, pool_w_b, ps, ob_p, w_out_bf, xp2, g_fin, bsz, seq)
    y_p = y_p.reshape(bsz, seq, D_MODEL)

    flat = lambda t: t.reshape(db * nl, t.shape[-1])
    y_s = _outproj_sample(flat(oa_s), flat(yo_s), flat(ecum_s), flat(ydx_s), proj_s, ng, w_out_bf,
                          x_sample, g_fin)

    return (y_p, y_s,
            npool_p[None], nconv_p[None], nssm_p[None],
            jnp.transpose(npool_s, (1, 0, 2))[None],
            jnp.transpose(nconv_s, (1, 0, 2))[None],
            nssm_s[None])
```

```python
import functools

import jax
import jax.numpy as jnp
from jax import lax
from jax.experimental import pallas as pl
from jax.experimental.pallas import tpu as pltpu

F32 = jnp.float32
BF16 = jnp.bfloat16

D_MODEL = 2048
D_POOL = 1024
POOL_WINDOWS = (2, 4, 8, 16)
POOL_GROUP = 256
POOL_HIST = 15
D_SSM = 3072
HEAD_DIM = 64
HEADS = 48
GROUPS = 8
HEADS_PER_GROUP = 6
D_STATE = 128
D_BC = GROUPS * D_STATE
CONV_WIDTH = 4
D_CONV = D_SSM + 2 * D_BC
D_MAIN = 2 * D_POOL + D_SSM + D_CONV
PAST_LEN = 16384
DEC_SEQ = 4
EPS = 1e-5

LANES = 128
SUBLANES = 8
VMEM_LIMIT = 56 * 1024 * 1024

OFF_Z = 0
OFF_XS = D_SSM
OFF_U = 2 * D_SSM
OFF_GATE = OFF_U + D_POOL
OFF_B = OFF_GATE + D_POOL
OFF_C = OFF_B + D_BC

PERM_BLOCK = 1024
INPROJ_TM = 1024
OUTPROJ_TM = 512
OUTPROJ_SAMPLE_TM = 256
PROMPT_TN = 2048
W_RING = 3
CHUNK = 128
SAMPLE_BLOCK = 32
NEG_BIG = -1e30
LOG2E = 1.4426950408889634


def _silu(v):
    h = 0.5 * v
    return h + h * jnp.tanh(h)


def _softplus(v):
    y = jnp.exp(-jnp.abs(v))
    u = 1.0 + y
    d = u - 1.0
    l1p = jnp.where(d == 0.0, y, jnp.log(u) * (y / jnp.where(d == 0.0, 1.0, d)))
    return jnp.maximum(v, 0.0) + l1p


def _split2(v):
    hi = v.astype(BF16)
    lo = (v - hi.astype(F32)).astype(BF16)
    return hi, lo


def _dot(a, b):
    return jnp.dot(a, b, preferred_element_type=F32)


def _dot_nt(a, b):
    return lax.dot_general(a, b, (((1,), (1,)), ((), ())), preferred_element_type=F32)


def _dot2(v, onehot):
    hi, lo = _split2(v)
    return _dot(hi, onehot) + _dot(lo, onehot)


def _inproj_kernel(x_ref, g_ref, w_ref, wdt_ref, o_ref, dt_ref, *rest):
    from_f32 = w_ref.dtype == F32
    (wb_out_ref, wdt_out_ref), rest = (rest[:2], rest[2:]) if from_f32 else ((None, None), rest)
    h_ref = rest[0]
    j = pl.program_id(1)

    if from_f32:
        wbuf, sem = rest[2], rest[3]
        tn = wbuf.shape[1]
        nj = pl.num_programs(1)

        def w_copy(step):
            slot = step % W_RING
            src = pl.multiple_of(_src_block(step) * tn, tn)
            return pltpu.make_async_copy(w_ref.at[pl.ds(src, tn), :], wbuf.at[slot], sem.at[slot])

        @pl.when(j == 0)
        def _():
            for s in range(W_RING - 1):
                w_copy(s).start()

        @pl.when(j + (W_RING - 1) < nj)
        def _():
            w_copy(j + (W_RING - 1)).start()

    @pl.when(j == 0)
    def _():
        if len(x_ref.shape) == 3:
            rows_ref = rest[1]
            sb = SAMPLE_BLOCK
            for b in range(x_ref.shape[0] // sb):
                for t in range(DEC_SEQ):
                    r0 = (b * DEC_SEQ + t) * sb
                    rows_ref[r0:r0 + sb, :] = x_ref[b * sb:(b + 1) * sb, t, :]
            x = rows_ref[...]
        else:
            x = x_ref[...]
        ms = jnp.mean(x * x, axis=-1, keepdims=True)
        h = (x * lax.rsqrt(ms + EPS) * g_ref[...]).astype(BF16)
        h_ref[...] = h
        if from_f32:
            wrow = lax.broadcasted_iota(jnp.int32, wdt_ref.shape, 0)
            wdt = jnp.where(wrow < HEADS, wdt_ref[...], 0.0).astype(BF16)
            wdt_out_ref[...] = wdt
        else:
            wdt = wdt_ref[...]
        dt_ref[...] = _dot_nt(h, wdt)

    if from_f32:
        w_copy(j).wait()
        wb = wbuf[j % W_RING].astype(BF16)
        wb_out_ref[...] = wb
    else:
        wb = w_ref[...]
    o_ref[...] = _dot_nt(h_ref[...], wb).astype(BF16)


def _src_block(j):
    nz = (D_SSM + D_SSM) // PERM_BLOCK
    npool = 2 * D_POOL // PERM_BLOCK
    return jnp.where(j < nz, j + npool, jnp.where(j < nz + npool, j - nz, j))


def _inproj(x2d, norm_g, w, w_dt, tn):
    whole = x2d.ndim == 3
    m = x2d.shape[0] * x2d.shape[1] if whole else x2d.shape[0]
    tm = min(INPROJ_TM, m)
    assert m == tm or not whole
    x_spec = (pl.BlockSpec(x2d.shape, lambda i, j: (0, 0, 0), pipeline_mode=pl.Buffered(1))
              if whole else pl.BlockSpec((tm, D_MODEL), lambda i, j: (i, 0)))
    scratch = [pltpu.VMEM((tm, D_MODEL), BF16)]
    from_f32 = w.dtype == F32
    assert whole == from_f32
    assert tn == PERM_BLOCK or not from_f32
    if from_f32:
        scratch += [pltpu.VMEM((tm, D_MODEL), F32),
                    pltpu.VMEM((W_RING, tn, D_MODEL), F32),
                    pltpu.SemaphoreType.DMA((W_RING,))]
    w_spec = (pl.BlockSpec(memory_space=pl.ANY) if from_f32
              else pl.BlockSpec((tn, D_MODEL), lambda i, j: (j, 0)))
    wdt_map = (lambda i, j: (D_MAIN // LANES, 0)) if from_f32 else (lambda i, j: (0, 0))
    out_specs = [
        pl.BlockSpec((tm, tn), lambda i, j: (i, j)),
        pl.BlockSpec((tm, LANES), lambda i, j: (i, 0)),
    ]
    out_shape = [
        jax.ShapeDtypeStruct((m, D_MAIN), BF16),
        jax.ShapeDtypeStruct((m, LANES), F32),
    ]
    if from_f32:
        assert m == tm
        out_specs += [pl.BlockSpec((tn, D_MODEL), lambda i, j: (j, 0)),
                      pl.BlockSpec((LANES, D_MODEL), lambda i, j: (0, 0))]
        out_shape += [jax.ShapeDtypeStruct((D_MAIN, D_MODEL), BF16),
                      jax.ShapeDtypeStruct((LANES, D_MODEL), BF16)]
    return pl.pallas_call(
        _inproj_kernel,
        grid=(m // tm, D_MAIN // tn),
        in_specs=[
            x_spec,
            pl.BlockSpec((1, D_MODEL), lambda i, j: (0, 0)),
            w_spec,
            pl.BlockSpec((LANES, D_MODEL), wdt_map),
        ],
        out_specs=out_specs,
        out_shape=out_shape,
        scratch_shapes=scratch,
        compiler_params=pltpu.CompilerParams(
            dimension_semantics=("arbitrary", "arbitrary"),
            vmem_limit_bytes=VMEM_LIMIT),
        name="inproj",
    )(x2d, norm_g, w, w_dt)


def _outproj_kernel(a_ref, b_ref, w_ref, x_ref, g_ref, y_ref):
    acc = (_dot(a_ref[...], w_ref[0:D_POOL, :])
           + _dot(b_ref[...], w_ref[D_POOL:D_POOL + D_SSM, :]))

    def residual_norm(x, part):
        r = x + part
        ms = jnp.mean(r * r, axis=-1, keepdims=True)
        return r * lax.rsqrt(ms + EPS) * g_ref[...]

    if len(x_ref.shape) == 3:
        sb = SAMPLE_BLOCK
        for b in range(x_ref.shape[0] // sb):
            for t in range(DEC_SEQ):
                r0 = (b * DEC_SEQ + t) * sb
                y_ref[b * sb:(b + 1) * sb, t, :] = residual_norm(
                    x_ref[b * sb:(b + 1) * sb, t, :], acc[r0:r0 + sb])
    else:
        y_ref[...] = residual_norm(x_ref[...], acc)


def _pool_kernel(u_ref, gate_ref, pw_ref, ps_ref, oa_ref, np_ref, ubuf, wbuf, *, tl, nt):
    t = pl.program_id(1)
    hist = POOL_HIST + 1
    nrows = hist + tl
    nv = nrows // SUBLANES
    bpg = POOL_GROUP // LANES

    @pl.when(t == 0)
    def _():
        ubuf[:, 0:hist, :] = jnp.zeros((ubuf.shape[0], hist, LANES), F32)

    pos = t * tl + lax.broadcasted_iota(jnp.int32, (tl, 1), 0)
    for gi, w in enumerate(POOL_WINDOWS):
        win = []
        for lb in range(gi * bpg, (gi + 1) * bpg):
            ubuf[lb, hist:nrows, :] = u_ref[:, lb * LANES:(lb + 1) * LANES].astype(F32)
            acc = [ubuf[lb, pl.ds(a, SUBLANES, stride=nv), :] for a in range(nv)]
            span = 1
            while span < w:
                wrap = [pltpu.roll(acc[nv - span + a], 1, 0) for a in range(span)]
                acc = [acc[a] + (acc[a - span] if a >= span else wrap[a]) for a in range(nv)]
                span *= 2
            for a in range(nv):
                wbuf[lb, pl.ds(a, SUBLANES, stride=nv), :] = acc[a]
            win.append(wbuf[lb, hist:nrows, :])
        cs = slice(gi * POOL_GROUP, (gi + 1) * POOL_GROUP)
        cnt = jnp.minimum(w, pos + 1).astype(F32)
        ug = jnp.concatenate([ubuf[lb, hist:nrows, :] for lb in range(gi * bpg, (gi + 1) * bpg)],
                             axis=1)
        pooled = jnp.concatenate(win, axis=1) / cnt - ug
        mixed = _dot(pooled.astype(BF16), pw_ref[gi].astype(BF16))
        gt = gate_ref[:, cs].astype(F32)
        oa_ref[:, cs] = (mixed * ps_ref[:, cs] * _silu(gt)).astype(BF16)

    @pl.when(t == nt - 1)
    def _():
        for lb in range(D_POOL // LANES):
            np_ref[:, lb * LANES:(lb + 1) * LANES] = ubuf[lb, tl + 1:nrows, :]

    ubuf[:, 0:hist, :] = ubuf[:, tl:nrows, :]


def _pool_outproj_kernel(u_ref, gate_ref, pw_ref, ps_ref, b_ref, w_ref, x_ref, g_ref,
                         y_ref, np_ref, ubuf, wbuf, a_ref, *, tl, nt):
    _pool_kernel(u_ref, gate_ref, pw_ref, ps_ref, a_ref, np_ref, ubuf, wbuf, tl=tl, nt=nt)
    _outproj_kernel(a_ref, b_ref, w_ref, x_ref, g_ref, y_ref)


def _pool_outproj_prompt(proj, pool_w, pool_scale, out_b, w_bf, x2d, final_g, bsz, seq):
    tl = min(OUTPROJ_TM, seq)
    nt = seq // tl
    row = lambda b, t: b * nt + t
    const = lambda b, t: (0, 0)
    return pl.pallas_call(
        functools.partial(_pool_outproj_kernel, tl=tl, nt=nt),
        grid=(bsz, nt),
        in_specs=[
            pl.BlockSpec((tl, D_POOL), lambda b, t: (row(b, t), OFF_U // D_POOL)),
            pl.BlockSpec((tl, D_POOL), lambda b, t: (row(b, t), OFF_GATE // D_POOL)),
            pl.BlockSpec((len(POOL_WINDOWS), POOL_GROUP, POOL_GROUP), lambda b, t: (0, 0, 0)),
            pl.BlockSpec((1, D_POOL), const),
            pl.BlockSpec((tl, D_SSM), lambda b, t: (row(b, t), 0)),
            pl.BlockSpec((D_POOL + D_SSM, D_MODEL), const, pipeline_mode=pl.Buffered(1)),
            pl.BlockSpec((tl, D_MODEL), lambda b, t: (row(b, t), 0)),
            pl.BlockSpec((1, D_MODEL), const),
        ],
        out_specs=[
            pl.BlockSpec((tl, D_MODEL), lambda b, t: (row(b, t), 0)),
            pl.BlockSpec((None, POOL_HIST, D_POOL), lambda b, t: (b, 0, 0)),
        ],
        out_shape=[
            jax.ShapeDtypeStruct((bsz * seq, D_MODEL), F32),
            jax.ShapeDtypeStruct((bsz, POOL_HIST, D_POOL), F32),
        ],
        scratch_shapes=[pltpu.VMEM((D_POOL // LANES, tl + POOL_HIST + 1, LANES), F32),
                        pltpu.VMEM((D_POOL // LANES, tl + POOL_HIST + 1, LANES), F32),
                        pltpu.VMEM((tl, D_POOL), BF16)],
        compiler_params=pltpu.CompilerParams(
            dimension_semantics=("arbitrary", "arbitrary"),
            vmem_limit_bytes=VMEM_LIMIT),
        name="pool_outproj",
    )(proj, proj, pool_w, pool_scale, out_b, w_bf, x2d, final_g)


def _ssd_body(z_ref, xs_ref, b_ref, c_ref, dtr_ref, cw_ref, cbias_ref, dtb_ref, alog_ref,
              dskip_ref, ng_ref,
              ob_ref, nconv_ref, nssm_ref,
              cbuf, st_ref, cvx_ref, cvb_ref, cvc_ref, y_ref, *, nc, after_group=None):
    q = CHUNK
    c_idx = pl.program_id(1)
    halo = SUBLANES

    nbx, nbb = D_SSM // LANES, D_BC // LANES

    @pl.when(c_idx == 0)
    def _():
        cbuf[:, 0:halo, :] = jnp.zeros((cbuf.shape[0], halo, LANES), F32)
        st_ref[...] = jnp.zeros(st_ref.shape, F32)

    nv = (halo + q) // SUBLANES
    for blk in range(D_CONV // LANES):
        ls = slice(blk * LANES, (blk + 1) * LANES)
        if blk < nbx:
            src_ref, off, dst = xs_ref, blk * LANES, cvx_ref.at[blk]
        elif blk < nbx + nbb:
            src_ref, off, dst = b_ref, (blk - nbx) * LANES, cvb_ref.at[blk - nbx]
        else:
            src_ref, off, dst = c_ref, (blk - nbx - nbb) * LANES, cvc_ref.at[blk - nbx - nbb]
        cbuf[blk, halo:halo + q, :] = src_ref[:, off:off + LANES].astype(F32)
        xv = [cbuf[blk, pl.ds(a, SUBLANES, stride=nv), :] for a in range(nv)]
        wrap = [pltpu.roll(xv[nv - k], 1, 0) for k in range(1, CONV_WIDTH)]
        taps = [0.5 * cw_ref[k:k + 1, ls] for k in range(CONV_WIDTH)]
        bias = 0.5 * cbias_ref[:, ls]
        for a in range(nv):
            h = bias + taps[CONV_WIDTH - 1] * xv[a]
            for k in range(1, CONV_WIDTH):
                src = xv[a - k] if a >= k else wrap[k - a - 1]
                h = h + taps[CONV_WIDTH - 1 - k] * src
            dst[pl.ds(a, SUBLANES, stride=nv), :] = h + h * jnp.tanh(h)
    rows = slice(halo, halo + q)

    dt = _softplus(dtr_ref[...] + dtb_ref[...])
    a_neg = -jnp.exp(alog_ref[...])
    da = dt * (a_neg * LOG2E)
    row = lax.broadcasted_iota(jnp.int32, (q, LANES), 0)
    a2 = da
    shift = 1
    while shift < q:
        a2 = a2 + jnp.where(row >= shift, pltpu.roll(a2, shift, 0), 0.0)
        shift *= 2
    a2_t = jnp.transpose(a2)
    ldt_t = jnp.log2(jnp.transpose(dt))
    a2_end_t = a2_t[:, q - 1:q]
    w_t = jnp.exp2(a2_end_t - a2_t + ldt_t)
    cdec_t = jnp.exp2(a2_end_t)
    srow_t = a2_t - ldt_t
    ea = jnp.exp2(a2)

    li = lax.broadcasted_iota(jnp.int32, (q, q), 0)
    si = lax.broadcasted_iota(jnp.int32, (q, q), 1)
    tri = li >= si
    lane = lax.broadcasted_iota(jnp.int32, (q, LANES), 1)
    lo_half = lane < HEAD_DIM

    for g in range(GROUPS):
        gs = slice(g * D_STATE, (g + 1) * D_STATE)
        c_gb = cvc_ref[g, rows, :].astype(BF16)
        b_g = cvb_ref[g, rows, :]
        cb = _dot_nt(c_gb, b_g.astype(BF16))
        b_t = jnp.transpose(b_g)
        ppg = HEADS_PER_GROUP // 2
        gl = slice(g * ppg * LANES, (g + 1) * ppg * LANES)
        y_off = _dot(c_gb, st_ref[:, gl].astype(BF16))
        for j in range(ppg):
            blk = g * ppg + j
            ls = slice(blk * LANES, (blk + 1) * LANES)
            sc, bw, ecol, dec = [], [], [], []
            for h in (2 * blk, 2 * blk + 1):
                a_col = jnp.broadcast_to(a2[:, h:h + 1], (q, q))
                decay_dt = jnp.exp2(jnp.where(tri, a_col - srow_t[h:h + 1, :], NEG_BIG))
                sc.append((cb * decay_dt).astype(BF16))
                bw.append((b_t * w_t[h:h + 1, :]).astype(BF16))
                ecol.append(jnp.broadcast_to(ea[:, h:h + 1], (q, LANES)))
                dec.append(jnp.broadcast_to(cdec_t[h:h + 1, :], (q, LANES)))
            x_p = cvx_ref[blk, rows, :].astype(BF16)
            zero_b = jnp.zeros_like(x_p)
            x_bd = jnp.concatenate([jnp.where(lo_half, x_p, zero_b),
                                    jnp.where(lo_half, zero_b, x_p)], axis=0)
            lhs1 = jnp.concatenate([jnp.concatenate(sc, axis=1),
                                    jnp.concatenate(bw, axis=1)], axis=0)
            r1 = _dot(lhs1, x_bd)
            y_ref[:, ls] = (r1[0:q] + y_off[:, j * LANES:(j + 1) * LANES]
                            * jnp.where(lo_half, ecol[0], ecol[1]))
            st_ref[:, ls] = (st_ref[:, ls] * jnp.where(lo_half, dec[0], dec[1])
                             + r1[q:2 * q])
        if after_group is not None:
            after_group(g)

    gw = D_SSM // GROUPS
    bpg = gw // LANES
    for g in range(GROUPS):
        yz, ssq = [], None
        for blk in range(g * bpg, (g + 1) * bpg):
            ls = slice(blk * LANES, (blk + 1) * LANES)
            y = y_ref[:, ls] + cvx_ref[blk, rows, :] * dskip_ref[:, ls]
            v = y * _silu(z_ref[:, ls].astype(F32))
            yz.append(v)
            ssq = v * v if ssq is None else ssq + v * v
        scale = lax.rsqrt(jnp.sum(ssq, axis=-1, keepdims=True) * (1.0 / gw) + EPS)
        for i, blk in enumerate(range(g * bpg, (g + 1) * bpg)):
            ls = slice(blk * LANES, (blk + 1) * LANES)
            ob_ref[:, ls] = (yz[i] * scale * ng_ref[:, ls]).astype(BF16)

    cbuf[:, 0:halo, :] = cbuf[:, q:q + halo, :]

    @pl.when(c_idx == nc - 1)
    def _():
        for blk in range(D_CONV // LANES):
            nconv_ref[:, blk * LANES:(blk + 1) * LANES] = cbuf[blk, q + halo - 3:q + halo, :]
        for blk in range(D_SSM // LANES):
            t = jnp.transpose(st_ref[:, blk * LANES:(blk + 1) * LANES])
            nssm_ref[2 * blk:2 * blk + 2] = t.reshape(2, HEAD_DIM, D_STATE)


def _sample1_kernel(pj_ref, dtr_ref, sp_ref, scv_ref, pw_ref, ps_ref, cw_ref, cbias_ref, dtb_ref,
                    alog_ref, dskip_ref, ehot_ref, ghot_ref,
                    oa_ref, npool_ref, nconv_ref, cs_ref, bs_ref, xw_ref, ydx_ref, ecum_ref,
                    cdec_ref, *, sb):
    nl = DEC_SEQ

    def rows(l):
        return slice(l * sb, (l + 1) * sb)

    def tok(l, off, width):
        return pj_ref[rows(l), off:off + width].astype(F32)

    for gi, w in enumerate(POOL_WINDOWS):
        c0 = gi * POOL_GROUP
        ext = [sp_ref[j, :, c0:c0 + POOL_GROUP] for j in range(POOL_HIST)]
        ext += [tok(l, OFF_U + c0, POOL_GROUP) for l in range(nl)]
        for l in range(nl):
            acc = ext[POOL_HIST + l]
            for k in range(1, w):
                acc = acc + ext[POOL_HIST + l - k]
            cnt = float(min(w, PAST_LEN + l + 1))
            pooled = acc / cnt - ext[POOL_HIST + l]
            mixed = _dot(pooled.astype(BF16), pw_ref[gi].astype(BF16))
            gt = tok(l, OFF_GATE + c0, POOL_GROUP)
            oa_ref[rows(l), c0:c0 + POOL_GROUP] = (
                mixed * ps_ref[:, c0:c0 + POOL_GROUP] * _silu(gt)).astype(BF16)
    for j in range(POOL_HIST):
        src = j + nl
        if src < POOL_HIST:
            npool_ref[j] = sp_ref[src]
        else:
            npool_ref[j] = tok(src - POOL_HIST, OFF_U, D_POOL)

    hist = CONV_WIDTH - 1
    conv_out = {}
    for name, poff, coff, width in (("x", OFF_XS, 0, D_SSM), ("b", OFF_B, D_SSM, D_BC),
                                    ("c", OFF_C, D_SSM + D_BC, D_BC)):
        ext = [scv_ref[j, :, coff:coff + width] for j in range(hist)]
        ext += [tok(l, poff, width) for l in range(nl)]
        outs = []
        for l in range(nl):
            acc = cbias_ref[:, coff:coff + width]
            for k in range(CONV_WIDTH):
                acc = acc + cw_ref[k:k + 1, coff:coff + width] * ext[l + k]
            outs.append(_silu(acc))
        conv_out[name] = outs
        for j in range(hist):
            nconv_ref[j, :, coff:coff + width] = ext[j + nl]
    xs, bs, cs = conv_out["x"], conv_out["b"], conv_out["c"]
    for l in range(nl):
        cs_ref[rows(l), :] = cs[l]
        bs_ref[rows(l), :] = bs[l].astype(BF16)

    a_neg = -jnp.exp(alog_ref[...])
    dt, a_cum = [], []
    run = None
    for l in range(nl):
        d = _softplus(dtr_ref[rows(l), :] + dtb_ref[...])
        dt.append(d)
        run = d * a_neg if run is None else run + d * a_neg
        a_cum.append(run)
    cdec_ref[...] = jnp.exp(a_cum[nl - 1])

    def onehot_rows(mats, onehot):
        parts = [_split2(m) for m in mats]
        stack = jnp.concatenate([p[0] for p in parts] + [p[1] for p in parts], axis=0)
        res = _dot(stack, onehot)
        n = len(mats)
        return [res[k * sb:(k + 1) * sb] + res[(n + k) * sb:(n + k + 1) * sb] for k in range(n)]

    pairs = [(l, s) for l in range(nl) for s in range(l + 1)]
    cbh = onehot_rows([cs[l] * bs[s] for l, s in pairs], ghot_ref[...])
    gls = [c * jnp.exp(a_cum[l] - a_cum[s]) * dt[s] for c, (l, s) in zip(cbh, pairs)]
    w_end = [jnp.exp(a_cum[nl - 1] - a_cum[l]) * dt[l] for l in range(nl)]
    e_cum = [jnp.exp(a_cum[l]) for l in range(nl)]
    chunk = 4 * LANES
    for c0 in range(0, D_SSM, chunk):
        cl = slice(c0, c0 + chunk)
        ex = onehot_rows(gls + w_end + e_cum, ehot_ref[:, cl])
        g_ex, w_ex, e_ex = ex[:len(pairs)], ex[len(pairs):len(pairs) + nl], ex[len(pairs) + nl:]
        xc = [x[:, cl] for x in xs]
        for l in range(nl):
            ydiag = dskip_ref[:, cl] * xc[l]
            for k, (pl_, ps_) in enumerate(pairs):
                if pl_ == l:
                    ydiag = ydiag + g_ex[k] * xc[ps_]
            ydx_ref[rows(l), cl] = ydiag
            xw_ref[rows(l), cl] = w_ex[l] * xc[l]
            ecum_ref[rows(l), cl] = e_ex[l]


def _sample1(pj3, dtr3, sp3, scv3, pool_w, pool_scale, conv_w, conv_b, dt_bias_p, a_log_p,
             dskip_row, ehot, ghot, sb):
    nblk = pj3.shape[0]
    db = nblk * sb
    rb = DEC_SEQ * sb
    blk2 = lambda width: pl.BlockSpec((None, rb, width), lambda i: (i, 0, 0))
    hist3 = lambda n, width: pl.BlockSpec((n, sb, width), lambda i: (0, i, 0))
    const2 = lambda shape: pl.BlockSpec(shape, lambda i: (0, 0))
    blk_shape = lambda width, dt: jax.ShapeDtypeStruct((nblk, rb, width), dt)
    return pl.pallas_call(
        functools.partial(_sample1_kernel, sb=sb),
        grid=(nblk,),
        in_specs=[
            blk2(D_MAIN), blk2(LANES), hist3(POOL_HIST, D_POOL), hist3(CONV_WIDTH - 1, D_CONV),
            pl.BlockSpec((len(POOL_WINDOWS), POOL_GROUP, POOL_GROUP), lambda i: (0, 0, 0)),
            const2((1, D_POOL)), const2((CONV_WIDTH, D_CONV)), const2((1, D_CONV)),
            const2((1, LANES)), const2((1, LANES)), const2((1, D_SSM)),
            const2((LANES, D_SSM)), const2((D_BC, LANES)),
        ],
        out_specs=[
            blk2(D_POOL), hist3(POOL_HIST, D_POOL), hist3(CONV_WIDTH - 1, D_CONV),
            blk2(D_BC), blk2(D_BC), blk2(D_SSM), blk2(D_SSM), blk2(D_SSM),
            pl.BlockSpec((sb, LANES), lambda i: (i, 0)),
        ],
        out_shape=[
            blk_shape(D_POOL, BF16),
            jax.ShapeDtypeStruct((POOL_HIST, db, D_POOL), F32),
            jax.ShapeDtypeStruct((CONV_WIDTH - 1, db, D_CONV), F32),
            blk_shape(D_BC, F32), blk_shape(D_BC, BF16),
            blk_shape(D_SSM, F32), blk_shape(D_SSM, F32), blk_shape(D_SSM, F32),
            jax.ShapeDtypeStruct((db, LANES), F32),
        ],
        compiler_params=pltpu.CompilerParams(
            dimension_semantics=("arbitrary",),
            vmem_limit_bytes=VMEM_LIMIT),
        name="sample_elementwise",
    )(pj3, dtr3, sp3, scv3, pool_w, pool_scale, conv_w, conv_b, dt_bias_p, a_log_p, dskip_row,
      ehot, ghot)


def _sample2_setup(xw_ref, cs_ref, xwt_ref, *, sb, per):
    j = pl.program_id(1)

    @pl.when(j == 0)
    def _():
        for blk in range(D_SSM // LANES):
            xwt_ref[blk * LANES:(blk + 1) * LANES, :] = jnp.transpose(
                xw_ref[:, blk * LANES:(blk + 1) * LANES]).astype(BF16)

    return [jnp.concatenate([cs_ref[pl.ds(l * sb + per * j + 2 * pr + bi, 1), :]
                             for bi in range(2) for l in range(DEC_SEQ)], axis=0).astype(BF16)
            for pr in range(per // 2)]


def _sample2_group(g, refs, c_rows, *, sb, per):
    cdec_ref, st_ref, _, bs_ref, _, nst_ref, yo_ref, xwt_ref = refs
    i = pl.program_id(0)
    j = pl.program_id(1)
    nl = DEC_SEQ
    rows = sb * nl
    seq_of_row = lax.broadcasted_iota(jnp.int32, (rows, 1), 0) & (sb - 1)
    gw = HEADS_PER_GROUP * HEAD_DIM
    gs = slice(g * D_STATE, (g + 1) * D_STATE)
    for pr in range(per // 2):
        q0 = per * j + 2 * pr
        c8 = c_rows[pr][:, gs]
        b_blk = bs_ref[:, gs]
        zero_b = jnp.zeros_like(b_blk)
        w2 = jnp.concatenate([jnp.where(seq_of_row == q0, b_blk, zero_b),
                              jnp.where(seq_of_row == q0 + 1, b_blk, zero_b)], axis=1)
        u2 = _dot(xwt_ref[g * gw:(g + 1) * gw, :], w2)
        for bi in range(2):
            sq = 2 * pr + bi
            s0 = st_ref[sq, g * HEADS_PER_GROUP:(g + 1) * HEADS_PER_GROUP].reshape(gw, D_STATE)
            yo = _dot_nt(c8, s0.astype(BF16))
            base = (i * sb + q0 + bi) * HEADS + g * HEADS_PER_GROUP
            for r in range(HEADS_PER_GROUP):
                dec = cdec_ref[base + r]
                rs = slice(r * HEAD_DIM, (r + 1) * HEAD_DIM)
                nst_ref[sq, g * HEADS_PER_GROUP + r] = (
                    s0[rs] * dec + u2[rs, bi * D_STATE:(bi + 1) * D_STATE])
            for l in range(nl):
                yo_ref[pl.ds(l * sb + q0 + bi, 1), g * gw:(g + 1) * gw] = (
                    yo[bi * nl + l:bi * nl + l + 1])


N_SSD_IN, N_SSD_OUT, N_SSD_SCRATCH = 11, 3, 6
N_ST_IN, N_ST_OUT = 5, 2


def _ssd_state_kernel(*refs, nc, sb, per):
    n_in = N_SSD_IN + N_ST_IN + 1
    ins, rest = refs[:n_in], refs[n_in:]
    n_out = N_SSD_OUT + N_ST_OUT + 1
    outs, scratch = rest[:n_out], rest[n_out:]
    outs[-1][...] = ins[-1][...].astype(BF16)
    st_refs = (*ins[N_SSD_IN:N_SSD_IN + N_ST_IN], *outs[N_SSD_OUT:N_SSD_OUT + N_ST_OUT],
               *scratch[N_SSD_SCRATCH:])
    c_rows = _sample2_setup(st_refs[2], st_refs[4], st_refs[7], sb=sb, per=per)
    _ssd_body(*ins[:N_SSD_IN], *outs[:N_SSD_OUT], *scratch[:N_SSD_SCRATCH], nc=nc,
              after_group=functools.partial(_sample2_group, refs=st_refs, c_rows=c_rows,
                                            sb=sb, per=per))


def _ssd_prompt_sample_state(proj, dt_raw, conv_w, conv_b, dt_bias_p, a_log_p, dskip_row, norm_g,
                             bsz, seq, cdec_flat, state, xw3, bs3, cs3, sb, w_out2d):
    q = CHUNK
    nc = seq // q
    db = state.shape[0]
    rb = DEC_SEQ * sb
    per = sb // nc
    assert rb == LANES and db // sb == bsz and per * nc == sb and per % 2 == 0
    d_mix = w_out2d.shape[0]
    slab = d_mix // (bsz * nc)
    assert slab * bsz * nc == d_mix and slab % (2 * SUBLANES) == 0
    wslab = pl.BlockSpec((slab, D_MODEL), lambda b, c: (b * nc + c, 0))
    row = lambda b, c: b * nc + c
    const = lambda b, c: (0, 0)
    blk2 = lambda width: pl.BlockSpec((None, rb, width), lambda i, j: (i, 0, 0))
    st_spec = pl.BlockSpec((per, HEADS, HEAD_DIM, D_STATE), lambda i, j: (i * nc + j, 0, 0, 0))
    return pl.pallas_call(
        functools.partial(_ssd_state_kernel, nc=nc, sb=sb, per=per),
        grid=(bsz, nc),
        in_specs=[
            pl.BlockSpec((q, D_SSM), lambda b, c: (row(b, c), OFF_Z // D_SSM)),
            pl.BlockSpec((q, D_SSM), lambda b, c: (row(b, c), OFF_XS // D_SSM)),
            pl.BlockSpec((q, D_BC), lambda b, c: (row(b, c), OFF_B // D_BC)),
            pl.BlockSpec((q, D_BC), lambda b, c: (row(b, c), OFF_C // D_BC)),
            pl.BlockSpec((q, LANES), lambda b, c: (row(b, c), 0)),
            pl.BlockSpec((CONV_WIDTH, D_CONV), const),
            pl.BlockSpec((1, D_CONV), const),
            pl.BlockSpec((1, LANES), const),
            pl.BlockSpec((1, LANES), const),
            pl.BlockSpec((1, D_SSM), const),
            pl.BlockSpec((1, D_SSM), const),
            pl.BlockSpec(memory_space=pltpu.SMEM),
            st_spec, blk2(D_SSM), blk2(D_BC), blk2(D_BC),
            wslab,
        ],
        out_specs=[
            pl.BlockSpec((q, D_SSM), lambda b, c: (row(b, c), 0)),
            pl.BlockSpec((None, CONV_WIDTH - 1, D_CONV), lambda b, c: (b, 0, 0)),
            pl.BlockSpec((None, HEADS, HEAD_DIM, D_STATE), lambda b, c: (b, 0, 0, 0)),
            st_spec, blk2(D_SSM),
            wslab,
        ],
        out_shape=[
            jax.ShapeDtypeStruct((bsz * seq, D_SSM), BF16),
            jax.ShapeDtypeStruct((bsz, CONV_WIDTH - 1, D_CONV), F32),
            jax.ShapeDtypeStruct((bsz, HEADS, HEAD_DIM, D_STATE), F32),
            jax.ShapeDtypeStruct(state.shape, F32),
            jax.ShapeDtypeStruct((db // sb, rb, D_SSM), F32),
            jax.ShapeDtypeStruct(w_out2d.shape, BF16),
        ],
        scratch_shapes=[
            pltpu.VMEM((D_CONV // LANES, q + SUBLANES, LANES), F32),
            pltpu.VMEM((D_STATE, D_SSM), F32),
            pltpu.VMEM((D_SSM // LANES, q + SUBLANES, LANES), F32),
            pltpu.VMEM((D_BC // LANES, q + SUBLANES, LANES), F32),
            pltpu.VMEM((D_BC // LANES, q + SUBLANES, LANES), F32),
            pltpu.VMEM((q, D_SSM), F32),
            pltpu.VMEM((D_SSM, rb), BF16),
        ],
        compiler_params=pltpu.CompilerParams(
            dimension_semantics=("arbitrary", "arbitrary"),
            vmem_limit_bytes=VMEM_LIMIT),
        name="ssd_prompt_sample_state",
    )(proj, proj, proj, proj, dt_raw, conv_w, conv_b, dt_bias_p, a_log_p, dskip_row, norm_g,
      cdec_flat, state, xw3, bs3, cs3, w_out2d)


def _outproj_sample_kernel(a_ref, yo_ref, ecum_ref, ydx_ref, z_ref, ng_ref, w_ref, x_ref,
                           g_ref, y_ref, b_ref):
    gw = D_SSM // GROUPS
    for g in range(GROUPS):
        cs = slice(g * gw, (g + 1) * gw)
        y = ydx_ref[:, cs] + ecum_ref[:, cs] * yo_ref[:, cs]
        yz = y * _silu(z_ref[:, cs].astype(F32))
        ms = jnp.sum(yz * yz, axis=-1, keepdims=True) * (1.0 / gw)
        b_ref[:, cs] = (yz * lax.rsqrt(ms + EPS) * ng_ref[:, cs]).astype(BF16)
    _outproj_kernel(a_ref, b_ref, w_ref, x_ref, g_ref, y_ref)


def _outproj_sample(out_a, yo, ecum, ydx, proj, norm_g, w_bf, x3d, final_g):
    db, nl, _ = x3d.shape
    m = db * nl
    tm = min(OUTPROJ_SAMPLE_TM, m)
    assert tm % (nl * SAMPLE_BLOCK) == 0
    seqblk = pl.BlockSpec((tm // nl, nl, D_MODEL), lambda i: (i, 0, 0))
    rowblk = lambda width: pl.BlockSpec((tm, width), lambda i: (i, 0))
    const = lambda shape, **kw: pl.BlockSpec(shape, lambda i: (0, 0), **kw)
    return pl.pallas_call(
        _outproj_sample_kernel,
        grid=(m // tm,),
        in_specs=[
            rowblk(D_POOL), rowblk(D_SSM), rowblk(D_SSM), rowblk(D_SSM),
            pl.BlockSpec((tm, D_SSM), lambda i: (i, OFF_Z // D_SSM)),
            const((1, D_SSM)),
            const((D_POOL + D_SSM, D_MODEL), pipeline_mode=pl.Buffered(1)),
            seqblk, const((1, D_MODEL)),
        ],
        out_specs=seqblk,
        out_shape=jax.ShapeDtypeStruct((db, nl, D_MODEL), F32),
        scratch_shapes=[pltpu.VMEM((tm, D_SSM), BF16)],
        compiler_params=pltpu.CompilerParams(
            dimension_semantics=("arbitrary",),
            vmem_limit_bytes=VMEM_LIMIT),
        name="outproj_sample",
    )(out_a, yo, ecum, ydx, proj, norm_g, w_bf, x3d, final_g)


def kernel(x_prompt, x_sample, state_pool, state_conv, state_ssm, norm_g, w_in, conv_w, conv_b,
           dt_bias, a_log, d_skip, ssm_norm_g, pool_w, pool_scale, w_out, final_g):
    bsz, seq, _ = x_prompt.shape
    db, nl, _ = x_sample.shape
    assert nl == DEC_SEQ and seq % CHUNK == 0 and w_in.shape[0] == 1

    w_t = jnp.transpose(w_in[0])
    pool_w_b = pool_w[0]
    g_in = norm_g[0][None, :]
    g_fin = final_g[None, :]
    ps = pool_scale[0][None, :]
    cw = conv_w[0]
    cbias = conv_b[0][None, :]
    pad_h = lambda v: jnp.pad(v, (0, LANES - HEADS))[None, :]
    dtb = pad_h(dt_bias[0])
    alog = pad_h(a_log[0])
    dskip_row = jnp.repeat(d_skip[0], HEAD_DIM)[None, :]
    ng = ssm_norm_g[0][None, :]

    sb = SAMPLE_BLOCK
    nblk = db // sb
    rb = nl * sb
    proj_s, dt_s, w_bf, wdt_bf = _inproj(x_sample, g_in, w_t, w_t, PERM_BLOCK)
    pj3 = proj_s.reshape(nblk, rb, D_MAIN)
    head_of_ch = jnp.arange(D_SSM, dtype=jnp.int32) // HEAD_DIM
    ehot = (jnp.arange(LANES, dtype=jnp.int32)[:, None] == head_of_ch[None, :]).astype(BF16)
    grp_of_row = jnp.arange(D_BC, dtype=jnp.int32) // D_STATE
    head_id = jnp.arange(LANES, dtype=jnp.int32)
    ghot = ((head_id[None, :] // HEADS_PER_GROUP == grp_of_row[:, None])
            & (head_id[None, :] < HEADS)).astype(BF16)
    (oa_s, npool_s, nconv_s, cs_s, bs_s, xw_s, ydx_s, ecum_s, cdec_s) = _sample1(
        pj3, dt_s.reshape(nblk, rb, LANES), jnp.transpose(state_pool[0], (1, 0, 2)),
        jnp.transpose(state_conv[0], (1, 0, 2)), pool_w_b, ps, cw, cbias, dtb, alog,
        dskip_row, ehot, ghot, sb)

    xp2 = x_prompt.reshape(bsz * seq, D_MODEL)
    proj_p, dt_p = _inproj(xp2, g_in, w_bf, wdt_bf, PROMPT_TN)
    ob_p, nconv_p, nssm_p, nssm_s, yo_s, w_out_bf = _ssd_prompt_sample_state(
        proj_p, dt_p, cw, cbias, dtb, alog, dskip_row, ng, bsz, seq,
        cdec_s[:, :HEADS].reshape(db * HEADS), state_ssm[0], xw_s, bs_s, cs_s, sb, w_out[0])
    y_p, npool_p = _pool_outproj_prompt(proj_p, pool_w_b, ps, ob_p, w_out_bf, xp2, g_fin, bsz, seq)
    y_p = y_p.reshape(bsz, seq, D_MODEL)

    flat = lambda t: t.reshape(db * nl, t.shape[-1])
    y_s = _outproj_sample(flat(oa_s), flat(yo_s), flat(ecum_s), flat(ydx_s), proj_s, ng, w_out_bf,
                          x_sample, g_fin)

    return (y_p, y_s,
            npool_p[None], nconv_p[None], nssm_p[None],
            jnp.transpose(npool_s, (1, 0, 2))[None],
            jnp.transpose(nconv_s, (1, 0, 2))[None],
            nssm_s[None])
```

```python
import functools

import jax
import jax.numpy as jnp
from jax import lax
from jax.experimental import pallas as pl
from jax.experimental.pallas import tpu as pltpu

F32 = jnp.float32
BF16 = jnp.bfloat16

D_MODEL = 2048
D_POOL = 1024
POOL_WINDOWS = (2, 4, 8, 16)
POOL_GROUP = 256
POOL_HIST = 15
D_SSM = 3072
HEAD_DIM = 64
HEADS = 48
GROUPS = 8
HEADS_PER_GROUP = 6
D_STATE = 128
D_BC = GROUPS * D_STATE
CONV_WIDTH = 4
D_CONV = D_SSM + 2 * D_BC
D_MAIN = 2 * D_POOL + D_SSM + D_CONV
PAST_LEN = 16384
DEC_SEQ = 4
EPS = 1e-5

LANES = 128
SUBLANES = 8
VMEM_LIMIT = 56 * 1024 * 1024

OFF_Z = 0
OFF_XS = D_SSM
OFF_U = 2 * D_SSM
OFF_GATE = OFF_U + D_POOL
OFF_B = OFF_GATE + D_POOL
OFF_C = OFF_B + D_BC

PERM_BLOCK = 1024
INPROJ_TM = 1024
OUTPROJ_TM = 512
OUTPROJ_SAMPLE_TM = 256
PROMPT_TN = 2048
CHUNK = 128
SAMPLE_BLOCK = 32
NEG_BIG = -1e30
LOG2E = 1.4426950408889634


def _silu(v):
    h = 0.5 * v
    return h + h * jnp.tanh(h)


def _softplus(v):
    y = jnp.exp(-jnp.abs(v))
    u = 1.0 + y
    d = u - 1.0
    l1p = jnp.where(d == 0.0, y, jnp.log(u) * (y / jnp.where(d == 0.0, 1.0, d)))
    return jnp.maximum(v, 0.0) + l1p


def _split2(v):
    hi = v.astype(BF16)
    lo = (v - hi.astype(F32)).astype(BF16)
    return hi, lo


def _dot(a, b):
    return jnp.dot(a, b, preferred_element_type=F32)


def _dot_nt(a, b):
    return lax.dot_general(a, b, (((1,), (1,)), ((), ())), preferred_element_type=F32)


def _dot2(v, onehot):
    hi, lo = _split2(v)
    return _dot(hi, onehot) + _dot(lo, onehot)


def _inproj_kernel(x_ref, g_ref, w_ref, wdt_ref, o_ref, dt_ref, *rest):
    from_f32 = w_ref.dtype == F32
    (wb_out_ref, wdt_out_ref), rest = (rest[:2], rest[2:]) if from_f32 else ((None, None), rest)
    h_ref = rest[0]

    @pl.when(pl.program_id(1) == 0)
    def _():
        if len(x_ref.shape) == 3:
            rows_ref = rest[1]
            sb = SAMPLE_BLOCK
            for b in range(x_ref.shape[0] // sb):
                for t in range(DEC_SEQ):
                    r0 = (b * DEC_SEQ + t) * sb
                    rows_ref[r0:r0 + sb, :] = x_ref[b * sb:(b + 1) * sb, t, :]
            x = rows_ref[...]
        else:
            x = x_ref[...]
        ms = jnp.mean(x * x, axis=-1, keepdims=True)
        h = (x * lax.rsqrt(ms + EPS) * g_ref[...]).astype(BF16)
        h_ref[...] = h
        if from_f32:
            wrow = lax.broadcasted_iota(jnp.int32, wdt_ref.shape, 0)
            wdt = jnp.where(wrow < HEADS, wdt_ref[...], 0.0).astype(BF16)
            wdt_out_ref[...] = wdt
        else:
            wdt = wdt_ref[...]
        dt_ref[...] = _dot_nt(h, wdt)

    if from_f32:
        wb = w_ref[...].astype(BF16)
        wb_out_ref[...] = wb
    else:
        wb = w_ref[...]
    o_ref[...] = _dot_nt(h_ref[...], wb).astype(BF16)


def _src_block(j):
    nz = (D_SSM + D_SSM) // PERM_BLOCK
    npool = 2 * D_POOL // PERM_BLOCK
    return jnp.where(j < nz, j + npool, jnp.where(j < nz + npool, j - nz, j))


def _inproj(x2d, norm_g, w, w_dt, tn):
    whole = x2d.ndim == 3
    m = x2d.shape[0] * x2d.shape[1] if whole else x2d.shape[0]
    tm = min(INPROJ_TM, m)
    assert m == tm or not whole
    x_spec = (pl.BlockSpec(x2d.shape, lambda i, j: (0, 0, 0), pipeline_mode=pl.Buffered(1))
              if whole else pl.BlockSpec((tm, D_MODEL), lambda i, j: (i, 0)))
    scratch = [pltpu.VMEM((tm, D_MODEL), BF16)]
    if whole:
        scratch.append(pltpu.VMEM((tm, D_MODEL), F32))
    from_f32 = w.dtype == F32
    assert tn == PERM_BLOCK or not from_f32
    w_map = (lambda i, j: (_src_block(j), 0)) if from_f32 else (lambda i, j: (j, 0))
    wdt_map = (lambda i, j: (D_MAIN // LANES, 0)) if from_f32 else (lambda i, j: (0, 0))
    out_specs = [
        pl.BlockSpec((tm, tn), lambda i, j: (i, j)),
        pl.BlockSpec((tm, LANES), lambda i, j: (i, 0)),
    ]
    out_shape = [
        jax.ShapeDtypeStruct((m, D_MAIN), BF16),
        jax.ShapeDtypeStruct((m, LANES), F32),
    ]
    if from_f32:
        assert m == tm
        out_specs += [pl.BlockSpec((tn, D_MODEL), lambda i, j: (j, 0)),
                      pl.BlockSpec((LANES, D_MODEL), lambda i, j: (0, 0))]
        out_shape += [jax.ShapeDtypeStruct((D_MAIN, D_MODEL), BF16),
                      jax.ShapeDtypeStruct((LANES, D_MODEL), BF16)]
    return pl.pallas_call(
        _inproj_kernel,
        grid=(m // tm, D_MAIN // tn),
        in_specs=[
            x_spec,
            pl.BlockSpec((1, D_MODEL), lambda i, j: (0, 0)),
            pl.BlockSpec((tn, D_MODEL), w_map),
            pl.BlockSpec((LANES, D_MODEL), wdt_map),
        ],
        out_specs=out_specs,
        out_shape=out_shape,
        scratch_shapes=scratch,
        compiler_params=pltpu.CompilerParams(
            dimension_semantics=("arbitrary", "arbitrary"),
            vmem_limit_bytes=VMEM_LIMIT),
        name="inproj",
    )(x2d, norm_g, w, w_dt)


def _outproj_kernel(a_ref, b_ref, w_ref, x_ref, g_ref, y_ref):
    acc = (_dot(a_ref[...], w_ref[0:D_POOL, :])
           + _dot(b_ref[...], w_ref[D_POOL:D_POOL + D_SSM, :]))

    def residual_norm(x, part):
        r = x + part
        ms = jnp.mean(r * r, axis=-1, keepdims=True)
        return r * lax.rsqrt(ms + EPS) * g_ref[...]

    if len(x_ref.shape) == 3:
        sb = SAMPLE_BLOCK
        for b in range(x_ref.shape[0] // sb):
            for t in range(DEC_SEQ):
                r0 = (b * DEC_SEQ + t) * sb
                y_ref[b * sb:(b + 1) * sb, t, :] = residual_norm(
                    x_ref[b * sb:(b + 1) * sb, t, :], acc[r0:r0 + sb])
    else:
        y_ref[...] = residual_norm(x_ref[...], acc)


def _pool_kernel(u_ref, gate_ref, pw_ref, ps_ref, oa_ref, np_ref, ubuf, wbuf, *, tl, nt, early):
    t = pl.program_id(1)
    hist = POOL_HIST + 1
    nrows = hist + tl
    nv = nrows // SUBLANES
    bpg = POOL_GROUP // LANES

    @pl.when(t == 0)
    def _():
        ubuf[:, 0:hist, :] = jnp.zeros((ubuf.shape[0], hist, LANES), F32)

    early_val = early()
    pos = t * tl + lax.broadcasted_iota(jnp.int32, (tl, 1), 0)
    for gi, w in enumerate(POOL_WINDOWS):
        win = []
        for lb in range(gi * bpg, (gi + 1) * bpg):
            ubuf[lb, hist:nrows, :] = u_ref[:, lb * LANES:(lb + 1) * LANES].astype(F32)
            acc = [ubuf[lb, pl.ds(a, SUBLANES, stride=nv), :] for a in range(nv)]
            span = 1
            while span < w:
                wrap = [pltpu.roll(acc[nv - span + a], 1, 0) for a in range(span)]
                acc = [acc[a] + (acc[a - span] if a >= span else wrap[a]) for a in range(nv)]
                span *= 2
            for a in range(nv):
                wbuf[lb, pl.ds(a, SUBLANES, stride=nv), :] = acc[a]
            win.append(wbuf[lb, hist:nrows, :])
        cs = slice(gi * POOL_GROUP, (gi + 1) * POOL_GROUP)
        cnt = jnp.minimum(w, pos + 1).astype(F32)
        ug = jnp.concatenate([ubuf[lb, hist:nrows, :] for lb in range(gi * bpg, (gi + 1) * bpg)],
                             axis=1)
        pooled = jnp.concatenate(win, axis=1) / cnt - ug
        mixed = _dot(pooled.astype(BF16), pw_ref[gi].astype(BF16))
        gt = gate_ref[:, cs].astype(F32)
        oa_ref[:, cs] = (mixed * ps_ref[:, cs] * _silu(gt)).astype(BF16)

    for lb in range(D_POOL // LANES):
        np_ref[:, lb * LANES:(lb + 1) * LANES] = ubuf[lb, tl + 1:nrows, :]

    ubuf[:, 0:hist, :] = ubuf[:, tl:nrows, :]
    return early_val


def _pool_outproj_kernel(u_ref, gate_ref, pw_ref, ps_ref, b_ref, w_ref, x_ref, g_ref,
                         y_ref, np_ref, ubuf, wbuf, a_ref, *, tl, nt):
    acc_b = _pool_kernel(u_ref, gate_ref, pw_ref, ps_ref, a_ref, np_ref, ubuf, wbuf, tl=tl, nt=nt,
                         early=lambda: _dot(b_ref[...], w_ref[D_POOL:D_POOL + D_SSM, :]))
    r = x_ref[...] + (_dot(a_ref[...], w_ref[0:D_POOL, :]) + acc_b)
    ms = jnp.mean(r * r, axis=-1, keepdims=True)
    y_ref[...] = r * lax.rsqrt(ms + EPS) * g_ref[...]


def _pool_outproj_prompt(proj, pool_w, pool_scale, out_b, w_bf, x2d, final_g, bsz, seq):
    tl = min(OUTPROJ_TM, seq)
    nt = seq // tl
    row = lambda b, t: b * nt + t
    const = lambda b, t: (0, 0)
    return pl.pallas_call(
        functools.partial(_pool_outproj_kernel, tl=tl, nt=nt),
        grid=(bsz, nt),
        in_specs=[
            pl.BlockSpec((tl, D_POOL), lambda b, t: (row(b, t), OFF_U // D_POOL)),
            pl.BlockSpec((tl, D_POOL), lambda b, t: (row(b, t), OFF_GATE // D_POOL)),
            pl.BlockSpec((len(POOL_WINDOWS), POOL_GROUP, POOL_GROUP), lambda b, t: (0, 0, 0)),
            pl.BlockSpec((1, D_POOL), const),
            pl.BlockSpec((tl, D_SSM), lambda b, t: (row(b, t), 0)),
            pl.BlockSpec((D_POOL + D_SSM, D_MODEL), const, pipeline_mode=pl.Buffered(1)),
            pl.BlockSpec((tl, D_MODEL), lambda b, t: (row(b, t), 0)),
            pl.BlockSpec((1, D_MODEL), const),
        ],
        out_specs=[
            pl.BlockSpec((tl, D_MODEL), lambda b, t: (row(b, t), 0)),
            pl.BlockSpec((None, POOL_HIST, D_POOL), lambda b, t: (b, 0, 0)),
        ],
        out_shape=[
            jax.ShapeDtypeStruct((bsz * seq, D_MODEL), F32),
            jax.ShapeDtypeStruct((bsz, POOL_HIST, D_POOL), F32),
        ],
        scratch_shapes=[pltpu.VMEM((D_POOL // LANES, tl + POOL_HIST + 1, LANES), F32),
                        pltpu.VMEM((D_POOL // LANES, tl + POOL_HIST + 1, LANES), F32),
                        pltpu.VMEM((tl, D_POOL), BF16)],
        compiler_params=pltpu.CompilerParams(
            dimension_semantics=("arbitrary", "arbitrary"),
            vmem_limit_bytes=VMEM_LIMIT),
        name="pool_outproj",
    )(proj, proj, pool_w, pool_scale, out_b, w_bf, x2d, final_g)


def _ssd_body(z_ref, xs_ref, b_ref, c_ref, dtr_ref, cw_ref, cbias_ref, dtb_ref, alog_ref,
              dskip_ref, ng_ref,
              ob_ref, nconv_ref, nssm_ref,
              cbuf, st_ref, cvx_ref, cvb_ref, cvc_ref, y_ref, *, nc, after_group=None):
    q = CHUNK
    c_idx = pl.program_id(1)
    halo = SUBLANES

    nbx, nbb = D_SSM // LANES, D_BC // LANES

    @pl.when(c_idx == 0)
    def _():
        cbuf[:, 0:halo, :] = jnp.zeros((cbuf.shape[0], halo, LANES), F32)
        st_ref[...] = jnp.zeros(st_ref.shape, F32)

    nv = (halo + q) // SUBLANES
    for blk in range(D_CONV // LANES):
        ls = slice(blk * LANES, (blk + 1) * LANES)
        if blk < nbx:
            src_ref, off, dst = xs_ref, blk * LANES, cvx_ref.at[blk]
        elif blk < nbx + nbb:
            src_ref, off, dst = b_ref, (blk - nbx) * LANES, cvb_ref.at[blk - nbx]
        else:
            src_ref, off, dst = c_ref, (blk - nbx - nbb) * LANES, cvc_ref.at[blk - nbx - nbb]
        cbuf[blk, halo:halo + q, :] = src_ref[:, off:off + LANES].astype(F32)
        xv = [cbuf[blk, pl.ds(a, SUBLANES, stride=nv), :] for a in range(nv)]
        wrap = [pltpu.roll(xv[nv - k], 1, 0) for k in range(1, CONV_WIDTH)]
        taps = [0.5 * cw_ref[k:k + 1, ls] for k in range(CONV_WIDTH)]
        bias = 0.5 * cbias_ref[:, ls]
        for a in range(nv):
            h = bias + taps[CONV_WIDTH - 1] * xv[a]
            for k in range(1, CONV_WIDTH):
                src = xv[a - k] if a >= k else wrap[k - a - 1]
                h = h + taps[CONV_WIDTH - 1 - k] * src
            dst[pl.ds(a, SUBLANES, stride=nv), :] = h + h * jnp.tanh(h)
    rows = slice(halo, halo + q)

    dt = _softplus(dtr_ref[...] + dtb_ref[...])
    a_neg = -jnp.exp(alog_ref[...])
    da = dt * (a_neg * LOG2E)
    row = lax.broadcasted_iota(jnp.int32, (q, LANES), 0)
    a2 = da
    shift = 1
    while shift < q:
        a2 = a2 + jnp.where(row >= shift, pltpu.roll(a2, shift, 0), 0.0)
        shift *= 2
    a2_t = jnp.transpose(a2)
    ldt_t = jnp.log2(jnp.transpose(dt))
    a2_end_t = a2_t[:, q - 1:q]
    w_t = jnp.exp2(a2_end_t - a2_t + ldt_t)
    cdec_t = jnp.exp2(a2_end_t)
    srow_t = a2_t - ldt_t
    ea = jnp.exp2(a2)

    li = lax.broadcasted_iota(jnp.int32, (q, q), 0)
    si = lax.broadcasted_iota(jnp.int32, (q, q), 1)
    tri = li >= si
    lane = lax.broadcasted_iota(jnp.int32, (q, LANES), 1)
    lo_half = lane < HEAD_DIM

    for g in range(GROUPS):
        gs = slice(g * D_STATE, (g + 1) * D_STATE)
        c_gb = cvc_ref[g, rows, :].astype(BF16)
        b_g = cvb_ref[g, rows, :]
        cb = _dot_nt(c_gb, b_g.astype(BF16))
        b_t = jnp.transpose(b_g)
        ppg = HEADS_PER_GROUP // 2
        gl = slice(g * ppg * LANES, (g + 1) * ppg * LANES)
        y_off = _dot(c_gb, st_ref[:, gl].astype(BF16))
        for j in range(ppg):
            blk = g * ppg + j
            ls = slice(blk * LANES, (blk + 1) * LANES)
            sc, bw, ecol, dec = [], [], [], []
            for h in (2 * blk, 2 * blk + 1):
                a_col = jnp.broadcast_to(a2[:, h:h + 1], (q, q))
                decay_dt = jnp.exp2(jnp.where(tri, a_col - srow_t[h:h + 1, :], NEG_BIG))
                sc.append((cb * decay_dt).astype(BF16))
                bw.append((b_t * w_t[h:h + 1, :]).astype(BF16))
                ecol.append(jnp.broadcast_to(ea[:, h:h + 1], (q, LANES)))
                dec.append(jnp.broadcast_to(cdec_t[h:h + 1, :], (q, LANES)))
            x_p = cvx_ref[blk, rows, :].astype(BF16)
            zero_b = jnp.zeros_like(x_p)
            x_bd = jnp.concatenate([jnp.where(lo_half, x_p, zero_b),
                                    jnp.where(lo_half, zero_b, x_p)], axis=0)
            lhs1 = jnp.concatenate([jnp.concatenate(sc, axis=1),
                                    jnp.concatenate(bw, axis=1)], axis=0)
            r1 = _dot(lhs1, x_bd)
            y_ref[:, ls] = (r1[0:q] + y_off[:, j * LANES:(j + 1) * LANES]
                            * jnp.where(lo_half, ecol[0], ecol[1]))
            st_ref[:, ls] = (st_ref[:, ls] * jnp.where(lo_half, dec[0], dec[1])
                             + r1[q:2 * q])
        if after_group is not None:
            after_group(g)

    gw = D_SSM // GROUPS
    bpg = gw // LANES
    for g in range(GROUPS):
        yz, ssq = [], None
        for blk in range(g * bpg, (g + 1) * bpg):
            ls = slice(blk * LANES, (blk + 1) * LANES)
            y = y_ref[:, ls] + cvx_ref[blk, rows, :] * dskip_ref[:, ls]
            v = y * _silu(z_ref[:, ls].astype(F32))
            yz.append(v)
            ssq = v * v if ssq is None else ssq + v * v
        scale = lax.rsqrt(jnp.sum(ssq, axis=-1, keepdims=True) * (1.0 / gw) + EPS)
        for i, blk in enumerate(range(g * bpg, (g + 1) * bpg)):
            ls = slice(blk * LANES, (blk + 1) * LANES)
            ob_ref[:, ls] = (yz[i] * scale * ng_ref[:, ls]).astype(BF16)

    cbuf[:, 0:halo, :] = cbuf[:, q:q + halo, :]

    @pl.when(c_idx == nc - 1)
    def _():
        for blk in range(D_CONV // LANES):
            nconv_ref[:, blk * LANES:(blk + 1) * LANES] = cbuf[blk, q + halo - 3:q + halo, :]
        for blk in range(D_SSM // LANES):
            t = jnp.transpose(st_ref[:, blk * LANES:(blk + 1) * LANES])
            nssm_ref[2 * blk:2 * blk + 2] = t.reshape(2, HEAD_DIM, D_STATE)


def _sample1_kernel(pj_ref, dtr_ref, sp_ref, scv_ref, pw_ref, ps_ref, cw_ref, cbias_ref, dtb_ref,
                    alog_ref, dskip_ref, ehot_ref, ghot_ref,
                    oa_ref, npool_ref, nconv_ref, cs_ref, bs_ref, xw_ref, ydx_ref, ecum_ref,
                    cdec_ref, *, sb):
    nl = DEC_SEQ

    def rows(l):
        return slice(l * sb, (l + 1) * sb)

    def tok(l, off, width):
        return pj_ref[rows(l), off:off + width].astype(F32)

    for gi, w in enumerate(POOL_WINDOWS):
        c0 = gi * POOL_GROUP
        ext = [sp_ref[j, :, c0:c0 + POOL_GROUP] for j in range(POOL_HIST)]
        ext += [tok(l, OFF_U + c0, POOL_GROUP) for l in range(nl)]
        for l in range(nl):
            acc = ext[POOL_HIST + l]
            for k in range(1, w):
                acc = acc + ext[POOL_HIST + l - k]
            cnt = float(min(w, PAST_LEN + l + 1))
            pooled = acc / cnt - ext[POOL_HIST + l]
            mixed = _dot(pooled.astype(BF16), pw_ref[gi].astype(BF16))
            gt = tok(l, OFF_GATE + c0, POOL_GROUP)
            oa_ref[rows(l), c0:c0 + POOL_GROUP] = (
                mixed * ps_ref[:, c0:c0 + POOL_GROUP] * _silu(gt)).astype(BF16)
    for j in range(POOL_HIST):
        src = j + nl
        if src < POOL_HIST:
            npool_ref[j] = sp_ref[src]
        else:
            npool_ref[j] = tok(src - POOL_HIST, OFF_U, D_POOL)

    hist = CONV_WIDTH - 1
    conv_out = {}
    for name, poff, coff, width in (("x", OFF_XS, 0, D_SSM), ("b", OFF_B, D_SSM, D_BC),
                                    ("c", OFF_C, D_SSM + D_BC, D_BC)):
        ext = [scv_ref[j, :, coff:coff + width] for j in range(hist)]
        ext += [tok(l, poff, width) for l in range(nl)]
        outs = []
        for l in range(nl):
            acc = cbias_ref[:, coff:coff + width]
            for k in range(CONV_WIDTH):
                acc = acc + cw_ref[k:k + 1, coff:coff + width] * ext[l + k]
            outs.append(_silu(acc))
        conv_out[name] = outs
        for j in range(hist):
            nconv_ref[j, :, coff:coff + width] = ext[j + nl]
    xs, bs, cs = conv_out["x"], conv_out["b"], conv_out["c"]
    for l in range(nl):
        cs_ref[rows(l), :] = cs[l]
        bs_ref[rows(l), :] = bs[l].astype(BF16)

    a_neg = -jnp.exp(alog_ref[...])
    dt, a_cum = [], []
    run = None
    for l in range(nl):
        d = _softplus(dtr_ref[rows(l), :] + dtb_ref[...])
        dt.append(d)
        run = d * a_neg if run is None else run + d * a_neg
        a_cum.append(run)
    cdec_ref[...] = jnp.exp(a_cum[nl - 1])

    def onehot_rows(mats, onehot):
        parts = [_split2(m) for m in mats]
        stack = jnp.concatenate([p[0] for p in parts] + [p[1] for p in parts], axis=0)
        res = _dot(stack, onehot)
        n = len(mats)
        return [res[k * sb:(k + 1) * sb] + res[(n + k) * sb:(n + k + 1) * sb] for k in range(n)]

    pairs = [(l, s) for l in range(nl) for s in range(l + 1)]
    cbh = onehot_rows([cs[l] * bs[s] for l, s in pairs], ghot_ref[...])
    gls = [c * jnp.exp(a_cum[l] - a_cum[s]) * dt[s] for c, (l, s) in zip(cbh, pairs)]
    w_end = [jnp.exp(a_cum[nl - 1] - a_cum[l]) * dt[l] for l in range(nl)]
    e_cum = [jnp.exp(a_cum[l]) for l in range(nl)]
    chunk = 4 * LANES
    for c0 in range(0, D_SSM, chunk):
        cl = slice(c0, c0 + chunk)
        ex = onehot_rows(gls + w_end + e_cum, ehot_ref[:, cl])
        g_ex, w_ex, e_ex = ex[:len(pairs)], ex[len(pairs):len(pairs) + nl], ex[len(pairs) + nl:]
        xc = [x[:, cl] for x in xs]
        for l in range(nl):
            ydiag = dskip_ref[:, cl] * xc[l]
            for k, (pl_, ps_) in enumerate(pairs):
                if pl_ == l:
                    ydiag = ydiag + g_ex[k] * xc[ps_]
            ydx_ref[rows(l), cl] = ydiag
            xw_ref[rows(l), cl] = w_ex[l] * xc[l]
            ecum_ref[rows(l), cl] = e_ex[l]


def _sample1(pj3, dtr3, sp3, scv3, pool_w, pool_scale, conv_w, conv_b, dt_bias_p, a_log_p,
             dskip_row, ehot, ghot, sb):
    nblk = pj3.shape[0]
    db = nblk * sb
    rb = DEC_SEQ * sb
    blk2 = lambda width: pl.BlockSpec((None, rb, width), lambda i: (i, 0, 0))
    hist3 = lambda n, width: pl.BlockSpec((n, sb, width), lambda i: (0, i, 0))
    const2 = lambda shape: pl.BlockSpec(shape, lambda i: (0, 0))
    blk_shape = lambda width, dt: jax.ShapeDtypeStruct((nblk, rb, width), dt)
    return pl.pallas_call(
        functools.partial(_sample1_kernel, sb=sb),
        grid=(nblk,),
        in_specs=[
            blk2(D_MAIN), blk2(LANES), hist3(POOL_HIST, D_POOL), hist3(CONV_WIDTH - 1, D_CONV),
            pl.BlockSpec((len(POOL_WINDOWS), POOL_GROUP, POOL_GROUP), lambda i: (0, 0, 0)),
            const2((1, D_POOL)), const2((CONV_WIDTH, D_CONV)), const2((1, D_CONV)),
            const2((1, LANES)), const2((1, LANES)), const2((1, D_SSM)),
            const2((LANES, D_SSM)), const2((D_BC, LANES)),
        ],
        out_specs=[
            blk2(D_POOL), hist3(POOL_HIST, D_POOL), hist3(CONV_WIDTH - 1, D_CONV),
            blk2(D_BC), blk2(D_BC), blk2(D_SSM), blk2(D_SSM), blk2(D_SSM),
            pl.BlockSpec((sb, LANES), lambda i: (i, 0)),
        ],
        out_shape=[
            blk_shape(D_POOL, BF16),
            jax.ShapeDtypeStruct((POOL_HIST, db, D_POOL), F32),
            jax.ShapeDtypeStruct((CONV_WIDTH - 1, db, D_CONV), F32),
            blk_shape(D_BC, F32), blk_shape(D_BC, BF16),
            blk_shape(D_SSM, F32), blk_shape(D_SSM, F32), blk_shape(D_SSM, F32),
            jax.ShapeDtypeStruct((db, LANES), F32),
        ],
        compiler_params=pltpu.CompilerParams(
            dimension_semantics=("arbitrary",),
            vmem_limit_bytes=VMEM_LIMIT),
        name="sample_elementwise",
    )(pj3, dtr3, sp3, scv3, pool_w, pool_scale, conv_w, conv_b, dt_bias_p, a_log_p, dskip_row,
      ehot, ghot)


def _sample2_setup(xw_ref, cs_ref, xwt_ref, *, sb, per):
    j = pl.program_id(1)

    @pl.when(j == 0)
    def _():
        for blk in range(D_SSM // LANES):
            xwt_ref[blk * LANES:(blk + 1) * LANES, :] = jnp.transpose(
                xw_ref[:, blk * LANES:(blk + 1) * LANES]).astype(BF16)

    return [jnp.concatenate([cs_ref[pl.ds(l * sb + per * j + 2 * pr + bi, 1), :]
                             for bi in range(2) for l in range(DEC_SEQ)], axis=0).astype(BF16)
            for pr in range(per // 2)]


def _sample2_group(g, refs, c_rows, *, sb, per):
    cdec_ref, st_ref, _, bs_ref, _, nst_ref, yo_ref, xwt_ref = refs
    i = pl.program_id(0)
    j = pl.program_id(1)
    nl = DEC_SEQ
    rows = sb * nl
    seq_of_row = lax.broadcasted_iota(jnp.int32, (rows, 1), 0) & (sb - 1)
    gw = HEADS_PER_GROUP * HEAD_DIM
    gs = slice(g * D_STATE, (g + 1) * D_STATE)
    for pr in range(per // 2):
        q0 = per * j + 2 * pr
        c8 = c_rows[pr][:, gs]
        b_blk = bs_ref[:, gs]
        zero_b = jnp.zeros_like(b_blk)
        w2 = jnp.concatenate([jnp.where(seq_of_row == q0, b_blk, zero_b),
                              jnp.where(seq_of_row == q0 + 1, b_blk, zero_b)], axis=1)
        u2 = _dot(xwt_ref[g * gw:(g + 1) * gw, :], w2)
        for bi in range(2):
            sq = 2 * pr + bi
            s0 = st_ref[sq, g * HEADS_PER_GROUP:(g + 1) * HEADS_PER_GROUP].reshape(gw, D_STATE)
            yo = _dot_nt(c8, s0.astype(BF16))
            base = (i * sb + q0 + bi) * HEADS + g * HEADS_PER_GROUP
            for r in range(HEADS_PER_GROUP):
                dec = cdec_ref[base + r]
                rs = slice(r * HEAD_DIM, (r + 1) * HEAD_DIM)
                nst_ref[sq, g * HEADS_PER_GROUP + r] = (
                    s0[rs] * dec + u2[rs, bi * D_STATE:(bi + 1) * D_STATE])
            for l in range(nl):
                yo_ref[pl.ds(l * sb + q0 + bi, 1), g * gw:(g + 1) * gw] = (
                    yo[bi * nl + l:bi * nl + l + 1])


N_SSD_IN, N_SSD_OUT, N_SSD_SCRATCH = 11, 3, 6
N_ST_IN, N_ST_OUT = 5, 2


def _ssd_state_kernel(*refs, nc, sb, per):
    n_in = N_SSD_IN + N_ST_IN + 1
    ins, rest = refs[:n_in], refs[n_in:]
    n_out = N_SSD_OUT + N_ST_OUT + 1
    outs, scratch = rest[:n_out], rest[n_out:]
    outs[-1][...] = ins[-1][...].astype(BF16)
    st_refs = (*ins[N_SSD_IN:N_SSD_IN + N_ST_IN], *outs[N_SSD_OUT:N_SSD_OUT + N_ST_OUT],
               *scratch[N_SSD_SCRATCH:])
    c_rows = _sample2_setup(st_refs[2], st_refs[4], st_refs[7], sb=sb, per=per)
    _ssd_body(*ins[:N_SSD_IN], *outs[:N_SSD_OUT], *scratch[:N_SSD_SCRATCH], nc=nc,
              after_group=functools.partial(_sample2_group, refs=st_refs, c_rows=c_rows,
                                            sb=sb, per=per))


def _ssd_prompt_sample_state(proj, dt_raw, conv_w, conv_b, dt_bias_p, a_log_p, dskip_row, norm_g,
                             bsz, seq, cdec_flat, state, xw3, bs3, cs3, sb, w_out2d):
    q = CHUNK
    nc = seq // q
    db = state.shape[0]
    rb = DEC_SEQ * sb
    per = sb // nc
    assert rb == LANES and db // sb == bsz and per * nc == sb and per % 2 == 0
    d_mix = w_out2d.shape[0]
    slab = d_mix // (bsz * nc)
    assert slab * bsz * nc == d_mix and slab % (2 * SUBLANES) == 0
    wslab = pl.BlockSpec((slab, D_MODEL), lambda b, c: (b * nc + c, 0))
    row = lambda b, c: b * nc + c
    const = lambda b, c: (0, 0)
    blk2 = lambda width: pl.BlockSpec((None, rb, width), lambda i, j: (i, 0, 0))
    st_spec = pl.BlockSpec((per, HEADS, HEAD_DIM, D_STATE), lambda i, j: (i * nc + j, 0, 0, 0))
    return pl.pallas_call(
        functools.partial(_ssd_state_kernel, nc=nc, sb=sb, per=per),
        grid=(bsz, nc),
        in_specs=[
            pl.BlockSpec((q, D_SSM), lambda b, c: (row(b, c), OFF_Z // D_SSM)),
            pl.BlockSpec((q, D_SSM), lambda b, c: (row(b, c), OFF_XS // D_SSM)),
            pl.BlockSpec((q, D_BC), lambda b, c: (row(b, c), OFF_B // D_BC)),
            pl.BlockSpec((q, D_BC), lambda b, c: (row(b, c), OFF_C // D_BC)),
            pl.BlockSpec((q, LANES), lambda b, c: (row(b, c), 0)),
            pl.BlockSpec((CONV_WIDTH, D_CONV), const),
            pl.BlockSpec((1, D_CONV), const),
            pl.BlockSpec((1, LANES), const),
            pl.BlockSpec((1, LANES), const),
            pl.BlockSpec((1, D_SSM), const),
            pl.BlockSpec((1, D_SSM), const),
            pl.BlockSpec(memory_space=pltpu.SMEM),
            st_spec, blk2(D_SSM), blk2(D_BC), blk2(D_BC),
            wslab,
        ],
        out_specs=[
            pl.BlockSpec((q, D_SSM), lambda b, c: (row(b, c), 0)),
            pl.BlockSpec((None, CONV_WIDTH - 1, D_CONV), lambda b, c: (b, 0, 0)),
            pl.BlockSpec((None, HEADS, HEAD_DIM, D_STATE), lambda b, c: (b, 0, 0, 0)),
            st_spec, blk2(D_SSM),
            wslab,
        ],
        out_shape=[
            jax.ShapeDtypeStruct((bsz * seq, D_SSM), BF16),
            jax.ShapeDtypeStruct((bsz, CONV_WIDTH - 1, D_CONV), F32),
            jax.ShapeDtypeStruct((bsz, HEADS, HEAD_DIM, D_STATE), F32),
            jax.ShapeDtypeStruct(state.shape, F32),
            jax.ShapeDtypeStruct((db // sb, rb, D_SSM), F32),
            jax.ShapeDtypeStruct(w_out2d.shape, BF16),
        ],
        scratch_shapes=[
            pltpu.VMEM((D_CONV // LANES, q + SUBLANES, LANES), F32),
            pltpu.VMEM((D_STATE, D_SSM), F32),
            pltpu.VMEM((D_SSM // LANES, q + SUBLANES, LANES), F32),
            pltpu.VMEM((D_BC // LANES, q + SUBLANES, LANES), F32),
            pltpu.VMEM((D_BC // LANES, q + SUBLANES, LANES), F32),
            pltpu.VMEM((q, D_SSM), F32),
            pltpu.VMEM((D_SSM, rb), BF16),
        ],
        compiler_params=pltpu.CompilerParams(
            dimension_semantics=("arbitrary", "arbitrary"),
            vmem_limit_bytes=VMEM_LIMIT),
        name="ssd_prompt_sample_state",
    )(proj, proj, proj, proj, dt_raw, conv_w, conv_b, dt_bias_p, a_log_p, dskip_row, norm_g,
      cdec_flat, state, xw3, bs3, cs3, w_out2d)


def _outproj_sample_kernel(a_ref, yo_ref, ecum_ref, ydx_ref, z_ref, ng_ref, w_ref, x_ref,
                           g_ref, y_ref, b_ref):
    gw = D_SSM // GROUPS
    for g in range(GROUPS):
        cs = slice(g * gw, (g + 1) * gw)
        y = ydx_ref[:, cs] + ecum_ref[:, cs] * yo_ref[:, cs]
        yz = y * _silu(z_ref[:, cs].astype(F32))
        ms = jnp.sum(yz * yz, axis=-1, keepdims=True) * (1.0 / gw)
        b_ref[:, cs] = (yz * lax.rsqrt(ms + EPS) * ng_ref[:, cs]).astype(BF16)
    _outproj_kernel(a_ref, b_ref, w_ref, x_ref, g_ref, y_ref)


def _outproj_sample(out_a, yo, ecum, ydx, proj, norm_g, w_bf, x3d, final_g):
    db, nl, _ = x3d.shape
    m = db * nl
    tm = min(OUTPROJ_SAMPLE_TM, m)
    assert tm % (nl * SAMPLE_BLOCK) == 0
    seqblk = pl.BlockSpec((tm // nl, nl, D_MODEL), lambda i: (i, 0, 0))
    rowblk = lambda width: pl.BlockSpec((tm, width), lambda i: (i, 0))
    const = lambda shape, **kw: pl.BlockSpec(shape, lambda i: (0, 0), **kw)
    return pl.pallas_call(
        _outproj_sample_kernel,
        grid=(m // tm,),
        in_specs=[
            rowblk(D_POOL), rowblk(D_SSM), rowblk(D_SSM), rowblk(D_SSM),
            pl.BlockSpec((tm, D_SSM), lambda i: (i, OFF_Z // D_SSM)),
            const((1, D_SSM)),
            const((D_POOL + D_SSM, D_MODEL), pipeline_mode=pl.Buffered(1)),
            seqblk, const((1, D_MODEL)),
        ],
        out_specs=seqblk,
        out_shape=jax.ShapeDtypeStruct((db, nl, D_MODEL), F32),
        scratch_shapes=[pltpu.VMEM((tm, D_SSM), BF16)],
        compiler_params=pltpu.CompilerParams(
            dimension_semantics=("arbitrary",),
            vmem_limit_bytes=VMEM_LIMIT),
        name="outproj_sample",
    )(out_a, yo, ecum, ydx, proj, norm_g, w_bf, x3d, final_g)


def kernel(x_prompt, x_sample, state_pool, state_conv, state_ssm, norm_g, w_in, conv_w, conv_b,
           dt_bias, a_log, d_skip, ssm_norm_g, pool_w, pool_scale, w_out, final_g):
    bsz, seq, _ = x_prompt.shape
    db, nl, _ = x_sample.shape
    assert nl == DEC_SEQ and seq % CHUNK == 0 and w_in.shape[0] == 1

    w_t = jnp.transpose(w_in[0])
    pool_w_b = pool_w[0]
    g_in = norm_g[0][None, :]
    g_fin = final_g[None, :]
    ps = pool_scale[0][None, :]
    cw = conv_w[0]
    cbias = conv_b[0][None, :]
    pad_h = lambda v: jnp.pad(v, (0, LANES - HEADS))[None, :]
    dtb = pad_h(dt_bias[0])
    alog = pad_h(a_log[0])
    dskip_row = jnp.repeat(d_skip[0], HEAD_DIM)[None, :]
    ng = ssm_norm_g[0][None, :]

    sb = SAMPLE_BLOCK
    nblk = db // sb
    rb = nl * sb
    proj_s, dt_s, w_bf, wdt_bf = _inproj(x_sample, g_in, w_t, w_t, PERM_BLOCK)
    pj3 = proj_s.reshape(nblk, rb, D_MAIN)
    head_of_ch = jnp.arange(D_SSM, dtype=jnp.int32) // HEAD_DIM
    ehot = (jnp.arange(LANES, dtype=jnp.int32)[:, None] == head_of_ch[None, :]).astype(BF16)
    grp_of_row = jnp.arange(D_BC, dtype=jnp.int32) // D_STATE
    head_id = jnp.arange(LANES, dtype=jnp.int32)
    ghot = ((head_id[None, :] // HEADS_PER_GROUP == grp_of_row[:, None])
            & (head_id[None, :] < HEADS)).astype(BF16)
    (oa_s, npool_s, nconv_s, cs_s, bs_s, xw_s, ydx_s, ecum_s, cdec_s) = _sample1(
        pj3, dt_s.reshape(nblk, rb, LANES), jnp.transpose(state_pool[0], (1, 0, 2)),
        jnp.transpose(state_conv[0], (1, 0, 2)), pool_w_b, ps, cw, cbias, dtb, alog,
        dskip_row, ehot, ghot, sb)

    xp2 = x_prompt.reshape(bsz * seq, D_MODEL)
    proj_p, dt_p = _inproj(xp2, g_in, w_bf, wdt_bf, PROMPT_TN)
    ob_p, nconv_p, nssm_p, nssm_s, yo_s, w_out_bf = _ssd_prompt_sample_state(
        proj_p, dt_p, cw, cbias, dtb, alog, dskip_row, ng, bsz, seq,
        cdec_s[:, :HEADS].reshape(db * HEADS), state_ssm[0], xw_s, bs_s, cs_s, sb, w_out[0])
    y_p, npool_p = _pool_outproj_prompt(proj_p, pool_w_b, ps, ob_p, w_out_bf, xp2, g_fin, bsz, seq)
    y_p = y_p.reshape(bsz, seq, D_MODEL)

    flat = lambda t: t.reshape(db * nl, t.shape[-1])
    y_s = _outproj_sample(flat(oa_s), flat(yo_s), flat(ecum_s), flat(ydx_s), proj_s, ng, w_out_bf,
                          x_sample, g_fin)

    return (y_p, y_s,
            npool_p[None], nconv_p[None], nssm_p[None],
            jnp.transpose(npool_s, (1, 0, 2))[None],
            jnp.transpose(nconv_s, (1, 0, 2))[None],
            nssm_s[None])
```
